```python
import jax, jax.numpy as jnp
from jax import lax
import numpy as np

D_MODEL = 2048
BATCH = 4
SEQ = 2048
DEPTH = 1
DEC_BATCH = 128
DEC_SEQ = 4
PAST_LEN = 16384
PAGE_SIZE = 128

HEAD_DIM = 64
N_HEADS = D_MODEL // HEAD_DIM
D_A = N_HEADS * HEAD_DIM
D_CONV = D_MODEL
CONV_W = 31
FFN_CONV_W = 3
D_FF = ((8 * D_MODEL // 3 + 127) // 128) * 128
DECAY_LORA = max(32, round(1.8 * D_MODEL ** 0.5 / 32) * 32)
AAA_LORA = max(32, round(1.8 * D_MODEL ** 0.5 / 32) * 32)
GATE_LORA = max(32, round(0.6 * D_MODEL ** 0.8 / 32) * 32)
GN_EPS = HEAD_DIM * 1e-5
N_MOD = 6
IN_COLS = 3 * D_A + 2 * D_CONV + D_A + D_CONV

kernel_name = 'rwkv7_conformer_gated_hybrid_step'


def _rmsnorm(x, g, eps=1e-6):
    xf = x.astype(jnp.float32)
    y = xf * lax.rsqrt(jnp.mean(xf * xf, axis=-1, keepdims=True) + eps)
    return (y * g.astype(jnp.float32)).astype(x.dtype)


def _layernorm(x, g, b, eps=1e-5):
    xf = x.astype(jnp.float32)
    m = jnp.mean(xf, axis=-1, keepdims=True)
    var = jnp.mean(jnp.square(xf - m), axis=-1, keepdims=True)
    y = (xf - m) * lax.rsqrt(var + eps)
    return (y * g.astype(jnp.float32) + b.astype(jnp.float32)).astype(x.dtype)


def _causal_dwconv(buf, u, k, b):
    z = jnp.concatenate([buf.astype(u.dtype), u], axis=1)
    y = lax.conv_general_dilated(z, k[:, None, :].astype(u.dtype), (1,), 'VALID',
                                 dimension_numbers=('NWC', 'WIO', 'NWC'),
                                 feature_group_count=u.shape[-1])
    return y + b.astype(u.dtype), z[:, z.shape[1] - (k.shape[0] - 1):]


def _wkv7(S0, r, decay, k, v, kk, b):
    def step(S, inp):
        r_t, w_t, k_t, v_t, kk_t, b_t = inp
        sa = jnp.einsum('bhvk,bhk->bhv', S, kk_t)
        S = (S * w_t[:, :, None, :] - sa[..., None] * b_t[:, :, None, :]
             + v_t[..., None] * k_t[:, :, None, :])
        return S, jnp.einsum('bhvk,bhk->bhv', S, r_t)
    seq = tuple(jnp.swapaxes(t, 0, 1) for t in (r, decay, k, v, kk, b))
    S, ys = lax.scan(step, S0.astype(jnp.float32), seq)
    return jnp.swapaxes(ys, 0, 1), S


def _token_mixers(h, shift0, wkv0, conv0, p):
    bt, t, _ = h.shape
    f32 = jnp.float32
    heads = lambda z: z.astype(f32).reshape(bt, t, N_HEADS, HEAD_DIM)
    h_prev = jnp.concatenate([shift0[:, None, :].astype(h.dtype), h[:, :-1]], axis=1)
    dx = h_prev - h
    mu = p['mu']
    xr, xw, xk, xv, xa, xg = (h + dx * mu[i] for i in range(6))
    w_in = p['w_in']
    r = xr @ w_in[:, :D_A]
    k = xk @ w_in[:, D_A:2 * D_A]
    v = xv @ w_in[:, 2 * D_A:3 * D_A]
    rest = h @ w_in[:, 3 * D_A:]
    u_glu = rest[..., :2 * D_CONV] + p['b_glu']
    gate_a = rest[..., 2 * D_CONV:2 * D_CONV + D_A]
    gate_b = rest[..., 2 * D_CONV + D_A:]
    w_pre = (p['w0'] + jnp.tanh(xw @ p['w1']) @ p['w2']).astype(f32)
    decay = jnp.exp(-jnp.exp(-jax.nn.softplus(-w_pre) - 0.5))
    a = jax.nn.sigmoid(p['a0'] + (xa @ p['a1']) @ p['a2'])
    g = jax.nn.sigmoid(xg @ p['g1']) @ p['g2']
    kk = heads(k * p['k_k'])
    kk = kk / jnp.maximum(jnp.sqrt(jnp.sum(kk * kk, axis=-1, keepdims=True)), 1e-12)
    k = k * (1 + (a - 1) * p['k_a'])
    r_h, k_h, v_h, a_h = heads(r), heads(k), heads(v), heads(a)
    y, wkv1 = _wkv7(wkv0, r_h, heads(decay), k_h, v_h, kk, kk * a_h)
    ym = jnp.mean(y, axis=-1, keepdims=True)
    yv = jnp.mean(jnp.square(y - ym), axis=-1, keepdims=True)
    y = ((y - ym) * lax.rsqrt(yv + GN_EPS)).reshape(bt, t, D_A)
    y = y * p['lnx_g'].astype(f32) + p['lnx_b'].astype(f32)
    bonus = jnp.sum(r_h * k_h * p['r_k'].astype(f32), axis=-1, keepdims=True) * v_h
    y_a = ((y + bonus.reshape(bt, t, D_A)) * g.astype(f32)).astype(h.dtype)
    glu = u_glu[..., :D_CONV] * jax.nn.sigmoid(u_glu[..., D_CONV:])
    zc, conv1 = _causal_dwconv(conv0, glu, p['dw_k'], p['dw_b'])
    y_b = jax.nn.silu(_layernorm(zc, p['ln_conv_g'], p['ln_conv_b']))
    merged = jax.nn.sigmoid(gate_a) * y_a + jax.nn.sigmoid(gate_b) * y_b
    return merged @ p['w_out'], h[:, -1], wkv1.astype(wkv0.dtype), conv1


def _conv_ffn(h, buf0, p):
    u = h @ p['w_up']
    u, buf1 = _causal_dwconv(buf0, u, p['ffn_dw_k'], p['ffn_dw_b'])
    ug, uv = u[..., :D_FF], u[..., D_FF:]
    return (jax.nn.silu(ug) * uv) @ p['w_down'], buf1


def _layer(x, c, shift0, wkv0, conv0, ffn0, p):
    mod = jnp.einsum('bd,de->be', jax.nn.silu(c), p['w_ada']) + p['b_ada']
    mod = mod.reshape(c.shape[0], 1, N_MOD, D_MODEL)
    sh1, sc1, gt1, sh2, sc2, gt2 = (mod[:, :, i] for i in range(N_MOD))
    h = _rmsnorm(x, p['norm1_g']) * (1 + sc1) + sh1
    mix, shift1, wkv1, conv1 = _token_mixers(h, shift0, wkv0, conv0, p)
    x = x + gt1 * mix
    h2 = _rmsnorm(x, p['norm2_g']) * (1 + sc2) + sh2
    f, ffn1 = _conv_ffn(h2, ffn0, p)
    x = x + gt2 * f
    return x, shift1, wkv1, conv1, ffn1


def setup_inputs(seed: int = 0) -> dict:
    key = jax.random.key(seed)
    ks = iter(jax.random.split(key, 48))
    nrm = lambda shape, s: jax.random.normal(next(ks), shape, jnp.float32) * s
    D = D_MODEL
    F2 = 2 * D_FF
    return {
        'x_prompt': nrm((BATCH, SEQ, D), 1.0),
        'x_sample': nrm((DEC_BATCH, DEC_SEQ, D), 1.0),
        'state_shift': nrm((DEC_BATCH, D), 1.0),
        'state_wkv': nrm((DEC_BATCH, N_HEADS, HEAD_DIM, HEAD_DIM), 0.3),
        'state_conv': nrm((DEC_BATCH, CONV_W - 1, D_CONV), 0.5),
        'state_ffn': nrm((DEC_BATCH, FFN_CONV_W - 1, F2), 1.0),
        'c_prompt': nrm((BATCH, D), 1.0),
        'c_sample': nrm((DEC_BATCH, D), 1.0),
        'norm1_g': 1.0 + nrm((D,), 0.02),
        'norm2_g': 1.0 + nrm((D,), 0.02),
        'normf_g': 1.0 + nrm((D,), 0.02),
        'w_ada': nrm((D, N_MOD * D), 0.5 * D ** -0.5),
        'b_ada': nrm((N_MOD * D,), 0.02),
        'mu': jax.random.uniform(next(ks), (6, D), jnp.float32),
        'w_in': nrm((D, IN_COLS), D ** -0.5),
        'b_glu': nrm((2 * D_CONV,), 0.02),
        'w0': nrm((D,), 0.5),
        'w1': nrm((D, DECAY_LORA), D ** -0.5),
        'w2': nrm((DECAY_LORA, D), DECAY_LORA ** -0.5),
        'a0': nrm((D,), 0.5),
        'a1': nrm((D, AAA_LORA), D ** -0.5),
        'a2': nrm((AAA_LORA, D), AAA_LORA ** -0.5),
        'g1': nrm((D, GATE_LORA), D ** -0.5),
        'g2': nrm((GATE_LORA, D), GATE_LORA ** -0.5),
        'k_k': 0.85 + nrm((D,), 0.02),
        'k_a': 1.0 + nrm((D,), 0.02),
        'r_k': nrm((N_HEADS, HEAD_DIM), 0.1),
        'lnx_g': 1.0 + nrm((D_A,), 0.02),
        'lnx_b': nrm((D_A,), 0.02),
        'dw_k': nrm((CONV_W, D_CONV), CONV_W ** -0.5),
        'dw_b': nrm((D_CONV,), 0.02),
        'ln_conv_g': 1.0 + nrm((D_CONV,), 0.02),
        'ln_conv_b': nrm((D_CONV,), 0.02),
        'w_out': nrm((D, D), D ** -0.5),
        'w_up': nrm((D, F2), D ** -0.5),
        'ffn_dw_k': nrm((FFN_CONV_W, F2), FFN_CONV_W ** -0.5),
        'ffn_dw_b': nrm((F2,), 0.02),
        'w_down': nrm((D_FF, D), D_FF ** -0.5),
    }


def reference(x_prompt, x_sample, state_shift, state_wkv, state_conv, state_ffn,
              c_prompt, c_sample, norm1_g, norm2_g, normf_g, w_ada, b_ada, mu, w_in, b_glu,
              w0, w1, w2, a0, a1, a2, g1, g2, k_k, k_a, r_k, lnx_g, lnx_b,
              dw_k, dw_b, ln_conv_g, ln_conv_b, w_out, w_up, ffn_dw_k, ffn_dw_b, w_down):
    p = dict(norm1_g=norm1_g, norm2_g=norm2_g, w_ada=w_ada, b_ada=b_ada, mu=mu, w_in=w_in,
             b_glu=b_glu, w0=w0, w1=w1, w2=w2, a0=a0, a1=a1, a2=a2, g1=g1, g2=g2,
             k_k=k_k, k_a=k_a, r_k=r_k, lnx_g=lnx_g, lnx_b=lnx_b, dw_k=dw_k, dw_b=dw_b,
             ln_conv_g=ln_conv_g, ln_conv_b=ln_conv_b, w_out=w_out, w_up=w_up,
             ffn_dw_k=ffn_dw_k, ffn_dw_b=ffn_dw_b, w_down=w_down)
    bp = x_prompt.shape[0]
    dt = x_prompt.dtype
    z_shift = jnp.zeros((bp, D_MODEL), dt)
    z_wkv = jnp.zeros((bp, N_HEADS, HEAD_DIM, HEAD_DIM), dt)
    z_conv = jnp.zeros((bp, CONV_W - 1, D_CONV), dt)
    z_ffn = jnp.zeros((bp, FFN_CONV_W - 1, 2 * D_FF), dt)
    y_p, y_s = x_prompt, x_sample
    for _ in range(DEPTH):
        y_p, shift_p, wkv_p, conv_p, ffn_p = _layer(y_p, c_prompt, z_shift, z_wkv, z_conv, z_ffn, p)
        y_s, shift_s, wkv_s, conv_s, ffn_s = _layer(y_s, c_sample, state_shift, state_wkv,
                                                    state_conv, state_ffn, p)
    y_p = _rmsnorm(y_p, normf_g)
    y_s = _rmsnorm(y_s, normf_g)
    return (y_p, y_s, shift_p, wkv_p, conv_p, ffn_p, shift_s, wkv_s, conv_s, ffn_s)
```

```python
import functools

import jax
import jax.numpy as jnp
from jax import lax
from jax.experimental import pallas as pl
from jax.experimental.pallas import tpu as pltpu

F32 = jnp.float32
BF16 = jnp.bfloat16

HEAD_DIM = 64
CONV_W = 31
FFN_CONV_W = 3
N_MOD = 6
GN_EPS = HEAD_DIM * 1e-5
SUBLANES = 8
LANES = 128
LORA_PAD = 128
SAMPLE_TILE = 8
V7X_VMEM_LIMIT = 56 * 1024 * 1024

SLAB_R, SLAB_K, SLAB_V, SLAB_H, SLAB_W, SLAB_A, SLAB_G = range(7)
MU_SLABS = (SLAB_R, SLAB_W, SLAB_K, SLAB_V, SLAB_A, SLAB_G)


def _cparams(n_axes):
    return pltpu.CompilerParams(dimension_semantics=("arbitrary",) * n_axes,
                                vmem_limit_bytes=V7X_VMEM_LIMIT)


def _sigmoid(x):
    return 1.0 / (1.0 + jnp.exp(-x))


def _silu(x):
    return x * _sigmoid(x)


def _softplus(x):
    return jnp.maximum(x, 0.0) + jnp.log(1.0 + jnp.exp(-jnp.abs(x)))


def _mod_kernel(c_ref, w_ref, b_ref, o_ref):
    s = _silu(c_ref[...]).astype(BF16)
    o_ref[...] = jnp.dot(s, w_ref[...].astype(BF16), preferred_element_type=F32) + b_ref[...]


def _modulation(c, w_ada, b_ada):
    rows, d = c.shape
    n = w_ada.shape[1]
    tn = 1024
    return pl.pallas_call(
        _mod_kernel,
        grid=(n // tn,),
        in_specs=[pl.BlockSpec((rows, d), lambda j: (0, 0)),
                  pl.BlockSpec((d, tn), lambda j: (0, j)),
                  pl.BlockSpec((1, tn), lambda j: (0, j))],
        out_specs=pl.BlockSpec((rows, tn), lambda j: (0, j)),
        out_shape=jax.ShapeDtypeStruct((rows, n), F32),
        compiler_params=_cparams(1),
        name="modulation",
    )(c, w_ada, b_ada.reshape(1, n))


def _prep_kernel(x_ref, m_ref, g_ref, mu_ref, s0_ref, xs_ref, so_ref, hbuf, *, tm, off):
    t = pl.program_id(1)
    x = x_ref[...]
    ms = jnp.mean(x * x, axis=-1, keepdims=True)
    h = (x * lax.rsqrt(ms + 1e-6) * g_ref[...]) * (1.0 + m_ref[1:2, :]) + m_ref[0:1, :]
    hbuf[SUBLANES:SUBLANES + tm, :] = h

    @pl.when(t == 0)
    def _():
        hbuf[SUBLANES - 1 + off:SUBLANES + off, :] = s0_ref[...]

    dx = hbuf[SUBLANES - 1:SUBLANES - 1 + tm, :] - h
    for mi, slab in enumerate(MU_SLABS):
        xs_ref[slab] = (h + dx * mu_ref[mi:mi + 1, :]).astype(BF16)
    xs_ref[SLAB_H] = h.astype(BF16)
    last = h[tm - 1:tm, :]
    hbuf[SUBLANES - 1:SUBLANES, :] = last
    so_ref[...] = last


def _prep(x, mod, norm_g, mu, shift0, *, tm, off):
    b, t, d = x.shape
    kern = functools.partial(_prep_kernel, tm=tm, off=off)
    return pl.pallas_call(
        kern,
        grid=(b, t // tm),
        in_specs=[pl.BlockSpec((None, tm, d), lambda i, j: (i, j, 0)),
                  pl.BlockSpec((None, N_MOD, d), lambda i, j: (i, 0, 0)),
                  pl.BlockSpec((1, d), lambda i, j: (0, 0)),
                  pl.BlockSpec((N_MOD, d), lambda i, j: (0, 0)),
                  pl.BlockSpec((None, 1, d), lambda i, j: (i, 0, 0))],
        out_specs=[pl.BlockSpec((7, None, tm, d), lambda i, j: (0, i, j, 0)),
                   pl.BlockSpec((None, 1, d), lambda i, j: (i, 0, 0))],
        out_shape=[jax.ShapeDtypeStruct((7, b, t, d), BF16),
                   jax.ShapeDtypeStruct((b, 1, d), F32)],
        scratch_shapes=[pltpu.VMEM((SUBLANES + tm, d), F32)],
        compiler_params=_cparams(2),
        name="prep",
    )(x, mod, norm_g.reshape(1, d), mu, shift0.reshape(b, 1, d))


def _gate_blocks(gate, rows, tm, rows_per_gate):
    if gate.ndim == 2:
        return gate.reshape(rows // tm, tm, gate.shape[-1]), 1
    assert rows_per_gate % tm == 0, (rows_per_gate, tm)
    return gate, rows_per_gate // tm


def _mm_kernel(*refs, gated):
    if gated:
        a_ref, w_ref, res_ref, gate_ref, o_ref, wb = refs
    else:
        a_ref, w_ref, o_ref, wb = refs

    @pl.when(pl.program_id(1) == 0)
    def _():
        wb[...] = w_ref[...].astype(BF16)

    acc = jnp.dot(a_ref[...], wb[...], preferred_element_type=F32)
    if gated:
        acc = res_ref[...] + gate_ref[...] * acc
    o_ref[...] = acc


def _matmul(a, w, *, tm, tn, slab_of=None, res=None, gate=None, rows_per_gate=None, name):
    _, rows, k = a.shape
    n = w.shape[1]
    gated = res is not None
    if slab_of is None:
        slab_of = lambda j: 0
    in_specs = [pl.BlockSpec((None, tm, k), lambda j, i: (slab_of(j), i, 0)),
                pl.BlockSpec((k, tn), lambda j, i: (0, j))]
    args = [a, w]
    if gated:
        gate, tiles_per_gate = _gate_blocks(gate, rows, tm, rows_per_gate)
        gr = gate.shape[1]
        in_specs += [pl.BlockSpec((tm, tn), lambda j, i: (i, j)),
                     pl.BlockSpec((None, gr, tn), lambda j, i: (i // tiles_per_gate, 0, j))]
        args += [res, gate]
    return pl.pallas_call(
        functools.partial(_mm_kernel, gated=gated),
        grid=(pl.cdiv(n, tn), rows // tm),
        in_specs=in_specs,
        out_specs=pl.BlockSpec((tm, tn), lambda j, i: (i, j)),
        out_shape=jax.ShapeDtypeStruct((rows, n), F32),
        scratch_shapes=[pltpu.VMEM((k, tn), BF16)],
        compiler_params=_cparams(2),
        name=name,
    )(*args)


def _lora_kernel(xw_ref, xa_ref, xg_ref, w1, w2, a1, a2, g1, g2, w0, a0, dec_ref, a_ref, g_ref):
    dot = functools.partial(jnp.dot, preferred_element_type=F32)
    hw = jnp.tanh(dot(xw_ref[...], w1[...])).astype(BF16)
    w_pre = w0[...] + dot(hw, w2[...])
    dec_ref[...] = jnp.exp(-jnp.exp(-_softplus(-w_pre) - 0.5))
    ha = dot(xa_ref[...], a1[...]).astype(BF16)
    a_ref[...] = _sigmoid(a0[...] + dot(ha, a2[...]))
    hg = _sigmoid(dot(xg_ref[...], g1[...])).astype(BF16)
    g_ref[...] = dot(hg, g2[...])


def _pad_lora(w_a, w_b):
    r = w_a.shape[1]
    return (jnp.pad(w_a, ((0, 0), (0, LORA_PAD - r))).astype(BF16),
            jnp.pad(w_b, ((0, LORA_PAD - r), (0, 0))).astype(BF16))


def _lora(xs, w0, w1, w2, a0, a1, a2, g1, g2, *, tm):
    _, rows, d = xs.shape
    w1p, w2p = _pad_lora(w1, w2)
    a1p, a2p = _pad_lora(a1, a2)
    g1b, g2b = g1.astype(BF16), g2.astype(BF16)
    full = lambda arr: pl.BlockSpec(arr.shape, lambda i: (0, 0))
    slab = lambda s: pl.BlockSpec((None, tm, d), lambda i: (s, i, 0))
    row = pl.BlockSpec((tm, d), lambda i: (i, 0))
    vec = pl.BlockSpec((1, d), lambda i: (0, 0))
    out = jax.ShapeDtypeStruct((rows, d), F32)
    return pl.pallas_call(
        _lora_kernel,
        grid=(rows // tm,),
        in_specs=[slab(SLAB_W), slab(SLAB_A), slab(SLAB_G), full(w1p), full(w2p), full(a1p),
                  full(a2p), full(g1b), full(g2b), vec, vec],
        out_specs=[row, row, row],
        out_shape=[out, out, out],
        compiler_params=_cparams(1),
        name="lora",
    )(xs, xs, xs, w1p, w2p, a1p, a2p, g1b, g2b, w0.reshape(1, d), a0.reshape(1, d))


def _wkv_kernel(r_ref, k_ref, v_ref, a_ref, d_ref, kk_p, ka_p, rk_p, lg_p, lb_p, s0_ref,
                y_ref, so_ref, kk_s, b_s, k2_s, w_s, r_s, *, tt):
    n = HEAD_DIM

    @pl.when(pl.program_id(1) == 0)
    def _():
        so_ref[...] = s0_ref[...]

    def step(t, carry):
        r = r_ref[t]
        k = k_ref[t]
        v = v_ref[t]
        a = a_ref[t]
        kk = k * kk_p[...]
        kk = kk / jnp.maximum(jnp.sqrt(jnp.sum(kk * kk, axis=0, keepdims=True)), 1e-12)
        k2 = k * (1.0 + (a - 1.0) * ka_p[...])
        kk_s[...] = kk
        b_s[...] = kk * a
        k2_s[...] = k2
        w_s[...] = d_ref[t]
        r_s[...] = r
        sa = jnp.zeros((n, LANES), F32)
        for i in range(n):
            sa = sa + so_ref[i] * kk_s[i:i + 1, :]
        y = jnp.zeros((n, LANES), F32)
        for i in range(n):
            si = so_ref[i] * w_s[i:i + 1, :] - sa * b_s[i:i + 1, :] + v * k2_s[i:i + 1, :]
            so_ref[i] = si
            y = y + si * r_s[i:i + 1, :]
        ym = jnp.mean(y, axis=0, keepdims=True)
        yc = y - ym
        yv = jnp.mean(yc * yc, axis=0, keepdims=True)
        yn = yc * lax.rsqrt(yv + GN_EPS) * lg_p[...] + lb_p[...]
        bonus = jnp.sum(r * k2 * rk_p[...], axis=0, keepdims=True) * v
        y_ref[t] = yn + bonus
        return carry

    lax.fori_loop(0, tt, step, 0)


def _head_lanes(p, groups_of):
    hn = p.reshape(-1, HEAD_DIM).T
    return jnp.tile(hn, (1, groups_of))


def _wkv(r, k, v, a, dec, s0, k_k, k_a, r_k, lnx_g, lnx_b, *, tt):
    g, t, n, lanes = r.shape
    bb = lanes // (k_k.shape[0] // HEAD_DIM)
    params = [_head_lanes(p, bb) for p in (k_k, k_a, r_k.reshape(-1), lnx_g, lnx_b)]
    seq = pl.BlockSpec((None, tt, n, lanes), lambda i, j: (i, j, 0, 0))
    par = pl.BlockSpec((n, lanes), lambda i, j: (0, 0))
    st = pl.BlockSpec((None, n, n, lanes), lambda i, j: (i, 0, 0, 0))
    return pl.pallas_call(
        functools.partial(_wkv_kernel, tt=tt),
        grid=(g, t // tt),
        in_specs=[seq] * 5 + [par] * 5 + [st],
        out_specs=[seq, st],
        out_shape=[jax.ShapeDtypeStruct((g, t, n, lanes), F32),
                   jax.ShapeDtypeStruct((g, n, n, lanes), F32)],
        scratch_shapes=[pltpu.VMEM((n, lanes), F32)] * 5,
        compiler_params=_cparams(2),
        name="wkv7",
    )(r, k, v, a, dec, *params, s0)


CONV_PAD = 32
CONV_ROWS = 32


def _conv_kernel(u1_ref, u2_ref, bg1, bg2, dwk, dwb, lg, lb, st0_ref, y_ref, sto_ref, zbuf, cbuf,
                 *, tm, off):
    t = pl.program_id(1)
    hist = CONV_W - 1
    lo = CONV_PAD - hist
    d = u1_ref.shape[-1]
    glu = (u1_ref[...] + bg1[...]) * _sigmoid(u2_ref[...] + bg2[...])
    zbuf[CONV_PAD:CONV_PAD + tm, :] = glu

    @pl.when(t == 0)
    def _():
        zbuf[lo + off:CONV_PAD + off, :] = st0_ref[...]

    rc = min(CONV_ROWS, tm)
    for r0 in range(0, tm, rc):
        for c0 in range(0, d, LANES):
            acc = jnp.zeros((rc, LANES), F32)
            for j in range(CONV_W):
                acc = acc + zbuf[lo + r0 + j:lo + r0 + j + rc, c0:c0 + LANES] * dwk[j:j + 1, c0:c0 + LANES]
            cbuf[r0:r0 + rc, c0:c0 + LANES] = acc
    zc = cbuf[...] + dwb[...]
    m = jnp.mean(zc, axis=-1, keepdims=True)
    ctr = zc - m
    var = jnp.mean(ctr * ctr, axis=-1, keepdims=True)
    ln = ctr * lax.rsqrt(var + 1e-5) * lg[...] + lb[...]
    y_ref[...] = _silu(ln)
    carry = zbuf[lo + tm:CONV_PAD + tm, :]
    zbuf[lo:CONV_PAD, :] = carry
    sto_ref[...] = carry


def _conv(proj, col0, b_glu, dw_k, dw_b, ln_g, ln_b, st0, *, tm, off):
    b, t, _ = proj.shape
    d = dw_k.shape[1]
    hist = CONV_W - 1
    vec = pl.BlockSpec((1, d), lambda i, j: (0, 0))
    return pl.pallas_call(
        functools.partial(_conv_kernel, tm=tm, off=off),
        grid=(b, t // tm),
        in_specs=[pl.BlockSpec((None, tm, d), lambda i, j: (i, j, col0)),
                  pl.BlockSpec((None, tm, d), lambda i, j: (i, j, col0 + 1)),
                  pl.BlockSpec((1, d), lambda i, j: (0, 0)),
                  pl.BlockSpec((1, d), lambda i, j: (0, 1)),
                  pl.BlockSpec((CONV_W, d), lambda i, j: (0, 0)),
                  vec, vec, vec,
                  pl.BlockSpec((None, hist, d), lambda i, j: (i, 0, 0))],
        out_specs=[pl.BlockSpec((None, tm, d), lambda i, j: (i, j, 0)),
                   pl.BlockSpec((None, hist, d), lambda i, j: (i, 0, 0))],
        out_shape=[jax.ShapeDtypeStruct((b, t, d), F32),
                   jax.ShapeDtypeStruct((b, hist, d), F32)],
        scratch_shapes=[pltpu.VMEM((CONV_PAD + tm, d), F32), pltpu.VMEM((tm, d), F32)],
        compiler_params=_cparams(2),
        name="conformer_conv",
    )(proj, proj, b_glu.reshape(1, 2 * d), b_glu.reshape(1, 2 * d), dw_k, dw_b.reshape(1, d),
      ln_g.reshape(1, d), ln_b.reshape(1, d), st0)


def _merge_kernel(yw_ref, g_ref, ga_ref, gb_ref, yb_ref, x_ref, wo_ref, gt_ref, sh_ref, sc_ref,
                  n2_ref, x1_ref, h2_ref):
    ya = yw_ref[...] * g_ref[...]
    merged = _sigmoid(ga_ref[...]) * ya + _sigmoid(gb_ref[...]) * yb_ref[...]
    mix = jnp.dot(merged.astype(BF16), wo_ref[...], preferred_element_type=F32)
    x1 = x_ref[...] + gt_ref[...] * mix
    x1_ref[...] = x1
    ms = jnp.mean(x1 * x1, axis=-1, keepdims=True)
    h2 = (x1 * lax.rsqrt(ms + 1e-6) * n2_ref[...]) * (1.0 + sc_ref[...]) + sh_ref[...]
    h2_ref[...] = h2.astype(BF16)


def _merge(yw, g, proj, col_a, yb, x, w_out, gt, sh, sc, norm_g, *, tm, rows_per_gate):
    rows, d = x.shape
    (gt, tpg), (sh, _), (sc, _) = (_gate_blocks(m, rows, tm, rows_per_gate) for m in (gt, sh, sc))
    gr = gt.shape[1]
    row = pl.BlockSpec((tm, d), lambda i: (i, 0))
    modspec = pl.BlockSpec((None, gr, d), lambda i: (i // tpg, 0, 0))
    return pl.pallas_call(
        _merge_kernel,
        grid=(rows // tm,),
        in_specs=[row, row,
                  pl.BlockSpec((tm, d), lambda i: (i, col_a)),
                  pl.BlockSpec((tm, d), lambda i: (i, col_a + 1)),
                  row, row,
                  pl.BlockSpec((d, d), lambda i: (0, 0)),
                  modspec, modspec, modspec,
                  pl.BlockSpec((1, d), lambda i: (0, 0))],
        out_specs=[row, row],
        out_shape=[jax.ShapeDtypeStruct((rows, d), F32), jax.ShapeDtypeStruct((rows, d), BF16)],
        compiler_params=_cparams(1),
        name="merge_out_proj",
    )(yw, g, proj, proj, yb, x, w_out.astype(BF16), gt, sh, sc, norm_g.reshape(1, d))


def _ffn_kernel(u_ref, k_ref, b_ref, st0_ref, act_ref, sto_ref, ubuf, *, tm, off, d_ff):
    t = pl.program_id(1)
    hist = FFN_CONV_W - 1
    lo = SUBLANES - hist
    ubuf[SUBLANES:SUBLANES + tm, :] = u_ref[...]

    @pl.when(t == 0)
    def _():
        ubuf[lo + off:SUBLANES + off, :] = st0_ref[...]

    z = b_ref[...]
    for j in range(FFN_CONV_W):
        z = z + ubuf[lo + j:lo + j + tm, :] * k_ref[j:j + 1, :]
    act_ref[...] = (_silu(z[:, :d_ff]) * z[:, d_ff:]).astype(BF16)
    carry = ubuf[lo + tm:SUBLANES + tm, :]
    ubuf[lo:SUBLANES, :] = carry
    sto_ref[...] = carry


def _ffn_act(u, ffn_dw_k, ffn_dw_b, st0, *, tm, off):
    b, t, f2 = u.shape
    d_ff = f2 // 2
    hist = FFN_CONV_W - 1
    return pl.pallas_call(
        functools.partial(_ffn_kernel, tm=tm, off=off, d_ff=d_ff),
        grid=(b, t // tm),
        in_specs=[pl.BlockSpec((None, tm, f2), lambda i, j: (i, j, 0)),
                  pl.BlockSpec((FFN_CONV_W, f2), lambda i, j: (0, 0)),
                  pl.BlockSpec((1, f2), lambda i, j: (0, 0)),
                  pl.BlockSpec((None, hist, f2), lambda i, j: (i, 0, 0))],
        out_specs=[pl.BlockSpec((None, tm, d_ff), lambda i, j: (i, j, 0)),
                   pl.BlockSpec((None, hist, f2), lambda i, j: (i, 0, 0))],
        out_shape=[jax.ShapeDtypeStruct((b, t, d_ff), BF16),
                   jax.ShapeDtypeStruct((b, hist, f2), F32)],
        scratch_shapes=[pltpu.VMEM((SUBLANES + tm, f2), F32)],
        compiler_params=_cparams(2),
        name="convffn_act",
    )(u, ffn_dw_k, ffn_dw_b.reshape(1, f2), st0)


def _norm_kernel(x_ref, g_ref, o_ref):
    x = x_ref[...]
    ms = jnp.mean(x * x, axis=-1, keepdims=True)
    o_ref[...] = x * lax.rsqrt(ms + 1e-6) * g_ref[...]


def _final_norm(x, g, *, tm):
    rows, d = x.shape
    row = pl.BlockSpec((tm, d), lambda i: (i, 0))
    return pl.pallas_call(
        _norm_kernel,
        grid=(rows // tm,),
        in_specs=[row, pl.BlockSpec((1, d), lambda i: (0, 0))],
        out_specs=row,
        out_shape=jax.ShapeDtypeStruct((rows, d), F32),
        compiler_params=_cparams(1),
        name="final_norm",
    )(x, g.reshape(1, d))


def _to_head_lanes(z, group, t0):
    b, t, d = z.shape
    h = d // HEAD_DIM
    z = z[:, t0:].reshape(b // group, group, t - t0, h, HEAD_DIM)
    return z.transpose(0, 2, 4, 1, 3).reshape(b // group, t - t0, HEAD_DIM, group * h)


def _from_head_lanes(y, group):
    g, t, n, lanes = y.shape
    h = lanes // group
    return y.reshape(g, t, n, group, h).transpose(0, 3, 1, 4, 2).reshape(g * group, t, h * n)


def _layer(x, mod, shift0, wkv0, conv0, ffn0, p, *, tm_seq, off, tm_tok, tm_mm, tt):
    b, t, d = x.shape
    rows = b * t
    heads = d // HEAD_DIM
    group = LANES // heads

    xs, shift1 = _prep(x, mod, p["norm1_g"], p["mu"], shift0, tm=tm_seq, off=off)
    xs = xs.reshape(7, rows, d)

    n_in = p["w_in"].shape[1]
    tn_in = 1024
    slab_of = lambda j: jnp.minimum(j * tn_in // d, SLAB_H)
    proj = _matmul(xs, p["w_in"], tm=tm_mm, tn=tn_in, slab_of=slab_of, name="in_proj")
    dec, a, g = _lora(xs, p["w0"], p["w1"], p["w2"], p["a0"], p["a1"], p["a2"], p["g1"], p["g2"],
                      tm=tm_tok)

    proj3 = proj.reshape(b, t, n_in)
    r_t, k_t, v_t = (_to_head_lanes(proj3[..., i * d:(i + 1) * d], group, off) for i in range(3))
    a_t = _to_head_lanes(a.reshape(b, t, d), group, off)
    d_t = _to_head_lanes(dec.reshape(b, t, d), group, off)
    s0 = wkv0.reshape(b // group, group, heads, HEAD_DIM, HEAD_DIM)
    s0 = s0.transpose(0, 4, 3, 1, 2).reshape(b // group, HEAD_DIM, HEAD_DIM, LANES)
    y_t, s1 = _wkv(r_t, k_t, v_t, a_t, d_t, s0, p["k_k"], p["k_a"], p["r_k"], p["lnx_g"],
                   p["lnx_b"], tt=tt)
    yw = _from_head_lanes(y_t, group)
    if off:
        yw = jnp.pad(yw, ((0, 0), (off, 0), (0, 0)))
    wkv1 = s1.reshape(b // group, HEAD_DIM, HEAD_DIM, group, heads)
    wkv1 = wkv1.transpose(0, 3, 4, 2, 1).reshape(b, heads, HEAD_DIM, HEAD_DIM)

    yb, conv1 = _conv(proj3, 3, p["b_glu"], p["dw_k"], p["dw_b"], p["ln_conv_g"], p["ln_conv_b"],
                      conv0, tm=tm_seq, off=off)

    if off:
        gate_rows = lambda i: jnp.repeat(mod[:, i], t, axis=0)
        rows_per_gate = None
    else:
        gate_rows = lambda i: mod[:, i][:, None, :]
        rows_per_gate = t
    x1, h2 = _merge(yw.reshape(rows, d), g, proj, 5, yb.reshape(rows, d), x.reshape(rows, d),
                    p["w_out"], gate_rows(2), gate_rows(3), gate_rows(4), p["norm2_g"],
                    tm=tm_tok, rows_per_gate=rows_per_gate)

    u = _matmul(h2[None], p["w_up"], tm=tm_mm, tn=1024, name="up_proj")
    f2 = u.shape[1]
    act, ffn1 = _ffn_act(u.reshape(b, t, f2), p["ffn_dw_k"], p["ffn_dw_b"], ffn0,
                         tm=min(tm_seq, 128), off=off)
    x2 = _matmul(act.reshape(1, rows, f2 // 2), p["w_down"], tm=min(tm_mm, 512), tn=512, res=x1,
                 gate=gate_rows(5), rows_per_gate=rows_per_gate, name="down_proj")
    y = _final_norm(x2, p["normf_g"], tm=tm_tok)
    return y.reshape(b, t, d), shift1.reshape(b, d), wkv1, conv1, ffn1


def kernel(x_prompt, x_sample, state_shift, state_wkv, state_conv, state_ffn, c_prompt, c_sample,
           norm1_g, norm2_g, normf_g, w_ada, b_ada, mu, w_in, b_glu, w0, w1, w2, a0, a1, a2, g1, g2,
           k_k, k_a, r_k, lnx_g, lnx_b, dw_k, dw_b, ln_conv_g, ln_conv_b, w_out, w_up, ffn_dw_k,
           ffn_dw_b, w_down):
    p = dict(norm1_g=norm1_g, norm2_g=norm2_g, normf_g=normf_g, mu=mu, w_in=w_in, b_glu=b_glu,
             w0=w0, w1=w1, w2=w2, a0=a0, a1=a1, a2=a2, g1=g1, g2=g2, k_k=k_k, k_a=k_a, r_k=r_k,
             lnx_g=lnx_g, lnx_b=lnx_b, dw_k=dw_k, dw_b=dw_b, ln_conv_g=ln_conv_g,
             ln_conv_b=ln_conv_b, w_out=w_out, w_up=w_up, ffn_dw_k=ffn_dw_k, ffn_dw_b=ffn_dw_b,
             w_down=w_down)
    bp, tp, d = x_prompt.shape
    bs, ts, _ = x_sample.shape
    f2 = w_up.shape[1]

    c_all = jnp.concatenate([c_prompt, c_sample], axis=0)
    c_rows = -(-c_all.shape[0] // SUBLANES) * SUBLANES
    c_all = jnp.pad(c_all, ((0, c_rows - c_all.shape[0]), (0, 0)))
    mod = _modulation(c_all, w_ada, b_ada).reshape(c_rows, N_MOD, d)
    mod_p, mod_s = mod[:bp], mod[bp:bp + bs]

    y_p, shift_p, wkv_p, conv_p, ffn_p = _layer(
        x_prompt, mod_p, jnp.zeros((bp, d), F32),
        jnp.zeros((bp, d // HEAD_DIM, HEAD_DIM, HEAD_DIM), F32),
        jnp.zeros((bp, CONV_W - 1, d), F32), jnp.zeros((bp, FFN_CONV_W - 1, f2), F32), p,
        tm_seq=256, off=0, tm_tok=256, tm_mm=1024, tt=32)

    off = SAMPLE_TILE - ts
    x_s = jnp.pad(x_sample, ((0, 0), (off, 0), (0, 0)))
    y_s, shift_s, wkv_s, conv_s, ffn_s = _layer(
        x_s, mod_s, state_shift, state_wkv, state_conv, state_ffn, p,
        tm_seq=SAMPLE_TILE, off=off, tm_tok=256, tm_mm=bs * SAMPLE_TILE, tt=ts)
    return (y_p, y_s[:, off:], shift_p, wkv_p, conv_p, ffn_p, shift_s, wkv_s, conv_s, ffn_s)
```

```python
import functools
from typing import NamedTuple

import jax
import jax.numpy as jnp
from jax import lax
from jax.experimental import pallas as pl
from jax.experimental.pallas import tpu as pltpu

F32 = jnp.float32
BF16 = jnp.bfloat16

HEAD_DIM = 64
CONV_W = 31
FFN_CONV_W = 3
N_MOD = 6
GN_EPS = HEAD_DIM * 1e-5
SUBLANES = 8
LANES = 128
LORA_PAD = 128
SAMPLE_TILE = 8
GROUP = 4
V7X_VMEM_LIMIT = 56 * 1024 * 1024

SLAB_R, SLAB_K, SLAB_V, SLAB_H, SLAB_W, SLAB_A, SLAB_G = range(7)
MU_SLABS = (SLAB_R, SLAB_W, SLAB_K, SLAB_V, SLAB_A, SLAB_G)


class _Tiling(NamedTuple):
    groups: int
    t: int
    gs: int
    tq: int
    flat: bool

    @property
    def grid(self):
        return (self.groups // self.gs, self.t // self.tq)

    @property
    def seqs(self):
        return self.gs * GROUP

    def plain(self, cols, col=0):
        if self.flat:
            return pl.BlockSpec((self.seqs * self.tq, cols), lambda ig, it: (ig, col))
        return pl.BlockSpec((self.seqs, self.tq, cols), lambda ig, it: (ig, it, col))

    def head_lanes(self, blk=0):
        return pl.BlockSpec((self.gs, HEAD_DIM, self.tq, LANES), lambda ig, it: (ig, blk, it, 0))

    def per_seq(self, rows, cols):
        return pl.BlockSpec((self.seqs, rows, cols), lambda ig, it: (ig, 0, 0))


def _cparams(n_axes):
    return pltpu.CompilerParams(dimension_semantics=("arbitrary",) * n_axes,
                                vmem_limit_bytes=V7X_VMEM_LIMIT)


def _const_spec(shape):
    return pl.BlockSpec(shape, lambda *_: (0,) * len(shape), pipeline_mode=pl.Buffered(1))


def _sigmoid(x):
    return 1.0 / (1.0 + jnp.exp(-x))


def _silu(x):
    return x * _sigmoid(x)


def _softplus(x):
    return jnp.maximum(x, 0.0) + jnp.log(1.0 + jnp.exp(-jnp.abs(x)))


def _swap_lane_groups(m, lane):
    half = lane < 2 * (LANES // GROUP)
    odd = (lane // (LANES // GROUP)) % 2 == 1
    n = [None] * 4
    for s in range(2):
        n[s] = jnp.where(half, m[s], pltpu.roll(m[s + 2], LANES // 2, axis=1))
        n[s + 2] = jnp.where(half, pltpu.roll(m[s], LANES // 2, axis=1), m[s + 2])
    out = [None] * 4
    for p in (0, 2):
        a, b = n[p], n[p + 1]
        out[p] = jnp.where(odd, pltpu.roll(b, LANES // GROUP, axis=1), a)
        out[p + 1] = jnp.where(odd, b, pltpu.roll(a, LANES - LANES // GROUP, axis=1))
    return out


def _store_head_lanes(val, out_ref, gs, tq):
    lane = lax.broadcasted_iota(jnp.int32, (tq, LANES), 1)
    for g in range(gs):
        for q in range(val.shape[1] // LANES):
            src = [val[(g * GROUP + s) * tq:(g * GROUP + s + 1) * tq, q * LANES:(q + 1) * LANES]
                   for s in range(GROUP)]
            out = _swap_lane_groups(src, lane)
            for j in range(GROUP):
                out_ref[g, GROUP * q + j] = out[j]


def _rows2d(ref):
    v = ref[...]
    return v.reshape(-1, v.shape[-1])


def _mod_kernel(c_ref, w_ref, b_ref, o_ref):
    s = _silu(c_ref[...]).astype(BF16)
    o_ref[...] = jnp.dot(s, w_ref[...].astype(BF16), preferred_element_type=F32) + b_ref[...]


def _modulation(c, w_ada, b_ada):
    rows, d = c.shape
    n = w_ada.shape[1]
    tn = 1024
    return pl.pallas_call(
        _mod_kernel,
        grid=(n // tn,),
        in_specs=[pl.BlockSpec((rows, d), lambda j: (0, 0)),
                  pl.BlockSpec((d, tn), lambda j: (0, j)),
                  pl.BlockSpec((1, tn), lambda j: (0, j))],
        out_specs=pl.BlockSpec((rows, tn), lambda j: (0, j)),
        out_shape=jax.ShapeDtypeStruct((rows, n), F32),
        compiler_params=_cparams(1),
        name="modulation",
    )(c, w_ada, b_ada.reshape(1, n))


def _per_sequence(bs, one):
    if bs == 1:
        one(0)
    else:
        lax.fori_loop(0, bs, lambda s, c: (one(s), c)[1], 0)


def _prep_kernel(x_ref, m_ref, g_ref, mu_ref, s0_ref, xs_ref, so_ref, hbuf, *, tm, off, bs):
    t = pl.program_id(1)

    def one(s):
        x = x_ref[s]
        m = m_ref[s]
        ms = jnp.mean(x * x, axis=-1, keepdims=True)
        h = (x * lax.rsqrt(ms + 1e-6) * g_ref[...]) * (1.0 + m[1:2, :]) + m[0:1, :]
        hbuf[SUBLANES:SUBLANES + tm, :] = h

        @pl.when(t == 0)
        def _():
            hbuf[SUBLANES - 1 + off:SUBLANES + off, :] = s0_ref[s]

        dx = hbuf[SUBLANES - 1:SUBLANES - 1 + tm, :] - h
        for mi, slab in enumerate(MU_SLABS):
            xs_ref[slab, s] = (h + dx * mu_ref[mi:mi + 1, :]).astype(BF16)
        xs_ref[SLAB_H, s] = h.astype(BF16)
        last = h[tm - 1:tm, :]
        hbuf[SUBLANES - 1:SUBLANES, :] = last
        so_ref[s] = last

    _per_sequence(bs, one)


def _prep(x, mod, norm_g, mu, shift0, *, tm, off, bs):
    b, t, d = x.shape
    assert bs == 1 or t == tm
    return pl.pallas_call(
        functools.partial(_prep_kernel, tm=tm, off=off, bs=bs),
        grid=(b // bs, t // tm),
        in_specs=[pl.BlockSpec((bs, tm, d), lambda i, j: (i, j, 0)),
                  pl.BlockSpec((bs, N_MOD, d), lambda i, j: (i, 0, 0)),
                  _const_spec((1, d)),
                  _const_spec((N_MOD, d)),
                  pl.BlockSpec((bs, 1, d), lambda i, j: (i, 0, 0))],
        out_specs=[pl.BlockSpec((7, bs, tm, d), lambda i, j: (0, i, j, 0)),
                   pl.BlockSpec((bs, 1, d), lambda i, j: (i, 0, 0))],
        out_shape=[jax.ShapeDtypeStruct((7, b, t, d), BF16),
                   jax.ShapeDtypeStruct((b, 1, d), F32)],
        scratch_shapes=[pltpu.VMEM((SUBLANES + tm, d), F32)],
        compiler_params=_cparams(2),
        name="prep",
    )(x, mod, norm_g.reshape(1, d), mu, shift0.reshape(b, 1, d))


def _gate_blocks(gate, rows, tm, rows_per_gate):
    if gate.ndim == 2:
        return gate.reshape(rows // tm, tm, gate.shape[-1]), 1
    assert rows_per_gate % tm == 0, (rows_per_gate, tm)
    return gate, rows_per_gate // tm


def _mm_kernel(*refs, gated):
    if gated:
        a_ref, w_ref, res_ref, gate_ref, o_ref, wb = refs
    else:
        a_ref, w_ref, o_ref, wb = refs

    @pl.when(pl.program_id(1) == 0)
    def _():
        wb[...] = w_ref[...].astype(BF16)

    acc = jnp.dot(a_ref[...], wb[...], preferred_element_type=F32)
    if gated:
        acc = res_ref[...] + gate_ref[...] * acc
    o_ref[...] = acc


def _matmul(a, w, *, tm, tn, slab=0, col0=0, n=None, res=None, gate=None, rows_per_gate=None,
            name):
    _, rows, k = a.shape
    n = w.shape[1] if n is None else n
    gated = res is not None
    in_specs = [pl.BlockSpec((None, tm, k), lambda j, i: (slab, i, 0)),
                pl.BlockSpec((k, tn), lambda j, i: (0, j + col0))]
    args = [a, w]
    if gated:
        gate, tiles_per_gate = _gate_blocks(gate, rows, tm, rows_per_gate)
        gr = gate.shape[1]
        in_specs += [pl.BlockSpec((tm, tn), lambda j, i: (i, j)),
                     pl.BlockSpec((None, gr, tn), lambda j, i: (i // tiles_per_gate, 0, j))]
        args += [res, gate]
    return pl.pallas_call(
        functools.partial(_mm_kernel, gated=gated),
        grid=(pl.cdiv(n, tn), rows // tm),
        in_specs=in_specs,
        out_specs=pl.BlockSpec((tm, tn), lambda j, i: (i, j)),
        out_shape=jax.ShapeDtypeStruct((rows, n), F32),
        scratch_shapes=[pltpu.VMEM((k, tn), BF16)],
        compiler_params=_cparams(2),
        name=name,
    )(*args)


def _proj_hl_kernel(a_ref, w_ref, o_ref, *, gs, tq):
    acc = jnp.dot(_rows2d(a_ref), w_ref[...], preferred_element_type=F32)
    _store_head_lanes(acc, o_ref, gs, tq)


def _proj_head_lanes(xs, w, tl, *, tn, n):
    k = w.shape[0]
    d = n // 3
    slab = lambda j: j * tn // d
    if tl.flat:
        a_spec = pl.BlockSpec((None, tl.seqs * tl.tq, k), lambda j, ig, it: (slab(j), ig, 0))
    else:
        a_spec = pl.BlockSpec((None, tl.seqs, tl.tq, k), lambda j, ig, it: (slab(j), ig, it, 0))
    nq = tn // (LANES // GROUP)
    return pl.pallas_call(
        functools.partial(_proj_hl_kernel, gs=tl.gs, tq=tl.tq),
        grid=(n // tn,) + tl.grid,
        in_specs=[a_spec, pl.BlockSpec((k, tn), lambda j, ig, it: (0, j))],
        out_specs=pl.BlockSpec((tl.gs, nq, tl.tq, LANES), lambda j, ig, it: (ig, j, it, 0)),
        out_shape=jax.ShapeDtypeStruct((tl.groups, n // (LANES // GROUP), tl.t, LANES), F32),
        compiler_params=_cparams(3),
        name="rkv_proj",
    )(xs, w)


def _lora_kernel(xw_ref, xa_ref, xg_ref, w1, w2, a1, a2, g1, g2, w0, a0, dec_ref, a_ref, g_ref,
                 *, gs, tq):
    dot = functools.partial(jnp.dot, preferred_element_type=F32)
    hw = jnp.tanh(dot(_rows2d(xw_ref), w1[...])).astype(BF16)
    w_pre = w0[...] + dot(hw, w2[...])
    _store_head_lanes(jnp.exp(-jnp.exp(-_softplus(-w_pre) - 0.5)), dec_ref, gs, tq)
    ha = dot(_rows2d(xa_ref), a1[...]).astype(BF16)
    _store_head_lanes(_sigmoid(a0[...] + dot(ha, a2[...])), a_ref, gs, tq)
    hg = _sigmoid(dot(_rows2d(xg_ref), g1[...])).astype(BF16)
    g_ref[...] = dot(hg, g2[...]).reshape(g_ref.shape)


def _pad_lora(w_a, w_b):
    r = w_a.shape[1]
    return (jnp.pad(w_a, ((0, 0), (0, LORA_PAD - r))).astype(BF16),
            jnp.pad(w_b, ((0, LORA_PAD - r), (0, 0))).astype(BF16))


def _lora(xs, w0, w1, w2, a0, a1, a2, g1, g2, tl):
    d = w0.shape[0]
    w1p, w2p = _pad_lora(w1, _head_minor(w2))
    a1p, a2p = _pad_lora(a1, _head_minor(a2))
    g1b, g2b = g1.astype(BF16), _head_minor(g2).astype(BF16)
    if tl.flat:
        slab = lambda s: pl.BlockSpec((None, tl.seqs * tl.tq, d), lambda ig, it: (s, ig, 0))
        g_shape = (tl.groups * GROUP * tl.t, d)
    else:
        slab = lambda s: pl.BlockSpec((None, tl.seqs, tl.tq, d), lambda ig, it: (s, ig, it, 0))
        g_shape = (tl.groups * GROUP, tl.t, d)
    hl_shape = jax.ShapeDtypeStruct((tl.groups, HEAD_DIM, tl.t, LANES), F32)
    consts = [w1p, w2p, a1p, a2p, g1b, g2b, _head_minor(w0).reshape(1, d),
              _head_minor(a0).reshape(1, d)]
    return pl.pallas_call(
        functools.partial(_lora_kernel, gs=tl.gs, tq=tl.tq),
        grid=tl.grid,
        in_specs=[slab(SLAB_W), slab(SLAB_A), slab(SLAB_G)] + [_const_spec(c.shape) for c in consts],
        out_specs=[tl.head_lanes(), tl.head_lanes(), tl.plain(d)],
        out_shape=[hl_shape, hl_shape, jax.ShapeDtypeStruct(g_shape, F32)],
        compiler_params=_cparams(2),
        name="lora",
    )(xs, xs, xs, *consts)


def _wkv_kernel(r_ref, k_ref, v_ref, a_ref, d_ref, kk_p, ka_p, rk_p, lg_p, lb_p, s0_ref,
                y_ref, so_ref, kk_s, b_s, k2_s, v_t, y_t, *, nb, t0):
    n = HEAD_DIM
    tt = nb * SUBLANES
    bc = lambda p: p[...][:, None, None, :]

    @pl.when(pl.program_id(1) == 0)
    def _():
        so_ref[...] = s0_ref[...]

    k = k_ref[...]
    a = a_ref[...]
    kk = k * bc(kk_p)
    kk = kk / jnp.maximum(jnp.sqrt(jnp.sum(kk * kk, axis=0, keepdims=True)), 1e-12)
    kk_s[...] = kk
    b_s[...] = kk * a
    k2_s[...] = k * (1.0 + (a - 1.0) * bc(ka_p))
    v_t[...] = jnp.swapaxes(v_ref[...], 0, 1)
    if t0:
        y_t[0:t0] = jnp.zeros((t0, n, LANES), F32)

    sa0 = jnp.zeros((n, LANES), F32)
    for i in range(n):
        sa0 = sa0 + so_ref[i] * kk_s[i, 0, t0:t0 + 1, :]

    def block(tb, sa):
        tb_next = jnp.minimum(tb + 1, nb - 1)
        for j in range(t0, SUBLANES):
            t = tb * SUBLANES + j
            v = v_t[t]
            y = jnp.zeros((n, LANES), F32)
            sa_next = jnp.zeros((n, LANES), F32)
            for i in range(n):
                si = (so_ref[i] * d_ref[i, tb, j:j + 1, :] - sa * b_s[i, tb, j:j + 1, :]
                      + v * k2_s[i, tb, j:j + 1, :])
                so_ref[i] = si
                y = y + si * r_ref[i, tb, j:j + 1, :]
                if j + 1 < SUBLANES:
                    sa_next = sa_next + si * kk_s[i, tb, j + 1:j + 2, :]
                else:
                    sa_next = sa_next + si * kk_s[i, tb_next, 0:1, :]
            y_t[t] = y
            sa = sa_next
        return sa

    lax.fori_loop(0, nb, block, sa0)

    y = jnp.swapaxes(y_t[...], 0, 1)
    ym = jnp.mean(y, axis=0, keepdims=True)
    yc = y - ym
    yv = jnp.mean(yc * yc, axis=0, keepdims=True)
    bcast = lambda p: p[...][:, None, :]
    yn = yc * lax.rsqrt(yv + GN_EPS) * bcast(lg_p) + bcast(lb_p)
    rk = (r_ref[...] * k2_s[...] * bc(rk_p)).reshape(n, tt, LANES)
    y_ref[...] = yn + jnp.sum(rk, axis=0, keepdims=True) * v_ref[...]


def _head_lanes(p):
    hn = p.reshape(-1, HEAD_DIM).T
    return jnp.tile(hn, (1, GROUP))


def _wkv(rkv, a, dec, s0, k_k, k_a, r_k, lnx_g, lnx_b, *, tt, t0):
    g, _, t, lanes = a.shape
    n = HEAD_DIM
    nb = tt // SUBLANES
    assert t0 == 0 or nb == 1
    by8 = lambda z: z.reshape(z.shape[0], z.shape[1], t // SUBLANES, SUBLANES, lanes)
    params = [_head_lanes(p) for p in (k_k, k_a, r_k.reshape(-1), lnx_g, lnx_b)]
    rows = lambda blk: pl.BlockSpec((None, n, nb, SUBLANES, lanes), lambda i, j: (i, blk, j, 0, 0))
    tile = lambda blk: pl.BlockSpec((None, n, tt, lanes), lambda i, j: (i, blk, j, 0))
    st = pl.BlockSpec((None, n, n, lanes), lambda i, j: (i, 0, 0, 0))
    return pl.pallas_call(
        functools.partial(_wkv_kernel, nb=nb, t0=t0),
        grid=(g, t // tt),
        in_specs=[rows(0), rows(1), tile(2), rows(0), rows(0)] + [_const_spec((n, lanes))] * 5 + [st],
        out_specs=[tile(0), st],
        out_shape=[jax.ShapeDtypeStruct((g, n, t, lanes), F32),
                   jax.ShapeDtypeStruct((g, n, n, lanes), F32)],
        scratch_shapes=[pltpu.VMEM((n, nb, SUBLANES, lanes), F32)] * 3
        + [pltpu.VMEM((tt, n, lanes), F32)] * 2,
        compiler_params=_cparams(2),
        name="wkv7",
    )(by8(rkv), by8(rkv), rkv, by8(a), by8(dec), *params, s0)


CONV_PAD = 32
CONV_ROWS = 32


def _conv_kernel(u1_ref, u2_ref, bg1, bg2, dwk, dwb, lg, lb, st0_ref, y_ref, sto_ref, zbuf, cbuf,
                 *, tm, off, bs):
    t = pl.program_id(1)
    hist = CONV_W - 1
    lo = CONV_PAD - hist
    d = u1_ref.shape[-1]
    rc = min(CONV_ROWS, tm)

    def one(s):
        glu = (u1_ref[s] + bg1[...]) * _sigmoid(u2_ref[s] + bg2[...])
        zbuf[CONV_PAD:CONV_PAD + tm, :] = glu

        @pl.when(t == 0)
        def _():
            zbuf[lo + off:CONV_PAD + off, :] = st0_ref[s]

        for r0 in range(0, tm, rc):
            for c0 in range(0, d, LANES):
                acc = jnp.zeros((rc, LANES), F32)
                for j in range(CONV_W):
                    acc = acc + (zbuf[lo + r0 + j:lo + r0 + j + rc, c0:c0 + LANES]
                                 * dwk[j:j + 1, c0:c0 + LANES])
                cbuf[r0:r0 + rc, c0:c0 + LANES] = acc
        zc = cbuf[...] + dwb[...]
        m = jnp.mean(zc, axis=-1, keepdims=True)
        ctr = zc - m
        var = jnp.mean(ctr * ctr, axis=-1, keepdims=True)
        ln = ctr * lax.rsqrt(var + 1e-5) * lg[...] + lb[...]
        y_ref[s] = _silu(ln)
        carry = zbuf[lo + tm:CONV_PAD + tm, :]
        zbuf[lo:CONV_PAD, :] = carry
        sto_ref[s] = carry

    _per_sequence(bs, one)


def _conv(glu_in, b_glu, dw_k, dw_b, ln_g, ln_b, st0, *, tm, off, bs):
    b, t, _ = glu_in.shape
    d = dw_k.shape[1]
    hist = CONV_W - 1
    assert bs == 1 or t == tm
    vec = _const_spec((1, d))
    return pl.pallas_call(
        functools.partial(_conv_kernel, tm=tm, off=off, bs=bs),
        grid=(b // bs, t // tm),
        in_specs=[pl.BlockSpec((bs, tm, d), lambda i, j: (i, j, 0)),
                  pl.BlockSpec((bs, tm, d), lambda i, j: (i, j, 1)),
                  pl.BlockSpec((1, d), lambda i, j: (0, 0)),
                  pl.BlockSpec((1, d), lambda i, j: (0, 1)),
                  _const_spec((CONV_W, d)),
                  vec, vec, vec,
                  pl.BlockSpec((bs, hist, d), lambda i, j: (i, 0, 0))],
        out_specs=[pl.BlockSpec((bs, tm, d), lambda i, j: (i, j, 0)),
                   pl.BlockSpec((bs, hist, d), lambda i, j: (i, 0, 0))],
        out_shape=[jax.ShapeDtypeStruct((b, t, d), F32),
                   jax.ShapeDtypeStruct((b, hist, d), F32)],
        scratch_shapes=[pltpu.VMEM((CONV_PAD + tm, d), F32), pltpu.VMEM((tm, d), F32)],
        compiler_params=_cparams(2),
        name="conformer_conv",
    )(glu_in, glu_in, b_glu.reshape(1, 2 * d), b_glu.reshape(1, 2 * d), dw_k, dw_b.reshape(1, d),
      ln_g.reshape(1, d), ln_b.reshape(1, d), st0)


def _merge_kernel(yw_ref, g_ref, ga_ref, gb_ref, yb_ref, x_ref, woa_ref, wob_ref, m_ref, n2_ref,
                  x1_ref, h2_ref, ybuf, *, gs, tq):
    seqs = gs * GROUP
    d = ybuf.shape[-1]
    lane = lax.broadcasted_iota(jnp.int32, (tq, LANES), 1)
    for g in range(gs):
        for q in range(d // LANES):
            out = _swap_lane_groups([yw_ref[g, GROUP * q + j] for j in range(GROUP)], lane)
            for s in range(GROUP):
                ybuf[g * GROUP + s, :, q * LANES:(q + 1) * LANES] = out[s]
    seq3 = lambda ref: ref[...].reshape(seqs, tq, d)
    m = m_ref[...]
    dot = lambda z, w_ref: jnp.dot(z.reshape(seqs * tq, d).astype(BF16), w_ref[...],
                                   preferred_element_type=F32)
    branch_a = _sigmoid(seq3(ga_ref)) * (ybuf[...] * seq3(g_ref))
    branch_b = _sigmoid(seq3(gb_ref)) * seq3(yb_ref)
    mix = (dot(branch_a, woa_ref) + dot(branch_b, wob_ref)).reshape(seqs, tq, d)
    x1 = seq3(x_ref) + m[:, 2:3, :] * mix
    x1_ref[...] = x1.reshape(x1_ref.shape)
    ms = jnp.mean(x1 * x1, axis=-1, keepdims=True)
    h2 = (x1 * lax.rsqrt(ms + 1e-6) * n2_ref[...]) * (1.0 + m[:, 4:5, :]) + m[:, 3:4, :]
    h2_ref[...] = h2.reshape(seqs * tq, d).astype(BF16).reshape(h2_ref.shape)


def _merge(yw, g, ga, gb, yb, x, w_out, mod, norm_g, tl):
    d = x.shape[-1]
    plain = tl.plain(d)
    w_out_b = w_out.astype(BF16)
    w_out_a = _head_minor(w_out.T).T.astype(BF16)
    return pl.pallas_call(
        functools.partial(_merge_kernel, gs=tl.gs, tq=tl.tq),
        grid=tl.grid,
        in_specs=[tl.head_lanes(), plain, plain, plain, plain, plain, _const_spec((d, d)),
                  _const_spec((d, d)), tl.per_seq(N_MOD, d), _const_spec((1, d))],
        out_specs=[plain, plain],
        out_shape=[jax.ShapeDtypeStruct(x.shape, F32), jax.ShapeDtypeStruct(x.shape, BF16)],
        scratch_shapes=[pltpu.VMEM((tl.seqs, tl.tq, d), F32)],
        compiler_params=_cparams(2),
        name="merge_out_proj",
    )(yw, g, ga, gb, yb, x, w_out_a, w_out_b, mod, norm_g.reshape(1, d))


def _ffn_kernel(u_ref, k_ref, b_ref, st0_ref, act_ref, sto_ref, ubuf, *, tm, off, d_ff, bs):
    t = pl.program_id(1)
    hist = FFN_CONV_W - 1
    lo = SUBLANES - hist

    def one(s):
        ubuf[SUBLANES:SUBLANES + tm, :] = u_ref[s]

        @pl.when(t == 0)
        def _():
            ubuf[lo + off:SUBLANES + off, :] = st0_ref[s]

        z = b_ref[...]
        for j in range(FFN_CONV_W):
            z = z + ubuf[lo + j:lo + j + tm, :] * k_ref[j:j + 1, :]
        act_ref[s] = (_silu(z[:, :d_ff]) * z[:, d_ff:]).astype(BF16)
        carry = ubuf[lo + tm:SUBLANES + tm, :]
        ubuf[lo:SUBLANES, :] = carry
        sto_ref[s] = carry

    _per_sequence(bs, one)


def _ffn_act(u, ffn_dw_k, ffn_dw_b, st0, *, tm, off, bs):
    b, t, f2 = u.shape
    d_ff = f2 // 2
    hist = FFN_CONV_W - 1
    assert bs == 1 or t == tm
    return pl.pallas_call(
        functools.partial(_ffn_kernel, tm=tm, off=off, d_ff=d_ff, bs=bs),
        grid=(b // bs, t // tm),
        in_specs=[pl.BlockSpec((bs, tm, f2), lambda i, j: (i, j, 0)),
                  _const_spec((FFN_CONV_W, f2)),
                  _const_spec((1, f2)),
                  pl.BlockSpec((bs, hist, f2), lambda i, j: (i, 0, 0))],
        out_specs=[pl.BlockSpec((bs, tm, d_ff), lambda i, j: (i, j, 0)),
                   pl.BlockSpec((bs, hist, f2), lambda i, j: (i, 0, 0))],
        out_shape=[jax.ShapeDtypeStruct((b, t, d_ff), BF16),
                   jax.ShapeDtypeStruct((b, hist, f2), F32)],
        scratch_shapes=[pltpu.VMEM((SUBLANES + tm, f2), F32)],
        compiler_params=_cparams(2),
        name="convffn_act",
    )(u, ffn_dw_k, ffn_dw_b.reshape(1, f2), st0)


def _norm_kernel(x_ref, g_ref, o_ref):
    x = x_ref[...]
    ms = jnp.mean(x * x, axis=-1, keepdims=True)
    o_ref[...] = x * lax.rsqrt(ms + 1e-6) * g_ref[...]


def _final_norm(x, g, *, tm):
    rows, d = x.shape
    row = pl.BlockSpec((tm, d), lambda i: (i, 0))
    return pl.pallas_call(
        _norm_kernel,
        grid=(rows // tm,),
        in_specs=[row, _const_spec((1, d))],
        out_specs=row,
        out_shape=jax.ShapeDtypeStruct((rows, d), F32),
        compiler_params=_cparams(1),
        name="final_norm",
    )(x, g.reshape(1, d))


def _head_minor(w):
    lead = w.shape[:-1]
    return w.reshape(*lead, -1, HEAD_DIM).swapaxes(-1, -2).reshape(*lead, w.shape[-1])


def _layer(x, mod, shift0, wkv0, conv0, ffn0, p, *, tm_seq, bs, off, flat, tq_proj, gs_proj,
           tq_tok, gs_tok, tm_mm, tt):
    b, t, d = x.shape
    rows = b * t
    heads = d // HEAD_DIM
    groups = b // GROUP
    w_in = p["w_in"]
    tok = (lambda z: z.reshape(rows, z.shape[-1])) if flat else (lambda z: z)

    xs, shift1 = _prep(x, mod, p["norm1_g"], p["mu"], shift0, tm=tm_seq, off=off, bs=bs)
    xs2 = xs.reshape(7, rows, d)
    xs_tok = xs2 if flat else xs

    w_hm = jnp.concatenate([_head_minor(w_in[:, i * d:(i + 1) * d]) for i in (0, 1, 2, 5)],
                           axis=1).astype(BF16)
    rkv = _proj_head_lanes(xs_tok, w_hm, _Tiling(groups, t, gs_proj, tq_proj, flat), tn=1024,
                           n=3 * d)
    tn = 1024
    ga = _matmul(xs2, w_hm, tm=tm_mm, tn=tn, slab=SLAB_H, col0=3 * d // tn, n=d, name="gate_a_proj")
    glu_in = _matmul(xs2, w_in, tm=tm_mm, tn=tn, slab=SLAB_H, col0=3 * d // tn, n=2 * d,
                     name="glu_proj")
    gb = _matmul(xs2, w_in, tm=tm_mm, tn=tn, slab=SLAB_H, col0=6 * d // tn, n=d, name="gate_b_proj")

    tl = _Tiling(groups, t, gs_tok, tq_tok, flat)
    dec, a, g = _lora(xs_tok, p["w0"], p["w1"], p["w2"], p["a0"], p["a1"], p["a2"], p["g1"], p["g2"],
                      tl)

    s0 = wkv0.reshape(groups, GROUP, heads, HEAD_DIM, HEAD_DIM)
    s0 = s0.transpose(0, 4, 3, 1, 2).reshape(groups, HEAD_DIM, HEAD_DIM, LANES)
    yw, s1 = _wkv(rkv, a, dec, s0, p["k_k"], p["k_a"], p["r_k"], p["lnx_g"], p["lnx_b"], tt=tt,
                  t0=off)
    wkv1 = s1.reshape(groups, HEAD_DIM, HEAD_DIM, GROUP, heads)
    wkv1 = wkv1.transpose(0, 3, 4, 2, 1).reshape(b, heads, HEAD_DIM, HEAD_DIM)

    yb, conv1 = _conv(glu_in.reshape(b, t, 2 * d), p["b_glu"], p["dw_k"], p["dw_b"],
                      p["ln_conv_g"], p["ln_conv_b"], conv0, tm=tm_seq, off=off, bs=bs)

    ga_t, gb_t = (z if flat else z.reshape(b, t, d) for z in (ga, gb))
    x1, h2 = _merge(yw, g, ga_t, gb_t, tok(yb), tok(x), p["w_out"], mod, p["norm2_g"], tl)

    u = _matmul(h2.reshape(1, rows, d), p["w_up"], tm=tm_mm, tn=1024, name="up_proj")
    f2 = u.shape[1]
    act, ffn1 = _ffn_act(u.reshape(b, t, f2), p["ffn_dw_k"], p["ffn_dw_b"], ffn0,
                         tm=min(tm_seq, 128), off=off, bs=bs)
    if off:
        gate2, rows_per_gate = jnp.repeat(mod[:, 5], t, axis=0), None
    else:
        gate2, rows_per_gate = mod[:, 5][:, None, :], t
    x2 = _matmul(act.reshape(1, rows, f2 // 2), p["w_down"], tm=min(tm_mm, 512), tn=512,
                 res=x1.reshape(rows, d), gate=gate2, rows_per_gate=rows_per_gate, name="down_proj")
    y = _final_norm(x2, p["normf_g"], tm=min(256, rows))
    return y.reshape(b, t, d), shift1.reshape(b, d), wkv1, conv1, ffn1


def kernel(x_prompt, x_sample, state_shift, state_wkv, state_conv, state_ffn, c_prompt, c_sample,
           norm1_g, norm2_g, normf_g, w_ada, b_ada, mu, w_in, b_glu, w0, w1, w2, a0, a1, a2, g1, g2,
           k_k, k_a, r_k, lnx_g, lnx_b, dw_k, dw_b, ln_conv_g, ln_conv_b, w_out, w_up, ffn_dw_k,
           ffn_dw_b, w_down):
    p = dict(norm1_g=norm1_g, norm2_g=norm2_g, normf_g=normf_g, mu=mu, w_in=w_in, b_glu=b_glu,
             w0=w0, w1=w1, w2=w2, a0=a0, a1=a1, a2=a2, g1=g1, g2=g2, k_k=k_k, k_a=k_a, r_k=r_k,
             lnx_g=lnx_g, lnx_b=lnx_b, dw_k=dw_k, dw_b=dw_b, ln_conv_g=ln_conv_g,
             ln_conv_b=ln_conv_b, w_out=w_out, w_up=w_up, ffn_dw_k=ffn_dw_k, ffn_dw_b=ffn_dw_b,
             w_down=w_down)
    bp, tp, d = x_prompt.shape
    bs, ts, _ = x_sample.shape
    f2 = w_up.shape[1]

    c_all = jnp.concatenate([c_prompt, c_sample], axis=0)
    c_rows = -(-c_all.shape[0] // SUBLANES) * SUBLANES
    c_all = jnp.pad(c_all, ((0, c_rows - c_all.shape[0]), (0, 0)))
    mod = _modulation(c_all, w_ada, b_ada).reshape(c_rows, N_MOD, d)
    mod_p, mod_s = mod[:bp], mod[bp:bp + bs]

    y_p, shift_p, wkv_p, conv_p, ffn_p = _layer(
        x_prompt, mod_p, jnp.zeros((bp, d), F32),
        jnp.zeros((bp, d // HEAD_DIM, HEAD_DIM, HEAD_DIM), F32),
        jnp.zeros((bp, CONV_W - 1, d), F32), jnp.zeros((bp, FFN_CONV_W - 1, f2), F32), p,
        tm_seq=256, bs=1, off=0, flat=False, tq_proj=256, gs_proj=1, tq_tok=32, gs_tok=1,
        tm_mm=1024, tt=32)

    off = SAMPLE_TILE - ts
    x_s = jnp.pad(x_sample, ((0, 0), (off, 0), (0, 0)))
    y_s, shift_s, wkv_s, conv_s, ffn_s = _layer(
        x_s, mod_s, state_shift, state_wkv, state_conv, state_ffn, p,
        tm_seq=SAMPLE_TILE, bs=8, off=off, flat=True, tq_proj=SAMPLE_TILE, gs_proj=bs // GROUP,
        tq_tok=SAMPLE_TILE, gs_tok=4, tm_mm=bs * SAMPLE_TILE, tt=SAMPLE_TILE)
    return (y_p, y_s[:, off:], shift_p, wkv_p, conv_p, ffn_p, shift_s, wkv_s, conv_s, ffn_s)
```

```python
import functools
from typing import NamedTuple

import jax
import jax.numpy as jnp
from jax import lax
from jax.experimental import pallas as pl
from jax.experimental.pallas import tpu as pltpu

F32 = jnp.float32
BF16 = jnp.bfloat16

HEAD_DIM = 64
CONV_W = 31
FFN_CONV_W = 3
N_MOD = 6
GN_EPS = HEAD_DIM * 1e-5
SUBLANES = 8
LANES = 128
LORA_PAD = 128
SAMPLE_TILE = 8
GROUP = 4
V7X_VMEM_LIMIT = 56 * 1024 * 1024

SLAB_R, SLAB_K, SLAB_V, SLAB_H, SLAB_W, SLAB_A, SLAB_G = range(7)
MU_SLABS = (SLAB_R, SLAB_W, SLAB_K, SLAB_V, SLAB_A, SLAB_G)


class _Tiling(NamedTuple):
    groups: int
    t: int
    gs: int
    tq: int
    flat: bool

    @property
    def grid(self):
        return (self.groups // self.gs, self.t // self.tq)

    @property
    def seqs(self):
        return self.gs * GROUP

    def plain(self, cols, col=0):
        if self.flat:
            return pl.BlockSpec((self.seqs * self.tq, cols), lambda ig, it: (ig, col))
        return pl.BlockSpec((self.seqs, self.tq, cols), lambda ig, it: (ig, it, col))

    def head_lanes(self, blk=0):
        return pl.BlockSpec((self.gs, HEAD_DIM, self.tq, LANES), lambda ig, it: (ig, blk, it, 0))

    def per_seq(self, rows, cols):
        return pl.BlockSpec((self.seqs, rows, cols), lambda ig, it: (ig, 0, 0))


def _cparams(n_axes):
    return pltpu.CompilerParams(dimension_semantics=("arbitrary",) * n_axes,
                                vmem_limit_bytes=V7X_VMEM_LIMIT)


def _const_spec(shape):
    return pl.BlockSpec(shape, lambda *_: (0,) * len(shape), pipeline_mode=pl.Buffered(1))


def _sigmoid(x):
    return 1.0 / (1.0 + jnp.exp(-x))


def _silu(x):
    return x * _sigmoid(x)


def _softplus(x):
    return jnp.maximum(x, 0.0) + jnp.log(1.0 + jnp.exp(-jnp.abs(x)))


def _swap_lane_groups(m, lane):
    half = lane < 2 * (LANES // GROUP)
    odd = (lane // (LANES // GROUP)) % 2 == 1
    n = [None] * 4
    for s in range(2):
        n[s] = jnp.where(half, m[s], pltpu.roll(m[s + 2], LANES // 2, axis=1))
        n[s + 2] = jnp.where(half, pltpu.roll(m[s], LANES // 2, axis=1), m[s + 2])
    out = [None] * 4
    for p in (0, 2):
        a, b = n[p], n[p + 1]
        out[p] = jnp.where(odd, pltpu.roll(b, LANES // GROUP, axis=1), a)
        out[p + 1] = jnp.where(odd, b, pltpu.roll(a, LANES - LANES // GROUP, axis=1))
    return out


def _store_head_lanes(val, out_ref, gs, tq):
    lane = lax.broadcasted_iota(jnp.int32, (tq, LANES), 1)
    for g in range(gs):
        for q in range(val.shape[1] // LANES):
            src = [val[(g * GROUP + s) * tq:(g * GROUP + s + 1) * tq, q * LANES:(q + 1) * LANES]
                   for s in range(GROUP)]
            out = _swap_lane_groups(src, lane)
            for j in range(GROUP):
                out_ref[g, GROUP * q + j] = out[j]


def _rows2d(ref):
    v = ref[...]
    return v.reshape(-1, v.shape[-1])


def _mod_kernel(c_ref, w_ref, b_ref, o_ref):
    s = _silu(c_ref[...]).astype(BF16)
    o_ref[...] = jnp.dot(s, w_ref[...].astype(BF16), preferred_element_type=F32) + b_ref[...]


def _modulation(c, w_ada, b_ada):
    rows, d = c.shape
    n = w_ada.shape[1]
    tn = 1024
    return pl.pallas_call(
        _mod_kernel,
        grid=(n // tn,),
        in_specs=[pl.BlockSpec((rows, d), lambda j: (0, 0)),
                  pl.BlockSpec((d, tn), lambda j: (0, j)),
                  pl.BlockSpec((1, tn), lambda j: (0, j))],
        out_specs=pl.BlockSpec((rows, tn), lambda j: (0, j)),
        out_shape=jax.ShapeDtypeStruct((rows, n), F32),
        compiler_params=_cparams(1),
        name="modulation",
    )(c, w_ada, b_ada.reshape(1, n))


def _per_sequence(bs, one):
    if bs == 1:
        one(0)
    else:
        lax.fori_loop(0, bs, lambda s, c: (one(s), c)[1], 0)


def _prep_kernel(x_ref, m_ref, g_ref, mu_ref, s0_ref, xs_ref, so_ref, hbuf, *, tm, off, bs):
    t = pl.program_id(1)

    def one(s):
        x = x_ref[s]
        m = m_ref[s]
        ms = jnp.mean(x * x, axis=-1, keepdims=True)
        h = (x * lax.rsqrt(ms + 1e-6) * g_ref[...]) * (1.0 + m[1:2, :]) + m[0:1, :]
        hbuf[SUBLANES:SUBLANES + tm, :] = h

        @pl.when(t == 0)
        def _():
            hbuf[SUBLANES - 1 + off:SUBLANES + off, :] = s0_ref[s]

        dx = hbuf[SUBLANES - 1:SUBLANES - 1 + tm, :] - h
        for mi, slab in enumerate(MU_SLABS):
            xs_ref[slab, s] = (h + dx * mu_ref[mi:mi + 1, :]).astype(BF16)
        xs_ref[SLAB_H, s] = h.astype(BF16)
        last = h[tm - 1:tm, :]
        hbuf[SUBLANES - 1:SUBLANES, :] = last
        so_ref[s] = last

    _per_sequence(bs, one)


def _prep(x, mod, norm_g, mu, shift0, *, tm, off, bs):
    b, t, d = x.shape
    assert bs == 1 or t == tm
    return pl.pallas_call(
        functools.partial(_prep_kernel, tm=tm, off=off, bs=bs),
        grid=(b // bs, t // tm),
        in_specs=[pl.BlockSpec((bs, tm, d), lambda i, j: (i, j, 0)),
                  pl.BlockSpec((bs, N_MOD, d), lambda i, j: (i, 0, 0)),
                  _const_spec((1, d)),
                  _const_spec((N_MOD, d)),
                  pl.BlockSpec((bs, 1, d), lambda i, j: (i, 0, 0))],
        out_specs=[pl.BlockSpec((7, bs, tm, d), lambda i, j: (0, i, j, 0)),
                   pl.BlockSpec((bs, 1, d), lambda i, j: (i, 0, 0))],
        out_shape=[jax.ShapeDtypeStruct((7, b, t, d), BF16),
                   jax.ShapeDtypeStruct((b, 1, d), F32)],
        scratch_shapes=[pltpu.VMEM((SUBLANES + tm, d), F32)],
        compiler_params=_cparams(2),
        name="prep",
    )(x, mod, norm_g.reshape(1, d), mu, shift0.reshape(b, 1, d))


def _gate_blocks(gate, rows, tm, rows_per_gate):
    if gate.ndim == 2:
        return gate.reshape(rows // tm, tm, gate.shape[-1]), 1
    assert rows_per_gate % tm == 0, (rows_per_gate, tm)
    return gate, rows_per_gate // tm


def _mm_kernel(*refs, gated):
    if gated:
        a_ref, w_ref, res_ref, gate_ref, o_ref, wb = refs
    else:
        a_ref, w_ref, o_ref, wb = refs

    @pl.when(pl.program_id(1) == 0)
    def _():
        wb[...] = w_ref[...].astype(BF16)

    acc = jnp.dot(a_ref[...], wb[...], preferred_element_type=F32)
    if gated:
        acc = res_ref[...] + gate_ref[...] * acc
    o_ref[...] = acc


def _matmul(a, w, *, tm, tn, slab=0, col0=0, n=None, res=None, gate=None, rows_per_gate=None,
            name):
    _, rows, k = a.shape
    n = w.shape[1] if n is None else n
    gated = res is not None
    in_specs = [pl.BlockSpec((None, tm, k), lambda j, i: (slab, i, 0)),
                pl.BlockSpec((k, tn), lambda j, i: (0, j + col0))]
    args = [a, w]
    if gated:
        gate, tiles_per_gate = _gate_blocks(gate, rows, tm, rows_per_gate)
        gr = gate.shape[1]
        in_specs += [pl.BlockSpec((tm, tn), lambda j, i: (i, j)),
                     pl.BlockSpec((None, gr, tn), lambda j, i: (i // tiles_per_gate, 0, j))]
        args += [res, gate]
    return pl.pallas_call(
        functools.partial(_mm_kernel, gated=gated),
        grid=(pl.cdiv(n, tn), rows // tm),
        in_specs=in_specs,
        out_specs=pl.BlockSpec((tm, tn), lambda j, i: (i, j)),
        out_shape=jax.ShapeDtypeStruct((rows, n), F32),
        scratch_shapes=[pltpu.VMEM((k, tn), BF16)],
        compiler_params=_cparams(2),
        name=name,
    )(*args)


def _proj_hl_kernel(a_ref, w_ref, o_ref, *, gs, tq):
    acc = jnp.dot(_rows2d(a_ref), w_ref[...], preferred_element_type=F32)
    _store_head_lanes(acc, o_ref, gs, tq)


def _proj_head_lanes(xs, w, tl, *, tn, n):
    k = w.shape[0]
    d = n // 3
    slab = lambda j: j * tn // d
    if tl.flat:
        a_spec = pl.BlockSpec((None, tl.seqs * tl.tq, k), lambda j, ig, it: (slab(j), ig, 0))
    else:
        a_spec = pl.BlockSpec((None, tl.seqs, tl.tq, k), lambda j, ig, it: (slab(j), ig, it, 0))
    nq = tn // (LANES // GROUP)
    return pl.pallas_call(
        functools.partial(_proj_hl_kernel, gs=tl.gs, tq=tl.tq),
        grid=(n // tn,) + tl.grid,
        in_specs=[a_spec, pl.BlockSpec((k, tn), lambda j, ig, it: (0, j))],
        out_specs=pl.BlockSpec((tl.gs, nq, tl.tq, LANES), lambda j, ig, it: (ig, j, it, 0)),
        out_shape=jax.ShapeDtypeStruct((tl.groups, n // (LANES // GROUP), tl.t, LANES), F32),
        compiler_params=_cparams(3),
        name="rkv_proj",
    )(xs, w)


def _lora_kernel(xw_ref, xa_ref, xg_ref, w1, w2, a1, a2, g1, g2, w0, a0, dec_ref, a_ref, g_ref,
                 *, gs, tq):
    dot = functools.partial(jnp.dot, preferred_element_type=F32)
    hw = jnp.tanh(dot(_rows2d(xw_ref), w1[...])).astype(BF16)
    w_pre = w0[...] + dot(hw, w2[...])
    _store_head_lanes(jnp.exp(-jnp.exp(-_softplus(-w_pre) - 0.5)), dec_ref, gs, tq)
    ha = dot(_rows2d(xa_ref), a1[...]).astype(BF16)
    _store_head_lanes(_sigmoid(a0[...] + dot(ha, a2[...])), a_ref, gs, tq)
    hg = _sigmoid(dot(_rows2d(xg_ref), g1[...])).astype(BF16)
    g_ref[...] = dot(hg, g2[...]).reshape(g_ref.shape)


def _pad_lora(w_a, w_b):
    r = w_a.shape[1]
    return (jnp.pad(w_a, ((0, 0), (0, LORA_PAD - r))).astype(BF16),
            jnp.pad(w_b, ((0, LORA_PAD - r), (0, 0))).astype(BF16))


def _lora(xs, w0, w1, w2, a0, a1, a2, g1, g2, tl):
    d = w0.shape[0]
    w1p, w2p = _pad_lora(w1, _head_minor(w2))
    a1p, a2p = _pad_lora(a1, _head_minor(a2))
    g1b, g2b = g1.astype(BF16), _head_minor(g2).astype(BF16)
    if tl.flat:
        slab = lambda s: pl.BlockSpec((None, tl.seqs * tl.tq, d), lambda ig, it: (s, ig, 0))
        g_shape = (tl.groups * GROUP * tl.t, d)
    else:
        slab = lambda s: pl.BlockSpec((None, tl.seqs, tl.tq, d), lambda ig, it: (s, ig, it, 0))
        g_shape = (tl.groups * GROUP, tl.t, d)
    hl_shape = jax.ShapeDtypeStruct((tl.groups, HEAD_DIM, tl.t, LANES), F32)
    consts = [w1p, w2p, a1p, a2p, g1b, g2b, _head_minor(w0).reshape(1, d),
              _head_minor(a0).reshape(1, d)]
    return pl.pallas_call(
        functools.partial(_lora_kernel, gs=tl.gs, tq=tl.tq),
        grid=tl.grid,
        in_specs=[slab(SLAB_W), slab(SLAB_A), slab(SLAB_G)] + [_const_spec(c.shape) for c in consts],
        out_specs=[tl.head_lanes(), tl.head_lanes(), tl.plain(d)],
        out_shape=[hl_shape, hl_shape, jax.ShapeDtypeStruct(g_shape, F32)],
        compiler_params=_cparams(2),
        name="lora",
    )(xs, xs, xs, *consts)


def _wkv_kernel(r_ref, k_ref, v_ref, a_ref, d_ref, kk_p, ka_p, rk_p, lg_p, lb_p, s0_ref,
                y_ref, so_ref, kk_s, b_s, k2_s, v_t, y_t, *, nb, t0):
    n = HEAD_DIM
    tt = nb * SUBLANES
    bc = lambda p: p[...][:, None, None, :]

    @pl.when(pl.program_id(1) == 0)
    def _():
        so_ref[...] = s0_ref[...]

    k = k_ref[...]
    a = a_ref[...]
    kk = k * bc(kk_p)
    kk = kk / jnp.maximum(jnp.sqrt(jnp.sum(kk * kk, axis=0, keepdims=True)), 1e-12)
    kk_s[...] = kk
    b_s[...] = kk * a
    k2_s[...] = k * (1.0 + (a - 1.0) * bc(ka_p))
    v_t[...] = jnp.swapaxes(v_ref[...], 0, 1)
    if t0:
        y_t[0:t0] = jnp.zeros((t0, n, LANES), F32)

    sa0 = jnp.zeros((n, LANES), F32)
    for i in range(n):
        sa0 = sa0 + so_ref[i] * kk_s[i, 0, t0:t0 + 1, :]

    def block(tb, sa):
        tb_next = jnp.minimum(tb + 1, nb - 1)
        for j in range(t0, SUBLANES):
            t = tb * SUBLANES + j
            v = v_t[t]
            y = jnp.zeros((n, LANES), F32)
            sa_next = jnp.zeros((n, LANES), F32)
            for i in range(n):
                si = (so_ref[i] * d_ref[i, tb, j:j + 1, :] - sa * b_s[i, tb, j:j + 1, :]
                      + v * k2_s[i, tb, j:j + 1, :])
                so_ref[i] = si
                y = y + si * r_ref[i, tb, j:j + 1, :]
                if j + 1 < SUBLANES:
                    sa_next = sa_next + si * kk_s[i, tb, j + 1:j + 2, :]
                else:
                    sa_next = sa_next + si * kk_s[i, tb_next, 0:1, :]
            y_t[t] = y
            sa = sa_next
        return sa

    lax.fori_loop(0, nb, block, sa0)

    y = jnp.swapaxes(y_t[...], 0, 1)
    ym = jnp.mean(y, axis=0, keepdims=True)
    yc = y - ym
    yv = jnp.mean(yc * yc, axis=0, keepdims=True)
    bcast = lambda p: p[...][:, None, :]
    yn = yc * lax.rsqrt(yv + GN_EPS) * bcast(lg_p) + bcast(lb_p)
    rk = (r_ref[...] * k2_s[...] * bc(rk_p)).reshape(n, tt, LANES)
    y_ref[...] = yn + jnp.sum(rk, axis=0, keepdims=True) * v_ref[...]


def _head_lanes(p):
    hn = p.reshape(-1, HEAD_DIM).T
    return jnp.tile(hn, (1, GROUP))


def _wkv(rkv, a, dec, s0, k_k, k_a, r_k, lnx_g, lnx_b, *, tt, t0):
    g, _, t, lanes = a.shape
    n = HEAD_DIM
    nb = tt // SUBLANES
    assert t0 == 0 or nb == 1
    by8 = lambda z: z.reshape(z.shape[0], z.shape[1], t // SUBLANES, SUBLANES, lanes)
    params = [_head_lanes(p) for p in (k_k, k_a, r_k.reshape(-1), lnx_g, lnx_b)]
    rows = lambda blk: pl.BlockSpec((None, n, nb, SUBLANES, lanes), lambda i, j: (i, blk, j, 0, 0))
    tile = lambda blk: pl.BlockSpec((None, n, tt, lanes), lambda i, j: (i, blk, j, 0))
    st = pl.BlockSpec((None, n, n, lanes), lambda i, j: (i, 0, 0, 0))
    return pl.pallas_call(
        functools.partial(_wkv_kernel, nb=nb, t0=t0),
        grid=(g, t // tt),
        in_specs=[rows(0), rows(1), tile(2), rows(0), rows(0)] + [_const_spec((n, lanes))] * 5 + [st],
        out_specs=[tile(0), st],
        out_shape=[jax.ShapeDtypeStruct((g, n, t, lanes), F32),
                   jax.ShapeDtypeStruct((g, n, n, lanes), F32)],
        scratch_shapes=[pltpu.VMEM((n, nb, SUBLANES, lanes), F32)] * 3
        + [pltpu.VMEM((tt, n, lanes), F32)] * 2,
        compiler_params=_cparams(2),
        name="wkv7",
    )(by8(rkv), by8(rkv), rkv, by8(a), by8(dec), *params, s0)


CONV_PAD = 32
CONV_ROWS = 32


def _conv_kernel(u1_ref, u2_ref, bg1, bg2, dwk, dwb, lg, lb, st0_ref, y_ref, sto_ref, zbuf, cbuf,
                 zs, *, tm, off, bs):
    t = pl.program_id(1)
    hist = CONV_W - 1
    lo = CONV_PAD - hist
    d = u1_ref.shape[-1]
    rc = min(CONV_ROWS, tm)

    def one(s):
        glu = (u1_ref[s] + bg1[...]) * _sigmoid(u2_ref[s] + bg2[...])
        zbuf[CONV_PAD:CONV_PAD + tm, :] = glu

        @pl.when(t == 0)
        def _():
            zbuf[lo + off:CONV_PAD + off, :] = st0_ref[s]

        for b in range(SUBLANES):
            span = tm + SUBLANES * ((hist - b) // SUBLANES)
            zs[b, 0:span, :] = zbuf[lo + b:lo + b + span, :]
        for r0 in range(0, tm, rc):
            for c0 in range(0, d, LANES):
                acc = jnp.zeros((rc, LANES), F32)
                for j in range(CONV_W):
                    a8 = SUBLANES * (j // SUBLANES)
                    acc = acc + (zs[j % SUBLANES, a8 + r0:a8 + r0 + rc, c0:c0 + LANES]
                                 * dwk[j:j + 1, c0:c0 + LANES])
                cbuf[r0:r0 + rc, c0:c0 + LANES] = acc
        zc = cbuf[...] + dwb[...]
        m = jnp.mean(zc, axis=-1, keepdims=True)
        ctr = zc - m
        var = jnp.mean(ctr * ctr, axis=-1, keepdims=True)
        ln = ctr * lax.rsqrt(var + 1e-5) * lg[...] + lb[...]
        y_ref[s] = _silu(ln)
        carry = zbuf[lo + tm:CONV_PAD + tm, :]
        zbuf[lo:CONV_PAD, :] = carry
        sto_ref[s] = carry

    _per_sequence(bs, one)


def _conv(glu_in, b_glu, dw_k, dw_b, ln_g, ln_b, st0, *, tm, off, bs):
    b, t, _ = glu_in.shape
    d = dw_k.shape[1]
    hist = CONV_W - 1
    assert bs == 1 or t == tm
    vec = _const_spec((1, d))
    return pl.pallas_call(
        functools.partial(_conv_kernel, tm=tm, off=off, bs=bs),
        grid=(b // bs, t // tm),
        in_specs=[pl.BlockSpec((bs, tm, d), lambda i, j: (i, j, 0)),
                  pl.BlockSpec((bs, tm, d), lambda i, j: (i, j, 1)),
                  pl.BlockSpec((1, d), lambda i, j: (0, 0)),
                  pl.BlockSpec((1, d), lambda i, j: (0, 1)),
                  _const_spec((CONV_W, d)),
                  vec, vec, vec,
                  pl.BlockSpec((bs, hist, d), lambda i, j: (i, 0, 0))],
        out_specs=[pl.BlockSpec((bs, tm, d), lambda i, j: (i, j, 0)),
                   pl.BlockSpec((bs, hist, d), lambda i, j: (i, 0, 0))],
        out_shape=[jax.ShapeDtypeStruct((b, t, d), F32),
                   jax.ShapeDtypeStruct((b, hist, d), F32)],
        scratch_shapes=[pltpu.VMEM((CONV_PAD + tm, d), F32), pltpu.VMEM((tm, d), F32),
                        pltpu.VMEM((SUBLANES, tm + CONV_PAD - SUBLANES, d), F32)],
        compiler_params=_cparams(2),
        name="conformer_conv",
    )(glu_in, glu_in, b_glu.reshape(1, 2 * d), b_glu.reshape(1, 2 * d), dw_k, dw_b.reshape(1, d),
      ln_g.reshape(1, d), ln_b.reshape(1, d), st0)


def _merge_kernel(yw_ref, g_ref, ga_ref, gb_ref, yb_ref, x_ref, woa_ref, wob_ref, m_ref, n2_ref,
                  x1_ref, h2_ref, ybuf, *, gs, tq):
    seqs = gs * GROUP
    d = ybuf.shape[-1]
    lane = lax.broadcasted_iota(jnp.int32, (tq, LANES), 1)
    for g in range(gs):
        for q in range(d // LANES):
            out = _swap_lane_groups([yw_ref[g, GROUP * q + j] for j in range(GROUP)], lane)
            for s in range(GROUP):
                ybuf[g * GROUP + s, :, q * LANES:(q + 1) * LANES] = out[s]
    seq3 = lambda ref: ref[...].reshape(seqs, tq, d)
    m = m_ref[...]
    dot = lambda z, w_ref: jnp.dot(z.reshape(seqs * tq, d).astype(BF16), w_ref[...],
                                   preferred_element_type=F32)
    branch_a = _sigmoid(seq3(ga_ref)) * (ybuf[...] * seq3(g_ref))
    branch_b = _sigmoid(seq3(gb_ref)) * seq3(yb_ref)
    mix = (dot(branch_a, woa_ref) + dot(branch_b, wob_ref)).reshape(seqs, tq, d)
    x1 = seq3(x_ref) + m[:, 2:3, :] * mix
    x1_ref[...] = x1.reshape(x1_ref.shape)
    ms = jnp.mean(x1 * x1, axis=-1, keepdims=True)
    h2 = (x1 * lax.rsqrt(ms + 1e-6) * n2_ref[...]) * (1.0 + m[:, 4:5, :]) + m[:, 3:4, :]
    h2_ref[...] = h2.reshape(seqs * tq, d).astype(BF16).reshape(h2_ref.shape)


def _merge(yw, g, ga, gb, yb, x, w_out, mod, norm_g, tl):
    d = x.shape[-1]
    plain = tl.plain(d)
    w_out_b = w_out.astype(BF16)
    w_out_a = _head_minor(w_out.T).T.astype(BF16)
    return pl.pallas_call(
        functools.partial(_merge_kernel, gs=tl.gs, tq=tl.tq),
        grid=tl.grid,
        in_specs=[tl.head_lanes(), plain, plain, plain, plain, plain, _const_spec((d, d)),
                  _const_spec((d, d)), tl.per_seq(N_MOD, d), _const_spec((1, d))],
        out_specs=[plain, plain],
        out_shape=[jax.ShapeDtypeStruct(x.shape, F32), jax.ShapeDtypeStruct(x.shape, BF16)],
        scratch_shapes=[pltpu.VMEM((tl.seqs, tl.tq, d), F32)],
        compiler_params=_cparams(2),
        name="merge_out_proj",
    )(yw, g, ga, gb, yb, x, w_out_a, w_out_b, mod, norm_g.reshape(1, d))


FFN_TN = 512


def _pad_halves(w, width):
    f = w.shape[-1] // 2
    pad = [(0, 0)] * (w.ndim - 1) + [(0, width - f)]
    return jnp.concatenate([jnp.pad(w[..., :f], pad), jnp.pad(w[..., f:], pad)], axis=-1)


def _up_act_kernel(h_ref, wg_ref, wv_ref, kg_ref, kv_ref, bg_ref, bv_ref, sg_ref, sv_ref,
                   act_ref, stg_ref, stv_ref, ubuf, *, nseq, tq, off, tiles_per_seq):
    hist = FFN_CONV_W - 1
    lo = SUBLANES - hist
    first = pl.program_id(1) % tiles_per_seq == 0
    h = h_ref[...]
    tn = wg_ref.shape[-1]
    z = []
    halves = ((wg_ref, kg_ref, bg_ref, sg_ref, stg_ref), (wv_ref, kv_ref, bv_ref, sv_ref, stv_ref))
    for half, (w_ref, k_ref, b_ref, s_ref, st_ref) in enumerate(halves):
        u = jnp.dot(h, w_ref[...], preferred_element_type=F32)
        ubuf[half, :, SUBLANES:SUBLANES + tq, :] = u.reshape(nseq, tq, tn)

        @pl.when(first)
        def _():
            ubuf[half, :, lo + off:SUBLANES + off, :] = s_ref[...]

        zh = b_ref[...]
        for j in range(FFN_CONV_W):
            zh = zh + ubuf[half, :, lo + j:lo + j + tq, :] * k_ref[j:j + 1, :]
        z.append(zh)
        carry = ubuf[half, :, lo + tq:SUBLANES + tq, :]
        ubuf[half, :, lo:SUBLANES, :] = carry
        st_ref[...] = carry
    act_ref[...] = (_silu(z[0]) * z[1]).reshape(nseq * tq, tn).astype(BF16)


def _up_act(h2, w_up, ffn_dw_k, ffn_dw_b, st0, *, nseq, tq, off, t):
    rows, k = h2.shape
    b = st0.shape[0]
    f = w_up.shape[1] // 2
    nj = f // FFN_TN
    hist = FFN_CONV_W - 1
    tiles_per_seq = t // tq
    assert nseq == 1 or tiles_per_seq == 1
    seq = lambda i: i // tiles_per_seq
    col = lambda shape, h: pl.BlockSpec(shape, lambda j, i: (0, j + h * nj))
    state = lambda h: pl.BlockSpec((nseq, hist, FFN_TN), lambda j, i: (seq(i), 0, j + h * nj))
    return pl.pallas_call(
        functools.partial(_up_act_kernel, nseq=nseq, tq=tq, off=off, tiles_per_seq=tiles_per_seq),
        grid=(nj, rows // (nseq * tq)),
        in_specs=[pl.BlockSpec((nseq * tq, k), lambda j, i: (i, 0)),
                  col((k, FFN_TN), 0), col((k, FFN_TN), 1),
                  col((FFN_CONV_W, FFN_TN), 0), col((FFN_CONV_W, FFN_TN), 1),
                  col((1, FFN_TN), 0), col((1, FFN_TN), 1),
                  state(0), state(1)],
        out_specs=[pl.BlockSpec((nseq * tq, FFN_TN), lambda j, i: (i, j)), state(0), state(0)],
        out_shape=[jax.ShapeDtypeStruct((rows, f), BF16),
                   jax.ShapeDtypeStruct((b, hist, f), F32),
                   jax.ShapeDtypeStruct((b, hist, f), F32)],
        scratch_shapes=[pltpu.VMEM((2, nseq, SUBLANES + tq, FFN_TN), F32)],
        compiler_params=_cparams(2),
        name="up_conv_act",
    )(h2, w_up, w_up, ffn_dw_k, ffn_dw_k, ffn_dw_b, ffn_dw_b, st0, st0)


def _norm_kernel(x_ref, g_ref, o_ref):
    x = x_ref[...]
    ms = jnp.mean(x * x, axis=-1, keepdims=True)
    o_ref[...] = x * lax.rsqrt(ms + 1e-6) * g_ref[...]


def _final_norm(x, g, *, tm):
    rows, d = x.shape
    row = pl.BlockSpec((tm, d), lambda i: (i, 0))
    return pl.pallas_call(
        _norm_kernel,
        grid=(rows // tm,),
        in_specs=[row, _const_spec((1, d))],
        out_specs=row,
        out_shape=jax.ShapeDtypeStruct((rows, d), F32),
        compiler_params=_cparams(1),
        name="final_norm",
    )(x, g.reshape(1, d))


def _head_minor(w):
    lead = w.shape[:-1]
    return w.reshape(*lead, -1, HEAD_DIM).swapaxes(-1, -2).reshape(*lead, w.shape[-1])


def _layer(x, mod, shift0, wkv0, conv0, ffn0, p, *, tm_seq, bs, off, flat, tq_proj, gs_proj,
           tq_tok, gs_tok, tm_mm, tt, nseq_ffn, tq_ffn):
    b, t, d = x.shape
    rows = b * t
    heads = d // HEAD_DIM
    groups = b // GROUP
    w_in = p["w_in"]
    tok = (lambda z: z.reshape(rows, z.shape[-1])) if flat else (lambda z: z)

    xs, shift1 = _prep(x, mod, p["norm1_g"], p["mu"], shift0, tm=tm_seq, off=off, bs=bs)
    xs2 = xs.reshape(7, rows, d)
    xs_tok = xs2 if flat else xs

    w_hm = jnp.concatenate([_head_minor(w_in[:, i * d:(i + 1) * d]) for i in (0, 1, 2, 5)],
                           axis=1).astype(BF16)
    rkv = _proj_head_lanes(xs_tok, w_hm, _Tiling(groups, t, gs_proj, tq_proj, flat), tn=1024,
                           n=3 * d)
    tn = 1024
    ga = _matmul(xs2, w_hm, tm=tm_mm, tn=tn, slab=SLAB_H, col0=3 * d // tn, n=d, name="gate_a_proj")
    glu_in = _matmul(xs2, w_in, tm=tm_mm, tn=tn, slab=SLAB_H, col0=3 * d // tn, n=2 * d,
                     name="glu_proj")
    gb = _matmul(xs2, w_in, tm=tm_mm, tn=tn, slab=SLAB_H, col0=6 * d // tn, n=d, name="gate_b_proj")

    tl = _Tiling(groups, t, gs_tok, tq_tok, flat)
    dec, a, g = _lora(xs_tok, p["w0"], p["w1"], p["w2"], p["a0"], p["a1"], p["a2"], p["g1"], p["g2"],
                      tl)

    s0 = wkv0.reshape(groups, GROUP, heads, HEAD_DIM, HEAD_DIM)
    s0 = s0.transpose(0, 4, 3, 1, 2).reshape(groups, HEAD_DIM, HEAD_DIM, LANES)
    yw, s1 = _wkv(rkv, a, dec, s0, p["k_k"], p["k_a"], p["r_k"], p["lnx_g"], p["lnx_b"], tt=tt,
                  t0=off)
    wkv1 = s1.reshape(groups, HEAD_DIM, HEAD_DIM, GROUP, heads)
    wkv1 = wkv1.transpose(0, 3, 4, 2, 1).reshape(b, heads, HEAD_DIM, HEAD_DIM)

    yb, conv1 = _conv(glu_in.reshape(b, t, 2 * d), p["b_glu"], p["dw_k"], p["dw_b"],
                      p["ln_conv_g"], p["ln_conv_b"], conv0, tm=tm_seq, off=off, bs=bs)

    ga_t, gb_t = (z if flat else z.reshape(b, t, d) for z in (ga, gb))
    x1, h2 = _merge(yw, g, ga_t, gb_t, tok(yb), tok(x), p["w_out"], mod, p["norm2_g"], tl)

    d_ff = p["w_down"].shape[0]
    f_pad = -(-d_ff // FFN_TN) * FFN_TN
    act, ffn_g, ffn_v = _up_act(
        h2.reshape(rows, d), _pad_halves(p["w_up"], f_pad).astype(BF16),
        _pad_halves(p["ffn_dw_k"], f_pad), _pad_halves(p["ffn_dw_b"], f_pad).reshape(1, 2 * f_pad),
        _pad_halves(ffn0, f_pad), nseq=nseq_ffn, tq=tq_ffn, off=off, t=t)
    ffn1 = jnp.concatenate([ffn_g[..., :d_ff], ffn_v[..., :d_ff]], axis=-1)
    if off:
        gate2, rows_per_gate = jnp.repeat(mod[:, 5], t, axis=0), None
    else:
        gate2, rows_per_gate = mod[:, 5][:, None, :], t
    w_down = jnp.pad(p["w_down"], ((0, f_pad - d_ff), (0, 0))).astype(BF16)
    x2 = _matmul(act[None], w_down, tm=min(tm_mm, 512), tn=512, res=x1.reshape(rows, d), gate=gate2,
                 rows_per_gate=rows_per_gate, name="down_proj")
    y = _final_norm(x2, p["normf_g"], tm=min(256, rows))
    return y.reshape(b, t, d), shift1.reshape(b, d), wkv1, conv1, ffn1


def kernel(x_prompt, x_sample, state_shift, state_wkv, state_conv, state_ffn, c_prompt, c_sample,
           norm1_g, norm2_g, normf_g, w_ada, b_ada, mu, w_in, b_glu, w0, w1, w2, a0, a1, a2, g1, g2,
           k_k, k_a, r_k, lnx_g, lnx_b, dw_k, dw_b, ln_conv_g, ln_conv_b, w_out, w_up, ffn_dw_k,
           ffn_dw_b, w_down):
    p = dict(norm1_g=norm1_g, norm2_g=norm2_g, normf_g=normf_g, mu=mu, w_in=w_in, b_glu=b_glu,
             w0=w0, w1=w1, w2=w2, a0=a0, a1=a1, a2=a2, g1=g1, g2=g2, k_k=k_k, k_a=k_a, r_k=r_k,
             lnx_g=lnx_g, lnx_b=lnx_b, dw_k=dw_k, dw_b=dw_b, ln_conv_g=ln_conv_g,
             ln_conv_b=ln_conv_b, w_out=w_out, w_up=w_up, ffn_dw_k=ffn_dw_k, ffn_dw_b=ffn_dw_b,
             w_down=w_down)
    bp, tp, d = x_prompt.shape
    bs, ts, _ = x_sample.shape
    f2 = w_up.shape[1]

    c_all = jnp.concatenate([c_prompt, c_sample], axis=0)
    c_rows = -(-c_all.shape[0] // SUBLANES) * SUBLANES
    c_all = jnp.pad(c_all, ((0, c_rows - c_all.shape[0]), (0, 0)))
    mod = _modulation(c_all, w_ada, b_ada).reshape(c_rows, N_MOD, d)
    mod_p, mod_s = mod[:bp], mod[bp:bp + bs]

    y_p, shift_p, wkv_p, conv_p, ffn_p = _layer(
        x_prompt, mod_p, jnp.zeros((bp, d), F32),
        jnp.zeros((bp, d // HEAD_DIM, HEAD_DIM, HEAD_DIM), F32),
        jnp.zeros((bp, CONV_W - 1, d), F32), jnp.zeros((bp, FFN_CONV_W - 1, f2), F32), p,
        tm_seq=256, bs=1, off=0, flat=False, tq_proj=256, gs_proj=1, tq_tok=32, gs_tok=1,
        tm_mm=1024, tt=32, nseq_ffn=1, tq_ffn=1024)

    off = SAMPLE_TILE - ts
    x_s = jnp.pad(x_sample, ((0, 0), (off, 0), (0, 0)))
    y_s, shift_s, wkv_s, conv_s, ffn_s = _layer(
        x_s, mod_s, state_shift, state_wkv, state_conv, state_ffn, p,
        tm_seq=SAMPLE_TILE, bs=8, off=off, flat=True, tq_proj=SAMPLE_TILE, gs_proj=bs // GROUP,
        tq_tok=SAMPLE_TILE, gs_tok=4, tm_mm=bs * SAMPLE_TILE, tt=SAMPLE_TILE, nseq_ffn=bs,
        tq_ffn=SAMPLE_TILE)
    return (y_p, y_s[:, off:], shift_p, wkv_p, conv_p, ffn_p, shift_s, wkv_s, conv_s, ffn_s)
```

```python
import functools
from typing import NamedTuple

import jax
import jax.numpy as jnp
from jax import lax
from jax.experimental import pallas as pl
from jax.experimental.pallas import tpu as pltpu

F32 = jnp.float32
BF16 = jnp.bfloat16

HEAD_DIM = 64
CONV_W = 31
FFN_CONV_W = 3
N_MOD = 6
GN_EPS = HEAD_DIM * 1e-5
SUBLANES = 8
LANES = 128
LORA_PAD = 128
SAMPLE_TILE = 8
GROUP = 4
V7X_VMEM_LIMIT = 56 * 1024 * 1024

SLAB_R, SLAB_K, SLAB_V, SLAB_H, SLAB_W, SLAB_A, SLAB_G = range(7)
MU_SLABS = (SLAB_R, SLAB_W, SLAB_K, SLAB_V, SLAB_A, SLAB_G)


class _Tiling(NamedTuple):
    groups: int
    t: int
    gs: int
    tq: int
    flat: bool

    @property
    def grid(self):
        return (self.groups // self.gs, self.t // self.tq)

    @property
    def seqs(self):
        return self.gs * GROUP

    def plain(self, cols, col=0):
        if self.flat:
            return pl.BlockSpec((self.seqs * self.tq, cols), lambda ig, it: (ig, col))
        return pl.BlockSpec((self.seqs, self.tq, cols), lambda ig, it: (ig, it, col))

    def head_lanes(self, blk=0):
        return pl.BlockSpec((self.gs, HEAD_DIM, self.tq, LANES), lambda ig, it: (ig, blk, it, 0))

    def per_seq(self, rows, cols):
        return pl.BlockSpec((self.seqs, rows, cols), lambda ig, it: (ig, 0, 0))


def _cparams(n_axes):
    return pltpu.CompilerParams(dimension_semantics=("arbitrary",) * n_axes,
                                vmem_limit_bytes=V7X_VMEM_LIMIT)


def _const_spec(shape):
    return pl.BlockSpec(shape, lambda *_: (0,) * len(shape), pipeline_mode=pl.Buffered(1))


def _sigmoid(x):
    return 1.0 / (1.0 + jnp.exp(-x))


def _silu(x):
    return x * _sigmoid(x)


def _softplus(x):
    return jnp.maximum(x, 0.0) + jnp.log(1.0 + jnp.exp(-jnp.abs(x)))


def _swap_lane_groups(m, lane):
    half = lane < 2 * (LANES // GROUP)
    odd = (lane // (LANES // GROUP)) % 2 == 1
    n = [None] * 4
    for s in range(2):
        n[s] = jnp.where(half, m[s], pltpu.roll(m[s + 2], LANES // 2, axis=1))
        n[s + 2] = jnp.where(half, pltpu.roll(m[s], LANES // 2, axis=1), m[s + 2])
    out = [None] * 4
    for p in (0, 2):
        a, b = n[p], n[p + 1]
        out[p] = jnp.where(odd, pltpu.roll(b, LANES // GROUP, axis=1), a)
        out[p + 1] = jnp.where(odd, b, pltpu.roll(a, LANES - LANES // GROUP, axis=1))
    return out


def _store_head_lanes(val, out_ref, gs, tq):
    lane = lax.broadcasted_iota(jnp.int32, (tq, LANES), 1)
    for g in range(gs):
        for q in range(val.shape[1] // LANES):
            src = [val[(g * GROUP + s) * tq:(g * GROUP + s + 1) * tq, q * LANES:(q + 1) * LANES]
                   for s in range(GROUP)]
            out = _swap_lane_groups(src, lane)
            for j in range(GROUP):
                out_ref[g, GROUP * q + j] = out[j]


def _rows2d(ref):
    v = ref[...]
    return v.reshape(-1, v.shape[-1])


def _mod_kernel(c_ref, w_ref, b_ref, o_ref):
    s = _silu(c_ref[...]).astype(BF16)
    o_ref[...] = jnp.dot(s, w_ref[...].astype(BF16), preferred_element_type=F32) + b_ref[...]


def _modulation(c, w_ada, b_ada):
    rows, d = c.shape
    n = w_ada.shape[1]
    tn = 1024
    return pl.pallas_call(
        _mod_kernel,
        grid=(n // tn,),
        in_specs=[pl.BlockSpec((rows, d), lambda j: (0, 0)),
                  pl.BlockSpec((d, tn), lambda j: (0, j)),
                  pl.BlockSpec((1, tn), lambda j: (0, j))],
        out_specs=pl.BlockSpec((rows, tn), lambda j: (0, j)),
        out_shape=jax.ShapeDtypeStruct((rows, n), F32),
        compiler_params=_cparams(1),
        name="modulation",
    )(c, w_ada, b_ada.reshape(1, n))


def _per_sequence(bs, one):
    if bs == 1:
        one(0)
    else:
        lax.fori_loop(0, bs, lambda s, c: (one(s), c)[1], 0)


def _prep_kernel(x_ref, m_ref, g_ref, mu_ref, s0_ref, xs_ref, so_ref, hbuf, *, tm, off, bs):
    t = pl.program_id(1)

    def one(s):
        x = x_ref[s]
        m = m_ref[s]
        ms = jnp.mean(x * x, axis=-1, keepdims=True)
        h = (x * lax.rsqrt(ms + 1e-6) * g_ref[...]) * (1.0 + m[1:2, :]) + m[0:1, :]
        hbuf[SUBLANES:SUBLANES + tm, :] = h

        @pl.when(t == 0)
        def _():
            hbuf[SUBLANES - 1 + off:SUBLANES + off, :] = s0_ref[s]

        dx = hbuf[SUBLANES - 1:SUBLANES - 1 + tm, :] - h
        for mi, slab in enumerate(MU_SLABS):
            xs_ref[slab, s] = (h + dx * mu_ref[mi:mi + 1, :]).astype(BF16)
        xs_ref[SLAB_H, s] = h.astype(BF16)
        last = h[tm - 1:tm, :]
        hbuf[SUBLANES - 1:SUBLANES, :] = last
        so_ref[s] = last

    _per_sequence(bs, one)


def _prep(x, mod, norm_g, mu, shift0, *, tm, off, bs):
    b, t, d = x.shape
    assert bs == 1 or t == tm
    return pl.pallas_call(
        functools.partial(_prep_kernel, tm=tm, off=off, bs=bs),
        grid=(b // bs, t // tm),
        in_specs=[pl.BlockSpec((bs, tm, d), lambda i, j: (i, j, 0)),
                  pl.BlockSpec((bs, N_MOD, d), lambda i, j: (i, 0, 0)),
                  _const_spec((1, d)),
                  _const_spec((N_MOD, d)),
                  pl.BlockSpec((bs, 1, d), lambda i, j: (i, 0, 0))],
        out_specs=[pl.BlockSpec((7, bs, tm, d), lambda i, j: (0, i, j, 0)),
                   pl.BlockSpec((bs, 1, d), lambda i, j: (i, 0, 0))],
        out_shape=[jax.ShapeDtypeStruct((7, b, t, d), BF16),
                   jax.ShapeDtypeStruct((b, 1, d), F32)],
        scratch_shapes=[pltpu.VMEM((SUBLANES + tm, d), F32)],
        compiler_params=_cparams(2),
        name="prep",
    )(x, mod, norm_g.reshape(1, d), mu, shift0.reshape(b, 1, d))


def _gate_blocks(gate, rows, tm, rows_per_gate):
    if gate.ndim == 2:
        return gate.reshape(rows // tm, tm, gate.shape[-1]), 1
    assert rows_per_gate % tm == 0, (rows_per_gate, tm)
    return gate, rows_per_gate // tm


def _mm_kernel(*refs, gated):
    if gated:
        a_ref, w_ref, res_ref, gate_ref, o_ref, wb = refs
    else:
        a_ref, w_ref, o_ref, wb = refs

    @pl.when(pl.program_id(1) == 0)
    def _():
        wb[...] = w_ref[...].astype(BF16)

    acc = jnp.dot(a_ref[...], wb[...], preferred_element_type=F32)
    if gated:
        acc = res_ref[...] + gate_ref[...] * acc
    o_ref[...] = acc


def _matmul(a, w, *, tm, tn, slab=0, col0=0, n=None, res=None, gate=None, rows_per_gate=None,
            name):
    _, rows, k = a.shape
    n = w.shape[1] if n is None else n
    gated = res is not None
    in_specs = [pl.BlockSpec((None, tm, k), lambda j, i: (slab, i, 0)),
                pl.BlockSpec((k, tn), lambda j, i: (0, j + col0))]
    args = [a, w]
    if gated:
        gate, tiles_per_gate = _gate_blocks(gate, rows, tm, rows_per_gate)
        gr = gate.shape[1]
        in_specs += [pl.BlockSpec((tm, tn), lambda j, i: (i, j)),
                     pl.BlockSpec((None, gr, tn), lambda j, i: (i // tiles_per_gate, 0, j))]
        args += [res, gate]
    return pl.pallas_call(
        functools.partial(_mm_kernel, gated=gated),
        grid=(pl.cdiv(n, tn), rows // tm),
        in_specs=in_specs,
        out_specs=pl.BlockSpec((tm, tn), lambda j, i: (i, j)),
        out_shape=jax.ShapeDtypeStruct((rows, n), F32),
        scratch_shapes=[pltpu.VMEM((k, tn), BF16)],
        compiler_params=_cparams(2),
        name=name,
    )(*args)


def _proj_hl_kernel(a_ref, w_ref, o_ref, *, gs, tq):
    acc = jnp.dot(_rows2d(a_ref), w_ref[...], preferred_element_type=F32)
    _store_head_lanes(acc, o_ref, gs, tq)


def _proj_head_lanes(xs, w, tl, *, tn, n):
    k = w.shape[0]
    d = n // 3
    slab = lambda j: j * tn // d
    if tl.flat:
        a_spec = pl.BlockSpec((None, tl.seqs * tl.tq, k), lambda j, ig, it: (slab(j), ig, 0))
    else:
        a_spec = pl.BlockSpec((None, tl.seqs, tl.tq, k), lambda j, ig, it: (slab(j), ig, it, 0))
    nq = tn // (LANES // GROUP)
    return pl.pallas_call(
        functools.partial(_proj_hl_kernel, gs=tl.gs, tq=tl.tq),
        grid=(n // tn,) + tl.grid,
        in_specs=[a_spec, pl.BlockSpec((k, tn), lambda j, ig, it: (0, j))],
        out_specs=pl.BlockSpec((tl.gs, nq, tl.tq, LANES), lambda j, ig, it: (ig, j, it, 0)),
        out_shape=jax.ShapeDtypeStruct((tl.groups, n // (LANES // GROUP), tl.t, LANES), F32),
        compiler_params=_cparams(3),
        name="rkv_proj",
    )(xs, w)


def _lora_kernel(xw_ref, xa_ref, xg_ref, w1, w2, a1, a2, g1, g2, w0, a0, dec_ref, a_ref, g_ref,
                 *, gs, tq):
    dot = functools.partial(jnp.dot, preferred_element_type=F32)
    hw = jnp.tanh(dot(_rows2d(xw_ref), w1[...])).astype(BF16)
    w_pre = w0[...] + dot(hw, w2[...])
    _store_head_lanes(jnp.exp(-jnp.exp(-_softplus(-w_pre) - 0.5)), dec_ref, gs, tq)
    ha = dot(_rows2d(xa_ref), a1[...]).astype(BF16)
    _store_head_lanes(_sigmoid(a0[...] + dot(ha, a2[...])), a_ref, gs, tq)
    hg = _sigmoid(dot(_rows2d(xg_ref), g1[...])).astype(BF16)
    g_ref[...] = dot(hg, g2[...]).reshape(g_ref.shape)


def _pad_lora(w_a, w_b):
    r = w_a.shape[1]
    return (jnp.pad(w_a, ((0, 0), (0, LORA_PAD - r))).astype(BF16),
            jnp.pad(w_b, ((0, LORA_PAD - r), (0, 0))).astype(BF16))


def _lora(xs, w0, w1, w2, a0, a1, a2, g1, g2, tl):
    d = w0.shape[0]
    w1p, w2p = _pad_lora(w1, _head_minor(w2))
    a1p, a2p = _pad_lora(a1, _head_minor(a2))
    g1b, g2b = g1.astype(BF16), _head_minor(g2).astype(BF16)
    if tl.flat:
        slab = lambda s: pl.BlockSpec((None, tl.seqs * tl.tq, d), lambda ig, it: (s, ig, 0))
        g_shape = (tl.groups * GROUP * tl.t, d)
    else:
        slab = lambda s: pl.BlockSpec((None, tl.seqs, tl.tq, d), lambda ig, it: (s, ig, it, 0))
        g_shape = (tl.groups * GROUP, tl.t, d)
    hl_shape = jax.ShapeDtypeStruct((tl.groups, HEAD_DIM, tl.t, LANES), F32)
    consts = [w1p, w2p, a1p, a2p, g1b, g2b, _head_minor(w0).reshape(1, d),
              _head_minor(a0).reshape(1, d)]
    return pl.pallas_call(
        functools.partial(_lora_kernel, gs=tl.gs, tq=tl.tq),
        grid=tl.grid,
        in_specs=[slab(SLAB_W), slab(SLAB_A), slab(SLAB_G)] + [_const_spec(c.shape) for c in consts],
        out_specs=[tl.head_lanes(), tl.head_lanes(), tl.plain(d)],
        out_shape=[hl_shape, hl_shape, jax.ShapeDtypeStruct(g_shape, F32)],
        compiler_params=_cparams(2),
        name="lora",
    )(xs, xs, xs, *consts)


def _wkv_kernel(r_ref, k_ref, v_ref, a_ref, d_ref, kk_p, ka_p, rk_p, lg_p, lb_p, s0_ref,
                y_ref, so_ref, st, kk_s, b_s, k2_s, v_t, y_t, *, nb, t0):
    n = HEAD_DIM
    tt = nb * SUBLANES
    bc = lambda p: p[...][:, None, None, :]

    @pl.when(pl.program_id(1) == 0)
    def _():
        s = s0_ref[...].reshape(LANES, n, n)
        st[...] = jnp.swapaxes(jnp.swapaxes(jnp.swapaxes(s, 0, 1), 1, 2), 0, 1)

    k = k_ref[...]
    a = a_ref[...]
    kk = k * bc(kk_p)
    kk = kk / jnp.maximum(jnp.sqrt(jnp.sum(kk * kk, axis=0, keepdims=True)), 1e-12)
    kk_s[...] = kk
    b_s[...] = kk * a
    k2_s[...] = k * (1.0 + (a - 1.0) * bc(ka_p))
    v_t[...] = jnp.swapaxes(v_ref[...], 0, 1)
    if t0:
        y_t[0:t0] = jnp.zeros((t0, n, LANES), F32)

    sa0 = jnp.zeros((n, LANES), F32)
    for i in range(n):
        sa0 = sa0 + st[i] * kk_s[i, 0, t0:t0 + 1, :]

    def block(tb, sa):
        tb_next = jnp.minimum(tb + 1, nb - 1)
        for j in range(t0, SUBLANES):
            t = tb * SUBLANES + j
            v = v_t[t]
            y = jnp.zeros((n, LANES), F32)
            sa_next = jnp.zeros((n, LANES), F32)
            for i in range(n):
                si = (st[i] * d_ref[i, tb, j:j + 1, :] - sa * b_s[i, tb, j:j + 1, :]
                      + v * k2_s[i, tb, j:j + 1, :])
                st[i] = si
                y = y + si * r_ref[i, tb, j:j + 1, :]
                if j + 1 < SUBLANES:
                    sa_next = sa_next + si * kk_s[i, tb, j + 1:j + 2, :]
                else:
                    sa_next = sa_next + si * kk_s[i, tb_next, 0:1, :]
            y_t[t] = y
            sa = sa_next
        return sa

    lax.fori_loop(0, nb, block, sa0)

    y = jnp.swapaxes(y_t[...], 0, 1)
    ym = jnp.mean(y, axis=0, keepdims=True)
    yc = y - ym
    yv = jnp.mean(yc * yc, axis=0, keepdims=True)
    bcast = lambda p: p[...][:, None, :]
    yn = yc * lax.rsqrt(yv + GN_EPS) * bcast(lg_p) + bcast(lb_p)
    rk = (r_ref[...] * k2_s[...] * bc(rk_p)).reshape(n, tt, LANES)
    y_ref[...] = yn + jnp.sum(rk, axis=0, keepdims=True) * v_ref[...]

    @pl.when(pl.program_id(1) == pl.num_programs(1) - 1)
    def _():
        s = jnp.swapaxes(jnp.swapaxes(jnp.swapaxes(st[...], 0, 1), 1, 2), 0, 1)
        so_ref[...] = s.reshape(so_ref.shape)


def _head_lanes(p):
    hn = p.reshape(-1, HEAD_DIM).T
    return jnp.tile(hn, (1, GROUP))


def _wkv(rkv, a, dec, s0, k_k, k_a, r_k, lnx_g, lnx_b, *, tt, t0):
    g, _, t, lanes = a.shape
    n = HEAD_DIM
    heads = s0.shape[1]
    nb = tt // SUBLANES
    assert t0 == 0 or nb == 1
    by8 = lambda z: z.reshape(z.shape[0], z.shape[1], t // SUBLANES, SUBLANES, lanes)
    params = [_head_lanes(p) for p in (k_k, k_a, r_k.reshape(-1), lnx_g, lnx_b)]
    rows = lambda blk: pl.BlockSpec((None, n, nb, SUBLANES, lanes), lambda i, j: (i, blk, j, 0, 0))
    tile = lambda blk: pl.BlockSpec((None, n, tt, lanes), lambda i, j: (i, blk, j, 0))
    st = pl.BlockSpec((GROUP, heads, n, n), lambda i, j: (i, 0, 0, 0))
    return pl.pallas_call(
        functools.partial(_wkv_kernel, nb=nb, t0=t0),
        grid=(g, t // tt),
        in_specs=[rows(0), rows(1), tile(2), rows(0), rows(0)] + [_const_spec((n, lanes))] * 5 + [st],
        out_specs=[tile(0), st],
        out_shape=[jax.ShapeDtypeStruct((g, n, t, lanes), F32),
                   jax.ShapeDtypeStruct(s0.shape, F32)],
        scratch_shapes=[pltpu.VMEM((n, n, lanes), F32)]
        + [pltpu.VMEM((n, nb, SUBLANES, lanes), F32)] * 3
        + [pltpu.VMEM((tt, n, lanes), F32)] * 2,
        compiler_params=_cparams(2),
        name="wkv7",
    )(by8(rkv), by8(rkv), rkv, by8(a), by8(dec), *params, s0)


CONV_PAD = 32
CONV_ROWS = 32


def _conv_kernel(u1_ref, u2_ref, bg1, bg2, dwk, dwb, lg, lb, st0_ref, y_ref, sto_ref, zbuf, cbuf,
                 zs, *, tm, off, bs):
    t = pl.program_id(1)
    hist = CONV_W - 1
    lo = CONV_PAD - hist
    d = u1_ref.shape[-1]
    rc = min(CONV_ROWS, tm)

    def one(s):
        glu = (u1_ref[s] + bg1[...]) * _sigmoid(u2_ref[s] + bg2[...])
        zbuf[CONV_PAD:CONV_PAD + tm, :] = glu

        @pl.when(t == 0)
        def _():
            zbuf[lo + off:CONV_PAD + off, :] = st0_ref[s]

        for b in range(SUBLANES):
            span = tm + SUBLANES * ((hist - b) // SUBLANES)
            zs[b, 0:span, :] = zbuf[lo + b:lo + b + span, :]
        for r0 in range(0, tm, rc):
            for c0 in range(0, d, LANES):
                acc = jnp.zeros((rc, LANES), F32)
                for j in range(CONV_W):
                    a8 = SUBLANES * (j // SUBLANES)
                    acc = acc + (zs[j % SUBLANES, a8 + r0:a8 + r0 + rc, c0:c0 + LANES]
                                 * dwk[j:j + 1, c0:c0 + LANES])
                cbuf[r0:r0 + rc, c0:c0 + LANES] = acc
        zc = cbuf[...] + dwb[...]
        m = jnp.mean(zc, axis=-1, keepdims=True)
        ctr = zc - m
        var = jnp.mean(ctr * ctr, axis=-1, keepdims=True)
        ln = ctr * lax.rsqrt(var + 1e-5) * lg[...] + lb[...]
        y_ref[s] = _silu(ln)
        carry = zbuf[lo + tm:CONV_PAD + tm, :]
        zbuf[lo:CONV_PAD, :] = carry
        sto_ref[s] = carry

    _per_sequence(bs, one)


def _conv(glu_in, b_glu, dw_k, dw_b, ln_g, ln_b, st0, *, tm, off, bs):
    b, t, _ = glu_in.shape
    d = dw_k.shape[1]
    hist = CONV_W - 1
    assert bs == 1 or t == tm
    vec = _const_spec((1, d))
    return pl.pallas_call(
        functools.partial(_conv_kernel, tm=tm, off=off, bs=bs),
        grid=(b // bs, t // tm),
        in_specs=[pl.BlockSpec((bs, tm, d), lambda i, j: (i, j, 0)),
                  pl.BlockSpec((bs, tm, d), lambda i, j: (i, j, 1)),
                  pl.BlockSpec((1, d), lambda i, j: (0, 0)),
                  pl.BlockSpec((1, d), lambda i, j: (0, 1)),
                  _const_spec((CONV_W, d)),
                  vec, vec, vec,
                  pl.BlockSpec((bs, hist, d), lambda i, j: (i, 0, 0))],
        out_specs=[pl.BlockSpec((bs, tm, d), lambda i, j: (i, j, 0)),
                   pl.BlockSpec((bs, hist, d), lambda i, j: (i, 0, 0))],
        out_shape=[jax.ShapeDtypeStruct((b, t, d), F32),
                   jax.ShapeDtypeStruct((b, hist, d), F32)],
        scratch_shapes=[pltpu.VMEM((CONV_PAD + tm, d), F32), pltpu.VMEM((tm, d), F32),
                        pltpu.VMEM((SUBLANES, tm + CONV_PAD - SUBLANES, d), F32)],
        compiler_params=_cparams(2),
        name="conformer_conv",
    )(glu_in, glu_in, b_glu.reshape(1, 2 * d), b_glu.reshape(1, 2 * d), dw_k, dw_b.reshape(1, d),
      ln_g.reshape(1, d), ln_b.reshape(1, d), st0)


def _merge_kernel(yw_ref, g_ref, ga_ref, gb_ref, yb_ref, x_ref, woa_ref, wob_ref, m_ref, n2_ref,
                  x1_ref, h2_ref, ybuf, *, gs, tq):
    seqs = gs * GROUP
    d = ybuf.shape[-1]
    lane = lax.broadcasted_iota(jnp.int32, (tq, LANES), 1)
    for g in range(gs):
        for q in range(d // LANES):
            out = _swap_lane_groups([yw_ref[g, GROUP * q + j] for j in range(GROUP)], lane)
            for s in range(GROUP):
                ybuf[g * GROUP + s, :, q * LANES:(q + 1) * LANES] = out[s]
    seq3 = lambda ref: ref[...].reshape(seqs, tq, d)
    m = m_ref[...]
    dot = lambda z, w_ref: jnp.dot(z.reshape(seqs * tq, d).astype(BF16), w_ref[...],
                                   preferred_element_type=F32)
    branch_a = _sigmoid(seq3(ga_ref)) * (ybuf[...] * seq3(g_ref))
    branch_b = _sigmoid(seq3(gb_ref)) * seq3(yb_ref)
    mix = (dot(branch_a, woa_ref) + dot(branch_b, wob_ref)).reshape(seqs, tq, d)
    x1 = seq3(x_ref) + m[:, 2:3, :] * mix
    x1_ref[...] = x1.reshape(x1_ref.shape)
    ms = jnp.mean(x1 * x1, axis=-1, keepdims=True)
    h2 = (x1 * lax.rsqrt(ms + 1e-6) * n2_ref[...]) * (1.0 + m[:, 4:5, :]) + m[:, 3:4, :]
    h2_ref[...] = h2.reshape(seqs * tq, d).astype(BF16).reshape(h2_ref.shape)


def _merge(yw, g, ga, gb, yb, x, w_out, mod, norm_g, tl):
    d = x.shape[-1]
    plain = tl.plain(d)
    w_out_b = w_out.astype(BF16)
    w_out_a = _head_minor(w_out.T).T.astype(BF16)
    return pl.pallas_call(
        functools.partial(_merge_kernel, gs=tl.gs, tq=tl.tq),
        grid=tl.grid,
        in_specs=[tl.head_lanes(), plain, plain, plain, plain, plain, _const_spec((d, d)),
                  _const_spec((d, d)), tl.per_seq(N_MOD, d), _const_spec((1, d))],
        out_specs=[plain, plain],
        out_shape=[jax.ShapeDtypeStruct(x.shape, F32), jax.ShapeDtypeStruct(x.shape, BF16)],
        scratch_shapes=[pltpu.VMEM((tl.seqs, tl.tq, d), F32)],
        compiler_params=_cparams(2),
        name="merge_out_proj",
    )(yw, g, ga, gb, yb, x, w_out_a, w_out_b, mod, norm_g.reshape(1, d))


FFN_TN = 512


def _pad_halves(w, width):
    f = w.shape[-1] // 2
    pad = [(0, 0)] * (w.ndim - 1) + [(0, width - f)]
    return jnp.concatenate([jnp.pad(w[..., :f], pad), jnp.pad(w[..., f:], pad)], axis=-1)


def _up_act_kernel(h_ref, wg_ref, wv_ref, kg_ref, kv_ref, bg_ref, bv_ref, sg_ref, sv_ref,
                   act_ref, stg_ref, stv_ref, carry, *, nseq, tq, off, tiles_per_seq):
    first = pl.program_id(1) % tiles_per_seq == 0
    h = h_ref[...]
    tn = wg_ref.shape[-1]
    row = lax.broadcasted_iota(jnp.int32, (nseq, tq, tn), 1)
    z = []
    halves = ((wg_ref, kg_ref, bg_ref, stg_ref), (wv_ref, kv_ref, bv_ref, stv_ref))

    @pl.when(first)
    def _():
        carry[0] = sg_ref[...]
        carry[1] = sv_ref[...]

    for half, (w_ref, k_ref, b_ref, st_ref) in enumerate(halves):
        u = jnp.dot(h, w_ref[...], preferred_element_type=F32).reshape(nseq, tq, tn)
        prev = carry[half]
        p0, p1 = prev[:, 0:1, :], prev[:, 1:2, :]
        u1 = jnp.where(row == off, p1, pltpu.roll(u, 1, axis=1))
        u2 = jnp.where(row == off, p0, jnp.where(row == off + 1, p1, pltpu.roll(u, 2, axis=1)))
        z.append(b_ref[...] + u2 * k_ref[0:1, :] + u1 * k_ref[1:2, :] + u * k_ref[2:3, :])
        last = u[:, tq - (FFN_CONV_W - 1):, :]
        carry[half] = last
        st_ref[...] = last
    act_ref[...] = (_silu(z[0]) * z[1]).reshape(nseq * tq, tn).astype(BF16)


def _up_act(h2, w_up, ffn_dw_k, ffn_dw_b, st0, *, nseq, tq, off, t):
    rows, k = h2.shape
    b = st0.shape[0]
    f = w_up.shape[1] // 2
    nj = f // FFN_TN
    hist = FFN_CONV_W - 1
    tiles_per_seq = t // tq
    assert nseq == 1 or tiles_per_seq == 1
    seq = lambda i: i // tiles_per_seq
    col = lambda shape, h: pl.BlockSpec(shape, lambda j, i: (0, j + h * nj))
    state = lambda h: pl.BlockSpec((nseq, hist, FFN_TN), lambda j, i: (seq(i), 0, j + h * nj))
    return pl.pallas_call(
        functools.partial(_up_act_kernel, nseq=nseq, tq=tq, off=off, tiles_per_seq=tiles_per_seq),
        grid=(nj, rows // (nseq * tq)),
        in_specs=[pl.BlockSpec((nseq * tq, k), lambda j, i: (i, 0)),
                  col((k, FFN_TN), 0), col((k, FFN_TN), 1),
                  col((FFN_CONV_W, FFN_TN), 0), col((FFN_CONV_W, FFN_TN), 1),
                  col((1, FFN_TN), 0), col((1, FFN_TN), 1),
                  state(0), state(1)],
        out_specs=[pl.BlockSpec((nseq * tq, FFN_TN), lambda j, i: (i, j)), state(0), state(0)],
        out_shape=[jax.ShapeDtypeStruct((rows, f), BF16),
                   jax.ShapeDtypeStruct((b, hist, f), F32),
                   jax.ShapeDtypeStruct((b, hist, f), F32)],
        scratch_shapes=[pltpu.VMEM((2, nseq, hist, FFN_TN), F32)],
        compiler_params=_cparams(2),
        name="up_conv_act",
    )(h2, w_up, w_up, ffn_dw_k, ffn_dw_k, ffn_dw_b, ffn_dw_b, st0, st0)


def _norm_kernel(x_ref, g_ref, o_ref):
    x = x_ref[...]
    ms = jnp.mean(x * x, axis=-1, keepdims=True)
    o_ref[...] = x * lax.rsqrt(ms + 1e-6) * g_ref[...]


def _final_norm(x, g, *, tm):
    rows, d = x.shape
    row = pl.BlockSpec((tm, d), lambda i: (i, 0))
    return pl.pallas_call(
        _norm_kernel,
        grid=(rows // tm,),
        in_specs=[row, _const_spec((1, d))],
        out_specs=row,
        out_shape=jax.ShapeDtypeStruct((rows, d), F32),
        compiler_params=_cparams(1),
        name="final_norm",
    )(x, g.reshape(1, d))


def _head_minor(w):
    lead = w.shape[:-1]
    return w.reshape(*lead, -1, HEAD_DIM).swapaxes(-1, -2).reshape(*lead, w.shape[-1])


def _layer(x, mod, shift0, wkv0, conv0, ffn0, p, *, tm_seq, bs, off, flat, tq_proj, gs_proj,
           tq_tok, gs_tok, tm_mm, tt, nseq_ffn, tq_ffn):
    b, t, d = x.shape
    rows = b * t
    groups = b // GROUP
    w_in = p["w_in"]
    tok = (lambda z: z.reshape(rows, z.shape[-1])) if flat else (lambda z: z)

    xs, shift1 = _prep(x, mod, p["norm1_g"], p["mu"], shift0, tm=tm_seq, off=off, bs=bs)
    xs2 = xs.reshape(7, rows, d)
    xs_tok = xs2 if flat else xs

    w_hm = jnp.concatenate([_head_minor(w_in[:, i * d:(i + 1) * d]) for i in (0, 1, 2, 5)],
                           axis=1).astype(BF16)
    rkv = _proj_head_lanes(xs_tok, w_hm, _Tiling(groups, t, gs_proj, tq_proj, flat), tn=1024,
                           n=3 * d)
    tn = 1024
    ga = _matmul(xs2, w_hm, tm=tm_mm, tn=tn, slab=SLAB_H, col0=3 * d // tn, n=d, name="gate_a_proj")
    glu_in = _matmul(xs2, w_in, tm=tm_mm, tn=tn, slab=SLAB_H, col0=3 * d // tn, n=2 * d,
                     name="glu_proj")
    gb = _matmul(xs2, w_in, tm=tm_mm, tn=tn, slab=SLAB_H, col0=6 * d // tn, n=d, name="gate_b_proj")

    tl = _Tiling(groups, t, gs_tok, tq_tok, flat)
    dec, a, g = _lora(xs_tok, p["w0"], p["w1"], p["w2"], p["a0"], p["a1"], p["a2"], p["g1"], p["g2"],
                      tl)

    yw, wkv1 = _wkv(rkv, a, dec, wkv0, p["k_k"], p["k_a"], p["r_k"], p["lnx_g"], p["lnx_b"], tt=tt,
                    t0=off)

    yb, conv1 = _conv(glu_in.reshape(b, t, 2 * d), p["b_glu"], p["dw_k"], p["dw_b"],
                      p["ln_conv_g"], p["ln_conv_b"], conv0, tm=tm_seq, off=off, bs=bs)

    ga_t, gb_t = (z if flat else z.reshape(b, t, d) for z in (ga, gb))
    x1, h2 = _merge(yw, g, ga_t, gb_t, tok(yb), tok(x), p["w_out"], mod, p["norm2_g"], tl)

    d_ff = p["w_down"].shape[0]
    f_pad = -(-d_ff // FFN_TN) * FFN_TN
    act, ffn_g, ffn_v = _up_act(
        h2.reshape(rows, d), _pad_halves(p["w_up"], f_pad).astype(BF16),
        _pad_halves(p["ffn_dw_k"], f_pad), _pad_halves(p["ffn_dw_b"], f_pad).reshape(1, 2 * f_pad),
        _pad_halves(ffn0, f_pad), nseq=nseq_ffn, tq=tq_ffn, off=off, t=t)
    ffn1 = jnp.concatenate([ffn_g[..., :d_ff], ffn_v[..., :d_ff]], axis=-1)
    if off:
        gate2, rows_per_gate = jnp.repeat(mod[:, 5], t, axis=0), None
    else:
        gate2, rows_per_gate = mod[:, 5][:, None, :], t
    w_down = jnp.pad(p["w_down"], ((0, f_pad - d_ff), (0, 0))).astype(BF16)
    x2 = _matmul(act[None], w_down, tm=min(tm_mm, 512), tn=512, res=x1.reshape(rows, d), gate=gate2,
                 rows_per_gate=rows_per_gate, name="down_proj")
    y = _final_norm(x2, p["normf_g"], tm=min(256, rows))
    return y.reshape(b, t, d), shift1.reshape(b, d), wkv1, conv1, ffn1


def kernel(x_prompt, x_sample, state_shift, state_wkv, state_conv, state_ffn, c_prompt, c_sample,
           norm1_g, norm2_g, normf_g, w_ada, b_ada, mu, w_in, b_glu, w0, w1, w2, a0, a1, a2, g1, g2,
           k_k, k_a, r_k, lnx_g, lnx_b, dw_k, dw_b, ln_conv_g, ln_conv_b, w_out, w_up, ffn_dw_k,
           ffn_dw_b, w_down):
    p = dict(norm1_g=norm1_g, norm2_g=norm2_g, normf_g=normf_g, mu=mu, w_in=w_in, b_glu=b_glu,
             w0=w0, w1=w1, w2=w2, a0=a0, a1=a1, a2=a2, g1=g1, g2=g2, k_k=k_k, k_a=k_a, r_k=r_k,
             lnx_g=lnx_g, lnx_b=lnx_b, dw_k=dw_k, dw_b=dw_b, ln_conv_g=ln_conv_g,
             ln_conv_b=ln_conv_b, w_out=w_out, w_up=w_up, ffn_dw_k=ffn_dw_k, ffn_dw_b=ffn_dw_b,
             w_down=w_down)
    bp, tp, d = x_prompt.shape
    bs, ts, _ = x_sample.shape
    f2 = w_up.shape[1]

    c_all = jnp.concatenate([c_prompt, c_sample], axis=0)
    c_rows = -(-c_all.shape[0] // SUBLANES) * SUBLANES
    c_all = jnp.pad(c_all, ((0, c_rows - c_all.shape[0]), (0, 0)))
    mod = _modulation(c_all, w_ada, b_ada).reshape(c_rows, N_MOD, d)
    mod_p, mod_s = mod[:bp], mod[bp:bp + bs]

    y_p, shift_p, wkv_p, conv_p, ffn_p = _layer(
        x_prompt, mod_p, jnp.zeros((bp, d), F32),
        jnp.zeros((bp, d // HEAD_DIM, HEAD_DIM, HEAD_DIM), F32),
        jnp.zeros((bp, CONV_W - 1, d), F32), jnp.zeros((bp, FFN_CONV_W - 1, f2), F32), p,
        tm_seq=256, bs=1, off=0, flat=False, tq_proj=256, gs_proj=1, tq_tok=32, gs_tok=1,
        tm_mm=1024, tt=32, nseq_ffn=1, tq_ffn=1024)

    off = SAMPLE_TILE - ts
    x_s = jnp.pad(x_sample, ((0, 0), (off, 0), (0, 0)))
    y_s, shift_s, wkv_s, conv_s, ffn_s = _layer(
        x_s, mod_s, state_shift, state_wkv, state_conv, state_ffn, p,
        tm_seq=SAMPLE_TILE, bs=8, off=off, flat=True, tq_proj=SAMPLE_TILE, gs_proj=bs // GROUP,
        tq_tok=SAMPLE_TILE, gs_tok=4, tm_mm=bs * SAMPLE_TILE, tt=SAMPLE_TILE, nseq_ffn=bs,
        tq_ffn=SAMPLE_TILE)
    return (y_p, y_s[:, off:], shift_p, wkv_p, conv_p, ffn_p, shift_s, wkv_s, conv_s, ffn_s)
```

```python
import functools
from typing import NamedTuple

import jax
import jax.numpy as jnp
from jax import lax
from jax.experimental import pallas as pl
from jax.experimental.pallas import tpu as pltpu

F32 = jnp.float32
BF16 = jnp.bfloat16

HEAD_DIM = 64
CONV_W = 31
FFN_CONV_W = 3
N_MOD = 6
GN_EPS = HEAD_DIM * 1e-5
SUBLANES = 8
LANES = 128
MXU_COLS = 256
LORA_PAD = 128
SAMPLE_TILE = 8
GROUP = 4
V7X_VMEM_LIMIT = 56 * 1024 * 1024

SLAB_R, SLAB_K, SLAB_V, SLAB_H, SLAB_W, SLAB_A, SLAB_G = range(7)
MU_SLABS = (SLAB_R, SLAB_W, SLAB_K, SLAB_V, SLAB_A, SLAB_G)


class _Tiling(NamedTuple):
    groups: int
    t: int
    gs: int
    tq: int
    flat: bool

    @property
    def grid(self):
        return (self.groups // self.gs, self.t // self.tq)

    @property
    def seqs(self):
        return self.gs * GROUP

    def plain(self, cols, col=0):
        if self.flat:
            return pl.BlockSpec((self.seqs * self.tq, cols), lambda ig, it: (ig, col))
        return pl.BlockSpec((self.seqs, self.tq, cols), lambda ig, it: (ig, it, col))

    def head_lanes(self, blk=0):
        return pl.BlockSpec((self.gs, HEAD_DIM, self.tq, LANES), lambda ig, it: (ig, blk, it, 0))

    def per_seq(self, rows, cols):
        return pl.BlockSpec((self.seqs, rows, cols), lambda ig, it: (ig, 0, 0))


def _cparams(n_axes):
    return pltpu.CompilerParams(dimension_semantics=("arbitrary",) * n_axes,
                                vmem_limit_bytes=V7X_VMEM_LIMIT)


def _const_spec(shape):
    return pl.BlockSpec(shape, lambda *_: (0,) * len(shape), pipeline_mode=pl.Buffered(1))


def _sigmoid(x):
    return 1.0 / (1.0 + jnp.exp(-x))


def _silu(x):
    return x * _sigmoid(x)


def _softplus(x):
    return jnp.maximum(x, 0.0) + jnp.log(1.0 + jnp.exp(-jnp.abs(x)))


def _swap_lane_groups(m, lane):
    half = lane < 2 * (LANES // GROUP)
    odd = (lane // (LANES // GROUP)) % 2 == 1
    n = [None] * 4
    for s in range(2):
        n[s] = jnp.where(half, m[s], pltpu.roll(m[s + 2], LANES // 2, axis=1))
        n[s + 2] = jnp.where(half, pltpu.roll(m[s], LANES // 2, axis=1), m[s + 2])
    out = [None] * 4
    for p in (0, 2):
        a, b = n[p], n[p + 1]
        out[p] = jnp.where(odd, pltpu.roll(b, LANES // GROUP, axis=1), a)
        out[p + 1] = jnp.where(odd, b, pltpu.roll(a, LANES - LANES // GROUP, axis=1))
    return out


def _store_head_lanes(val, out_ref, gs, tq, q0=0):
    lane = lax.broadcasted_iota(jnp.int32, (tq, LANES), 1)
    for g in range(gs):
        for q in range(val.shape[1] // LANES):
            src = [val[(g * GROUP + s) * tq:(g * GROUP + s + 1) * tq, q * LANES:(q + 1) * LANES]
                   for s in range(GROUP)]
            out = _swap_lane_groups(src, lane)
            for j in range(GROUP):
                out_ref[g, GROUP * (q0 + q) + j] = out[j]


def _rows2d(ref):
    v = ref[...]
    return v.reshape(-1, v.shape[-1])


def _mod_kernel(c_ref, w_ref, b_ref, o_ref):
    s = _silu(c_ref[...]).astype(BF16)
    o_ref[...] = jnp.dot(s, w_ref[...].astype(BF16), preferred_element_type=F32) + b_ref[...]


def _modulation(c, w_ada, b_ada):
    rows, d = c.shape
    n = w_ada.shape[1]
    tn = 1024
    return pl.pallas_call(
        _mod_kernel,
        grid=(n // tn,),
        in_specs=[pl.BlockSpec((rows, d), lambda j: (0, 0)),
                  pl.BlockSpec((d, tn), lambda j: (0, j)),
                  pl.BlockSpec((1, tn), lambda j: (0, j))],
        out_specs=pl.BlockSpec((rows, tn), lambda j: (0, j)),
        out_shape=jax.ShapeDtypeStruct((rows, n), F32),
        compiler_params=_cparams(1),
        name="modulation",
    )(c, w_ada, b_ada.reshape(1, n))


def _per_sequence(bs, one):
    if bs == 1:
        one(0)
    else:
        lax.fori_loop(0, bs, lambda s, c: (one(s), c)[1], 0)


def _prep_kernel(x_ref, m_ref, g_ref, mu_ref, s0_ref, xs_ref, so_ref, hbuf, *, tm, off, bs):
    t = pl.program_id(1)

    def one(s):
        x = x_ref[s]
        m = m_ref[s]
        ms = jnp.mean(x * x, axis=-1, keepdims=True)
        h = (x * lax.rsqrt(ms + 1e-6) * g_ref[...]) * (1.0 + m[1:2, :]) + m[0:1, :]
        hbuf[SUBLANES:SUBLANES + tm, :] = h

        @pl.when(t == 0)
        def _():
            if off:
                hbuf[SUBLANES - 1:SUBLANES, :] = jnp.zeros((1, hbuf.shape[1]), F32)
            hbuf[SUBLANES - 1 + off:SUBLANES + off, :] = s0_ref[s]

        dx = hbuf[SUBLANES - 1:SUBLANES - 1 + tm, :] - h
        for mi, slab in enumerate(MU_SLABS):
            xs_ref[slab, s] = (h + dx * mu_ref[mi:mi + 1, :]).astype(BF16)
        xs_ref[SLAB_H, s] = h.astype(BF16)
        last = h[tm - 1:tm, :]
        hbuf[SUBLANES - 1:SUBLANES, :] = last
        so_ref[s] = last

    _per_sequence(bs, one)


def _prep(x, mod, norm_g, mu, shift0, *, tm, off, bs):
    b, t, d = x.shape
    assert bs == 1 or t == tm
    return pl.pallas_call(
        functools.partial(_prep_kernel, tm=tm, off=off, bs=bs),
        grid=(b // bs, t // tm),
        in_specs=[pl.BlockSpec((bs, tm, d), lambda i, j: (i, j, 0)),
                  pl.BlockSpec((bs, N_MOD, d), lambda i, j: (i, 0, 0)),
                  _const_spec((1, d)),
                  _const_spec((N_MOD, d)),
                  pl.BlockSpec((bs, 1, d), lambda i, j: (i, 0, 0))],
        out_specs=[pl.BlockSpec((7, bs, tm, d), lambda i, j: (0, i, j, 0)),
                   pl.BlockSpec((bs, 1, d), lambda i, j: (i, 0, 0))],
        out_shape=[jax.ShapeDtypeStruct((7, b, t, d), BF16),
                   jax.ShapeDtypeStruct((b, 1, d), F32)],
        scratch_shapes=[pltpu.VMEM((SUBLANES + tm, d), F32)],
        compiler_params=_cparams(2),
        name="prep",
    )(x, mod, norm_g.reshape(1, d), mu, shift0.reshape(b, 1, d))


def _gate_blocks(gate, rows, tm, rows_per_gate):
    if gate.ndim == 2:
        return gate.reshape(rows // tm, tm, gate.shape[-1]), 1
    assert rows_per_gate % tm == 0, (rows_per_gate, tm)
    return gate, rows_per_gate // tm


def _mm_kernel(*refs, gated, sigmoid):
    if gated:
        a_ref, w_ref, res_ref, gate_ref, o_ref, wb = refs
    else:
        a_ref, w_ref, o_ref, wb = refs

    @pl.when(pl.program_id(1) == 0)
    def _():
        wb[...] = w_ref[...].astype(BF16)

    acc = jnp.dot(a_ref[...], wb[...], preferred_element_type=F32)
    if gated:
        acc = res_ref[...] + gate_ref[...] * acc
    if sigmoid:
        acc = _sigmoid(acc)
    o_ref[...] = acc.astype(o_ref.dtype)


def _matmul(a, w, *, tm, tn, slab=0, col0=0, n=None, res=None, gate=None, rows_per_gate=None,
            sigmoid=False, out_dtype=F32, name):
    _, rows, k = a.shape
    n = w.shape[1] if n is None else n
    gated = res is not None
    in_specs = [pl.BlockSpec((None, tm, k), lambda j, i: (slab, i, 0)),
                pl.BlockSpec((k, tn), lambda j, i: (0, j + col0))]
    args = [a, w]
    if gated:
        gate, tiles_per_gate = _gate_blocks(gate, rows, tm, rows_per_gate)
        gr = gate.shape[1]
        in_specs += [pl.BlockSpec((tm, tn), lambda j, i: (i, j)),
                     pl.BlockSpec((None, gr, tn), lambda j, i: (i // tiles_per_gate, 0, j))]
        args += [res, gate]
    return pl.pallas_call(
        functools.partial(_mm_kernel, gated=gated, sigmoid=sigmoid),
        grid=(pl.cdiv(n, tn), rows // tm),
        in_specs=in_specs,
        out_specs=pl.BlockSpec((tm, tn), lambda j, i: (i, j)),
        out_shape=jax.ShapeDtypeStruct((rows, n), out_dtype),
        scratch_shapes=[pltpu.VMEM((k, tn), BF16)],
        compiler_params=_cparams(2),
        name=name,
    )(*args)


def _proj_hl_kernel(a_ref, w_ref, o_ref, *, gs, tq):
    a = _rows2d(a_ref)
    for c in range(0, w_ref.shape[1], MXU_COLS):
        acc = jnp.dot(a, w_ref[:, c:c + MXU_COLS], preferred_element_type=F32)
        _store_head_lanes(acc, o_ref, gs, tq, q0=c // LANES)


def _proj_head_lanes(xs, w, tl, *, tn, n):
    k = w.shape[0]
    d = n // 3
    slab = lambda j: j * tn // d
    if tl.flat:
        a_spec = pl.BlockSpec((None, tl.seqs * tl.tq, k), lambda j, ig, it: (slab(j), ig, 0))
    else:
        a_spec = pl.BlockSpec((None, tl.seqs, tl.tq, k), lambda j, ig, it: (slab(j), ig, it, 0))
    nq = tn // (LANES // GROUP)
    return pl.pallas_call(
        functools.partial(_proj_hl_kernel, gs=tl.gs, tq=tl.tq),
        grid=(n // tn,) + tl.grid,
        in_specs=[a_spec, pl.BlockSpec((k, tn), lambda j, ig, it: (0, j))],
        out_specs=pl.BlockSpec((tl.gs, nq, tl.tq, LANES), lambda j, ig, it: (ig, j, it, 0)),
        out_shape=jax.ShapeDtypeStruct((tl.groups, n // (LANES // GROUP), tl.t, LANES), F32),
        compiler_params=_cparams(3),
        name="rkv_proj",
    )(xs, w)


def _lora_kernel(xw_ref, xa_ref, xg_ref, w1, w2, a1, a2, g1, g2, w0, a0, dec_ref, a_ref, g_ref,
                 *, gs, tq):
    dot = functools.partial(jnp.dot, preferred_element_type=F32)
    hw = jnp.tanh(dot(_rows2d(xw_ref), w1[...])).astype(BF16)
    w_pre = w0[...] + dot(hw, w2[...])
    _store_head_lanes(jnp.exp(-jnp.exp(-_softplus(-w_pre) - 0.5)), dec_ref, gs, tq)
    ha = dot(_rows2d(xa_ref), a1[...]).astype(BF16)
    _store_head_lanes(_sigmoid(a0[...] + dot(ha, a2[...])), a_ref, gs, tq)
    hg = _sigmoid(dot(_rows2d(xg_ref), g1[...])).astype(BF16)
    g_ref[...] = dot(hg, g2[...]).reshape(g_ref.shape)


def _pad_lora(w_a, w_b):
    r = w_a.shape[1]
    return (jnp.pad(w_a, ((0, 0), (0, LORA_PAD - r))).astype(BF16),
            jnp.pad(w_b, ((0, LORA_PAD - r), (0, 0))).astype(BF16))


def _lora(xs, w0, w1, w2, a0, a1, a2, g1, g2, tl):
    d = w0.shape[0]
    w1p, w2p = _pad_lora(w1, _head_minor(w2))
    a1p, a2p = _pad_lora(a1, _head_minor(a2))
    g1b, g2b = g1.astype(BF16), _head_minor(g2).astype(BF16)
    if tl.flat:
        slab = lambda s: pl.BlockSpec((None, tl.seqs * tl.tq, d), lambda ig, it: (s, ig, 0))
        g_shape = (tl.groups * GROUP * tl.t, d)
    else:
        slab = lambda s: pl.BlockSpec((None, tl.seqs, tl.tq, d), lambda ig, it: (s, ig, it, 0))
        g_shape = (tl.groups * GROUP, tl.t, d)
    hl_shape = jax.ShapeDtypeStruct((tl.groups, HEAD_DIM, tl.t, LANES), F32)
    consts = [w1p, w2p, a1p, a2p, g1b, g2b, _head_minor(w0).reshape(1, d),
              _head_minor(a0).reshape(1, d)]
    return pl.pallas_call(
        functools.partial(_lora_kernel, gs=tl.gs, tq=tl.tq),
        grid=tl.grid,
        in_specs=[slab(SLAB_W), slab(SLAB_A), slab(SLAB_G)] + [_const_spec(c.shape) for c in consts],
        out_specs=[tl.head_lanes(), tl.head_lanes(), tl.plain(d)],
        out_shape=[hl_shape, hl_shape, jax.ShapeDtypeStruct(g_shape, F32)],
        compiler_params=_cparams(2),
        name="lora",
    )(xs, xs, xs, *consts)


def _wkv_kernel(r_ref, k_ref, v_ref, a_ref, d_ref, kk_p, ka_p, rk_p, lg_p, lb_p, s0_ref,
                y_ref, so_ref, st, kk_s, b_s, k2_s, v_t, y_t, *, nb, t0):
    n = HEAD_DIM
    tt = nb * SUBLANES
    bc = lambda p: p[...][:, None, None, :]

    @pl.when(pl.program_id(1) == 0)
    def _():
        s = s0_ref[...].reshape(LANES, n, n)
        st[...] = jnp.swapaxes(jnp.swapaxes(jnp.swapaxes(s, 0, 1), 1, 2), 0, 1)

    k = k_ref[...]
    a = a_ref[...]
    kk = k * bc(kk_p)
    kk = kk / jnp.maximum(jnp.sqrt(jnp.sum(kk * kk, axis=0, keepdims=True)), 1e-12)
    kk_s[...] = kk
    b_s[...] = kk * a
    k2_s[...] = k * (1.0 + (a - 1.0) * bc(ka_p))
    v_t[...] = jnp.swapaxes(v_ref[...], 0, 1)
    if t0:
        y_t[0:t0] = jnp.zeros((t0, n, LANES), F32)

    sa0 = jnp.zeros((n, LANES), F32)
    for i in range(n):
        sa0 = sa0 + st[i] * kk_s[i, 0, t0:t0 + 1, :]

    def block(tb, sa):
        tb_next = jnp.minimum(tb + 1, nb - 1)
        for j in range(t0, SUBLANES):
            t = tb * SUBLANES + j
            v = v_t[t]
            y = jnp.zeros((n, LANES), F32)
            sa_next = jnp.zeros((n, LANES), F32)
            for i in range(n):
                si = (st[i] * d_ref[i, tb, j:j + 1, :] - sa * b_s[i, tb, j:j + 1, :]
                      + v * k2_s[i, tb, j:j + 1, :])
                st[i] = si
                y = y + si * r_ref[i, tb, j:j + 1, :]
                if j + 1 < SUBLANES:
                    sa_next = sa_next + si * kk_s[i, tb, j + 1:j + 2, :]
                else:
                    sa_next = sa_next + si * kk_s[i, tb_next, 0:1, :]
            y_t[t] = y
            sa = sa_next
        return sa

    lax.fori_loop(0, nb, block, sa0)

    y = jnp.swapaxes(y_t[...], 0, 1)
    ym = jnp.mean(y, axis=0, keepdims=True)
    yc = y - ym
    yv = jnp.mean(yc * yc, axis=0, keepdims=True)
    bcast = lambda p: p[...][:, None, :]
    yn = yc * lax.rsqrt(yv + GN_EPS) * bcast(lg_p) + bcast(lb_p)
    rk = (r_ref[...] * k2_s[...] * bc(rk_p)).reshape(n, tt, LANES)
    y_ref[...] = yn + jnp.sum(rk, axis=0, keepdims=True) * v_ref[...]

    @pl.when(pl.program_id(1) == pl.num_programs(1) - 1)
    def _():
        s = jnp.swapaxes(jnp.swapaxes(jnp.swapaxes(st[...], 0, 1), 1, 2), 0, 1)
        so_ref[...] = s.reshape(so_ref.shape)


def _head_lanes(p):
    hn = p.reshape(-1, HEAD_DIM).T
    return jnp.tile(hn, (1, GROUP))


def _wkv(rkv, a, dec, s0, k_k, k_a, r_k, lnx_g, lnx_b, *, tt, t0):
    g, _, t, lanes = a.shape
    n = HEAD_DIM
    heads = s0.shape[1]
    nb = tt // SUBLANES
    assert t0 == 0 or nb == 1
    by8 = lambda z: z.reshape(z.shape[0], z.shape[1], t // SUBLANES, SUBLANES, lanes)
    params = [_head_lanes(p) for p in (k_k, k_a, r_k.reshape(-1), lnx_g, lnx_b)]
    rows = lambda blk: pl.BlockSpec((None, n, nb, SUBLANES, lanes), lambda i, j: (i, blk, j, 0, 0))
    tile = lambda blk: pl.BlockSpec((None, n, tt, lanes), lambda i, j: (i, blk, j, 0))
    st = pl.BlockSpec((GROUP, heads, n, n), lambda i, j: (i, 0, 0, 0))
    return pl.pallas_call(
        functools.partial(_wkv_kernel, nb=nb, t0=t0),
        grid=(g, t // tt),
        in_specs=[rows(0), rows(1), tile(2), rows(0), rows(0)] + [_const_spec((n, lanes))] * 5 + [st],
        out_specs=[tile(0), st],
        out_shape=[jax.ShapeDtypeStruct((g, n, t, lanes), F32),
                   jax.ShapeDtypeStruct(s0.shape, F32)],
        scratch_shapes=[pltpu.VMEM((n, n, lanes), F32)]
        + [pltpu.VMEM((n, nb, SUBLANES, lanes), F32)] * 3
        + [pltpu.VMEM((tt, n, lanes), F32)] * 2,
        compiler_params=_cparams(2),
        name="wkv7",
    )(by8(rkv), by8(rkv), rkv, by8(a), by8(dec), *params, s0)


CONV_PAD = 32
CONV_ROWS = 32


def _conv_kernel(u1_ref, u2_ref, bg1, bg2, dwk, dwb, lg, lb, st0_ref, y_ref, sto_ref, zbuf, cbuf,
                 zs, *, tm, off, bs):
    t = pl.program_id(1)
    hist = CONV_W - 1
    lo = CONV_PAD - hist
    d = u1_ref.shape[-1]
    rc = min(CONV_ROWS, tm)

    def one(s):
        glu = (u1_ref[s] + bg1[...]) * _sigmoid(u2_ref[s] + bg2[...])
        zbuf[CONV_PAD:CONV_PAD + tm, :] = glu

        @pl.when(t == 0)
        def _():
            if off:
                zbuf[lo:lo + off, :] = jnp.zeros((off, d), F32)
            zbuf[lo + off:CONV_PAD + off, :] = st0_ref[s]

        for b in range(SUBLANES):
            span = tm + SUBLANES * ((hist - b) // SUBLANES)
            zs[b, 0:span, :] = zbuf[lo + b:lo + b + span, :]
        for r0 in range(0, tm, rc):
            for c0 in range(0, d, LANES):
                acc = jnp.zeros((rc, LANES), F32)
                for j in range(CONV_W):
                    a8 = SUBLANES * (j // SUBLANES)
                    acc = acc + (zs[j % SUBLANES, a8 + r0:a8 + r0 + rc, c0:c0 + LANES]
                                 * dwk[j:j + 1, c0:c0 + LANES])
                cbuf[r0:r0 + rc, c0:c0 + LANES] = acc
        zc = cbuf[...] + dwb[...]
        m = jnp.mean(zc, axis=-1, keepdims=True)
        ctr = zc - m
        var = jnp.mean(ctr * ctr, axis=-1, keepdims=True)
        ln = ctr * lax.rsqrt(var + 1e-5) * lg[...] + lb[...]
        y_ref[s] = _silu(ln).astype(BF16)
        carry = zbuf[lo + tm:CONV_PAD + tm, :]
        zbuf[lo:CONV_PAD, :] = carry
        sto_ref[s] = carry

    _per_sequence(bs, one)


def _conv(glu_in, b_glu, dw_k, dw_b, ln_g, ln_b, st0, *, tm, off, bs):
    b, t, _ = glu_in.shape
    d = dw_k.shape[1]
    hist = CONV_W - 1
    assert bs == 1 or t == tm
    vec = _const_spec((1, d))
    return pl.pallas_call(
        functools.partial(_conv_kernel, tm=tm, off=off, bs=bs),
        grid=(b // bs, t // tm),
        in_specs=[pl.BlockSpec((bs, tm, d), lambda i, j: (i, j, 0)),
                  pl.BlockSpec((bs, tm, d), lambda i, j: (i, j, 1)),
                  pl.BlockSpec((1, d), lambda i, j: (0, 0)),
                  pl.BlockSpec((1, d), lambda i, j: (0, 1)),
                  _const_spec((CONV_W, d)),
                  vec, vec, vec,
                  pl.BlockSpec((bs, hist, d), lambda i, j: (i, 0, 0))],
        out_specs=[pl.BlockSpec((bs, tm, d), lambda i, j: (i, j, 0)),
                   pl.BlockSpec((bs, hist, d), lambda i, j: (i, 0, 0))],
        out_shape=[jax.ShapeDtypeStruct((b, t, d), BF16),
                   jax.ShapeDtypeStruct((b, hist, d), F32)],
        scratch_shapes=[pltpu.VMEM((CONV_PAD + tm, d), F32), pltpu.VMEM((tm, d), F32),
                        pltpu.VMEM((SUBLANES, tm + CONV_PAD - SUBLANES, d), F32)],
        compiler_params=_cparams(2),
        name="conformer_conv",
    )(glu_in, glu_in, b_glu.reshape(1, 2 * d), b_glu.reshape(1, 2 * d), dw_k, dw_b.reshape(1, d),
      ln_g.reshape(1, d), ln_b.reshape(1, d), st0)


def _merge_kernel(yw_ref, g_ref, ga_ref, gb_ref, yb_ref, x_ref, woa_ref, wob_ref, m_ref, n2_ref,
                  x1_ref, h2_ref, ybuf, *, gs, tq):
    seqs = gs * GROUP
    d = ybuf.shape[-1]
    lane = lax.broadcasted_iota(jnp.int32, (tq, LANES), 1)
    for g in range(gs):
        for q in range(d // LANES):
            out = _swap_lane_groups([yw_ref[g, GROUP * q + j] for j in range(GROUP)], lane)
            for s in range(GROUP):
                ybuf[g * GROUP + s, :, q * LANES:(q + 1) * LANES] = out[s]
    seq3 = lambda ref: ref[...].astype(F32).reshape(seqs, tq, d)
    m = m_ref[...]
    dot = lambda z, w_ref: jnp.dot(z.reshape(seqs * tq, d).astype(BF16), w_ref[...],
                                   preferred_element_type=F32)
    branch_a = seq3(ga_ref) * (ybuf[...] * seq3(g_ref))
    branch_b = seq3(gb_ref) * seq3(yb_ref)
    mix = (dot(branch_a, woa_ref) + dot(branch_b, wob_ref)).reshape(seqs, tq, d)
    x1 = seq3(x_ref) + m[:, 2:3, :] * mix
    x1_ref[...] = x1.reshape(x1_ref.shape)
    ms = jnp.mean(x1 * x1, axis=-1, keepdims=True)
    h2 = (x1 * lax.rsqrt(ms + 1e-6) * n2_ref[...]) * (1.0 + m[:, 4:5, :]) + m[:, 3:4, :]
    h2_ref[...] = h2.reshape(seqs * tq, d).astype(BF16).reshape(h2_ref.shape)


def _merge(yw, g, ga, gb, yb, x, w_out, mod, norm_g, tl):
    d = x.shape[-1]
    plain = tl.plain(d)
    w_out_b = w_out.astype(BF16)
    w_out_a = _head_minor(w_out.T).T.astype(BF16)
    return pl.pallas_call(
        functools.partial(_merge_kernel, gs=tl.gs, tq=tl.tq),
        grid=tl.grid,
        in_specs=[tl.head_lanes(), plain, plain, plain, plain, plain, _const_spec((d, d)),
                  _const_spec((d, d)), tl.per_seq(N_MOD, d), _const_spec((1, d))],
        out_specs=[plain, plain],
        out_shape=[jax.ShapeDtypeStruct(x.shape, F32), jax.ShapeDtypeStruct(x.shape, BF16)],
        scratch_shapes=[pltpu.VMEM((tl.seqs, tl.tq, d), F32)],
        compiler_params=_cparams(2),
        name="merge_out_proj",
    )(yw, g, ga, gb, yb, x, w_out_a, w_out_b, mod, norm_g.reshape(1, d))


FFN_TN = 512


def _pad_halves(w, width):
    f = w.shape[-1] // 2
    pad = [(0, 0)] * (w.ndim - 1) + [(0, width - f)]
    return jnp.concatenate([jnp.pad(w[..., :f], pad), jnp.pad(w[..., f:], pad)], axis=-1)


def _up_act_kernel(h_ref, wg_ref, wv_ref, kg_ref, kv_ref, bg_ref, bv_ref, sg_ref, sv_ref,
                   act_ref, stg_ref, stv_ref, carry, *, nseq, tq, off, tiles_per_seq):
    first = pl.program_id(1) % tiles_per_seq == 0
    h = h_ref[...]
    tn = wg_ref.shape[-1]
    row = lax.broadcasted_iota(jnp.int32, (nseq, tq, tn), 1)
    z = []
    halves = ((wg_ref, kg_ref, bg_ref, stg_ref), (wv_ref, kv_ref, bv_ref, stv_ref))

    @pl.when(first)
    def _():
        carry[0] = sg_ref[...]
        carry[1] = sv_ref[...]

    for half, (w_ref, k_ref, b_ref, st_ref) in enumerate(halves):
        u = jnp.dot(h, w_ref[...], preferred_element_type=F32).reshape(nseq, tq, tn)
        prev = carry[half]
        p0, p1 = prev[:, 0:1, :], prev[:, 1:2, :]
        u1 = jnp.where(row == off, p1, pltpu.roll(u, 1, axis=1))
        u2 = jnp.where(row == off, p0, jnp.where(row == off + 1, p1, pltpu.roll(u, 2, axis=1)))
        z.append(b_ref[...] + u2 * k_ref[0:1, :] + u1 * k_ref[1:2, :] + u * k_ref[2:3, :])
        last = u[:, tq - (FFN_CONV_W - 1):, :]
        carry[half] = last
        st_ref[...] = last
    act_ref[...] = (_silu(z[0]) * z[1]).reshape(nseq * tq, tn).astype(BF16)


def _up_act(h2, w_up, ffn_dw_k, ffn_dw_b, st0, *, nseq, tq, off, t):
    rows, k = h2.shape
    b = st0.shape[0]
    f = w_up.shape[1] // 2
    nj = f // FFN_TN
    hist = FFN_CONV_W - 1
    tiles_per_seq = t // tq
    assert nseq == 1 or tiles_per_seq == 1
    seq = lambda i: i // tiles_per_seq
    col = lambda shape, h: pl.BlockSpec(shape, lambda j, i: (0, j + h * nj))
    state = lambda h: pl.BlockSpec((nseq, hist, FFN_TN), lambda j, i: (seq(i), 0, j + h * nj))
    return pl.pallas_call(
        functools.partial(_up_act_kernel, nseq=nseq, tq=tq, off=off, tiles_per_seq=tiles_per_seq),
        grid=(nj, rows // (nseq * tq)),
        in_specs=[pl.BlockSpec((nseq * tq, k), lambda j, i: (i, 0)),
                  col((k, FFN_TN), 0), col((k, FFN_TN), 1),
                  col((FFN_CONV_W, FFN_TN), 0), col((FFN_CONV_W, FFN_TN), 1),
                  col((1, FFN_TN), 0), col((1, FFN_TN), 1),
                  state(0), state(1)],
        out_specs=[pl.BlockSpec((nseq * tq, FFN_TN), lambda j, i: (i, j)), state(0), state(0)],
        out_shape=[jax.ShapeDtypeStruct((rows, f), BF16),
                   jax.ShapeDtypeStruct((b, hist, f), F32),
                   jax.ShapeDtypeStruct((b, hist, f), F32)],
        scratch_shapes=[pltpu.VMEM((2, nseq, hist, FFN_TN), F32)],
        compiler_params=_cparams(2),
        name="up_conv_act",
    )(h2, w_up, w_up, ffn_dw_k, ffn_dw_k, ffn_dw_b, ffn_dw_b, st0, st0)


def _norm_kernel(x_ref, g_ref, o_ref):
    x = x_ref[...]
    ms = jnp.mean(x * x, axis=-1, keepdims=True)
    o_ref[...] = x * lax.rsqrt(ms + 1e-6) * g_ref[...]


def _final_norm(x, g, *, tm):
    rows, d = x.shape
    row = pl.BlockSpec((tm, d), lambda i: (i, 0))
    return pl.pallas_call(
        _norm_kernel,
        grid=(rows // tm,),
        in_specs=[row, _const_spec((1, d))],
        out_specs=row,
        out_shape=jax.ShapeDtypeStruct((rows, d), F32),
        compiler_params=_cparams(1),
        name="final_norm",
    )(x, g.reshape(1, d))


def _head_minor(w):
    lead = w.shape[:-1]
    return w.reshape(*lead, -1, HEAD_DIM).swapaxes(-1, -2).reshape(*lead, w.shape[-1])


def _layer(x, mod, shift0, wkv0, conv0, ffn0, p, *, tm_seq, bs, off, flat, tq_proj, gs_proj,
           tq_tok, gs_tok, tm_mm, tt, nseq_ffn, tq_ffn):
    b, t, d = x.shape
    rows = b * t
    groups = b // GROUP
    w_in = p["w_in"]
    tok = (lambda z: z.reshape(rows, z.shape[-1])) if flat else (lambda z: z)

    xs, shift1 = _prep(x, mod, p["norm1_g"], p["mu"], shift0, tm=tm_seq, off=off, bs=bs)
    xs2 = xs.reshape(7, rows, d)
    xs_tok = xs2 if flat else xs

    w_hm = jnp.concatenate([_head_minor(w_in[:, i * d:(i + 1) * d]) for i in (0, 1, 2, 5)],
                           axis=1).astype(BF16)
    rkv = _proj_head_lanes(xs_tok, w_hm, _Tiling(groups, t, gs_proj, tq_proj, flat), tn=1024,
                           n=3 * d)
    tn = 1024
    ga = _matmul(xs2, w_hm, tm=tm_mm, tn=tn, slab=SLAB_H, col0=3 * d // tn, n=d, sigmoid=True,
                 out_dtype=BF16, name="gate_a_proj")
    glu_in = _matmul(xs2, w_in, tm=tm_mm, tn=tn, slab=SLAB_H, col0=3 * d // tn, n=2 * d,
                     name="glu_proj")
    gb = _matmul(xs2, w_in, tm=tm_mm, tn=tn, slab=SLAB_H, col0=6 * d // tn, n=d, sigmoid=True,
                 out_dtype=BF16, name="gate_b_proj")

    tl = _Tiling(groups, t, gs_tok, tq_tok, flat)
    dec, a, g = _lora(xs_tok, p["w0"], p["w1"], p["w2"], p["a0"], p["a1"], p["a2"], p["g1"], p["g2"],
                      tl)

    yw, wkv1 = _wkv(rkv, a, dec, wkv0, p["k_k"], p["k_a"], p["r_k"], p["lnx_g"], p["lnx_b"], tt=tt,
                    t0=off)

    yb, conv1 = _conv(glu_in.reshape(b, t, 2 * d), p["b_glu"], p["dw_k"], p["dw_b"],
                      p["ln_conv_g"], p["ln_conv_b"], conv0, tm=tm_seq, off=off, bs=bs)

    ga_t, gb_t = (z if flat else z.reshape(b, t, d) for z in (ga, gb))
    x1, h2 = _merge(yw, g, ga_t, gb_t, tok(yb), tok(x), p["w_out"], mod, p["norm2_g"], tl)

    d_ff = p["w_down"].shape[0]
    f_pad = -(-d_ff // FFN_TN) * FFN_TN
    act, ffn_g, ffn_v = _up_act(
        h2.reshape(rows, d), _pad_halves(p["w_up"], f_pad).astype(BF16),
        _pad_halves(p["ffn_dw_k"], f_pad), _pad_halves(p["ffn_dw_b"], f_pad).reshape(1, 2 * f_pad),
        _pad_halves(ffn0, f_pad), nseq=nseq_ffn, tq=tq_ffn, off=off, t=t)
    ffn1 = jnp.concatenate([ffn_g[..., :d_ff], ffn_v[..., :d_ff]], axis=-1)
    if off:
        gate2, rows_per_gate = jnp.repeat(mod[:, 5], t, axis=0), None
    else:
        gate2, rows_per_gate = mod[:, 5][:, None, :], t
    w_down = jnp.pad(p["w_down"], ((0, f_pad - d_ff), (0, 0))).astype(BF16)
    x2 = _matmul(act[None], w_down, tm=min(tm_mm, 512), tn=512, res=x1.reshape(rows, d), gate=gate2,
                 rows_per_gate=rows_per_gate, name="down_proj")
    y = _final_norm(x2, p["normf_g"], tm=min(256, rows))
    return y.reshape(b, t, d), shift1.reshape(b, d), wkv1, conv1, ffn1


def kernel(x_prompt, x_sample, state_shift, state_wkv, state_conv, state_ffn, c_prompt, c_sample,
           norm1_g, norm2_g, normf_g, w_ada, b_ada, mu, w_in, b_glu, w0, w1, w2, a0, a1, a2, g1, g2,
           k_k, k_a, r_k, lnx_g, lnx_b, dw_k, dw_b, ln_conv_g, ln_conv_b, w_out, w_up, ffn_dw_k,
           ffn_dw_b, w_down):
    p = dict(norm1_g=norm1_g, norm2_g=norm2_g, normf_g=normf_g, mu=mu, w_in=w_in, b_glu=b_glu,
             w0=w0, w1=w1, w2=w2, a0=a0, a1=a1, a2=a2, g1=g1, g2=g2, k_k=k_k, k_a=k_a, r_k=r_k,
             lnx_g=lnx_g, lnx_b=lnx_b, dw_k=dw_k, dw_b=dw_b, ln_conv_g=ln_conv_g,
             ln_conv_b=ln_conv_b, w_out=w_out, w_up=w_up, ffn_dw_k=ffn_dw_k, ffn_dw_b=ffn_dw_b,
             w_down=w_down)
    bp, tp, d = x_prompt.shape
    bs, ts, _ = x_sample.shape
    f2 = w_up.shape[1]

    c_all = jnp.concatenate([c_prompt, c_sample], axis=0)
    c_rows = -(-c_all.shape[0] // SUBLANES) * SUBLANES
    c_all = jnp.pad(c_all, ((0, c_rows - c_all.shape[0]), (0, 0)))
    mod = _modulation(c_all, w_ada, b_ada).reshape(c_rows, N_MOD, d)
    mod_p, mod_s = mod[:bp], mod[bp:bp + bs]

    y_p, shift_p, wkv_p, conv_p, ffn_p = _layer(
        x_prompt, mod_p, jnp.zeros((bp, d), F32),
        jnp.zeros((bp, d // HEAD_DIM, HEAD_DIM, HEAD_DIM), F32),
        jnp.zeros((bp, CONV_W - 1, d), F32), jnp.zeros((bp, FFN_CONV_W - 1, f2), F32), p,
        tm_seq=256, bs=1, off=0, flat=False, tq_proj=256, gs_proj=1, tq_tok=64, gs_tok=1,
        tm_mm=1024, tt=32, nseq_ffn=1, tq_ffn=1024)

    off = SAMPLE_TILE - ts
    x_s = jnp.pad(x_sample, ((0, 0), (off, 0), (0, 0)))
    y_s, shift_s, wkv_s, conv_s, ffn_s = _layer(
        x_s, mod_s, state_shift, state_wkv, state_conv, state_ffn, p,
        tm_seq=SAMPLE_TILE, bs=8, off=off, flat=True, tq_proj=SAMPLE_TILE, gs_proj=bs // GROUP,
        tq_tok=SAMPLE_TILE, gs_tok=4, tm_mm=bs * SAMPLE_TILE, tt=SAMPLE_TILE, nseq_ffn=bs,
        tq_ffn=SAMPLE_TILE)
    return (y_p, y_s[:, off:], shift_p, wkv_p, conv_p, ffn_p, shift_s, wkv_s, conv_s, ffn_s)
```

```python
import functools
from typing import NamedTuple

import jax
import jax.numpy as jnp
from jax import lax
from jax.experimental import pallas as pl
from jax.experimental.pallas import tpu as pltpu

F32 = jnp.float32
BF16 = jnp.bfloat16

HEAD_DIM = 64
CONV_W = 31
FFN_CONV_W = 3
N_MOD = 6
GN_EPS = HEAD_DIM * 1e-5
SUBLANES = 8
LANES = 128
MXU_COLS = 256
LORA_PAD = 128
SAMPLE_TILE = 8
GROUP = 4
V7X_VMEM_LIMIT = 56 * 1024 * 1024

SLAB_R, SLAB_K, SLAB_V, SLAB_H, SLAB_W, SLAB_A, SLAB_G = range(7)
MU_SLABS = (SLAB_R, SLAB_W, SLAB_K, SLAB_V, SLAB_A, SLAB_G)


class _Tiling(NamedTuple):
    groups: int
    t: int
    gs: int
    tq: int
    flat: bool

    @property
    def grid(self):
        return (self.groups // self.gs, self.t // self.tq)

    @property
    def seqs(self):
        return self.gs * GROUP

    def plain(self, cols, col=0):
        if self.flat:
            return pl.BlockSpec((self.seqs * self.tq, cols), lambda ig, it: (ig, col))
        return pl.BlockSpec((self.seqs, self.tq, cols), lambda ig, it: (ig, it, col))

    def head_lanes(self, blk=0):
        return pl.BlockSpec((self.gs, HEAD_DIM, self.tq, LANES), lambda ig, it: (ig, blk, it, 0))

    def per_seq(self, rows, cols):
        return pl.BlockSpec((self.seqs, rows, cols), lambda ig, it: (ig, 0, 0))


def _cparams(n_axes):
    return pltpu.CompilerParams(dimension_semantics=("arbitrary",) * n_axes,
                                vmem_limit_bytes=V7X_VMEM_LIMIT)


def _const_spec(shape):
    return pl.BlockSpec(shape, lambda *_: (0,) * len(shape), pipeline_mode=pl.Buffered(1))


def _sigmoid(x):
    return 1.0 / (1.0 + jnp.exp(-x))


def _silu(x):
    return x * _sigmoid(x)


def _softplus(x):
    return jnp.maximum(x, 0.0) + jnp.log(1.0 + jnp.exp(-jnp.abs(x)))


def _swap_lane_groups(m, lane):
    half = lane < 2 * (LANES // GROUP)
    odd = (lane // (LANES // GROUP)) % 2 == 1
    n = [None] * 4
    for s in range(2):
        n[s] = jnp.where(half, m[s], pltpu.roll(m[s + 2], LANES // 2, axis=1))
        n[s + 2] = jnp.where(half, pltpu.roll(m[s], LANES // 2, axis=1), m[s + 2])
    out = [None] * 4
    for p in (0, 2):
        a, b = n[p], n[p + 1]
        out[p] = jnp.where(odd, pltpu.roll(b, LANES // GROUP, axis=1), a)
        out[p + 1] = jnp.where(odd, b, pltpu.roll(a, LANES - LANES // GROUP, axis=1))
    return out


def _store_head_lanes(val, out_ref, gs, tq, q0=0):
    lane = lax.broadcasted_iota(jnp.int32, (tq, LANES), 1)
    for g in range(gs):
        for q in range(val.shape[1] // LANES):
            src = [val[(g * GROUP + s) * tq:(g * GROUP + s + 1) * tq, q * LANES:(q + 1) * LANES]
                   for s in range(GROUP)]
            out = _swap_lane_groups(src, lane)
            for j in range(GROUP):
                out_ref[g, GROUP * (q0 + q) + j] = out[j]


def _rows2d(ref):
    v = ref[...]
    return v.reshape(-1, v.shape[-1])


def _mod_kernel(c_ref, w_ref, b_ref, o_ref):
    s = _silu(c_ref[...]).astype(BF16)
    o_ref[...] = jnp.dot(s, w_ref[...].astype(BF16), preferred_element_type=F32) + b_ref[...]


def _modulation(c, w_ada, b_ada):
    rows, d = c.shape
    n = w_ada.shape[1]
    tn = 1024
    return pl.pallas_call(
        _mod_kernel,
        grid=(n // tn,),
        in_specs=[pl.BlockSpec((rows, d), lambda j: (0, 0)),
                  pl.BlockSpec((d, tn), lambda j: (0, j)),
                  pl.BlockSpec((1, tn), lambda j: (0, j))],
        out_specs=pl.BlockSpec((rows, tn), lambda j: (0, j)),
        out_shape=jax.ShapeDtypeStruct((rows, n), F32),
        compiler_params=_cparams(1),
        name="modulation",
    )(c, w_ada, b_ada.reshape(1, n))


def _per_sequence(bs, one):
    if bs == 1:
        one(0)
    else:
        lax.fori_loop(0, bs, lambda s, c: (one(s), c)[1], 0)


def _prep_kernel(x_ref, m_ref, g_ref, mu_ref, s0_ref, xs_ref, so_ref, hbuf, *, tm, off, bs):
    t = pl.program_id(1)

    def one(s):
        x = x_ref[s]
        m = m_ref[s]
        ms = jnp.mean(x * x, axis=-1, keepdims=True)
        h = (x * lax.rsqrt(ms + 1e-6) * g_ref[...]) * (1.0 + m[1:2, :]) + m[0:1, :]
        hbuf[SUBLANES:SUBLANES + tm, :] = h

        @pl.when(t == 0)
        def _():
            if off:
                hbuf[SUBLANES - 1:SUBLANES, :] = jnp.zeros((1, hbuf.shape[1]), F32)
            hbuf[SUBLANES - 1 + off:SUBLANES + off, :] = s0_ref[s]

        dx = hbuf[SUBLANES - 1:SUBLANES - 1 + tm, :] - h
        for mi, slab in enumerate(MU_SLABS):
            xs_ref[slab, s] = (h + dx * mu_ref[mi:mi + 1, :]).astype(BF16)
        xs_ref[SLAB_H, s] = h.astype(BF16)
        last = h[tm - 1:tm, :]
        hbuf[SUBLANES - 1:SUBLANES, :] = last
        so_ref[s] = last

    _per_sequence(bs, one)


def _prep(x, mod, norm_g, mu, shift0, *, tm, off, bs):
    b, t, d = x.shape
    assert bs == 1 or t == tm
    return pl.pallas_call(
        functools.partial(_prep_kernel, tm=tm, off=off, bs=bs),
        grid=(b // bs, t // tm),
        in_specs=[pl.BlockSpec((bs, tm, d), lambda i, j: (i, j, 0)),
                  pl.BlockSpec((bs, N_MOD, d), lambda i, j: (i, 0, 0)),
                  _const_spec((1, d)),
                  _const_spec((N_MOD, d)),
                  pl.BlockSpec((bs, 1, d), lambda i, j: (i, 0, 0))],
        out_specs=[pl.BlockSpec((7, bs, tm, d), lambda i, j: (0, i, j, 0)),
                   pl.BlockSpec((bs, 1, d), lambda i, j: (i, 0, 0))],
        out_shape=[jax.ShapeDtypeStruct((7, b, t, d), BF16),
                   jax.ShapeDtypeStruct((b, 1, d), F32)],
        scratch_shapes=[pltpu.VMEM((SUBLANES + tm, d), F32)],
        compiler_params=_cparams(2),
        name="prep",
    )(x, mod, norm_g.reshape(1, d), mu, shift0.reshape(b, 1, d))


def _gate_blocks(gate, rows, tm, rows_per_gate):
    if gate.ndim == 2:
        return gate.reshape(rows // tm, tm, gate.shape[-1]), 1
    assert rows_per_gate % tm == 0, (rows_per_gate, tm)
    return gate, rows_per_gate // tm


def _mm_kernel(*refs, gated, sigmoid):
    if gated:
        a_ref, w_ref, res_ref, gate_ref, o_ref, wb = refs
    else:
        a_ref, w_ref, o_ref, wb = refs

    @pl.when(pl.program_id(1) == 0)
    def _():
        wb[...] = w_ref[...].astype(BF16)

    acc = jnp.dot(a_ref[...], wb[...], preferred_element_type=F32)
    if gated:
        acc = res_ref[...] + gate_ref[...] * acc
    if sigmoid:
        acc = _sigmoid(acc)
    o_ref[...] = acc.astype(o_ref.dtype)


def _matmul(a, w, *, tm, tn, slab=0, col0=0, n=None, res=None, gate=None, rows_per_gate=None,
            sigmoid=False, out_dtype=F32, name):
    _, rows, k = a.shape
    n = w.shape[1] if n is None else n
    gated = res is not None
    in_specs = [pl.BlockSpec((None, tm, k), lambda j, i: (slab, i, 0)),
                pl.BlockSpec((k, tn), lambda j, i: (0, j + col0))]
    args = [a, w]
    if gated:
        gate, tiles_per_gate = _gate_blocks(gate, rows, tm, rows_per_gate)
        gr = gate.shape[1]
        in_specs += [pl.BlockSpec((tm, tn), lambda j, i: (i, j)),
                     pl.BlockSpec((None, gr, tn), lambda j, i: (i // tiles_per_gate, 0, j))]
        args += [res, gate]
    return pl.pallas_call(
        functools.partial(_mm_kernel, gated=gated, sigmoid=sigmoid),
        grid=(pl.cdiv(n, tn), rows // tm),
        in_specs=in_specs,
        out_specs=pl.BlockSpec((tm, tn), lambda j, i: (i, j)),
        out_shape=jax.ShapeDtypeStruct((rows, n), out_dtype),
        scratch_shapes=[pltpu.VMEM((k, tn), BF16)],
        compiler_params=_cparams(2),
        name=name,
    )(*args)


def _proj_hl_kernel(a_ref, w_ref, o_ref, *, gs, tq):
    a = _rows2d(a_ref)
    for c in range(0, w_ref.shape[1], MXU_COLS):
        acc = jnp.dot(a, w_ref[:, c:c + MXU_COLS], preferred_element_type=F32)
        _store_head_lanes(acc, o_ref, gs, tq, q0=c // LANES)


def _proj_head_lanes(xs, w, tl, *, tn, n):
    k = w.shape[0]
    d = n // 3
    slab = lambda j: j * tn // d
    if tl.flat:
        a_spec = pl.BlockSpec((None, tl.seqs * tl.tq, k), lambda j, ig, it: (slab(j), ig, 0))
    else:
        a_spec = pl.BlockSpec((None, tl.seqs, tl.tq, k), lambda j, ig, it: (slab(j), ig, it, 0))
    nq = tn // (LANES // GROUP)
    return pl.pallas_call(
        functools.partial(_proj_hl_kernel, gs=tl.gs, tq=tl.tq),
        grid=(n // tn,) + tl.grid,
        in_specs=[a_spec, pl.BlockSpec((k, tn), lambda j, ig, it: (0, j))],
        out_specs=pl.BlockSpec((tl.gs, nq, tl.tq, LANES), lambda j, ig, it: (ig, j, it, 0)),
        out_shape=jax.ShapeDtypeStruct((tl.groups, n // (LANES // GROUP), tl.t, LANES), F32),
        compiler_params=_cparams(3),
        name="rkv_proj",
    )(xs, w)


def _lora_heads(xw_ref, xa_ref, xg_ref, w1, w2, a1, a2, g1, g2, w0, a0):
    dot = functools.partial(jnp.dot, preferred_element_type=F32)
    hw = jnp.tanh(dot(_rows2d(xw_ref), w1[...])).astype(BF16)
    w_pre = w0[...] + dot(hw, w2[...])
    dec = jnp.exp(-jnp.exp(-_softplus(-w_pre) - 0.5))
    ha = dot(_rows2d(xa_ref), a1[...]).astype(BF16)
    a = _sigmoid(a0[...] + dot(ha, a2[...]))
    hg = _sigmoid(dot(_rows2d(xg_ref), g1[...])).astype(BF16)
    return dec, a, dot(hg, g2[...])


def _lora_kernel(*refs, gs, tq):
    dec_ref, a_ref, g_ref = refs[-3:]
    dec, a, g = _lora_heads(*refs[:-3])
    _store_head_lanes(dec, dec_ref, gs, tq)
    _store_head_lanes(a, a_ref, gs, tq)
    g_ref[...] = g.reshape(g_ref.shape)


def _pad_lora(w_a, w_b):
    r = w_a.shape[1]
    return (jnp.pad(w_a, ((0, 0), (0, LORA_PAD - r))).astype(BF16),
            jnp.pad(w_b, ((0, LORA_PAD - r), (0, 0))).astype(BF16))


def _lora(xs, w0, w1, w2, a0, a1, a2, g1, g2, tl):
    d = w0.shape[0]
    w1p, w2p = _pad_lora(w1, _head_minor(w2))
    a1p, a2p = _pad_lora(a1, _head_minor(a2))
    g1b, g2b = g1.astype(BF16), _head_minor(g2).astype(BF16)
    if tl.flat:
        slab = lambda s: pl.BlockSpec((None, tl.seqs * tl.tq, d), lambda ig, it: (s, ig, 0))
        g_shape = (tl.groups * GROUP * tl.t, d)
    else:
        slab = lambda s: pl.BlockSpec((None, tl.seqs, tl.tq, d), lambda ig, it: (s, ig, it, 0))
        g_shape = (tl.groups * GROUP, tl.t, d)
    hl_shape = jax.ShapeDtypeStruct((tl.groups, HEAD_DIM, tl.t, LANES), F32)
    consts = [w1p, w2p, a1p, a2p, g1b, g2b, _head_minor(w0).reshape(1, d),
              _head_minor(a0).reshape(1, d)]
    return pl.pallas_call(
        functools.partial(_lora_kernel, gs=tl.gs, tq=tl.tq),
        grid=tl.grid,
        in_specs=[slab(SLAB_W), slab(SLAB_A), slab(SLAB_G)] + [_const_spec(c.shape) for c in consts],
        out_specs=[tl.head_lanes(), tl.head_lanes(), tl.plain(d)],
        out_shape=[hl_shape, hl_shape, jax.ShapeDtypeStruct(g_shape, F32)],
        compiler_params=_cparams(2),
        name="lora",
    )(xs, xs, xs, *consts)


def _wkv_step(st, sa, v, w_row, b_row, k2_row, r_row, kk_next_row):
    y = jnp.zeros((HEAD_DIM, LANES), F32)
    sa_next = jnp.zeros((HEAD_DIM, LANES), F32)
    for i in range(HEAD_DIM):
        si = st[i] * w_row(i) - sa * b_row(i) + v * k2_row(i)
        st[i] = si
        y = y + si * r_row(i)
        sa_next = sa_next + si * kk_next_row(i)
    return y, sa_next


def _wkv_kernel(r_ref, k_ref, v_ref, a_ref, d_ref, kk_p, ka_p, rk_p, lg_p, lb_p, s0_ref,
                y_ref, so_ref, st, kk_s, b_s, k2_s, v_t, y_t, *, nb, t0):
    n = HEAD_DIM
    tt = nb * SUBLANES
    bc = lambda p: p[...][:, None, None, :]

    @pl.when(pl.program_id(1) == 0)
    def _():
        s = s0_ref[...].reshape(LANES, n, n)
        st[...] = jnp.swapaxes(jnp.swapaxes(jnp.swapaxes(s, 0, 1), 1, 2), 0, 1)

    k = k_ref[...]
    a = a_ref[...]
    kk = k * bc(kk_p)
    kk = kk / jnp.maximum(jnp.sqrt(jnp.sum(kk * kk, axis=0, keepdims=True)), 1e-12)
    kk_s[...] = kk
    b_s[...] = kk * a
    k2_s[...] = k * (1.0 + (a - 1.0) * bc(ka_p))
    v_t[...] = jnp.swapaxes(v_ref[...], 0, 1)
    if t0:
        y_t[0:t0] = jnp.zeros((t0, n, LANES), F32)

    sa0 = jnp.zeros((n, LANES), F32)
    for i in range(n):
        sa0 = sa0 + st[i] * kk_s[i, 0, t0:t0 + 1, :]

    def block(tb, sa):
        tb_next = jnp.minimum(tb + 1, nb - 1)
        for j in range(t0, SUBLANES):
            t = tb * SUBLANES + j
            row = lambda ref: (lambda i: ref[i, tb, j:j + 1, :])
            if j + 1 < SUBLANES:
                kk_next = lambda i: kk_s[i, tb, j + 1:j + 2, :]
            else:
                kk_next = lambda i: kk_s[i, tb_next, 0:1, :]
            y_t[t], sa = _wkv_step(st, sa, v_t[t], row(d_ref), row(b_s), row(k2_s), row(r_ref),
                                   kk_next)
        return sa

    lax.fori_loop(0, nb, block, sa0)

    y = jnp.swapaxes(y_t[...], 0, 1)
    ym = jnp.mean(y, axis=0, keepdims=True)
    yc = y - ym
    yv = jnp.mean(yc * yc, axis=0, keepdims=True)
    bcast = lambda p: p[...][:, None, :]
    yn = yc * lax.rsqrt(yv + GN_EPS) * bcast(lg_p) + bcast(lb_p)
    rk = (r_ref[...] * k2_s[...] * bc(rk_p)).reshape(n, tt, LANES)
    y_ref[...] = yn + jnp.sum(rk, axis=0, keepdims=True) * v_ref[...]

    @pl.when(pl.program_id(1) == pl.num_programs(1) - 1)
    def _():
        s = jnp.swapaxes(jnp.swapaxes(jnp.swapaxes(st[...], 0, 1), 1, 2), 0, 1)
        so_ref[...] = s.reshape(so_ref.shape)


def _head_lanes(p):
    hn = p.reshape(-1, HEAD_DIM).T
    return jnp.tile(hn, (1, GROUP))


def _wkv(rkv, a, dec, s0, k_k, k_a, r_k, lnx_g, lnx_b, *, tt, t0):
    g, _, t, lanes = a.shape
    n = HEAD_DIM
    heads = s0.shape[1]
    nb = tt // SUBLANES
    assert t0 == 0 or nb == 1
    by8 = lambda z: z.reshape(z.shape[0], z.shape[1], t // SUBLANES, SUBLANES, lanes)
    params = [_head_lanes(p) for p in (k_k, k_a, r_k.reshape(-1), lnx_g, lnx_b)]
    rows = lambda blk: pl.BlockSpec((None, n, nb, SUBLANES, lanes), lambda i, j: (i, blk, j, 0, 0))
    tile = lambda blk: pl.BlockSpec((None, n, tt, lanes), lambda i, j: (i, blk, j, 0))
    st = pl.BlockSpec((GROUP, heads, n, n), lambda i, j: (i, 0, 0, 0))
    return pl.pallas_call(
        functools.partial(_wkv_kernel, nb=nb, t0=t0),
        grid=(g, t // tt),
        in_specs=[rows(0), rows(1), tile(2), rows(0), rows(0)] + [_const_spec((n, lanes))] * 5 + [st],
        out_specs=[tile(0), st],
        out_shape=[jax.ShapeDtypeStruct((g, n, t, lanes), F32),
                   jax.ShapeDtypeStruct(s0.shape, F32)],
        scratch_shapes=[pltpu.VMEM((n, n, lanes), F32)]
        + [pltpu.VMEM((n, nb, SUBLANES, lanes), F32)] * 3
        + [pltpu.VMEM((tt, n, lanes), F32)] * 2,
        compiler_params=_cparams(2),
        name="wkv7",
    )(by8(rkv), by8(rkv), rkv, by8(a), by8(dec), *params, s0)


CONV_PAD = 32
CONV_ROWS = 32


def _conv_kernel(u1_ref, u2_ref, bg1, bg2, dwk, dwb, lg, lb, st0_ref, y_ref, sto_ref, zbuf, cbuf,
                 zs, *, tm, off, bs):
    t = pl.program_id(1)
    hist = CONV_W - 1
    lo = CONV_PAD - hist
    d = u1_ref.shape[-1]
    rc = min(CONV_ROWS, tm)

    def one(s):
        glu = (u1_ref[s] + bg1[...]) * _sigmoid(u2_ref[s] + bg2[...])
        zbuf[CONV_PAD:CONV_PAD + tm, :] = glu

        @pl.when(t == 0)
        def _():
            if off:
                zbuf[lo:lo + off, :] = jnp.zeros((off, d), F32)
            zbuf[lo + off:CONV_PAD + off, :] = st0_ref[s]

        for b in range(SUBLANES):
            span = tm + SUBLANES * ((hist - b) // SUBLANES)
            zs[b, 0:span, :] = zbuf[lo + b:lo + b + span, :]
        for r0 in range(0, tm, rc):
            for c0 in range(0, d, LANES):
                acc = jnp.zeros((rc, LANES), F32)
                for j in range(CONV_W):
                    a8 = SUBLANES * (j // SUBLANES)
                    acc = acc + (zs[j % SUBLANES, a8 + r0:a8 + r0 + rc, c0:c0 + LANES]
                                 * dwk[j:j + 1, c0:c0 + LANES])
                cbuf[r0:r0 + rc, c0:c0 + LANES] = acc
        zc = cbuf[...] + dwb[...]
        m = jnp.mean(zc, axis=-1, keepdims=True)
        ctr = zc - m
        var = jnp.mean(ctr * ctr, axis=-1, keepdims=True)
        ln = ctr * lax.rsqrt(var + 1e-5) * lg[...] + lb[...]
        y_ref[s] = _silu(ln).astype(BF16)
        carry = zbuf[lo + tm:CONV_PAD + tm, :]
        zbuf[lo:CONV_PAD, :] = carry
        sto_ref[s] = carry

    _per_sequence(bs, one)


def _conv(glu_in, b_glu, dw_k, dw_b, ln_g, ln_b, st0, *, tm, off, bs):
    b, t, _ = glu_in.shape
    d = dw_k.shape[1]
    hist = CONV_W - 1
    assert bs == 1 or t == tm
    vec = _const_spec((1, d))
    return pl.pallas_call(
        functools.partial(_conv_kernel, tm=tm, off=off, bs=bs),
        grid=(b // bs, t // tm),
        in_specs=[pl.BlockSpec((bs, tm, d), lambda i, j: (i, j, 0)),
                  pl.BlockSpec((bs, tm, d), lambda i, j: (i, j, 1)),
                  pl.BlockSpec((1, d), lambda i, j: (0, 0)),
                  pl.BlockSpec((1, d), lambda i, j: (0, 1)),
                  _const_spec((CONV_W, d)),
                  vec, vec, vec,
                  pl.BlockSpec((bs, hist, d), lambda i, j: (i, 0, 0))],
        out_specs=[pl.BlockSpec((bs, tm, d), lambda i, j: (i, j, 0)),
                   pl.BlockSpec((bs, hist, d), lambda i, j: (i, 0, 0))],
        out_shape=[jax.ShapeDtypeStruct((b, t, d), BF16),
                   jax.ShapeDtypeStruct((b, hist, d), F32)],
        scratch_shapes=[pltpu.VMEM((CONV_PAD + tm, d), F32), pltpu.VMEM((tm, d), F32),
                        pltpu.VMEM((SUBLANES, tm + CONV_PAD - SUBLANES, d), F32)],
        compiler_params=_cparams(2),
        name="conformer_conv",
    )(glu_in, glu_in, b_glu.reshape(1, 2 * d), b_glu.reshape(1, 2 * d), dw_k, dw_b.reshape(1, d),
      ln_g.reshape(1, d), ln_b.reshape(1, d), st0)


def _merge_kernel(yw_ref, g_ref, ga_ref, gb_ref, yb_ref, x_ref, woa_ref, wob_ref, m_ref, n2_ref,
                  x1_ref, h2_ref, ybuf, *, gs, tq):
    seqs = gs * GROUP
    d = ybuf.shape[-1]
    lane = lax.broadcasted_iota(jnp.int32, (tq, LANES), 1)
    for g in range(gs):
        for q in range(d // LANES):
            out = _swap_lane_groups([yw_ref[g, GROUP * q + j] for j in range(GROUP)], lane)
            for s in range(GROUP):
                ybuf[g * GROUP + s, :, q * LANES:(q + 1) * LANES] = out[s]
    seq3 = lambda ref: ref[...].astype(F32).reshape(seqs, tq, d)
    m = m_ref[...]
    dot = lambda z, w_ref: jnp.dot(z.reshape(seqs * tq, d).astype(BF16), w_ref[...],
                                   preferred_element_type=F32)
    branch_a = seq3(ga_ref) * (ybuf[...] * seq3(g_ref))
    branch_b = seq3(gb_ref) * seq3(yb_ref)
    mix = (dot(branch_a, woa_ref) + dot(branch_b, wob_ref)).reshape(seqs, tq, d)
    x1 = seq3(x_ref) + m[:, 2:3, :] * mix
    x1_ref[...] = x1.reshape(x1_ref.shape)
    ms = jnp.mean(x1 * x1, axis=-1, keepdims=True)
    h2 = (x1 * lax.rsqrt(ms + 1e-6) * n2_ref[...]) * (1.0 + m[:, 4:5, :]) + m[:, 3:4, :]
    h2_ref[...] = h2.reshape(seqs * tq, d).astype(BF16).reshape(h2_ref.shape)


def _merge(yw, g, ga, gb, yb, x, w_out, mod, norm_g, tl):
    d = x.shape[-1]
    plain = tl.plain(d)
    w_out_b = w_out.astype(BF16)
    w_out_a = _head_minor(w_out.T).T.astype(BF16)
    return pl.pallas_call(
        functools.partial(_merge_kernel, gs=tl.gs, tq=tl.tq),
        grid=tl.grid,
        in_specs=[tl.head_lanes(), plain, plain, plain, plain, plain, _const_spec((d, d)),
                  _const_spec((d, d)), tl.per_seq(N_MOD, d), _const_spec((1, d))],
        out_specs=[plain, plain],
        out_shape=[jax.ShapeDtypeStruct(x.shape, F32), jax.ShapeDtypeStruct(x.shape, BF16)],
        scratch_shapes=[pltpu.VMEM((tl.seqs, tl.tq, d), F32)],
        compiler_params=_cparams(2),
        name="merge_out_proj",
    )(yw, g, ga, gb, yb, x, w_out_a, w_out_b, mod, norm_g.reshape(1, d))


FFN_TN = 512


def _pad_halves(w, width):
    f = w.shape[-1] // 2
    pad = [(0, 0)] * (w.ndim - 1) + [(0, width - f)]
    return jnp.concatenate([jnp.pad(w[..., :f], pad), jnp.pad(w[..., f:], pad)], axis=-1)


def _up_act_kernel(h_ref, wg_ref, wv_ref, kg_ref, kv_ref, bg_ref, bv_ref, sg_ref, sv_ref,
                   act_ref, stg_ref, stv_ref, carry, *, nseq, tq, off, tiles_per_seq):
    first = pl.program_id(1) % tiles_per_seq == 0
    h = h_ref[...]
    tn = wg_ref.shape[-1]
    row = lax.broadcasted_iota(jnp.int32, (nseq, tq, tn), 1)
    z = []
    halves = ((wg_ref, kg_ref, bg_ref, stg_ref), (wv_ref, kv_ref, bv_ref, stv_ref))

    @pl.when(first)
    def _():
        carry[0] = sg_ref[...]
        carry[1] = sv_ref[...]

    for half, (w_ref, k_ref, b_ref, st_ref) in enumerate(halves):
        u = jnp.dot(h, w_ref[...], preferred_element_type=F32).reshape(nseq, tq, tn)
        prev = carry[half]
        p0, p1 = prev[:, 0:1, :], prev[:, 1:2, :]
        u1 = jnp.where(row == off, p1, pltpu.roll(u, 1, axis=1))
        u2 = jnp.where(row == off, p0, jnp.where(row == off + 1, p1, pltpu.roll(u, 2, axis=1)))
        z.append(b_ref[...] + u2 * k_ref[0:1, :] + u1 * k_ref[1:2, :] + u * k_ref[2:3, :])
        last = u[:, tq - (FFN_CONV_W - 1):, :]
        carry[half] = last
        st_ref[...] = last
    act_ref[...] = (_silu(z[0]) * z[1]).reshape(nseq * tq, tn).astype(BF16)


def _up_act(h2, w_up, ffn_dw_k, ffn_dw_b, st0, *, nseq, tq, off, t):
    rows, k = h2.shape
    b = st0.shape[0]
    f = w_up.shape[1] // 2
    nj = f // FFN_TN
    hist = FFN_CONV_W - 1
    tiles_per_seq = t // tq
    assert nseq == 1 or tiles_per_seq == 1
    seq = lambda i: i // tiles_per_seq
    col = lambda shape, h: pl.BlockSpec(shape, lambda j, i: (0, j + h * nj))
    state = lambda h: pl.BlockSpec((nseq, hist, FFN_TN), lambda j, i: (seq(i), 0, j + h * nj))
    return pl.pallas_call(
        functools.partial(_up_act_kernel, nseq=nseq, tq=tq, off=off, tiles_per_seq=tiles_per_seq),
        grid=(nj, rows // (nseq * tq)),
        in_specs=[pl.BlockSpec((nseq * tq, k), lambda j, i: (i, 0)),
                  col((k, FFN_TN), 0), col((k, FFN_TN), 1),
                  col((FFN_CONV_W, FFN_TN), 0), col((FFN_CONV_W, FFN_TN), 1),
                  col((1, FFN_TN), 0), col((1, FFN_TN), 1),
                  state(0), state(1)],
        out_specs=[pl.BlockSpec((nseq * tq, FFN_TN), lambda j, i: (i, j)), state(0), state(0)],
        out_shape=[jax.ShapeDtypeStruct((rows, f), BF16),
                   jax.ShapeDtypeStruct((b, hist, f), F32),
                   jax.ShapeDtypeStruct((b, hist, f), F32)],
        scratch_shapes=[pltpu.VMEM((2, nseq, hist, FFN_TN), F32)],
        compiler_params=_cparams(2),
        name="up_conv_act",
    )(h2, w_up, w_up, ffn_dw_k, ffn_dw_k, ffn_dw_b, ffn_dw_b, st0, st0)


def _norm_kernel(x_ref, g_ref, o_ref):
    x = x_ref[...]
    ms = jnp.mean(x * x, axis=-1, keepdims=True)
    o_ref[...] = x * lax.rsqrt(ms + 1e-6) * g_ref[...]


def _final_norm(x, g, *, tm):
    rows, d = x.shape
    row = pl.BlockSpec((tm, d), lambda i: (i, 0))
    return pl.pallas_call(
        _norm_kernel,
        grid=(rows // tm,),
        in_specs=[row, _const_spec((1, d))],
        out_specs=row,
        out_shape=jax.ShapeDtypeStruct((rows, d), F32),
        compiler_params=_cparams(1),
        name="final_norm",
    )(x, g.reshape(1, d))


def _prep_tm_kernel(x_ref, m_ref, g_ref, mu_ref, s0_ref, xs_ref, so_ref, hprev):
    @pl.when(pl.program_id(0) == 0)
    def _():
        hprev[...] = s0_ref[...]

    x = x_ref[...]
    ms = jnp.mean(x * x, axis=-1, keepdims=True)
    h = (x * lax.rsqrt(ms + 1e-6) * g_ref[...]) * (1.0 + m_ref[1]) + m_ref[0]
    dx = hprev[...] - h
    for mi, slab in enumerate(MU_SLABS):
        xs_ref[slab] = (h + dx * mu_ref[mi:mi + 1, :]).astype(BF16)
    xs_ref[SLAB_H] = h.astype(BF16)
    hprev[...] = h
    so_ref[...] = h


def _prep_tm(x, mod, norm_g, mu, shift0):
    t, b, d = x.shape
    return pl.pallas_call(
        _prep_tm_kernel,
        grid=(t,),
        in_specs=[pl.BlockSpec((None, b, d), lambda i: (i, 0, 0)), _const_spec((N_MOD, b, d)),
                  _const_spec((1, d)), _const_spec((N_MOD, d)), _const_spec((b, d))],
        out_specs=[pl.BlockSpec((7, b, d), lambda i: (0, i, 0)),
                   pl.BlockSpec((b, d), lambda i: (0, 0))],
        out_shape=[jax.ShapeDtypeStruct((7, t * b, d), BF16), jax.ShapeDtypeStruct((b, d), F32)],
        scratch_shapes=[pltpu.VMEM((b, d), F32)],
        compiler_params=_cparams(1),
        name="prep_tm",
    )(x, mod, norm_g.reshape(1, d), mu, shift0)


def _proj_t_kernel(a_ref, w_ref, o_ref):
    acc = jnp.dot(a_ref[...], w_ref[...].astype(BF16), preferred_element_type=F32)
    o_ref[...] = acc.T


def _proj_transposed(xs, w, *, tn, n):
    _, rows, k = xs.shape
    d = n // 3
    return pl.pallas_call(
        _proj_t_kernel,
        grid=(n // tn,),
        in_specs=[pl.BlockSpec((None, rows, k), lambda j: (j * tn // d, 0, 0)),
                  pl.BlockSpec((k, tn), lambda j: (0, j))],
        out_specs=pl.BlockSpec((tn, rows), lambda j: (j, 0)),
        out_shape=jax.ShapeDtypeStruct((n, rows), F32),
        compiler_params=_cparams(1),
        name="rkv_proj_tm",
    )(xs, w)


def _lora_t_kernel(*refs):
    dec_ref, a_ref, g_ref = refs[-3:]
    dec, a, g = _lora_heads(*refs[:-3])
    dec_ref[...] = dec.T
    a_ref[...] = a.T
    g_ref[...] = g


def _lora_transposed(xs, w0, w1, w2, a0, a1, a2, g1, g2):
    _, rows, d = xs.shape
    w1p, w2p = _pad_lora(w1, w2)
    a1p, a2p = _pad_lora(a1, a2)
    consts = [w1p, w2p, a1p, a2p, g1.astype(BF16), g2.astype(BF16), w0.reshape(1, d),
              a0.reshape(1, d)]
    slab = lambda s: pl.BlockSpec((None, rows, d), lambda i: (s, 0, 0))
    t_shape = jax.ShapeDtypeStruct((d, rows), F32)
    return pl.pallas_call(
        _lora_t_kernel,
        grid=(1,),
        in_specs=[slab(SLAB_W), slab(SLAB_A), slab(SLAB_G)] + [_const_spec(c.shape) for c in consts],
        out_specs=[pl.BlockSpec((d, rows), lambda i: (0, 0)), pl.BlockSpec((d, rows), lambda i: (0, 0)),
                   pl.BlockSpec((rows, d), lambda i: (0, 0))],
        out_shape=[t_shape, t_shape, jax.ShapeDtypeStruct((rows, d), F32)],
        compiler_params=_cparams(1),
        name="lora_tm",
    )(xs, xs, xs, *consts)


def _wkv_bl_kernel(r_ref, k_ref, v_ref, a_ref, d_ref, kk_p, ka_p, rk_p, lg_p, lb_p, s0_ref,
                   y_ref, so_ref, st, kk_s, b_s, k2_s, *, steps):
    n = HEAD_DIM
    tile = lambda ref, t: ref[:, t * LANES:(t + 1) * LANES]
    st[...] = jnp.swapaxes(s0_ref[...], 0, 1)
    bonus = []
    for t in range(steps):
        k = tile(k_ref, t)
        a = tile(a_ref, t)
        kk = k * kk_p[...]
        kk = kk / jnp.maximum(jnp.sqrt(jnp.sum(kk * kk, axis=0, keepdims=True)), 1e-12)
        k2 = k * (1.0 + (a - 1.0) * ka_p[...])
        kk_s[:, t * LANES:(t + 1) * LANES] = kk
        b_s[:, t * LANES:(t + 1) * LANES] = kk * a
        k2_s[:, t * LANES:(t + 1) * LANES] = k2
        bonus.append(jnp.sum(tile(r_ref, t) * k2 * rk_p[...], axis=0, keepdims=True))

    sa = jnp.zeros((n, LANES), F32)
    for i in range(n):
        sa = sa + st[i] * kk_s[i:i + 1, 0:LANES]
    for t in range(steps):
        t_next = min(t + 1, steps - 1)
        row = lambda ref, t=t: (lambda i: ref[i:i + 1, t * LANES:(t + 1) * LANES])
        v = tile(v_ref, t)
        y, sa = _wkv_step(st, sa, v, row(d_ref), row(b_s), row(k2_s), row(r_ref),
                          row(kk_s, t_next))
        ym = jnp.mean(y, axis=0, keepdims=True)
        yc = y - ym
        yv = jnp.mean(yc * yc, axis=0, keepdims=True)
        yn = yc * lax.rsqrt(yv + GN_EPS) * lg_p[...] + lb_p[...]
        y_ref[:, t * LANES:(t + 1) * LANES] = yn + bonus[t] * v
    so_ref[...] = jnp.swapaxes(st[...], 0, 1)


def _wkv_batch_lanes(rkv, a, dec, s0, k_k, k_a, r_k, lnx_g, lnx_b, *, steps):
    d, cols = a.shape
    heads, n = s0.shape[0], HEAD_DIM
    params = [jnp.broadcast_to(p.reshape(heads, n, 1), (heads, n, LANES))
              for p in (k_k, k_a, r_k, lnx_g, lnx_b)]
    chan = lambda c: pl.BlockSpec((n, cols), lambda h: (c * heads + h, 0))
    par = pl.BlockSpec((None, n, LANES), lambda h: (h, 0, 0))
    st = pl.BlockSpec((None, n, n, LANES), lambda h: (h, 0, 0, 0))
    return pl.pallas_call(
        functools.partial(_wkv_bl_kernel, steps=steps),
        grid=(heads,),
        in_specs=[chan(0), chan(1), chan(2), chan(0), chan(0)] + [par] * 5 + [st],
        out_specs=[chan(0), st],
        out_shape=[jax.ShapeDtypeStruct((d, cols), F32), jax.ShapeDtypeStruct(s0.shape, F32)],
        scratch_shapes=[pltpu.VMEM((n, n, LANES), F32)] + [pltpu.VMEM((n, cols), F32)] * 3,
        compiler_params=_cparams(1),
        name="wkv7_tm",
    )(rkv, rkv, rkv, a, dec, *params, s0)


CONV_TM_COLS = 512


def _conv_tm_kernel(u1_ref, u2_ref, bg1, bg2, dwk, dwb, lg, lb, st_ref, y_ref, sto_ref, zc,
                    *, steps, b):
    j = pl.program_id(0)
    hist = CONV_W - 1
    tc = u1_ref.shape[-1]
    for c0 in range(0, tc, LANES):
        cs = slice(c0, c0 + LANES)
        glu = [(u1_ref[t * b:(t + 1) * b, cs] + bg1[:, cs])
               * _sigmoid(u2_ref[t * b:(t + 1) * b, cs] + bg2[:, cs]) for t in range(steps)]
        window = lambda i: st_ref[i, :, cs] if i < hist else glu[i - hist]
        for t in range(steps):
            acc = jnp.zeros((b, LANES), F32)
            for tap in range(CONV_W):
                acc = acc + window(t + tap) * dwk[tap:tap + 1, cs]
            zc[j, t * b:(t + 1) * b, cs] = acc + dwb[:, cs]
        for i in range(hist):
            sto_ref[i, :, cs] = window(i + steps)

    @pl.when(j == pl.num_programs(0) - 1)
    def _():
        z = zc[...]
        m = jnp.mean(jnp.mean(z, axis=-1, keepdims=True), axis=0, keepdims=True)
        ctr = z - m
        var = jnp.mean(jnp.mean(ctr * ctr, axis=-1, keepdims=True), axis=0, keepdims=True)
        ln = ctr * lax.rsqrt(var + 1e-5)
        for jj in range(zc.shape[0]):
            cols = slice(jj * tc, (jj + 1) * tc)
            y_ref[:, cols] = _silu(ln[jj] * lg[:, cols] + lb[:, cols]).astype(BF16)


def _conv_tm(glu_in, b_glu, dw_k, dw_b, ln_g, ln_b, st0, *, steps):
    rows, _ = glu_in.shape
    hist, b, d = st0.shape
    tc = CONV_TM_COLS
    nct = d // tc
    col = lambda shape, h=0: pl.BlockSpec(shape, lambda j: (0, j + h * nct))
    state = pl.BlockSpec((hist, b, tc), lambda j: (0, 0, j))
    return pl.pallas_call(
        functools.partial(_conv_tm_kernel, steps=steps, b=b),
        grid=(nct,),
        in_specs=[col((rows, tc)), col((rows, tc), 1), col((1, tc)), col((1, tc), 1),
                  col((CONV_W, tc)), col((1, tc)), _const_spec((1, d)), _const_spec((1, d)), state],
        out_specs=[pl.BlockSpec((rows, d), lambda j: (0, 0)), state],
        out_shape=[jax.ShapeDtypeStruct((rows, d), BF16), jax.ShapeDtypeStruct(st0.shape, F32)],
        scratch_shapes=[pltpu.VMEM((nct, rows, tc), F32)],
        compiler_params=_cparams(1),
        name="conformer_conv_tm",
    )(glu_in, glu_in, b_glu.reshape(1, 2 * d), b_glu.reshape(1, 2 * d), dw_k, dw_b.reshape(1, d),
      ln_g.reshape(1, d), ln_b.reshape(1, d), st0)


def _merge_tm_kernel(yt_ref, g_ref, ga_ref, gb_ref, yb_ref, x_ref, wo_ref, m_ref, n2_ref,
                     x1_ref, h2_ref):
    f32 = lambda ref: ref[...].astype(F32)
    merged = f32(ga_ref) * (yt_ref[...].T * g_ref[...]) + f32(gb_ref) * f32(yb_ref)
    mix = jnp.dot(merged.astype(BF16), wo_ref[...], preferred_element_type=F32)
    x1 = x_ref[...] + m_ref[2] * mix
    x1_ref[...] = x1
    ms = jnp.mean(x1 * x1, axis=-1, keepdims=True)
    h2 = (x1 * lax.rsqrt(ms + 1e-6) * n2_ref[...]) * (1.0 + m_ref[4]) + m_ref[3]
    h2_ref[...] = h2.astype(BF16)


def _merge_tm(yt, g, ga, gb, yb, x, w_out, mod, norm_g, *, steps):
    rows, d = x.shape
    b = rows // steps
    row = pl.BlockSpec((b, d), lambda i: (i, 0))
    return pl.pallas_call(
        _merge_tm_kernel,
        grid=(steps,),
        in_specs=[pl.BlockSpec((d, b), lambda i: (0, i)), row, row, row, row, row,
                  _const_spec((d, d)), _const_spec((N_MOD, b, d)), _const_spec((1, d))],
        out_specs=[row, row],
        out_shape=[jax.ShapeDtypeStruct((rows, d), F32), jax.ShapeDtypeStruct((rows, d), BF16)],
        compiler_params=_cparams(1),
        name="merge_out_proj_tm",
    )(yt, g, ga, gb, yb, x, w_out, mod, norm_g.reshape(1, d))


def _up_act_tm_kernel(h_ref, wg_ref, wv_ref, kg_ref, kv_ref, bg_ref, bv_ref, sg_ref, sv_ref,
                      act_ref, stg_ref, stv_ref, *, steps, b):
    hist = FFN_CONV_W - 1
    h = h_ref[...]
    tn = wg_ref.shape[-1]
    z = []
    halves = ((wg_ref, kg_ref, bg_ref, sg_ref, stg_ref), (wv_ref, kv_ref, bv_ref, sv_ref, stv_ref))
    for w_ref, k_ref, b_ref, s_ref, st_ref in halves:
        u = jnp.dot(h, w_ref[...], preferred_element_type=F32).reshape(steps, b, tn)
        window = [s_ref[i] for i in range(hist)] + [u[t] for t in range(steps)]
        z.append(jnp.stack([b_ref[...] + sum(window[t + tap] * k_ref[tap:tap + 1, :]
                                             for tap in range(FFN_CONV_W))
                            for t in range(steps)]))
        for i in range(hist):
            st_ref[i] = window[steps + i]
    act_ref[...] = (_silu(z[0]) * z[1]).reshape(steps * b, tn).astype(BF16)


def _up_act_tm(h2, w_up, ffn_dw_k, ffn_dw_b, st0, *, steps):
    rows, k = h2.shape
    hist, b, _ = st0.shape
    f = w_up.shape[1] // 2
    nj = f // FFN_TN
    col = lambda shape, h: pl.BlockSpec(shape, lambda j: (0, j + h * nj))
    state = lambda h: pl.BlockSpec((hist, b, FFN_TN), lambda j: (0, 0, j + h * nj))
    return pl.pallas_call(
        functools.partial(_up_act_tm_kernel, steps=steps, b=b),
        grid=(nj,),
        in_specs=[_const_spec((rows, k)), col((k, FFN_TN), 0), col((k, FFN_TN), 1),
                  col((FFN_CONV_W, FFN_TN), 0), col((FFN_CONV_W, FFN_TN), 1),
                  col((1, FFN_TN), 0), col((1, FFN_TN), 1), state(0), state(1)],
        out_specs=[pl.BlockSpec((rows, FFN_TN), lambda j: (0, j)), state(0), state(0)],
        out_shape=[jax.ShapeDtypeStruct((rows, f), BF16),
                   jax.ShapeDtypeStruct((hist, b, f), F32),
                   jax.ShapeDtypeStruct((hist, b, f), F32)],
        compiler_params=_cparams(1),
        name="up_conv_act_tm",
    )(h2, w_up, w_up, ffn_dw_k, ffn_dw_k, ffn_dw_b, ffn_dw_b, st0, st0)


def _sample_layer(x, mod, shift0, wkv0, conv0, ffn0, p):
    b, t, d = x.shape
    assert b == LANES
    rows = t * b
    tm = lambda z: jnp.swapaxes(z, 0, 1)
    w_in = p["w_in"]
    x_tm = tm(x)
    mod_tm = tm(mod)
    xs, shift1 = _prep_tm(x_tm, mod_tm, p["norm1_g"], p["mu"], shift0)

    tn = 1024
    rkv = _proj_transposed(xs, w_in, tn=tn, n=3 * d)
    proj = functools.partial(_matmul, xs, w_in, tm=rows, tn=tn, slab=SLAB_H)
    glu_in = proj(col0=3 * d // tn, n=2 * d, name="glu_proj")
    ga = proj(col0=5 * d // tn, n=d, sigmoid=True, out_dtype=BF16, name="gate_a_proj")
    gb = proj(col0=6 * d // tn, n=d, sigmoid=True, out_dtype=BF16, name="gate_b_proj")
    dec, a, g = _lora_transposed(xs, p["w0"], p["w1"], p["w2"], p["a0"], p["a1"], p["a2"],
                                 p["g1"], p["g2"])

    yt, s1 = _wkv_batch_lanes(rkv, a, dec, wkv0.transpose(1, 2, 3, 0), p["k_k"], p["k_a"],
                              p["r_k"], p["lnx_g"], p["lnx_b"], steps=t)
    wkv1 = s1.transpose(3, 0, 1, 2)

    yb, conv1 = _conv_tm(glu_in, p["b_glu"], p["dw_k"], p["dw_b"], p["ln_conv_g"],
                         p["ln_conv_b"], tm(conv0), steps=t)
    x1, h2 = _merge_tm(yt, g, ga, gb, yb, x_tm.reshape(rows, d), p["w_out"].astype(BF16), mod_tm,
                       p["norm2_g"], steps=t)

    d_ff = p["w_down"].shape[0]
    f_pad = -(-d_ff // FFN_TN) * FFN_TN
    act, ffn_g, ffn_v = _up_act_tm(
        h2, _pad_halves(p["w_up"], f_pad).astype(BF16), _pad_halves(p["ffn_dw_k"], f_pad),
        _pad_halves(p["ffn_dw_b"], f_pad).reshape(1, 2 * f_pad), _pad_halves(tm(ffn0), f_pad),
        steps=t)
    ffn1 = tm(jnp.concatenate([ffn_g[..., :d_ff], ffn_v[..., :d_ff]], axis=-1))
    w_down = jnp.pad(p["w_down"], ((0, f_pad - d_ff), (0, 0))).astype(BF16)
    x2 = _matmul(act[None], w_down, tm=rows, tn=512, res=x1, gate=jnp.tile(mod_tm[5], (t, 1)),
                 name="down_proj")
    y = _final_norm(x2, p["normf_g"], tm=min(256, rows))
    return tm(y.reshape(t, b, d)), shift1, wkv1, tm(conv1), ffn1


def _head_minor(w):
    lead = w.shape[:-1]
    return w.reshape(*lead, -1, HEAD_DIM).swapaxes(-1, -2).reshape(*lead, w.shape[-1])


def _layer(x, mod, shift0, wkv0, conv0, ffn0, p, *, tm_seq, bs, off, flat, tq_proj, gs_proj,
           tq_tok, gs_tok, tm_mm, tt, nseq_ffn, tq_ffn):
    b, t, d = x.shape
    rows = b * t
    groups = b // GROUP
    w_in = p["w_in"]
    tok = (lambda z: z.reshape(rows, z.shape[-1])) if flat else (lambda z: z)

    xs, shift1 = _prep(x, mod, p["norm1_g"], p["mu"], shift0, tm=tm_seq, off=off, bs=bs)
    xs2 = xs.reshape(7, rows, d)
    xs_tok = xs2 if flat else xs

    w_hm = jnp.concatenate([_head_minor(w_in[:, i * d:(i + 1) * d]) for i in (0, 1, 2, 5)],
                           axis=1).astype(BF16)
    rkv = _proj_head_lanes(xs_tok, w_hm, _Tiling(groups, t, gs_proj, tq_proj, flat), tn=1024,
                           n=3 * d)
    tn = 1024
    ga = _matmul(xs2, w_hm, tm=tm_mm, tn=tn, slab=SLAB_H, col0=3 * d // tn, n=d, sigmoid=True,
                 out_dtype=BF16, name="gate_a_proj")
    glu_in = _matmul(xs2, w_in, tm=tm_mm, tn=tn, slab=SLAB_H, col0=3 * d // tn, n=2 * d,
                     name="glu_proj")
    gb = _matmul(xs2, w_in, tm=tm_mm, tn=tn, slab=SLAB_H, col0=6 * d // tn, n=d, sigmoid=True,
                 out_dtype=BF16, name="gate_b_proj")

    tl = _Tiling(groups, t, gs_tok, tq_tok, flat)
    dec, a, g = _lora(xs_tok, p["w0"], p["w1"], p["w2"], p["a0"], p["a1"], p["a2"], p["g1"], p["g2"],
                      tl)

    yw, wkv1 = _wkv(rkv, a, dec, wkv0, p["k_k"], p["k_a"], p["r_k"], p["lnx_g"], p["lnx_b"], tt=tt,
                    t0=off)

    yb, conv1 = _conv(glu_in.reshape(b, t, 2 * d), p["b_glu"], p["dw_k"], p["dw_b"],
                      p["ln_conv_g"], p["ln_conv_b"], conv0, tm=tm_seq, off=off, bs=bs)

    ga_t, gb_t = (z if flat else z.reshape(b, t, d) for z in (ga, gb))
    x1, h2 = _merge(yw, g, ga_t, gb_t, tok(yb), tok(x), p["w_out"], mod, p["norm2_g"], tl)

    d_ff = p["w_down"].shape[0]
    f_pad = -(-d_ff // FFN_TN) * FFN_TN
    act, ffn_g, ffn_v = _up_act(
        h2.reshape(rows, d), _pad_halves(p["w_up"], f_pad).astype(BF16),
        _pad_halves(p["ffn_dw_k"], f_pad), _pad_halves(p["ffn_dw_b"], f_pad).reshape(1, 2 * f_pad),
        _pad_halves(ffn0, f_pad), nseq=nseq_ffn, tq=tq_ffn, off=off, t=t)
    ffn1 = jnp.concatenate([ffn_g[..., :d_ff], ffn_v[..., :d_ff]], axis=-1)
    if off:
        gate2, rows_per_gate = jnp.repeat(mod[:, 5], t, axis=0), None
    else:
        gate2, rows_per_gate = mod[:, 5][:, None, :], t
    w_down = jnp.pad(p["w_down"], ((0, f_pad - d_ff), (0, 0))).astype(BF16)
    x2 = _matmul(act[None], w_down, tm=min(tm_mm, 512), tn=512, res=x1.reshape(rows, d), gate=gate2,
                 rows_per_gate=rows_per_gate, name="down_proj")
    y = _final_norm(x2, p["normf_g"], tm=min(256, rows))
    return y.reshape(b, t, d), shift1.reshape(b, d), wkv1, conv1, ffn1


def kernel(x_prompt, x_sample, state_shift, state_wkv, state_conv, state_ffn, c_prompt, c_sample,
           norm1_g, norm2_g, normf_g, w_ada, b_ada, mu, w_in, b_glu, w0, w1, w2, a0, a1, a2, g1, g2,
           k_k, k_a, r_k, lnx_g, lnx_b, dw_k, dw_b, ln_conv_g, ln_conv_b, w_out, w_up, ffn_dw_k,
           ffn_dw_b, w_down):
    p = dict(norm1_g=norm1_g, norm2_g=norm2_g, normf_g=normf_g, mu=mu, w_in=w_in, b_glu=b_glu,
             w0=w0, w1=w1, w2=w2, a0=a0, a1=a1, a2=a2, g1=g1, g2=g2, k_k=k_k, k_a=k_a, r_k=r_k,
             lnx_g=lnx_g, lnx_b=lnx_b, dw_k=dw_k, dw_b=dw_b, ln_conv_g=ln_conv_g,
             ln_conv_b=ln_conv_b, w_out=w_out, w_up=w_up, ffn_dw_k=ffn_dw_k, ffn_dw_b=ffn_dw_b,
             w_down=w_down)
    bp, tp, d = x_prompt.shape
    bs, ts, _ = x_sample.shape
    f2 = w_up.shape[1]

    c_all = jnp.concatenate([c_prompt, c_sample], axis=0)
    c_rows = -(-c_all.shape[0] // SUBLANES) * SUBLANES
    c_all = jnp.pad(c_all, ((0, c_rows - c_all.shape[0]), (0, 0)))
    mod = _modulation(c_all, w_ada, b_ada).reshape(c_rows, N_MOD, d)
    mod_p, mod_s = mod[:bp], mod[bp:bp + bs]

    y_p, shift_p, wkv_p, conv_p, ffn_p = _layer(
        x_prompt, mod_p, jnp.zeros((bp, d), F32),
        jnp.zeros((bp, d // HEAD_DIM, HEAD_DIM, HEAD_DIM), F32),
        jnp.zeros((bp, CONV_W - 1, d), F32), jnp.zeros((bp, FFN_CONV_W - 1, f2), F32), p,
        tm_seq=256, bs=1, off=0, flat=False, tq_proj=256, gs_proj=1, tq_tok=64, gs_tok=1,
        tm_mm=1024, tt=32, nseq_ffn=1, tq_ffn=1024)

    y_s, shift_s, wkv_s, conv_s, ffn_s = _sample_layer(
        x_sample, mod_s, state_shift, state_wkv, state_conv, state_ffn, p)
    return (y_p, y_s, shift_p, wkv_p, conv_p, ffn_p, shift_s, wkv_s, conv_s, ffn_s)
```

```python
import functools
from typing import NamedTuple

import jax
import jax.numpy as jnp
from jax import lax
from jax.experimental import pallas as pl
from jax.experimental.pallas import tpu as pltpu

F32 = jnp.float32
BF16 = jnp.bfloat16

HEAD_DIM = 64
CONV_W = 31
FFN_CONV_W = 3
N_MOD = 6
GN_EPS = HEAD_DIM * 1e-5
SUBLANES = 8
LANES = 128
MXU_COLS = 256
LORA_PAD = 128
SAMPLE_TILE = 8
GROUP = 4
V7X_VMEM_LIMIT = 56 * 1024 * 1024
WKV_VMEM_LIMIT = 60 * 1024 * 1024

SLAB_R, SLAB_K, SLAB_V, SLAB_H, SLAB_W, SLAB_A, SLAB_G = range(7)
MU_SLABS = (SLAB_R, SLAB_W, SLAB_K, SLAB_V, SLAB_A, SLAB_G)


class _Tiling(NamedTuple):
    groups: int
    t: int
    gs: int
    tq: int
    flat: bool

    @property
    def grid(self):
        return (self.groups // self.gs, self.t // self.tq)

    @property
    def seqs(self):
        return self.gs * GROUP

    def plain(self, cols, col=0):
        if self.flat:
            return pl.BlockSpec((self.seqs * self.tq, cols), lambda ig, it: (ig, col))
        return pl.BlockSpec((self.seqs, self.tq, cols), lambda ig, it: (ig, it, col))

    def head_lanes(self, blk=0):
        return pl.BlockSpec((self.gs, HEAD_DIM, self.tq, LANES), lambda ig, it: (ig, blk, it, 0))

    def per_seq(self, rows, cols):
        return pl.BlockSpec((self.seqs, rows, cols), lambda ig, it: (ig, 0, 0))


def _cparams(n_axes):
    return pltpu.CompilerParams(dimension_semantics=("arbitrary",) * n_axes,
                                vmem_limit_bytes=V7X_VMEM_LIMIT)


def _const_spec(shape):
    return pl.BlockSpec(shape, lambda *_: (0,) * len(shape), pipeline_mode=pl.Buffered(1))


def _sigmoid(x):
    return 1.0 / (1.0 + jnp.exp(-x))


def _silu(x):
    return x * _sigmoid(x)


def _softplus(x):
    return jnp.maximum(x, 0.0) + jnp.log(1.0 + jnp.exp(-jnp.abs(x)))


def _swap_lane_groups(m, lane):
    half = lane < 2 * (LANES // GROUP)
    odd = (lane // (LANES // GROUP)) % 2 == 1
    n = [None] * 4
    for s in range(2):
        n[s] = jnp.where(half, m[s], pltpu.roll(m[s + 2], LANES // 2, axis=1))
        n[s + 2] = jnp.where(half, pltpu.roll(m[s], LANES // 2, axis=1), m[s + 2])
    out = [None] * 4
    for p in (0, 2):
        a, b = n[p], n[p + 1]
        out[p] = jnp.where(odd, pltpu.roll(b, LANES // GROUP, axis=1), a)
        out[p + 1] = jnp.where(odd, b, pltpu.roll(a, LANES - LANES // GROUP, axis=1))
    return out


def _store_head_lanes(val, out_ref, gs, tq, q0=0):
    lane = lax.broadcasted_iota(jnp.int32, (tq, LANES), 1)
    for g in range(gs):
        for q in range(val.shape[1] // LANES):
            src = [val[(g * GROUP + s) * tq:(g * GROUP + s + 1) * tq, q * LANES:(q + 1) * LANES]
                   for s in range(GROUP)]
            out = _swap_lane_groups(src, lane)
            for j in range(GROUP):
                out_ref[g, GROUP * (q0 + q) + j] = out[j]


def _rows2d(ref):
    v = ref[...]
    return v.reshape(-1, v.shape[-1])


def _mod_kernel(c_ref, w_ref, b_ref, o_ref):
    s = _silu(c_ref[...]).astype(BF16)
    o_ref[...] = jnp.dot(s, w_ref[...].astype(BF16), preferred_element_type=F32) + b_ref[...]


def _modulation(c, w_ada, b_ada):
    rows, d = c.shape
    n = w_ada.shape[1]
    tn = 1024
    return pl.pallas_call(
        _mod_kernel,
        grid=(n // tn,),
        in_specs=[pl.BlockSpec((rows, d), lambda j: (0, 0)),
                  pl.BlockSpec((d, tn), lambda j: (0, j)),
                  pl.BlockSpec((1, tn), lambda j: (0, j))],
        out_specs=pl.BlockSpec((rows, tn), lambda j: (0, j)),
        out_shape=jax.ShapeDtypeStruct((rows, n), F32),
        compiler_params=_cparams(1),
        name="modulation",
    )(c, w_ada, b_ada.reshape(1, n))


def _per_sequence(bs, one):
    if bs == 1:
        one(0)
    else:
        lax.fori_loop(0, bs, lambda s, c: (one(s), c)[1], 0)


def _prep_kernel(x_ref, m_ref, g_ref, mu_ref, s0_ref, xs_ref, so_ref, hbuf, *, tm, off, bs):
    t = pl.program_id(1)

    def one(s):
        x = x_ref[s]
        m = m_ref[s]
        ms = jnp.mean(x * x, axis=-1, keepdims=True)
        h = (x * lax.rsqrt(ms + 1e-6) * g_ref[...]) * (1.0 + m[1:2, :]) + m[0:1, :]
        hbuf[SUBLANES:SUBLANES + tm, :] = h

        @pl.when(t == 0)
        def _():
            if off:
                hbuf[SUBLANES - 1:SUBLANES, :] = jnp.zeros((1, hbuf.shape[1]), F32)
            hbuf[SUBLANES - 1 + off:SUBLANES + off, :] = s0_ref[s]

        dx = hbuf[SUBLANES - 1:SUBLANES - 1 + tm, :] - h
        for mi, slab in enumerate(MU_SLABS):
            xs_ref[slab, s] = (h + dx * mu_ref[mi:mi + 1, :]).astype(BF16)
        xs_ref[SLAB_H, s] = h.astype(BF16)
        last = h[tm - 1:tm, :]
        hbuf[SUBLANES - 1:SUBLANES, :] = last
        so_ref[s] = last

    _per_sequence(bs, one)


def _prep(x, mod, norm_g, mu, shift0, *, tm, off, bs):
    b, t, d = x.shape
    assert bs == 1 or t == tm
    return pl.pallas_call(
        functools.partial(_prep_kernel, tm=tm, off=off, bs=bs),
        grid=(b // bs, t // tm),
        in_specs=[pl.BlockSpec((bs, tm, d), lambda i, j: (i, j, 0)),
                  pl.BlockSpec((bs, N_MOD, d), lambda i, j: (i, 0, 0)),
                  _const_spec((1, d)),
                  _const_spec((N_MOD, d)),
                  pl.BlockSpec((bs, 1, d), lambda i, j: (i, 0, 0))],
        out_specs=[pl.BlockSpec((7, bs, tm, d), lambda i, j: (0, i, j, 0)),
                   pl.BlockSpec((bs, 1, d), lambda i, j: (i, 0, 0))],
        out_shape=[jax.ShapeDtypeStruct((7, b, t, d), BF16),
                   jax.ShapeDtypeStruct((b, 1, d), F32)],
        scratch_shapes=[pltpu.VMEM((SUBLANES + tm, d), F32)],
        compiler_params=_cparams(2),
        name="prep",
    )(x, mod, norm_g.reshape(1, d), mu, shift0.reshape(b, 1, d))


def _gate_blocks(gate, rows, tm, rows_per_gate):
    if gate.ndim == 2:
        return gate.reshape(rows // tm, tm, gate.shape[-1]), 1
    assert rows_per_gate % tm == 0, (rows_per_gate, tm)
    return gate, rows_per_gate // tm


def _mm_kernel(*refs, gated, sigmoid):
    if gated:
        a_ref, w_ref, res_ref, gate_ref, o_ref, wb = refs
    else:
        a_ref, w_ref, o_ref, wb = refs

    @pl.when(pl.program_id(1) == 0)
    def _():
        wb[...] = w_ref[...].astype(BF16)

    acc = jnp.dot(a_ref[...], wb[...], preferred_element_type=F32)
    if gated:
        acc = res_ref[...] + gate_ref[...] * acc
    if sigmoid:
        acc = _sigmoid(acc)
    o_ref[...] = acc.astype(o_ref.dtype)


def _matmul(a, w, *, tm, tn, slab=0, col0=0, n=None, res=None, gate=None, rows_per_gate=None,
            sigmoid=False, out_dtype=F32, name):
    _, rows, k = a.shape
    n = w.shape[1] if n is None else n
    gated = res is not None
    in_specs = [pl.BlockSpec((None, tm, k), lambda j, i: (slab, i, 0)),
                pl.BlockSpec((k, tn), lambda j, i: (0, j + col0))]
    args = [a, w]
    if gated:
        gate, tiles_per_gate = _gate_blocks(gate, rows, tm, rows_per_gate)
        gr = gate.shape[1]
        in_specs += [pl.BlockSpec((tm, tn), lambda j, i: (i, j)),
                     pl.BlockSpec((None, gr, tn), lambda j, i: (i // tiles_per_gate, 0, j))]
        args += [res, gate]
    return pl.pallas_call(
        functools.partial(_mm_kernel, gated=gated, sigmoid=sigmoid),
        grid=(pl.cdiv(n, tn), rows // tm),
        in_specs=in_specs,
        out_specs=pl.BlockSpec((tm, tn), lambda j, i: (i, j)),
        out_shape=jax.ShapeDtypeStruct((rows, n), out_dtype),
        scratch_shapes=[pltpu.VMEM((k, tn), BF16)],
        compiler_params=_cparams(2),
        name=name,
    )(*args)


def _proj_hl_kernel(a_ref, w_ref, o_ref, *, gs, tq):
    a = _rows2d(a_ref)
    for c in range(0, w_ref.shape[1], MXU_COLS):
        acc = jnp.dot(a, w_ref[:, c:c + MXU_COLS], preferred_element_type=F32)
        _store_head_lanes(acc, o_ref, gs, tq, q0=c // LANES)


def _proj_head_lanes(xs, w, tl, *, tn, n):
    k = w.shape[0]
    d = n // 3
    slab = lambda j: j * tn // d
    if tl.flat:
        a_spec = pl.BlockSpec((None, tl.seqs * tl.tq, k), lambda j, ig, it: (slab(j), ig, 0))
    else:
        a_spec = pl.BlockSpec((None, tl.seqs, tl.tq, k), lambda j, ig, it: (slab(j), ig, it, 0))
    nq = tn // (LANES // GROUP)
    return pl.pallas_call(
        functools.partial(_proj_hl_kernel, gs=tl.gs, tq=tl.tq),
        grid=(n // tn,) + tl.grid,
        in_specs=[a_spec, pl.BlockSpec((k, tn), lambda j, ig, it: (0, j))],
        out_specs=pl.BlockSpec((tl.gs, nq, tl.tq, LANES), lambda j, ig, it: (ig, j, it, 0)),
        out_shape=jax.ShapeDtypeStruct((tl.groups, n // (LANES // GROUP), tl.t, LANES), F32),
        compiler_params=_cparams(3),
        name="rkv_proj",
    )(xs, w)


def _lora_heads(xw_ref, xa_ref, xg_ref, w1, w2, a1, a2, g1, g2, w0, a0):
    dot = functools.partial(jnp.dot, preferred_element_type=F32)
    hw = jnp.tanh(dot(_rows2d(xw_ref), w1[...])).astype(BF16)
    w_pre = w0[...] + dot(hw, w2[...])
    dec = jnp.exp(-jnp.exp(-_softplus(-w_pre) - 0.5))
    ha = dot(_rows2d(xa_ref), a1[...]).astype(BF16)
    a = _sigmoid(a0[...] + dot(ha, a2[...]))
    hg = _sigmoid(dot(_rows2d(xg_ref), g1[...])).astype(BF16)
    return dec, a, dot(hg, g2[...])


def _lora_kernel(*refs, gs, tq):
    dec_ref, a_ref, g_ref = refs[-3:]
    dec, a, g = _lora_heads(*refs[:-3])
    _store_head_lanes(dec, dec_ref, gs, tq)
    _store_head_lanes(a, a_ref, gs, tq)
    g_ref[...] = g.reshape(g_ref.shape)


def _pad_lora(w_a, w_b):
    r = w_a.shape[1]
    return (jnp.pad(w_a, ((0, 0), (0, LORA_PAD - r))).astype(BF16),
            jnp.pad(w_b, ((0, LORA_PAD - r), (0, 0))).astype(BF16))


def _lora(xs, w0, w1, w2, a0, a1, a2, g1, g2, tl):
    d = w0.shape[0]
    w1p, w2p = _pad_lora(w1, _head_minor(w2))
    a1p, a2p = _pad_lora(a1, _head_minor(a2))
    g1b, g2b = g1.astype(BF16), _head_minor(g2).astype(BF16)
    if tl.flat:
        slab = lambda s: pl.BlockSpec((None, tl.seqs * tl.tq, d), lambda ig, it: (s, ig, 0))
        g_shape = (tl.groups * GROUP * tl.t, d)
    else:
        slab = lambda s: pl.BlockSpec((None, tl.seqs, tl.tq, d), lambda ig, it: (s, ig, it, 0))
        g_shape = (tl.groups * GROUP, tl.t, d)
    hl_shape = jax.ShapeDtypeStruct((tl.groups, HEAD_DIM, tl.t, LANES), F32)
    consts = [w1p, w2p, a1p, a2p, g1b, g2b, _head_minor(w0).reshape(1, d),
              _head_minor(a0).reshape(1, d)]
    return pl.pallas_call(
        functools.partial(_lora_kernel, gs=tl.gs, tq=tl.tq),
        grid=tl.grid,
        in_specs=[slab(SLAB_W), slab(SLAB_A), slab(SLAB_G)] + [_const_spec(c.shape) for c in consts],
        out_specs=[tl.head_lanes(), tl.head_lanes(), tl.plain(d)],
        out_shape=[hl_shape, hl_shape, jax.ShapeDtypeStruct(g_shape, F32)],
        compiler_params=_cparams(2),
        name="lora",
    )(xs, xs, xs, *consts)


def _wkv_step(st, sa, v, w_row, b_row, k2_row, r_row, kk_next_row):
    y = jnp.zeros((HEAD_DIM, LANES), F32)
    sa_next = jnp.zeros((HEAD_DIM, LANES), F32)
    for i in range(HEAD_DIM):
        si = st[i] * w_row(i) - sa * b_row(i) + v * k2_row(i)
        st[i] = si
        y = y + si * r_row(i)
        sa_next = sa_next + si * kk_next_row(i)
    return y, sa_next


def _wkv_kernel(r_ref, k_ref, v_ref, a_ref, d_ref, kk_p, ka_p, rk_p, lg_p, lb_p, s0_ref,
                xa_ref, xw_ref, y_ref, so_ref, xo_ref, st, kk_s, b_s, k2_s, v_t, y_t, *, nb, t0):
    n = HEAD_DIM
    tt = nb * SUBLANES
    bc = lambda p: p[...][:, None, None, :]

    @pl.when(pl.program_id(1) == 0)
    def _():
        s = s0_ref[...].reshape(LANES, n, n)
        st[...] = jnp.swapaxes(jnp.swapaxes(jnp.swapaxes(s, 0, 1), 1, 2), 0, 1)

    k = k_ref[...]
    a = a_ref[...]
    kk = k * bc(kk_p)
    kk = kk / jnp.maximum(jnp.sqrt(jnp.sum(kk * kk, axis=0, keepdims=True)), 1e-12)
    kk_s[...] = kk
    b_s[...] = kk * a
    k2_s[...] = k * (1.0 + (a - 1.0) * bc(ka_p))
    v_t[...] = jnp.swapaxes(v_ref[...], 0, 1)
    if t0:
        y_t[0:t0] = jnp.zeros((t0, n, LANES), F32)

    sa0 = jnp.zeros((n, LANES), F32)
    for i in range(n):
        sa0 = sa0 + st[i] * kk_s[i, 0, t0:t0 + 1, :]

    side_rows = xa_ref.shape[0] // nb

    def block(tb, sa):
        tb_next = jnp.minimum(tb + 1, nb - 1)
        r0 = pl.multiple_of(tb * side_rows, side_rows)
        xo_ref[pl.ds(r0, side_rows), :] = jnp.dot(xa_ref[pl.ds(r0, side_rows), :], xw_ref[...],
                                                  preferred_element_type=F32)
        for j in range(t0, SUBLANES):
            t = tb * SUBLANES + j
            row = lambda ref: (lambda i: ref[i, tb, j:j + 1, :])
            if j + 1 < SUBLANES:
                kk_next = lambda i: kk_s[i, tb, j + 1:j + 2, :]
            else:
                kk_next = lambda i: kk_s[i, tb_next, 0:1, :]
            y_t[t], sa = _wkv_step(st, sa, v_t[t], row(d_ref), row(b_s), row(k2_s), row(r_ref),
                                   kk_next)
        return sa

    lax.fori_loop(0, nb, block, sa0)

    y = jnp.swapaxes(y_t[...], 0, 1)
    ym = jnp.mean(y, axis=0, keepdims=True)
    yc = y - ym
    yv = jnp.mean(yc * yc, axis=0, keepdims=True)
    bcast = lambda p: p[...][:, None, :]
    yn = yc * lax.rsqrt(yv + GN_EPS) * bcast(lg_p) + bcast(lb_p)
    rk = (r_ref[...] * k2_s[...] * bc(rk_p)).reshape(n, tt, LANES)
    y_ref[...] = yn + jnp.sum(rk, axis=0, keepdims=True) * v_ref[...]

    @pl.when(pl.program_id(1) == pl.num_programs(1) - 1)
    def _():
        s = jnp.swapaxes(jnp.swapaxes(jnp.swapaxes(st[...], 0, 1), 1, 2), 0, 1)
        so_ref[...] = s.reshape(so_ref.shape)


def _head_lanes(p):
    hn = p.reshape(-1, HEAD_DIM).T
    return jnp.tile(hn, (1, GROUP))


WKV_SIDE_TN = 1024


def _wkv(rkv, a, dec, s0, k_k, k_a, r_k, lnx_g, lnx_b, side_a, side_slab, side_w, *, tt, t0):
    g, _, t, lanes = a.shape
    n = HEAD_DIM
    heads = s0.shape[1]
    nb = tt // SUBLANES
    assert t0 == 0 or nb == 1
    steps = t // tt
    _, side_rows, side_k = side_a.shape
    side_n = side_w.shape[1]
    col_tiles = min(side_n // WKV_SIDE_TN, steps)
    assert g == 1 and steps % col_tiles == 0
    side_tn = side_n // col_tiles
    side_tm = side_rows // (steps // col_tiles)
    by8 = lambda z: z.reshape(z.shape[0], z.shape[1], t // SUBLANES, SUBLANES, lanes)
    params = [_head_lanes(p) for p in (k_k, k_a, r_k.reshape(-1), lnx_g, lnx_b)]
    rows = lambda blk: pl.BlockSpec((None, n, nb, SUBLANES, lanes), lambda i, j: (i, blk, j, 0, 0))
    tile = lambda blk: pl.BlockSpec((None, n, tt, lanes), lambda i, j: (i, blk, j, 0))
    st = pl.BlockSpec((GROUP, heads, n, n), lambda i, j: (i, 0, 0, 0))
    return pl.pallas_call(
        functools.partial(_wkv_kernel, nb=nb, t0=t0),
        grid=(g, t // tt),
        in_specs=[rows(0), rows(1), tile(2), rows(0), rows(0)] + [_const_spec((n, lanes))] * 5 + [st]
        + [pl.BlockSpec((None, side_tm, side_k), lambda i, j: (side_slab, j // col_tiles, 0),
                        pipeline_mode=pl.Buffered(1)),
           pl.BlockSpec((side_k, side_tn), lambda i, j: (0, j % col_tiles))],
        out_specs=[tile(0), st,
                   pl.BlockSpec((side_tm, side_tn), lambda i, j: (j // col_tiles, j % col_tiles))],
        out_shape=[jax.ShapeDtypeStruct((g, n, t, lanes), F32),
                   jax.ShapeDtypeStruct(s0.shape, F32),
                   jax.ShapeDtypeStruct((side_rows, side_n), F32)],
        scratch_shapes=[pltpu.VMEM((n, n, lanes), F32)]
        + [pltpu.VMEM((n, nb, SUBLANES, lanes), F32)] * 3
        + [pltpu.VMEM((tt, n, lanes), F32)] * 2,
        compiler_params=pltpu.CompilerParams(dimension_semantics=("arbitrary", "arbitrary"),
                                             vmem_limit_bytes=WKV_VMEM_LIMIT),
        name="wkv7",
    )(by8(rkv), by8(rkv), rkv, by8(a), by8(dec), *params, s0, side_a, side_w)


CONV_PAD = 32
CONV_ROWS = 32


def _conv_kernel(u1_ref, u2_ref, bg1, bg2, dwk, dwb, lg, lb, st0_ref, y_ref, sto_ref, zbuf, cbuf,
                 zs, *, tm, off, bs):
    t = pl.program_id(1)
    hist = CONV_W - 1
    lo = CONV_PAD - hist
    d = u1_ref.shape[-1]
    rc = min(CONV_ROWS, tm)

    def one(s):
        glu = (u1_ref[s] + bg1[...]) * _sigmoid(u2_ref[s] + bg2[...])
        zbuf[CONV_PAD:CONV_PAD + tm, :] = glu

        @pl.when(t == 0)
        def _():
            if off:
                zbuf[lo:lo + off, :] = jnp.zeros((off, d), F32)
            zbuf[lo + off:CONV_PAD + off, :] = st0_ref[s]

        for b in range(SUBLANES):
            span = tm + SUBLANES * ((hist - b) // SUBLANES)
            zs[b, 0:span, :] = zbuf[lo + b:lo + b + span, :]
        for r0 in range(0, tm, rc):
            for c0 in range(0, d, LANES):
                acc = jnp.zeros((rc, LANES), F32)
                for j in range(CONV_W):
                    a8 = SUBLANES * (j // SUBLANES)
                    acc = acc + (zs[j % SUBLANES, a8 + r0:a8 + r0 + rc, c0:c0 + LANES]
                                 * dwk[j:j + 1, c0:c0 + LANES])
                cbuf[r0:r0 + rc, c0:c0 + LANES] = acc
        zc = cbuf[...] + dwb[...]
        m = jnp.mean(zc, axis=-1, keepdims=True)
        ctr = zc - m
        var = jnp.mean(ctr * ctr, axis=-1, keepdims=True)
        ln = ctr * lax.rsqrt(var + 1e-5) * lg[...] + lb[...]
        y_ref[s] = _silu(ln).astype(BF16)
        carry = zbuf[lo + tm:CONV_PAD + tm, :]
        zbuf[lo:CONV_PAD, :] = carry
        sto_ref[s] = carry

    _per_sequence(bs, one)


def _conv(glu_in, b_glu, dw_k, dw_b, ln_g, ln_b, st0, *, tm, off, bs):
    b, t, _ = glu_in.shape
    d = dw_k.shape[1]
    hist = CONV_W - 1
    assert bs == 1 or t == tm
    vec = _const_spec((1, d))
    return pl.pallas_call(
        functools.partial(_conv_kernel, tm=tm, off=off, bs=bs),
        grid=(b // bs, t // tm),
        in_specs=[pl.BlockSpec((bs, tm, d), lambda i, j: (i, j, 0)),
                  pl.BlockSpec((bs, tm, d), lambda i, j: (i, j, 1)),
                  pl.BlockSpec((1, d), lambda i, j: (0, 0)),
                  pl.BlockSpec((1, d), lambda i, j: (0, 1)),
                  _const_spec((CONV_W, d)),
                  vec, vec, vec,
                  pl.BlockSpec((bs, hist, d), lambda i, j: (i, 0, 0))],
        out_specs=[pl.BlockSpec((bs, tm, d), lambda i, j: (i, j, 0)),
                   pl.BlockSpec((bs, hist, d), lambda i, j: (i, 0, 0))],
        out_shape=[jax.ShapeDtypeStruct((b, t, d), BF16),
                   jax.ShapeDtypeStruct((b, hist, d), F32)],
        scratch_shapes=[pltpu.VMEM((CONV_PAD + tm, d), F32), pltpu.VMEM((tm, d), F32),
                        pltpu.VMEM((SUBLANES, tm + CONV_PAD - SUBLANES, d), F32)],
        compiler_params=_cparams(2),
        name="conformer_conv",
    )(glu_in, glu_in, b_glu.reshape(1, 2 * d), b_glu.reshape(1, 2 * d), dw_k, dw_b.reshape(1, d),
      ln_g.reshape(1, d), ln_b.reshape(1, d), st0)


def _merge_kernel(yw_ref, g_ref, ga_ref, gb_ref, yb_ref, x_ref, woa_ref, wob_ref, m_ref, n2_ref,
                  x1_ref, h2_ref, ybuf, *, gs, tq):
    seqs = gs * GROUP
    d = ybuf.shape[-1]
    lane = lax.broadcasted_iota(jnp.int32, (tq, LANES), 1)
    for g in range(gs):
        for q in range(d // LANES):
            out = _swap_lane_groups([yw_ref[g, GROUP * q + j] for j in range(GROUP)], lane)
            for s in range(GROUP):
                ybuf[g * GROUP + s, :, q * LANES:(q + 1) * LANES] = out[s]
    seq3 = lambda ref: ref[...].astype(F32).reshape(seqs, tq, d)
    m = m_ref[...]
    dot = lambda z, w_ref: jnp.dot(z.reshape(seqs * tq, d).astype(BF16), w_ref[...],
                                   preferred_element_type=F32)
    branch_a = _sigmoid(seq3(ga_ref)) * (ybuf[...] * seq3(g_ref))
    branch_b = _sigmoid(seq3(gb_ref)) * seq3(yb_ref)
    mix = (dot(branch_a, woa_ref) + dot(branch_b, wob_ref)).reshape(seqs, tq, d)
    x1 = seq3(x_ref) + m[:, 2:3, :] * mix
    x1_ref[...] = x1.reshape(x1_ref.shape)
    ms = jnp.mean(x1 * x1, axis=-1, keepdims=True)
    h2 = (x1 * lax.rsqrt(ms + 1e-6) * n2_ref[...]) * (1.0 + m[:, 4:5, :]) + m[:, 3:4, :]
    h2_ref[...] = h2.reshape(seqs * tq, d).astype(BF16).reshape(h2_ref.shape)


def _merge(yw, g, rest, yb, x, w_out, mod, norm_g, tl):
    d = x.shape[-1]
    plain = tl.plain(d)
    w_out_b = w_out.astype(BF16)
    w_out_a = _head_minor(w_out.T).T.astype(BF16)
    return pl.pallas_call(
        functools.partial(_merge_kernel, gs=tl.gs, tq=tl.tq),
        grid=tl.grid,
        in_specs=[tl.head_lanes(), plain, tl.plain(d, 2), tl.plain(d, 3), plain, plain,
                  _const_spec((d, d)), _const_spec((d, d)), tl.per_seq(N_MOD, d), _const_spec((1, d))],
        out_specs=[plain, plain],
        out_shape=[jax.ShapeDtypeStruct(x.shape, F32), jax.ShapeDtypeStruct(x.shape, BF16)],
        scratch_shapes=[pltpu.VMEM((tl.seqs, tl.tq, d), F32)],
        compiler_params=_cparams(2),
        name="merge_out_proj",
    )(yw, g, rest, rest, yb, x, w_out_a, w_out_b, mod, norm_g.reshape(1, d))


FFN_TN = 512


def _pad_halves(w, width):
    f = w.shape[-1] // 2
    pad = [(0, 0)] * (w.ndim - 1) + [(0, width - f)]
    return jnp.concatenate([jnp.pad(w[..., :f], pad), jnp.pad(w[..., f:], pad)], axis=-1)


def _up_act_kernel(h_ref, wg_ref, wv_ref, kg_ref, kv_ref, bg_ref, bv_ref, sg_ref, sv_ref,
                   act_ref, stg_ref, stv_ref, carry, *, nseq, tq, off, tiles_per_seq):
    first = pl.program_id(1) % tiles_per_seq == 0
    h = h_ref[...]
    tn = wg_ref.shape[-1]
    row = lax.broadcasted_iota(jnp.int32, (nseq, tq, tn), 1)
    z = []
    halves = ((wg_ref, kg_ref, bg_ref, stg_ref), (wv_ref, kv_ref, bv_ref, stv_ref))

    @pl.when(first)
    def _():
        carry[0] = sg_ref[...]
        carry[1] = sv_ref[...]

    for half, (w_ref, k_ref, b_ref, st_ref) in enumerate(halves):
        u = jnp.dot(h, w_ref[...], preferred_element_type=F32).reshape(nseq, tq, tn)
        prev = carry[half]
        p0, p1 = prev[:, 0:1, :], prev[:, 1:2, :]
        u1 = jnp.where(row == off, p1, pltpu.roll(u, 1, axis=1))
        u2 = jnp.where(row == off, p0, jnp.where(row == off + 1, p1, pltpu.roll(u, 2, axis=1)))
        z.append(b_ref[...] + u2 * k_ref[0:1, :] + u1 * k_ref[1:2, :] + u * k_ref[2:3, :])
        last = u[:, tq - (FFN_CONV_W - 1):, :]
        carry[half] = last
        st_ref[...] = last
    act_ref[...] = (_silu(z[0]) * z[1]).reshape(nseq * tq, tn).astype(BF16)


def _up_act(h2, w_up, ffn_dw_k, ffn_dw_b, st0, *, nseq, tq, off, t):
    rows, k = h2.shape
    b = st0.shape[0]
    f = w_up.shape[1] // 2
    nj = f // FFN_TN
    hist = FFN_CONV_W - 1
    tiles_per_seq = t // tq
    assert nseq == 1 or tiles_per_seq == 1
    seq = lambda i: i // tiles_per_seq
    col = lambda shape, h: pl.BlockSpec(shape, lambda j, i: (0, j + h * nj))
    state = lambda h: pl.BlockSpec((nseq, hist, FFN_TN), lambda j, i: (seq(i), 0, j + h * nj))
    return pl.pallas_call(
        functools.partial(_up_act_kernel, nseq=nseq, tq=tq, off=off, tiles_per_seq=tiles_per_seq),
        grid=(nj, rows // (nseq * tq)),
        in_specs=[pl.BlockSpec((nseq * tq, k), lambda j, i: (i, 0)),
                  col((k, FFN_TN), 0), col((k, FFN_TN), 1),
                  col((FFN_CONV_W, FFN_TN), 0), col((FFN_CONV_W, FFN_TN), 1),
                  col((1, FFN_TN), 0), col((1, FFN_TN), 1),
                  state(0), state(1)],
        out_specs=[pl.BlockSpec((nseq * tq, FFN_TN), lambda j, i: (i, j)), state(0), state(0)],
        out_shape=[jax.ShapeDtypeStruct((rows, f), BF16),
                   jax.ShapeDtypeStruct((b, hist, f), F32),
                   jax.ShapeDtypeStruct((b, hist, f), F32)],
        scratch_shapes=[pltpu.VMEM((2, nseq, hist, FFN_TN), F32)],
        compiler_params=_cparams(2),
        name="up_conv_act",
    )(h2, w_up, w_up, ffn_dw_k, ffn_dw_k, ffn_dw_b, ffn_dw_b, st0, st0)


def _norm_kernel(x_ref, g_ref, o_ref):
    x = x_ref[...]
    ms = jnp.mean(x * x, axis=-1, keepdims=True)
    o_ref[...] = x * lax.rsqrt(ms + 1e-6) * g_ref[...]


def _final_norm(x, g, *, tm):
    rows, d = x.shape
    row = pl.BlockSpec((tm, d), lambda i: (i, 0))
    return pl.pallas_call(
        _norm_kernel,
        grid=(rows // tm,),
        in_specs=[row, _const_spec((1, d))],
        out_specs=row,
        out_shape=jax.ShapeDtypeStruct((rows, d), F32),
        compiler_params=_cparams(1),
        name="final_norm",
    )(x, g.reshape(1, d))


def _prep_tm_kernel(x_ref, m_ref, g_ref, mu_ref, s0_ref, xs_ref, so_ref, hprev):
    @pl.when(pl.program_id(0) == 0)
    def _():
        hprev[...] = s0_ref[...]

    x = x_ref[...]
    ms = jnp.mean(x * x, axis=-1, keepdims=True)
    h = (x * lax.rsqrt(ms + 1e-6) * g_ref[...]) * (1.0 + m_ref[1]) + m_ref[0]
    dx = hprev[...] - h
    for mi, slab in enumerate(MU_SLABS):
        xs_ref[slab] = (h + dx * mu_ref[mi:mi + 1, :]).astype(BF16)
    xs_ref[SLAB_H] = h.astype(BF16)
    hprev[...] = h
    so_ref[...] = h


def _prep_tm(x, mod, norm_g, mu, shift0):
    t, b, d = x.shape
    return pl.pallas_call(
        _prep_tm_kernel,
        grid=(t,),
        in_specs=[pl.BlockSpec((None, b, d), lambda i: (i, 0, 0)), _const_spec((N_MOD, b, d)),
                  _const_spec((1, d)), _const_spec((N_MOD, d)), _const_spec((b, d))],
        out_specs=[pl.BlockSpec((7, b, d), lambda i: (0, i, 0)),
                   pl.BlockSpec((b, d), lambda i: (0, 0))],
        out_shape=[jax.ShapeDtypeStruct((7, t * b, d), BF16), jax.ShapeDtypeStruct((b, d), F32)],
        scratch_shapes=[pltpu.VMEM((b, d), F32)],
        compiler_params=_cparams(1),
        name="prep_tm",
    )(x, mod, norm_g.reshape(1, d), mu, shift0)


def _proj_t_kernel(a_ref, w_ref, o_ref):
    acc = jnp.dot(a_ref[...], w_ref[...].astype(BF16), preferred_element_type=F32)
    o_ref[...] = acc.T


def _proj_transposed(xs, w, *, tn, n):
    _, rows, k = xs.shape
    d = n // 3
    return pl.pallas_call(
        _proj_t_kernel,
        grid=(n // tn,),
        in_specs=[pl.BlockSpec((None, rows, k), lambda j: (j * tn // d, 0, 0)),
                  pl.BlockSpec((k, tn), lambda j: (0, j))],
        out_specs=pl.BlockSpec((tn, rows), lambda j: (j, 0)),
        out_shape=jax.ShapeDtypeStruct((n, rows), F32),
        compiler_params=_cparams(1),
        name="rkv_proj_tm",
    )(xs, w)


def _lora_t_kernel(*refs):
    dec_ref, a_ref, g_ref = refs[-3:]
    dec, a, g = _lora_heads(*refs[:-3])
    dec_ref[...] = dec.T
    a_ref[...] = a.T
    g_ref[...] = g


def _lora_transposed(xs, w0, w1, w2, a0, a1, a2, g1, g2):
    _, rows, d = xs.shape
    w1p, w2p = _pad_lora(w1, w2)
    a1p, a2p = _pad_lora(a1, a2)
    consts = [w1p, w2p, a1p, a2p, g1.astype(BF16), g2.astype(BF16), w0.reshape(1, d),
              a0.reshape(1, d)]
    slab = lambda s: pl.BlockSpec((None, rows, d), lambda i: (s, 0, 0))
    t_shape = jax.ShapeDtypeStruct((d, rows), F32)
    return pl.pallas_call(
        _lora_t_kernel,
        grid=(1,),
        in_specs=[slab(SLAB_W), slab(SLAB_A), slab(SLAB_G)] + [_const_spec(c.shape) for c in consts],
        out_specs=[pl.BlockSpec((d, rows), lambda i: (0, 0)), pl.BlockSpec((d, rows), lambda i: (0, 0)),
                   pl.BlockSpec((rows, d), lambda i: (0, 0))],
        out_shape=[t_shape, t_shape, jax.ShapeDtypeStruct((rows, d), F32)],
        compiler_params=_cparams(1),
        name="lora_tm",
    )(xs, xs, xs, *consts)


def _wkv_bl_kernel(r_ref, k_ref, v_ref, a_ref, d_ref, kk_p, ka_p, rk_p, lg_p, lb_p, s0_ref,
                   y_ref, so_ref, st, kk_s, b_s, k2_s, *, steps):
    n = HEAD_DIM
    tile = lambda ref, t: ref[:, t * LANES:(t + 1) * LANES]
    st[...] = jnp.swapaxes(s0_ref[...], 0, 1)
    bonus = []
    for t in range(steps):
        k = tile(k_ref, t)
        a = tile(a_ref, t)
        kk = k * kk_p[...]
        kk = kk / jnp.maximum(jnp.sqrt(jnp.sum(kk * kk, axis=0, keepdims=True)), 1e-12)
        k2 = k * (1.0 + (a - 1.0) * ka_p[...])
        kk_s[:, t * LANES:(t + 1) * LANES] = kk
        b_s[:, t * LANES:(t + 1) * LANES] = kk * a
        k2_s[:, t * LANES:(t + 1) * LANES] = k2
        bonus.append(jnp.sum(tile(r_ref, t) * k2 * rk_p[...], axis=0, keepdims=True))

    sa = jnp.zeros((n, LANES), F32)
    for i in range(n):
        sa = sa + st[i] * kk_s[i:i + 1, 0:LANES]
    for t in range(steps):
        t_next = min(t + 1, steps - 1)
        row = lambda ref, t=t: (lambda i: ref[i:i + 1, t * LANES:(t + 1) * LANES])
        v = tile(v_ref, t)
        y, sa = _wkv_step(st, sa, v, row(d_ref), row(b_s), row(k2_s), row(r_ref),
                          row(kk_s, t_next))
        ym = jnp.mean(y, axis=0, keepdims=True)
        yc = y - ym
        yv = jnp.mean(yc * yc, axis=0, keepdims=True)
        yn = yc * lax.rsqrt(yv + GN_EPS) * lg_p[...] + lb_p[...]
        y_ref[:, t * LANES:(t + 1) * LANES] = yn + bonus[t] * v
    so_ref[...] = jnp.swapaxes(st[...], 0, 1)


def _wkv_batch_lanes(rkv, a, dec, s0, k_k, k_a, r_k, lnx_g, lnx_b, *, steps):
    d, cols = a.shape
    heads, n = s0.shape[0], HEAD_DIM
    params = [jnp.broadcast_to(p.reshape(heads, n, 1), (heads, n, LANES))
              for p in (k_k, k_a, r_k, lnx_g, lnx_b)]
    chan = lambda c: pl.BlockSpec((n, cols), lambda h: (c * heads + h, 0))
    par = pl.BlockSpec((None, n, LANES), lambda h: (h, 0, 0))
    st = pl.BlockSpec((None, n, n, LANES), lambda h: (h, 0, 0, 0))
    return pl.pallas_call(
        functools.partial(_wkv_bl_kernel, steps=steps),
        grid=(heads,),
        in_specs=[chan(0), chan(1), chan(2), chan(0), chan(0)] + [par] * 5 + [st],
        out_specs=[chan(0), st],
        out_shape=[jax.ShapeDtypeStruct((d, cols), F32), jax.ShapeDtypeStruct(s0.shape, F32)],
        scratch_shapes=[pltpu.VMEM((n, n, LANES), F32)] + [pltpu.VMEM((n, cols), F32)] * 3,
        compiler_params=_cparams(1),
        name="wkv7_tm",
    )(rkv, rkv, rkv, a, dec, *params, s0)


CONV_TM_COLS = 512


def _conv_tm_kernel(u1_ref, u2_ref, bg1, bg2, dwk, dwb, lg, lb, st_ref, y_ref, sto_ref, zc,
                    *, steps, b):
    j = pl.program_id(0)
    hist = CONV_W - 1
    tc = u1_ref.shape[-1]
    for c0 in range(0, tc, LANES):
        cs = slice(c0, c0 + LANES)
        glu = [(u1_ref[t * b:(t + 1) * b, cs] + bg1[:, cs])
               * _sigmoid(u2_ref[t * b:(t + 1) * b, cs] + bg2[:, cs]) for t in range(steps)]
        window = lambda i: st_ref[i, :, cs] if i < hist else glu[i - hist]
        for t in range(steps):
            acc = jnp.zeros((b, LANES), F32)
            for tap in range(CONV_W):
                acc = acc + window(t + tap) * dwk[tap:tap + 1, cs]
            zc[j, t * b:(t + 1) * b, cs] = acc + dwb[:, cs]
        for i in range(hist):
            sto_ref[i, :, cs] = window(i + steps)

    @pl.when(j == pl.num_programs(0) - 1)
    def _():
        z = zc[...]
        m = jnp.mean(jnp.mean(z, axis=-1, keepdims=True), axis=0, keepdims=True)
        ctr = z - m
        var = jnp.mean(jnp.mean(ctr * ctr, axis=-1, keepdims=True), axis=0, keepdims=True)
        ln = ctr * lax.rsqrt(var + 1e-5)
        for jj in range(zc.shape[0]):
            cols = slice(jj * tc, (jj + 1) * tc)
            y_ref[:, cols] = _silu(ln[jj] * lg[:, cols] + lb[:, cols]).astype(BF16)


def _conv_tm(glu_in, b_glu, dw_k, dw_b, ln_g, ln_b, st0, *, steps):
    rows, _ = glu_in.shape
    hist, b, d = st0.shape
    tc = CONV_TM_COLS
    nct = d // tc
    col = lambda shape, h=0: pl.BlockSpec(shape, lambda j: (0, j + h * nct))
    state = pl.BlockSpec((hist, b, tc), lambda j: (0, 0, j))
    return pl.pallas_call(
        functools.partial(_conv_tm_kernel, steps=steps, b=b),
        grid=(nct,),
        in_specs=[col((rows, tc)), col((rows, tc), 1), col((1, tc)), col((1, tc), 1),
                  col((CONV_W, tc)), col((1, tc)), _const_spec((1, d)), _const_spec((1, d)), state],
        out_specs=[pl.BlockSpec((rows, d), lambda j: (0, 0)), state],
        out_shape=[jax.ShapeDtypeStruct((rows, d), BF16), jax.ShapeDtypeStruct(st0.shape, F32)],
        scratch_shapes=[pltpu.VMEM((nct, rows, tc), F32)],
        compiler_params=_cparams(1),
        name="conformer_conv_tm",
    )(glu_in, glu_in, b_glu.reshape(1, 2 * d), b_glu.reshape(1, 2 * d), dw_k, dw_b.reshape(1, d),
      ln_g.reshape(1, d), ln_b.reshape(1, d), st0)


def _merge_tm_kernel(yt_ref, g_ref, ga_ref, gb_ref, yb_ref, x_ref, wo_ref, m_ref, n2_ref,
                     x1_ref, h2_ref):
    f32 = lambda ref: ref[...].astype(F32)
    merged = f32(ga_ref) * (yt_ref[...].T * g_ref[...]) + f32(gb_ref) * f32(yb_ref)
    mix = jnp.dot(merged.astype(BF16), wo_ref[...], preferred_element_type=F32)
    x1 = x_ref[...] + m_ref[2] * mix
    x1_ref[...] = x1
    ms = jnp.mean(x1 * x1, axis=-1, keepdims=True)
    h2 = (x1 * lax.rsqrt(ms + 1e-6) * n2_ref[...]) * (1.0 + m_ref[4]) + m_ref[3]
    h2_ref[...] = h2.astype(BF16)


def _merge_tm(yt, g, ga, gb, yb, x, w_out, mod, norm_g, *, steps):
    rows, d = x.shape
    b = rows // steps
    row = pl.BlockSpec((b, d), lambda i: (i, 0))
    return pl.pallas_call(
        _merge_tm_kernel,
        grid=(steps,),
        in_specs=[pl.BlockSpec((d, b), lambda i: (0, i)), row, row, row, row, row,
                  _const_spec((d, d)), _const_spec((N_MOD, b, d)), _const_spec((1, d))],
        out_specs=[row, row],
        out_shape=[jax.ShapeDtypeStruct((rows, d), F32), jax.ShapeDtypeStruct((rows, d), BF16)],
        compiler_params=_cparams(1),
        name="merge_out_proj_tm",
    )(yt, g, ga, gb, yb, x, w_out, mod, norm_g.reshape(1, d))


def _up_act_tm_kernel(h_ref, wg_ref, wv_ref, kg_ref, kv_ref, bg_ref, bv_ref, sg_ref, sv_ref,
                      act_ref, stg_ref, stv_ref, *, steps, b):
    hist = FFN_CONV_W - 1
    h = h_ref[...]
    tn = wg_ref.shape[-1]
    z = []
    halves = ((wg_ref, kg_ref, bg_ref, sg_ref, stg_ref), (wv_ref, kv_ref, bv_ref, sv_ref, stv_ref))
    for w_ref, k_ref, b_ref, s_ref, st_ref in halves:
        u = jnp.dot(h, w_ref[...], preferred_element_type=F32).reshape(steps, b, tn)
        window = [s_ref[i] for i in range(hist)] + [u[t] for t in range(steps)]
        z.append(jnp.stack([b_ref[...] + sum(window[t + tap] * k_ref[tap:tap + 1, :]
                                             for tap in range(FFN_CONV_W))
                            for t in range(steps)]))
        for i in range(hist):
            st_ref[i] = window[steps + i]
    act_ref[...] = (_silu(z[0]) * z[1]).reshape(steps * b, tn).astype(BF16)


def _up_act_tm(h2, w_up, ffn_dw_k, ffn_dw_b, st0, *, steps):
    rows, k = h2.shape
    hist, b, _ = st0.shape
    f = w_up.shape[1] // 2
    nj = f // FFN_TN
    col = lambda shape, h: pl.BlockSpec(shape, lambda j: (0, j + h * nj))
    state = lambda h: pl.BlockSpec((hist, b, FFN_TN), lambda j: (0, 0, j + h * nj))
    return pl.pallas_call(
        functools.partial(_up_act_tm_kernel, steps=steps, b=b),
        grid=(nj,),
        in_specs=[_const_spec((rows, k)), col((k, FFN_TN), 0), col((k, FFN_TN), 1),
                  col((FFN_CONV_W, FFN_TN), 0), col((FFN_CONV_W, FFN_TN), 1),
                  col((1, FFN_TN), 0), col((1, FFN_TN), 1), state(0), state(1)],
        out_specs=[pl.BlockSpec((rows, FFN_TN), lambda j: (0, j)), state(0), state(0)],
        out_shape=[jax.ShapeDtypeStruct((rows, f), BF16),
                   jax.ShapeDtypeStruct((hist, b, f), F32),
                   jax.ShapeDtypeStruct((hist, b, f), F32)],
        compiler_params=_cparams(1),
        name="up_conv_act_tm",
    )(h2, w_up, w_up, ffn_dw_k, ffn_dw_k, ffn_dw_b, ffn_dw_b, st0, st0)


def _sample_layer(x, mod, shift0, wkv0, conv0, ffn0, p):
    b, t, d = x.shape
    assert b == LANES
    rows = t * b
    tm = lambda z: jnp.swapaxes(z, 0, 1)
    w_in = p["w_in"]
    x_tm = tm(x)
    mod_tm = tm(mod)
    xs, shift1 = _prep_tm(x_tm, mod_tm, p["norm1_g"], p["mu"], shift0)

    tn = 1024
    rkv = _proj_transposed(xs, w_in, tn=tn, n=3 * d)
    proj = functools.partial(_matmul, xs, w_in, tm=rows, tn=tn, slab=SLAB_H)
    glu_in = proj(col0=3 * d // tn, n=2 * d, name="glu_proj")
    ga = proj(col0=5 * d // tn, n=d, sigmoid=True, out_dtype=BF16, name="gate_a_proj")
    gb = proj(col0=6 * d // tn, n=d, sigmoid=True, out_dtype=BF16, name="gate_b_proj")
    dec, a, g = _lora_transposed(xs, p["w0"], p["w1"], p["w2"], p["a0"], p["a1"], p["a2"],
                                 p["g1"], p["g2"])

    yt, s1 = _wkv_batch_lanes(rkv, a, dec, wkv0.transpose(1, 2, 3, 0), p["k_k"], p["k_a"],
                              p["r_k"], p["lnx_g"], p["lnx_b"], steps=t)
    wkv1 = s1.transpose(3, 0, 1, 2)

    yb, conv1 = _conv_tm(glu_in, p["b_glu"], p["dw_k"], p["dw_b"], p["ln_conv_g"],
                         p["ln_conv_b"], tm(conv0), steps=t)
    x1, h2 = _merge_tm(yt, g, ga, gb, yb, x_tm.reshape(rows, d), p["w_out"].astype(BF16), mod_tm,
                       p["norm2_g"], steps=t)

    d_ff = p["w_down"].shape[0]
    f_pad = -(-d_ff // FFN_TN) * FFN_TN
    act, ffn_g, ffn_v = _up_act_tm(
        h2, _pad_halves(p["w_up"], f_pad).astype(BF16), _pad_halves(p["ffn_dw_k"], f_pad),
        _pad_halves(p["ffn_dw_b"], f_pad).reshape(1, 2 * f_pad), _pad_halves(tm(ffn0), f_pad),
        steps=t)
    ffn1 = tm(jnp.concatenate([ffn_g[..., :d_ff], ffn_v[..., :d_ff]], axis=-1))
    w_down = jnp.pad(p["w_down"], ((0, f_pad - d_ff), (0, 0))).astype(BF16)
    x2 = _matmul(act[None], w_down, tm=rows, tn=512, res=x1, gate=jnp.tile(mod_tm[5], (t, 1)),
                 name="down_proj")
    y = _final_norm(x2, p["normf_g"], tm=min(256, rows))
    return tm(y.reshape(t, b, d)), shift1, wkv1, tm(conv1), ffn1


def _head_minor(w):
    lead = w.shape[:-1]
    return w.reshape(*lead, -1, HEAD_DIM).swapaxes(-1, -2).reshape(*lead, w.shape[-1])


def _layer(x, mod, shift0, wkv0, conv0, ffn0, p, *, tm_seq, bs, off, flat, tq_proj, gs_proj,
           tq_tok, gs_tok, tm_mm, tt, nseq_ffn, tq_ffn):
    b, t, d = x.shape
    rows = b * t
    groups = b // GROUP
    w_in = p["w_in"]
    tok = (lambda z: z.reshape(rows, z.shape[-1])) if flat else (lambda z: z)

    xs, shift1 = _prep(x, mod, p["norm1_g"], p["mu"], shift0, tm=tm_seq, off=off, bs=bs)
    xs2 = xs.reshape(7, rows, d)
    xs_tok = xs2 if flat else xs

    w_hm = jnp.concatenate([_head_minor(w_in[:, i * d:(i + 1) * d]) for i in range(3)],
                           axis=1).astype(BF16)
    rkv = _proj_head_lanes(xs_tok, w_hm, _Tiling(groups, t, gs_proj, tq_proj, flat), tn=1024,
                           n=3 * d)
    tl = _Tiling(groups, t, gs_tok, tq_tok, flat)
    dec, a, g = _lora(xs_tok, p["w0"], p["w1"], p["w2"], p["a0"], p["a1"], p["a2"], p["g1"], p["g2"],
                      tl)

    w_rest = jnp.concatenate([w_in[:, 3 * d:5 * d], _head_minor(w_in[:, 5 * d:6 * d]),
                              w_in[:, 6 * d:]], axis=1).astype(BF16)
    yw, wkv1, rest = _wkv(rkv, a, dec, wkv0, p["k_k"], p["k_a"], p["r_k"], p["lnx_g"], p["lnx_b"],
                          xs2, SLAB_H, w_rest, tt=tt, t0=off)
    rest = rest.reshape(b, t, 4 * d)

    yb, conv1 = _conv(rest, p["b_glu"], p["dw_k"], p["dw_b"], p["ln_conv_g"], p["ln_conv_b"],
                      conv0, tm=tm_seq, off=off, bs=bs)
    x1, h2 = _merge(yw, g, rest, tok(yb), tok(x), p["w_out"], mod, p["norm2_g"], tl)

    d_ff = p["w_down"].shape[0]
    f_pad = -(-d_ff // FFN_TN) * FFN_TN
    act, ffn_g, ffn_v = _up_act(
        h2.reshape(rows, d), _pad_halves(p["w_up"], f_pad).astype(BF16),
        _pad_halves(p["ffn_dw_k"], f_pad), _pad_halves(p["ffn_dw_b"], f_pad).reshape(1, 2 * f_pad),
        _pad_halves(ffn0, f_pad), nseq=nseq_ffn, tq=tq_ffn, off=off, t=t)
    ffn1 = jnp.concatenate([ffn_g[..., :d_ff], ffn_v[..., :d_ff]], axis=-1)
    if off:
        gate2, rows_per_gate = jnp.repeat(mod[:, 5], t, axis=0), None
    else:
        gate2, rows_per_gate = mod[:, 5][:, None, :], t
    w_down = jnp.pad(p["w_down"], ((0, f_pad - d_ff), (0, 0))).astype(BF16)
    x2 = _matmul(act[None], w_down, tm=min(tm_mm, 512), tn=512, res=x1.reshape(rows, d), gate=gate2,
                 rows_per_gate=rows_per_gate, name="down_proj")
    y = _final_norm(x2, p["normf_g"], tm=min(256, rows))
    return y.reshape(b, t, d), shift1.reshape(b, d), wkv1, conv1, ffn1


def kernel(x_prompt, x_sample, state_shift, state_wkv, state_conv, state_ffn, c_prompt, c_sample,
           norm1_g, norm2_g, normf_g, w_ada, b_ada, mu, w_in, b_glu, w0, w1, w2, a0, a1, a2, g1, g2,
           k_k, k_a, r_k, lnx_g, lnx_b, dw_k, dw_b, ln_conv_g, ln_conv_b, w_out, w_up, ffn_dw_k,
           ffn_dw_b, w_down):
    p = dict(norm1_g=norm1_g, norm2_g=norm2_g, normf_g=normf_g, mu=mu, w_in=w_in, b_glu=b_glu,
             w0=w0, w1=w1, w2=w2, a0=a0, a1=a1, a2=a2, g1=g1, g2=g2, k_k=k_k, k_a=k_a, r_k=r_k,
             lnx_g=lnx_g, lnx_b=lnx_b, dw_k=dw_k, dw_b=dw_b, ln_conv_g=ln_conv_g,
             ln_conv_b=ln_conv_b, w_out=w_out, w_up=w_up, ffn_dw_k=ffn_dw_k, ffn_dw_b=ffn_dw_b,
             w_down=w_down)
    bp, tp, d = x_prompt.shape
    bs, ts, _ = x_sample.shape
    f2 = w_up.shape[1]

    c_all = jnp.concatenate([c_prompt, c_sample], axis=0)
    c_rows = -(-c_all.shape[0] // SUBLANES) * SUBLANES
    c_all = jnp.pad(c_all, ((0, c_rows - c_all.shape[0]), (0, 0)))
    mod = _modulation(c_all, w_ada, b_ada).reshape(c_rows, N_MOD, d)
    mod_p, mod_s = mod[:bp], mod[bp:bp + bs]

    y_p, shift_p, wkv_p, conv_p, ffn_p = _layer(
        x_prompt, mod_p, jnp.zeros((bp, d), F32),
        jnp.zeros((bp, d // HEAD_DIM, HEAD_DIM, HEAD_DIM), F32),
        jnp.zeros((bp, CONV_W - 1, d), F32), jnp.zeros((bp, FFN_CONV_W - 1, f2), F32), p,
        tm_seq=256, bs=1, off=0, flat=False, tq_proj=256, gs_proj=1, tq_tok=64, gs_tok=1,
        tm_mm=1024, tt=32, nseq_ffn=1, tq_ffn=1024)

    y_s, shift_s, wkv_s, conv_s, ffn_s = _sample_layer(
        x_sample, mod_s, state_shift, state_wkv, state_conv, state_ffn, p)
    return (y_p, y_s, shift_p, wkv_p, conv_p, ffn_p, shift_s, wkv_s, conv_s, ffn_s)
```

```python
import functools
from typing import NamedTuple

import jax
import jax.numpy as jnp
from jax import lax
from jax.experimental import pallas as pl
from jax.experimental.pallas import tpu as pltpu

F32 = jnp.float32
BF16 = jnp.bfloat16

HEAD_DIM = 64
CONV_W = 31
FFN_CONV_W = 3
N_MOD = 6
GN_EPS = HEAD_DIM * 1e-5
SUBLANES = 8
LANES = 128
MXU_COLS = 256
LORA_PAD = 128
GROUP = 4
V7X_VMEM_LIMIT = 56 * 1024 * 1024

SLAB_R, SLAB_K, SLAB_V, SLAB_H, SLAB_W, SLAB_A, SLAB_G = range(7)
MU_SLABS = (SLAB_R, SLAB_W, SLAB_K, SLAB_V, SLAB_A, SLAB_G)


class _Tiling(NamedTuple):
    groups: int
    t: int
    gs: int
    tq: int

    @property
    def grid(self):
        return (self.groups // self.gs, self.t // self.tq)

    @property
    def seqs(self):
        return self.gs * GROUP

    def plain(self, cols, col=0):
        return pl.BlockSpec((self.seqs, self.tq, cols), lambda ig, it: (ig, it, col))

    def head_lanes(self, blk=0):
        return pl.BlockSpec((self.gs, HEAD_DIM, self.tq, LANES), lambda ig, it: (ig, blk, it, 0))

    def per_seq(self, rows, cols):
        return pl.BlockSpec((self.seqs, rows, cols), lambda ig, it: (ig, 0, 0))


def _cparams(n_axes):
    return pltpu.CompilerParams(dimension_semantics=("arbitrary",) * n_axes,
                                vmem_limit_bytes=V7X_VMEM_LIMIT)


def _const_spec(shape):
    return pl.BlockSpec(shape, lambda *_: (0,) * len(shape), pipeline_mode=pl.Buffered(1))


def _sigmoid(x):
    return 1.0 / (1.0 + jnp.exp(-x))


def _silu(x):
    return x * _sigmoid(x)


def _softplus(x):
    return jnp.maximum(x, 0.0) + jnp.log(1.0 + jnp.exp(-jnp.abs(x)))


def _swap_lane_groups(m, lane):
    half = lane < 2 * (LANES // GROUP)
    odd = (lane // (LANES // GROUP)) % 2 == 1
    n = [None] * 4
    for s in range(2):
        n[s] = jnp.where(half, m[s], pltpu.roll(m[s + 2], LANES // 2, axis=1))
        n[s + 2] = jnp.where(half, pltpu.roll(m[s], LANES // 2, axis=1), m[s + 2])
    out = [None] * 4
    for p in (0, 2):
        a, b = n[p], n[p + 1]
        out[p] = jnp.where(odd, pltpu.roll(b, LANES // GROUP, axis=1), a)
        out[p + 1] = jnp.where(odd, b, pltpu.roll(a, LANES - LANES // GROUP, axis=1))
    return out


def _store_head_lanes(val, out_ref, gs, tq, q0=0):
    lane = lax.broadcasted_iota(jnp.int32, (tq, LANES), 1)
    for g in range(gs):
        for q in range(val.shape[1] // LANES):
            src = [val[(g * GROUP + s) * tq:(g * GROUP + s + 1) * tq, q * LANES:(q + 1) * LANES]
                   for s in range(GROUP)]
            out = _swap_lane_groups(src, lane)
            for j in range(GROUP):
                out_ref[g, GROUP * (q0 + q) + j] = out[j]


def _rows2d(ref):
    v = ref[...]
    return v.reshape(-1, v.shape[-1])


def _mod_kernel(c_ref, w_ref, b_ref, o_ref):
    s = _silu(c_ref[...]).astype(BF16)
    o_ref[...] = jnp.dot(s, w_ref[...].astype(BF16), preferred_element_type=F32) + b_ref[...]


def _modulation(c, w_ada, b_ada):
    rows, d = c.shape
    n = w_ada.shape[1]
    tn = 1024
    return pl.pallas_call(
        _mod_kernel,
        grid=(n // tn,),
        in_specs=[pl.BlockSpec((rows, d), lambda j: (0, 0)),
                  pl.BlockSpec((d, tn), lambda j: (0, j)),
                  pl.BlockSpec((1, tn), lambda j: (0, j))],
        out_specs=pl.BlockSpec((rows, tn), lambda j: (0, j)),
        out_shape=jax.ShapeDtypeStruct((rows, n), F32),
        compiler_params=_cparams(1),
        name="modulation",
    )(c, w_ada, b_ada.reshape(1, n))


def _prep_kernel(x_ref, m_ref, g_ref, mu_ref, s0_ref, xs_ref, so_ref, hbuf, *, tm):
    x = x_ref[...]
    m = m_ref[...]
    ms = jnp.mean(x * x, axis=-1, keepdims=True)
    h = (x * lax.rsqrt(ms + 1e-6) * g_ref[...]) * (1.0 + m[1:2, :]) + m[0:1, :]
    hbuf[SUBLANES:SUBLANES + tm, :] = h

    @pl.when(pl.program_id(1) == 0)
    def _():
        hbuf[SUBLANES - 1:SUBLANES, :] = s0_ref[...]

    dx = hbuf[SUBLANES - 1:SUBLANES - 1 + tm, :] - h
    for mi, slab in enumerate(MU_SLABS):
        xs_ref[slab] = (h + dx * mu_ref[mi:mi + 1, :]).astype(BF16)
    xs_ref[SLAB_H] = h.astype(BF16)
    last = h[tm - 1:tm, :]
    hbuf[SUBLANES - 1:SUBLANES, :] = last
    so_ref[...] = last


def _prep(x, mod, norm_g, mu, shift0, *, tm):
    b, t, d = x.shape
    return pl.pallas_call(
        functools.partial(_prep_kernel, tm=tm),
        grid=(b, t // tm),
        in_specs=[pl.BlockSpec((None, tm, d), lambda i, j: (i, j, 0)),
                  pl.BlockSpec((None, N_MOD, d), lambda i, j: (i, 0, 0)),
                  _const_spec((1, d)),
                  _const_spec((N_MOD, d)),
                  pl.BlockSpec((None, 1, d), lambda i, j: (i, 0, 0))],
        out_specs=[pl.BlockSpec((7, None, tm, d), lambda i, j: (0, i, j, 0)),
                   pl.BlockSpec((None, 1, d), lambda i, j: (i, 0, 0))],
        out_shape=[jax.ShapeDtypeStruct((7, b, t, d), BF16),
                   jax.ShapeDtypeStruct((b, 1, d), F32)],
        scratch_shapes=[pltpu.VMEM((SUBLANES + tm, d), F32)],
        compiler_params=_cparams(2),
        name="prep",
    )(x, mod, norm_g.reshape(1, d), mu, shift0.reshape(b, 1, d))


def _gate_blocks(gate, rows, tm, rows_per_gate):
    if gate.ndim == 2:
        return gate.reshape(rows // tm, tm, gate.shape[-1]), 1
    assert rows_per_gate % tm == 0, (rows_per_gate, tm)
    return gate, rows_per_gate // tm


def _mm_kernel(*refs, gated, sigmoid):
    if gated:
        a_ref, w_ref, res_ref, gate_ref, o_ref, wb = refs
    else:
        a_ref, w_ref, o_ref, wb = refs

    @pl.when(pl.program_id(1) == 0)
    def _():
        wb[...] = w_ref[...].astype(BF16)

    acc = jnp.dot(a_ref[...], wb[...], preferred_element_type=F32)
    if gated:
        acc = res_ref[...] + gate_ref[...] * acc
    if sigmoid:
        acc = _sigmoid(acc)
    o_ref[...] = acc.astype(o_ref.dtype)


def _matmul(a, w, *, tm, tn, slab=0, col0=0, n=None, res=None, gate=None, rows_per_gate=None,
            sigmoid=False, out_dtype=F32, name):
    _, rows, k = a.shape
    n = w.shape[1] if n is None else n
    gated = res is not None
    in_specs = [pl.BlockSpec((None, tm, k), lambda j, i: (slab, i, 0)),
                pl.BlockSpec((k, tn), lambda j, i: (0, j + col0))]
    args = [a, w]
    if gated:
        gate, tiles_per_gate = _gate_blocks(gate, rows, tm, rows_per_gate)
        gr = gate.shape[1]
        in_specs += [pl.BlockSpec((tm, tn), lambda j, i: (i, j)),
                     pl.BlockSpec((None, gr, tn), lambda j, i: (i // tiles_per_gate, 0, j))]
        args += [res, gate]
    return pl.pallas_call(
        functools.partial(_mm_kernel, gated=gated, sigmoid=sigmoid),
        grid=(pl.cdiv(n, tn), rows // tm),
        in_specs=in_specs,
        out_specs=pl.BlockSpec((tm, tn), lambda j, i: (i, j)),
        out_shape=jax.ShapeDtypeStruct((rows, n), out_dtype),
        scratch_shapes=[pltpu.VMEM((k, tn), BF16)],
        compiler_params=_cparams(2),
        name=name,
    )(*args)


def _proj_hl_kernel(a_ref, w_ref, o_ref, *, gs, tq):
    a = _rows2d(a_ref)
    for c in range(0, w_ref.shape[1], MXU_COLS):
        acc = jnp.dot(a, w_ref[:, c:c + MXU_COLS], preferred_element_type=F32)
        _store_head_lanes(acc, o_ref, gs, tq, q0=c // LANES)


def _proj_head_lanes(xs, w, tl, *, tn, n):
    k = w.shape[0]
    d = n // 3
    slab = lambda j: j * tn // d
    a_spec = pl.BlockSpec((None, tl.seqs, tl.tq, k), lambda j, ig, it: (slab(j), ig, it, 0))
    nq = tn // (LANES // GROUP)
    return pl.pallas_call(
        functools.partial(_proj_hl_kernel, gs=tl.gs, tq=tl.tq),
        grid=(n // tn,) + tl.grid,
        in_specs=[a_spec, pl.BlockSpec((k, tn), lambda j, ig, it: (0, j))],
        out_specs=pl.BlockSpec((tl.gs, nq, tl.tq, LANES), lambda j, ig, it: (ig, j, it, 0)),
        out_shape=jax.ShapeDtypeStruct((tl.groups, n // (LANES // GROUP), tl.t, LANES), F32),
        compiler_params=_cparams(3),
        name="rkv_proj",
    )(xs, w)


def _lora_heads(xw_ref, xa_ref, xg_ref, w1, w2, a1, a2, g1, g2, w0, a0):
    dot = functools.partial(jnp.dot, preferred_element_type=F32)
    hw = jnp.tanh(dot(_rows2d(xw_ref), w1[...])).astype(BF16)
    w_pre = w0[...] + dot(hw, w2[...])
    dec = jnp.exp(-jnp.exp(-_softplus(-w_pre) - 0.5))
    ha = dot(_rows2d(xa_ref), a1[...]).astype(BF16)
    a = _sigmoid(a0[...] + dot(ha, a2[...]))
    hg = _sigmoid(dot(_rows2d(xg_ref), g1[...])).astype(BF16)
    return dec, a, dot(hg, g2[...])


def _lora_kernel(*refs, gs, tq):
    dec_ref, a_ref, g_ref = refs[-3:]
    dec, a, g = _lora_heads(*refs[:-3])
    _store_head_lanes(dec, dec_ref, gs, tq)
    _store_head_lanes(a, a_ref, gs, tq)
    g_ref[...] = g.reshape(g_ref.shape)


def _pad_lora(w_a, w_b):
    r = w_a.shape[1]
    return (jnp.pad(w_a, ((0, 0), (0, LORA_PAD - r))).astype(BF16),
            jnp.pad(w_b, ((0, LORA_PAD - r), (0, 0))).astype(BF16))


def _lora(xs, w0, w1, w2, a0, a1, a2, g1, g2, tl):
    d = w0.shape[0]
    w1p, w2p = _pad_lora(w1, _head_minor(w2))
    a1p, a2p = _pad_lora(a1, _head_minor(a2))
    g1b, g2b = g1.astype(BF16), _head_minor(g2).astype(BF16)
    slab = lambda s: pl.BlockSpec((None, tl.seqs, tl.tq, d), lambda ig, it: (s, ig, it, 0))
    g_shape = (tl.groups * GROUP, tl.t, d)
    hl_shape = jax.ShapeDtypeStruct((tl.groups, HEAD_DIM, tl.t, LANES), F32)
    consts = [w1p, w2p, a1p, a2p, g1b, g2b, _head_minor(w0).reshape(1, d),
              _head_minor(a0).reshape(1, d)]
    return pl.pallas_call(
        functools.partial(_lora_kernel, gs=tl.gs, tq=tl.tq),
        grid=tl.grid,
        in_specs=[slab(SLAB_W), slab(SLAB_A), slab(SLAB_G)] + [_const_spec(c.shape) for c in consts],
        out_specs=[tl.head_lanes(), tl.head_lanes(), tl.plain(d)],
        out_shape=[hl_shape, hl_shape, jax.ShapeDtypeStruct(g_shape, F32)],
        compiler_params=_cparams(2),
        name="lora",
    )(xs, xs, xs, *consts)


def _wkv_step(st, sa, v, w_row, b_row, k2_row, r_row, kk_next_row):
    y = jnp.zeros((HEAD_DIM, LANES), F32)
    sa_next = jnp.zeros((HEAD_DIM, LANES), F32)
    for i in range(HEAD_DIM):
        decayed = st[i] if w_row is None else st[i] * w_row(i)
        si = decayed - sa * b_row(i) + v * k2_row(i)
        st[i] = si
        y = y + si * r_row(i)
        sa_next = sa_next + si * kk_next_row(i)
    return y, sa_next


def _wkv_kernel(r_ref, k_ref, v_ref, a_ref, d_ref, kk_p, ka_p, rk_p, lg_p, lb_p, s0_ref,
                y_ref, so_ref, st, kk_s, b_s, k2_s, r_s, pe_s, bon_s, v_t, y_t, *, nb):
    n = HEAD_DIM
    tt = nb * SUBLANES
    bc = lambda p: p[...][:, None, None, :]

    @pl.when(pl.program_id(1) == 0)
    def _():
        s = s0_ref[...].reshape(LANES, n, n)
        st[...] = jnp.swapaxes(jnp.swapaxes(jnp.swapaxes(s, 0, 1), 1, 2), 0, 1)

    lw = jnp.log(d_ref[...])
    x = lw.reshape(n * nb, SUBLANES, LANES)
    step = lax.broadcasted_iota(jnp.int32, x.shape, 1)
    for sh in (1, 2, 4):
        x = x + jnp.where(step >= sh, pltpu.roll(x, sh, axis=1), 0.0)
    x = x.reshape(n, nb, SUBLANES, LANES)
    run = jnp.zeros((n, 1, LANES), F32)
    blocks = []
    for tb in range(nb):
        blocks.append(x[:, tb] + run)
        run = run + x[:, tb, SUBLANES - 1:SUBLANES, :]
    c = jnp.stack(blocks, axis=1)
    p_t = jnp.exp(c)
    inv_p = jnp.exp(-c)

    k = k_ref[...]
    a = a_ref[...]
    r = r_ref[...]
    kk = k * bc(kk_p)
    kk = kk / jnp.maximum(jnp.sqrt(jnp.sum(kk * kk, axis=0, keepdims=True)), 1e-12)
    k2 = k * (1.0 + (a - 1.0) * bc(ka_p))
    kk_s[...] = kk * jnp.exp(c - lw)
    b_s[...] = kk * a * inv_p
    k2_s[...] = k2 * inv_p
    r_s[...] = r * p_t
    pe_s[...] = p_t[:, nb - 1, SUBLANES - 1, :]
    bon_s[...] = jnp.sum(r * k2 * bc(rk_p), axis=0)
    v_t[...] = jnp.swapaxes(v_ref[...], 0, 1)

    sa0 = jnp.zeros((n, LANES), F32)
    for i in range(n):
        sa0 = sa0 + st[i] * kk_s[i, 0, 0:1, :]

    def block(tb, sa):
        tb_next = jnp.minimum(tb + 1, nb - 1)
        for j in range(SUBLANES):
            t = tb * SUBLANES + j
            row = lambda ref: (lambda i: ref[i, tb, j:j + 1, :])
            if j + 1 < SUBLANES:
                kk_next = lambda i: kk_s[i, tb, j + 1:j + 2, :]
            else:
                kk_next = lambda i: kk_s[i, tb_next, 0:1, :]
            y_t[t], sa = _wkv_step(st, sa, v_t[t], None, row(b_s), row(k2_s), row(r_s), kk_next)
        return sa

    lax.fori_loop(0, nb, block, sa0)
    for i in range(n):
        st[i] = st[i] * pe_s[i:i + 1, :]

    y = jnp.swapaxes(y_t[...], 0, 1)
    ym = jnp.mean(y, axis=0, keepdims=True)
    yc = y - ym
    yv = jnp.mean(yc * yc, axis=0, keepdims=True)
    bcast = lambda p: p[...][:, None, :]
    yn = yc * lax.rsqrt(yv + GN_EPS) * bcast(lg_p) + bcast(lb_p)
    y_ref[...] = yn + bon_s[...].reshape(1, tt, LANES) * v_ref[...]

    @pl.when(pl.program_id(1) == pl.num_programs(1) - 1)
    def _():
        s = jnp.swapaxes(jnp.swapaxes(jnp.swapaxes(st[...], 0, 1), 1, 2), 0, 1)
        so_ref[...] = s.reshape(so_ref.shape)


def _head_lanes(p):
    hn = p.reshape(-1, HEAD_DIM).T
    return jnp.tile(hn, (1, GROUP))


def _wkv(rkv, a, dec, s0, k_k, k_a, r_k, lnx_g, lnx_b, *, tt):
    g, _, t, lanes = a.shape
    n = HEAD_DIM
    heads = s0.shape[1]
    nb = tt // SUBLANES
    by8 = lambda z: z.reshape(z.shape[0], z.shape[1], t // SUBLANES, SUBLANES, lanes)
    params = [_head_lanes(p) for p in (k_k, k_a, r_k.reshape(-1), lnx_g, lnx_b)]
    rows = lambda blk: pl.BlockSpec((None, n, nb, SUBLANES, lanes), lambda i, j: (i, blk, j, 0, 0))
    tile = lambda blk: pl.BlockSpec((None, n, tt, lanes), lambda i, j: (i, blk, j, 0))
    st = pl.BlockSpec((GROUP, heads, n, n), lambda i, j: (i, 0, 0, 0))
    return pl.pallas_call(
        functools.partial(_wkv_kernel, nb=nb),
        grid=(g, t // tt),
        in_specs=[rows(0), rows(1), tile(2), rows(0), rows(0)] + [_const_spec((n, lanes))] * 5 + [st],
        out_specs=[tile(0), st],
        out_shape=[jax.ShapeDtypeStruct((g, n, t, lanes), F32),
                   jax.ShapeDtypeStruct(s0.shape, F32)],
        scratch_shapes=[pltpu.VMEM((n, n, lanes), F32)]
        + [pltpu.VMEM((n, nb, SUBLANES, lanes), F32)] * 4
        + [pltpu.VMEM((n, lanes), F32), pltpu.VMEM((nb, SUBLANES, lanes), F32)]
        + [pltpu.VMEM((tt, n, lanes), F32)] * 2,
        compiler_params=_cparams(2),
        name="wkv7",
    )(by8(rkv), by8(rkv), rkv, by8(a), by8(dec), *params, s0)


CONV_PAD = 32
CONV_ROWS = 32


def _conv_kernel(u1_ref, u2_ref, bg1, bg2, dwk, dwb, lg, lb, st0_ref, y_ref, sto_ref, zbuf, cbuf,
                 zs, *, tm):
    hist = CONV_W - 1
    lo = CONV_PAD - hist
    d = u1_ref.shape[-1]
    rc = min(CONV_ROWS, tm)
    glu = (u1_ref[...] + bg1[...]) * _sigmoid(u2_ref[...] + bg2[...])
    zbuf[CONV_PAD:CONV_PAD + tm, :] = glu

    @pl.when(pl.program_id(1) == 0)
    def _():
        zbuf[lo:CONV_PAD, :] = st0_ref[...]

    for b in range(SUBLANES):
        span = tm + SUBLANES * ((hist - b) // SUBLANES)
        zs[b, 0:span, :] = zbuf[lo + b:lo + b + span, :]
    for r0 in range(0, tm, rc):
        for c0 in range(0, d, LANES):
            acc = jnp.zeros((rc, LANES), F32)
            for j in range(CONV_W):
                a8 = SUBLANES * (j // SUBLANES)
                acc = acc + (zs[j % SUBLANES, a8 + r0:a8 + r0 + rc, c0:c0 + LANES]
                             * dwk[j:j + 1, c0:c0 + LANES])
            cbuf[r0:r0 + rc, c0:c0 + LANES] = acc
    zc = cbuf[...] + dwb[...]
    m = jnp.mean(zc, axis=-1, keepdims=True)
    ctr = zc - m
    var = jnp.mean(ctr * ctr, axis=-1, keepdims=True)
    ln = ctr * lax.rsqrt(var + 1e-5) * lg[...] + lb[...]
    y_ref[...] = _silu(ln).astype(BF16)
    carry = zbuf[lo + tm:CONV_PAD + tm, :]
    zbuf[lo:CONV_PAD, :] = carry
    sto_ref[...] = carry


def _conv(glu_in, b_glu, dw_k, dw_b, ln_g, ln_b, st0, *, tm):
    b, t, _ = glu_in.shape
    d = dw_k.shape[1]
    hist = CONV_W - 1
    vec = _const_spec((1, d))
    return pl.pallas_call(
        functools.partial(_conv_kernel, tm=tm),
        grid=(b, t // tm),
        in_specs=[pl.BlockSpec((None, tm, d), lambda i, j: (i, j, 0)),
                  pl.BlockSpec((None, tm, d), lambda i, j: (i, j, 1)),
                  pl.BlockSpec((1, d), lambda i, j: (0, 0)),
                  pl.BlockSpec((1, d), lambda i, j: (0, 1)),
                  _const_spec((CONV_W, d)),
                  vec, vec, vec,
                  pl.BlockSpec((None, hist, d), lambda i, j: (i, 0, 0))],
        out_specs=[pl.BlockSpec((None, tm, d), lambda i, j: (i, j, 0)),
                   pl.BlockSpec((None, hist, d), lambda i, j: (i, 0, 0))],
        out_shape=[jax.ShapeDtypeStruct((b, t, d), BF16),
                   jax.ShapeDtypeStruct((b, hist, d), F32)],
        scratch_shapes=[pltpu.VMEM((CONV_PAD + tm, d), F32), pltpu.VMEM((tm, d), F32),
                        pltpu.VMEM((SUBLANES, tm + CONV_PAD - SUBLANES, d), F32)],
        compiler_params=_cparams(2),
        name="conformer_conv",
    )(glu_in, glu_in, b_glu.reshape(1, 2 * d), b_glu.reshape(1, 2 * d), dw_k, dw_b.reshape(1, d),
      ln_g.reshape(1, d), ln_b.reshape(1, d), st0)


def _merge_kernel(yw_ref, g_ref, ga_ref, gb_ref, yb_ref, x_ref, woa_ref, wob_ref, m_ref, n2_ref,
                  x1_ref, h2_ref, ybuf, *, gs, tq):
    seqs = gs * GROUP
    d = ybuf.shape[-1]
    lane = lax.broadcasted_iota(jnp.int32, (tq, LANES), 1)
    for g in range(gs):
        for q in range(d // LANES):
            out = _swap_lane_groups([yw_ref[g, GROUP * q + j] for j in range(GROUP)], lane)
            for s in range(GROUP):
                ybuf[g * GROUP + s, :, q * LANES:(q + 1) * LANES] = out[s]
    seq3 = lambda ref: ref[...].astype(F32).reshape(seqs, tq, d)
    m = m_ref[...]
    dot = lambda z, w_ref: jnp.dot(z.reshape(seqs * tq, d).astype(BF16), w_ref[...],
                                   preferred_element_type=F32)
    branch_a = seq3(ga_ref) * (ybuf[...] * seq3(g_ref))
    branch_b = seq3(gb_ref) * seq3(yb_ref)
    mix = (dot(branch_a, woa_ref) + dot(branch_b, wob_ref)).reshape(seqs, tq, d)
    x1 = seq3(x_ref) + m[:, 2:3, :] * mix
    x1_ref[...] = x1.reshape(x1_ref.shape)
    ms = jnp.mean(x1 * x1, axis=-1, keepdims=True)
    h2 = (x1 * lax.rsqrt(ms + 1e-6) * n2_ref[...]) * (1.0 + m[:, 4:5, :]) + m[:, 3:4, :]
    h2_ref[...] = h2.reshape(seqs * tq, d).astype(BF16).reshape(h2_ref.shape)


def _merge(yw, g, ga, gb, yb, x, w_out, mod, norm_g, tl):
    d = x.shape[-1]
    plain = tl.plain(d)
    w_out_b = w_out.astype(BF16)
    w_out_a = _head_minor(w_out.T).T.astype(BF16)
    return pl.pallas_call(
        functools.partial(_merge_kernel, gs=tl.gs, tq=tl.tq),
        grid=tl.grid,
        in_specs=[tl.head_lanes(), plain, plain, plain, plain, plain, _const_spec((d, d)),
                  _const_spec((d, d)), tl.per_seq(N_MOD, d), _const_spec((1, d))],
        out_specs=[plain, plain],
        out_shape=[jax.ShapeDtypeStruct(x.shape, F32), jax.ShapeDtypeStruct(x.shape, BF16)],
        scratch_shapes=[pltpu.VMEM((tl.seqs, tl.tq, d), F32)],
        compiler_params=_cparams(2),
        name="merge_out_proj",
    )(yw, g, ga, gb, yb, x, w_out_a, w_out_b, mod, norm_g.reshape(1, d))


FFN_TN = 512


def _pad_halves(w, width):
    f = w.shape[-1] // 2
    pad = [(0, 0)] * (w.ndim - 1) + [(0, width - f)]
    return jnp.concatenate([jnp.pad(w[..., :f], pad), jnp.pad(w[..., f:], pad)], axis=-1)


def _up_act_kernel(h_ref, wg_ref, wv_ref, kg_ref, kv_ref, bg_ref, bv_ref, sg_ref, sv_ref,
                   act_ref, stg_ref, stv_ref, carry, *, tq, tiles_per_seq):
    first = pl.program_id(1) % tiles_per_seq == 0
    h = h_ref[...]
    tn = wg_ref.shape[-1]
    row = lax.broadcasted_iota(jnp.int32, (tq, tn), 0)
    z = []
    halves = ((wg_ref, kg_ref, bg_ref, stg_ref), (wv_ref, kv_ref, bv_ref, stv_ref))

    @pl.when(first)
    def _():
        carry[0] = sg_ref[...]
        carry[1] = sv_ref[...]

    for half, (w_ref, k_ref, b_ref, st_ref) in enumerate(halves):
        u = jnp.dot(h, w_ref[...], preferred_element_type=F32)
        prev = carry[half]
        p0, p1 = prev[0:1, :], prev[1:2, :]
        u1 = jnp.where(row == 0, p1, pltpu.roll(u, 1, axis=0))
        u2 = jnp.where(row == 0, p0, jnp.where(row == 1, p1, pltpu.roll(u, 2, axis=0)))
        z.append(b_ref[...] + u2 * k_ref[0:1, :] + u1 * k_ref[1:2, :] + u * k_ref[2:3, :])
        last = u[tq - (FFN_CONV_W - 1):, :]
        carry[half] = last
        st_ref[...] = last
    act_ref[...] = (_silu(z[0]) * z[1]).astype(BF16)


def _up_act(h2, w_up, ffn_dw_k, ffn_dw_b, st0, *, tq, t):
    rows, k = h2.shape
    b = st0.shape[0]
    f = w_up.shape[1] // 2
    nj = f // FFN_TN
    hist = FFN_CONV_W - 1
    tiles_per_seq = t // tq
    col = lambda shape, h: pl.BlockSpec(shape, lambda j, i: (0, j + h * nj))
    state = lambda h: pl.BlockSpec((None, hist, FFN_TN),
                                   lambda j, i: (i // tiles_per_seq, 0, j + h * nj))
    return pl.pallas_call(
        functools.partial(_up_act_kernel, tq=tq, tiles_per_seq=tiles_per_seq),
        grid=(nj, rows // tq),
        in_specs=[pl.BlockSpec((tq, k), lambda j, i: (i, 0)),
                  col((k, FFN_TN), 0), col((k, FFN_TN), 1),
                  col((FFN_CONV_W, FFN_TN), 0), col((FFN_CONV_W, FFN_TN), 1),
                  col((1, FFN_TN), 0), col((1, FFN_TN), 1),
                  state(0), state(1)],
        out_specs=[pl.BlockSpec((tq, FFN_TN), lambda j, i: (i, j)), state(0), state(0)],
        out_shape=[jax.ShapeDtypeStruct((rows, f), BF16),
                   jax.ShapeDtypeStruct((b, hist, f), F32),
                   jax.ShapeDtypeStruct((b, hist, f), F32)],
        scratch_shapes=[pltpu.VMEM((2, hist, FFN_TN), F32)],
        compiler_params=_cparams(2),
        name="up_conv_act",
    )(h2, w_up, w_up, ffn_dw_k, ffn_dw_k, ffn_dw_b, ffn_dw_b, st0, st0)


def _norm_kernel(x_ref, g_ref, o_ref):
    x = x_ref[...]
    ms = jnp.mean(x * x, axis=-1, keepdims=True)
    o_ref[...] = x * lax.rsqrt(ms + 1e-6) * g_ref[...]


def _final_norm(x, g, *, tm):
    rows, d = x.shape
    row = pl.BlockSpec((tm, d), lambda i: (i, 0))
    return pl.pallas_call(
        _norm_kernel,
        grid=(rows // tm,),
        in_specs=[row, _const_spec((1, d))],
        out_specs=row,
        out_shape=jax.ShapeDtypeStruct((rows, d), F32),
        compiler_params=_cparams(1),
        name="final_norm",
    )(x, g.reshape(1, d))


def _prep_tm_kernel(x_ref, m_ref, g_ref, mu_ref, s0_ref, xs_ref, so_ref, hprev):
    @pl.when(pl.program_id(0) == 0)
    def _():
        hprev[...] = s0_ref[...]

    x = x_ref[...]
    ms = jnp.mean(x * x, axis=-1, keepdims=True)
    h = (x * lax.rsqrt(ms + 1e-6) * g_ref[...]) * (1.0 + m_ref[1]) + m_ref[0]
    dx = hprev[...] - h
    for mi, slab in enumerate(MU_SLABS):
        xs_ref[slab] = (h + dx * mu_ref[mi:mi + 1, :]).astype(BF16)
    xs_ref[SLAB_H] = h.astype(BF16)
    hprev[...] = h
    so_ref[...] = h


def _prep_tm(x, mod, norm_g, mu, shift0):
    t, b, d = x.shape
    return pl.pallas_call(
        _prep_tm_kernel,
        grid=(t,),
        in_specs=[pl.BlockSpec((None, b, d), lambda i: (i, 0, 0)), _const_spec((N_MOD, b, d)),
                  _const_spec((1, d)), _const_spec((N_MOD, d)), _const_spec((b, d))],
        out_specs=[pl.BlockSpec((7, b, d), lambda i: (0, i, 0)),
                   pl.BlockSpec((b, d), lambda i: (0, 0))],
        out_shape=[jax.ShapeDtypeStruct((7, t * b, d), BF16), jax.ShapeDtypeStruct((b, d), F32)],
        scratch_shapes=[pltpu.VMEM((b, d), F32)],
        compiler_params=_cparams(1),
        name="prep_tm",
    )(x, mod, norm_g.reshape(1, d), mu, shift0)


def _proj_t_kernel(a_ref, w_ref, o_ref):
    acc = jnp.dot(a_ref[...], w_ref[...].astype(BF16), preferred_element_type=F32)
    o_ref[...] = acc.T


def _proj_transposed(xs, w, *, tn, n):
    _, rows, k = xs.shape
    d = n // 3
    return pl.pallas_call(
        _proj_t_kernel,
        grid=(n // tn,),
        in_specs=[pl.BlockSpec((None, rows, k), lambda j: (j * tn // d, 0, 0)),
                  pl.BlockSpec((k, tn), lambda j: (0, j))],
        out_specs=pl.BlockSpec((tn, rows), lambda j: (j, 0)),
        out_shape=jax.ShapeDtypeStruct((n, rows), F32),
        compiler_params=_cparams(1),
        name="rkv_proj_tm",
    )(xs, w)


def _lora_t_kernel(*refs):
    dec_ref, a_ref, g_ref = refs[-3:]
    dec, a, g = _lora_heads(*refs[:-3])
    dec_ref[...] = dec.T
    a_ref[...] = a.T
    g_ref[...] = g


def _lora_transposed(xs, w0, w1, w2, a0, a1, a2, g1, g2):
    _, rows, d = xs.shape
    w1p, w2p = _pad_lora(w1, w2)
    a1p, a2p = _pad_lora(a1, a2)
    consts = [w1p, w2p, a1p, a2p, g1.astype(BF16), g2.astype(BF16), w0.reshape(1, d),
              a0.reshape(1, d)]
    slab = lambda s: pl.BlockSpec((None, rows, d), lambda i: (s, 0, 0))
    t_shape = jax.ShapeDtypeStruct((d, rows), F32)
    return pl.pallas_call(
        _lora_t_kernel,
        grid=(1,),
        in_specs=[slab(SLAB_W), slab(SLAB_A), slab(SLAB_G)] + [_const_spec(c.shape) for c in consts],
        out_specs=[pl.BlockSpec((d, rows), lambda i: (0, 0)), pl.BlockSpec((d, rows), lambda i: (0, 0)),
                   pl.BlockSpec((rows, d), lambda i: (0, 0))],
        out_shape=[t_shape, t_shape, jax.ShapeDtypeStruct((rows, d), F32)],
        compiler_params=_cparams(1),
        name="lora_tm",
    )(xs, xs, xs, *consts)


def _wkv_bl_kernel(r_ref, k_ref, v_ref, a_ref, d_ref, kk_p, ka_p, rk_p, lg_p, lb_p, s0_ref,
                   y_ref, so_ref, st, kk_s, b_s, k2_s, *, steps):
    n = HEAD_DIM
    tile = lambda ref, t: ref[:, t * LANES:(t + 1) * LANES]
    st[...] = jnp.swapaxes(s0_ref[...], 0, 1)
    bonus = []
    for t in range(steps):
        k = tile(k_ref, t)
        a = tile(a_ref, t)
        kk = k * kk_p[...]
        kk = kk / jnp.maximum(jnp.sqrt(jnp.sum(kk * kk, axis=0, keepdims=True)), 1e-12)
        k2 = k * (1.0 + (a - 1.0) * ka_p[...])
        kk_s[:, t * LANES:(t + 1) * LANES] = kk
        b_s[:, t * LANES:(t + 1) * LANES] = kk * a
        k2_s[:, t * LANES:(t + 1) * LANES] = k2
        bonus.append(jnp.sum(tile(r_ref, t) * k2 * rk_p[...], axis=0, keepdims=True))

    sa = jnp.zeros((n, LANES), F32)
    for i in range(n):
        sa = sa + st[i] * kk_s[i:i + 1, 0:LANES]
    for t in range(steps):
        t_next = min(t + 1, steps - 1)
        row = lambda ref, t=t: (lambda i: ref[i:i + 1, t * LANES:(t + 1) * LANES])
        v = tile(v_ref, t)
        y, sa = _wkv_step(st, sa, v, row(d_ref), row(b_s), row(k2_s), row(r_ref),
                          row(kk_s, t_next))
        ym = jnp.mean(y, axis=0, keepdims=True)
        yc = y - ym
        yv = jnp.mean(yc * yc, axis=0, keepdims=True)
        yn = yc * lax.rsqrt(yv + GN_EPS) * lg_p[...] + lb_p[...]
        y_ref[:, t * LANES:(t + 1) * LANES] = yn + bonus[t] * v
    so_ref[...] = jnp.swapaxes(st[...], 0, 1)


def _wkv_batch_lanes(rkv, a, dec, s0, k_k, k_a, r_k, lnx_g, lnx_b, *, steps):
    d, cols = a.shape
    heads, n = s0.shape[0], HEAD_DIM
    params = [jnp.broadcast_to(p.reshape(heads, n, 1), (heads, n, LANES))
              for p in (k_k, k_a, r_k, lnx_g, lnx_b)]
    chan = lambda c: pl.BlockSpec((n, cols), lambda h: (c * heads + h, 0))
    par = pl.BlockSpec((None, n, LANES), lambda h: (h, 0, 0))
    st = pl.BlockSpec((None, n, n, LANES), lambda h: (h, 0, 0, 0))
    return pl.pallas_call(
        functools.partial(_wkv_bl_kernel, steps=steps),
        grid=(heads,),
        in_specs=[chan(0), chan(1), chan(2), chan(0), chan(0)] + [par] * 5 + [st],
        out_specs=[chan(0), st],
        out_shape=[jax.ShapeDtypeStruct((d, cols), F32), jax.ShapeDtypeStruct(s0.shape, F32)],
        scratch_shapes=[pltpu.VMEM((n, n, LANES), F32)] + [pltpu.VMEM((n, cols), F32)] * 3,
        compiler_params=_cparams(1),
        name="wkv7_tm",
    )(rkv, rkv, rkv, a, dec, *params, s0)


CONV_TM_COLS = 512


def _conv_tm_kernel(u1_ref, u2_ref, bg1, bg2, dwk, dwb, lg, lb, st_ref, y_ref, sto_ref, zc,
                    *, steps, b):
    j = pl.program_id(0)
    hist = CONV_W - 1
    tc = u1_ref.shape[-1]
    for c0 in range(0, tc, LANES):
        cs = slice(c0, c0 + LANES)
        glu = [(u1_ref[t * b:(t + 1) * b, cs] + bg1[:, cs])
               * _sigmoid(u2_ref[t * b:(t + 1) * b, cs] + bg2[:, cs]) for t in range(steps)]
        window = lambda i: st_ref[i, :, cs] if i < hist else glu[i - hist]
        for t in range(steps):
            acc = jnp.zeros((b, LANES), F32)
            for tap in range(CONV_W):
                acc = acc + window(t + tap) * dwk[tap:tap + 1, cs]
            zc[j, t * b:(t + 1) * b, cs] = acc + dwb[:, cs]
        for i in range(hist):
            sto_ref[i, :, cs] = window(i + steps)

    @pl.when(j == pl.num_programs(0) - 1)
    def _():
        z = zc[...]
        m = jnp.mean(jnp.mean(z, axis=-1, keepdims=True), axis=0, keepdims=True)
        ctr = z - m
        var = jnp.mean(jnp.mean(ctr * ctr, axis=-1, keepdims=True), axis=0, keepdims=True)
        ln = ctr * lax.rsqrt(var + 1e-5)
        for jj in range(zc.shape[0]):
            cols = slice(jj * tc, (jj + 1) * tc)
            y_ref[:, cols] = _silu(ln[jj] * lg[:, cols] + lb[:, cols]).astype(BF16)


def _conv_tm(glu_in, b_glu, dw_k, dw_b, ln_g, ln_b, st0, *, steps):
    rows, _ = glu_in.shape
    hist, b, d = st0.shape
    tc = CONV_TM_COLS
    nct = d // tc
    col = lambda shape, h=0: pl.BlockSpec(shape, lambda j: (0, j + h * nct))
    state = pl.BlockSpec((hist, b, tc), lambda j: (0, 0, j))
    return pl.pallas_call(
        functools.partial(_conv_tm_kernel, steps=steps, b=b),
        grid=(nct,),
        in_specs=[col((rows, tc)), col((rows, tc), 1), col((1, tc)), col((1, tc), 1),
                  col((CONV_W, tc)), col((1, tc)), _const_spec((1, d)), _const_spec((1, d)), state],
        out_specs=[pl.BlockSpec((rows, d), lambda j: (0, 0)), state],
        out_shape=[jax.ShapeDtypeStruct((rows, d), BF16), jax.ShapeDtypeStruct(st0.shape, F32)],
        scratch_shapes=[pltpu.VMEM((nct, rows, tc), F32)],
        compiler_params=_cparams(1),
        name="conformer_conv_tm",
    )(glu_in, glu_in, b_glu.reshape(1, 2 * d), b_glu.reshape(1, 2 * d), dw_k, dw_b.reshape(1, d),
      ln_g.reshape(1, d), ln_b.reshape(1, d), st0)


def _merge_tm_kernel(yt_ref, g_ref, ga_ref, gb_ref, yb_ref, x_ref, wo_ref, m_ref, n2_ref,
                     x1_ref, h2_ref):
    f32 = lambda ref: ref[...].astype(F32)
    merged = f32(ga_ref) * (yt_ref[...].T * g_ref[...]) + f32(gb_ref) * f32(yb_ref)
    mix = jnp.dot(merged.astype(BF16), wo_ref[...], preferred_element_type=F32)
    x1 = x_ref[...] + m_ref[2] * mix
    x1_ref[...] = x1
    ms = jnp.mean(x1 * x1, axis=-1, keepdims=True)
    h2 = (x1 * lax.rsqrt(ms + 1e-6) * n2_ref[...]) * (1.0 + m_ref[4]) + m_ref[3]
    h2_ref[...] = h2.astype(BF16)


def _merge_tm(yt, g, ga, gb, yb, x, w_out, mod, norm_g, *, steps):
    rows, d = x.shape
    b = rows // steps
    row = pl.BlockSpec((b, d), lambda i: (i, 0))
    return pl.pallas_call(
        _merge_tm_kernel,
        grid=(steps,),
        in_specs=[pl.BlockSpec((d, b), lambda i: (0, i)), row, row, row, row, row,
                  _const_spec((d, d)), _const_spec((N_MOD, b, d)), _const_spec((1, d))],
        out_specs=[row, row],
        out_shape=[jax.ShapeDtypeStruct((rows, d), F32), jax.ShapeDtypeStruct((rows, d), BF16)],
        compiler_params=_cparams(1),
        name="merge_out_proj_tm",
    )(yt, g, ga, gb, yb, x, w_out, mod, norm_g.reshape(1, d))


def _up_act_tm_kernel(h_ref, wg_ref, wv_ref, kg_ref, kv_ref, bg_ref, bv_ref, sg_ref, sv_ref,
                      act_ref, stg_ref, stv_ref, *, steps, b):
    hist = FFN_CONV_W - 1
    h = h_ref[...]
    tn = wg_ref.shape[-1]
    z = []
    halves = ((wg_ref, kg_ref, bg_ref, sg_ref, stg_ref), (wv_ref, kv_ref, bv_ref, sv_ref, stv_ref))
    for w_ref, k_ref, b_ref, s_ref, st_ref in halves:
        u = jnp.dot(h, w_ref[...], preferred_element_type=F32).reshape(steps, b, tn)
        window = [s_ref[i] for i in range(hist)] + [u[t] for t in range(steps)]
        z.append(jnp.stack([b_ref[...] + sum(window[t + tap] * k_ref[tap:tap + 1, :]
                                             for tap in range(FFN_CONV_W))
                            for t in range(steps)]))
        for i in range(hist):
            st_ref[i] = window[steps + i]
    act_ref[...] = (_silu(z[0]) * z[1]).reshape(steps * b, tn).astype(BF16)


def _up_act_tm(h2, w_up, ffn_dw_k, ffn_dw_b, st0, *, steps):
    rows, k = h2.shape
    hist, b, _ = st0.shape
    f = w_up.shape[1] // 2
    nj = f // FFN_TN
    col = lambda shape, h: pl.BlockSpec(shape, lambda j: (0, j + h * nj))
    state = lambda h: pl.BlockSpec((hist, b, FFN_TN), lambda j: (0, 0, j + h * nj))
    return pl.pallas_call(
        functools.partial(_up_act_tm_kernel, steps=steps, b=b),
        grid=(nj,),
        in_specs=[_const_spec((rows, k)), col((k, FFN_TN), 0), col((k, FFN_TN), 1),
                  col((FFN_CONV_W, FFN_TN), 0), col((FFN_CONV_W, FFN_TN), 1),
                  col((1, FFN_TN), 0), col((1, FFN_TN), 1), state(0), state(1)],
        out_specs=[pl.BlockSpec((rows, FFN_TN), lambda j: (0, j)), state(0), state(0)],
        out_shape=[jax.ShapeDtypeStruct((rows, f), BF16),
                   jax.ShapeDtypeStruct((hist, b, f), F32),
                   jax.ShapeDtypeStruct((hist, b, f), F32)],
        compiler_params=_cparams(1),
        name="up_conv_act_tm",
    )(h2, w_up, w_up, ffn_dw_k, ffn_dw_k, ffn_dw_b, ffn_dw_b, st0, st0)


def _sample_layer(x, mod, shift0, wkv0, conv0, ffn0, p):
    b, t, d = x.shape
    assert b == LANES
    rows = t * b
    tm = lambda z: jnp.swapaxes(z, 0, 1)
    w_in = p["w_in"]
    x_tm = tm(x)
    mod_tm = tm(mod)
    xs, shift1 = _prep_tm(x_tm, mod_tm, p["norm1_g"], p["mu"], shift0)

    tn = 1024
    rkv = _proj_transposed(xs, w_in, tn=tn, n=3 * d)
    proj = functools.partial(_matmul, xs, w_in, tm=rows, tn=tn, slab=SLAB_H)
    glu_in = proj(col0=3 * d // tn, n=2 * d, name="glu_proj")
    ga = proj(col0=5 * d // tn, n=d, sigmoid=True, out_dtype=BF16, name="gate_a_proj")
    gb = proj(col0=6 * d // tn, n=d, sigmoid=True, out_dtype=BF16, name="gate_b_proj")
    dec, a, g = _lora_transposed(xs, p["w0"], p["w1"], p["w2"], p["a0"], p["a1"], p["a2"],
                                 p["g1"], p["g2"])

    yt, s1 = _wkv_batch_lanes(rkv, a, dec, wkv0.transpose(1, 2, 3, 0), p["k_k"], p["k_a"],
                              p["r_k"], p["lnx_g"], p["lnx_b"], steps=t)
    wkv1 = s1.transpose(3, 0, 1, 2)

    yb, conv1 = _conv_tm(glu_in, p["b_glu"], p["dw_k"], p["dw_b"], p["ln_conv_g"],
                         p["ln_conv_b"], tm(conv0), steps=t)
    x1, h2 = _merge_tm(yt, g, ga, gb, yb, x_tm.reshape(rows, d), p["w_out"].astype(BF16), mod_tm,
                       p["norm2_g"], steps=t)

    d_ff = p["w_down"].shape[0]
    f_pad = -(-d_ff // FFN_TN) * FFN_TN
    act, ffn_g, ffn_v = _up_act_tm(
        h2, _pad_halves(p["w_up"], f_pad).astype(BF16), _pad_halves(p["ffn_dw_k"], f_pad),
        _pad_halves(p["ffn_dw_b"], f_pad).reshape(1, 2 * f_pad), _pad_halves(tm(ffn0), f_pad),
        steps=t)
    ffn1 = tm(jnp.concatenate([ffn_g[..., :d_ff], ffn_v[..., :d_ff]], axis=-1))
    w_down = jnp.pad(p["w_down"], ((0, f_pad - d_ff), (0, 0))).astype(BF16)
    x2 = _matmul(act[None], w_down, tm=rows, tn=512, res=x1, gate=jnp.tile(mod_tm[5], (t, 1)),
                 name="down_proj")
    y = _final_norm(x2, p["normf_g"], tm=min(256, rows))
    return tm(y.reshape(t, b, d)), shift1, wkv1, tm(conv1), ffn1


def _head_minor(w):
    lead = w.shape[:-1]
    return w.reshape(*lead, -1, HEAD_DIM).swapaxes(-1, -2).reshape(*lead, w.shape[-1])


def _prompt_layer(x, mod, shift0, wkv0, conv0, ffn0, p, *, tm_seq, tq_proj, tq_tok, tm_mm, tt,
                  tq_ffn):
    b, t, d = x.shape
    rows = b * t
    groups = b // GROUP
    w_in = p["w_in"]

    xs, shift1 = _prep(x, mod, p["norm1_g"], p["mu"], shift0, tm=tm_seq)
    xs2 = xs.reshape(7, rows, d)

    w_hm = jnp.concatenate([_head_minor(w_in[:, i * d:(i + 1) * d]) for i in (0, 1, 2, 5)],
                           axis=1).astype(BF16)
    rkv = _proj_head_lanes(xs, w_hm, _Tiling(groups, t, 1, tq_proj), tn=1024, n=3 * d)
    tn = 1024
    ga = _matmul(xs2, w_hm, tm=tm_mm, tn=tn, slab=SLAB_H, col0=3 * d // tn, n=d, sigmoid=True,
                 out_dtype=BF16, name="gate_a_proj")
    glu_in = _matmul(xs2, w_in, tm=tm_mm, tn=tn, slab=SLAB_H, col0=3 * d // tn, n=2 * d,
                     name="glu_proj")
    gb = _matmul(xs2, w_in, tm=tm_mm, tn=tn, slab=SLAB_H, col0=6 * d // tn, n=d, sigmoid=True,
                 out_dtype=BF16, name="gate_b_proj")

    tl = _Tiling(groups, t, 1, tq_tok)
    dec, a, g = _lora(xs, p["w0"], p["w1"], p["w2"], p["a0"], p["a1"], p["a2"], p["g1"], p["g2"], tl)
    yw, wkv1 = _wkv(rkv, a, dec, wkv0, p["k_k"], p["k_a"], p["r_k"], p["lnx_g"], p["lnx_b"], tt=tt)
    yb, conv1 = _conv(glu_in.reshape(b, t, 2 * d), p["b_glu"], p["dw_k"], p["dw_b"],
                      p["ln_conv_g"], p["ln_conv_b"], conv0, tm=tm_seq)
    x1, h2 = _merge(yw, g, ga.reshape(b, t, d), gb.reshape(b, t, d), yb, x, p["w_out"], mod,
                    p["norm2_g"], tl)

    d_ff = p["w_down"].shape[0]
    f_pad = -(-d_ff // FFN_TN) * FFN_TN
    act, ffn_g, ffn_v = _up_act(
        h2.reshape(rows, d), _pad_halves(p["w_up"], f_pad).astype(BF16),
        _pad_halves(p["ffn_dw_k"], f_pad), _pad_halves(p["ffn_dw_b"], f_pad).reshape(1, 2 * f_pad),
        _pad_halves(ffn0, f_pad), tq=tq_ffn, t=t)
    ffn1 = jnp.concatenate([ffn_g[..., :d_ff], ffn_v[..., :d_ff]], axis=-1)
    w_down = jnp.pad(p["w_down"], ((0, f_pad - d_ff), (0, 0))).astype(BF16)
    x2 = _matmul(act[None], w_down, tm=min(tm_mm, 512), tn=512, res=x1.reshape(rows, d),
                 gate=mod[:, 5][:, None, :], rows_per_gate=t, name="down_proj")
    y = _final_norm(x2, p["normf_g"], tm=min(256, rows))
    return y.reshape(b, t, d), shift1.reshape(b, d), wkv1, conv1, ffn1


def kernel(x_prompt, x_sample, state_shift, state_wkv, state_conv, state_ffn, c_prompt, c_sample,
           norm1_g, norm2_g, normf_g, w_ada, b_ada, mu, w_in, b_glu, w0, w1, w2, a0, a1, a2, g1, g2,
           k_k, k_a, r_k, lnx_g, lnx_b, dw_k, dw_b, ln_conv_g, ln_conv_b, w_out, w_up, ffn_dw_k,
           ffn_dw_b, w_down):
    p = dict(norm1_g=norm1_g, norm2_g=norm2_g, normf_g=normf_g, mu=mu, w_in=w_in, b_glu=b_glu,
             w0=w0, w1=w1, w2=w2, a0=a0, a1=a1, a2=a2, g1=g1, g2=g2, k_k=k_k, k_a=k_a, r_k=r_k,
             lnx_g=lnx_g, lnx_b=lnx_b, dw_k=dw_k, dw_b=dw_b, ln_conv_g=ln_conv_g,
             ln_conv_b=ln_conv_b, w_out=w_out, w_up=w_up, ffn_dw_k=ffn_dw_k, ffn_dw_b=ffn_dw_b,
             w_down=w_down)
    bp, _, d = x_prompt.shape
    bs = x_sample.shape[0]
    f2 = w_up.shape[1]

    c_all = jnp.concatenate([c_prompt, c_sample], axis=0)
    c_rows = -(-c_all.shape[0] // SUBLANES) * SUBLANES
    c_all = jnp.pad(c_all, ((0, c_rows - c_all.shape[0]), (0, 0)))
    mod = _modulation(c_all, w_ada, b_ada).reshape(c_rows, N_MOD, d)
    mod_p, mod_s = mod[:bp], mod[bp:bp + bs]

    y_p, shift_p, wkv_p, conv_p, ffn_p = _prompt_layer(
        x_prompt, mod_p, jnp.zeros((bp, d), F32),
        jnp.zeros((bp, d // HEAD_DIM, HEAD_DIM, HEAD_DIM), F32),
        jnp.zeros((bp, CONV_W - 1, d), F32), jnp.zeros((bp, FFN_CONV_W - 1, f2), F32), p,
        tm_seq=256, tq_proj=256, tq_tok=64, tm_mm=1024, tt=32, tq_ffn=1024)

    y_s, shift_s, wkv_s, conv_s, ffn_s = _sample_layer(
        x_sample, mod_s, state_shift, state_wkv, state_conv, state_ffn, p)
    return (y_p, y_s, shift_p, wkv_p, conv_p, ffn_p, shift_s, wkv_s, conv_s, ffn_s)
```

```python
import functools
from typing import NamedTuple

import jax
import jax.numpy as jnp
from jax import lax
from jax.experimental import pallas as pl
from jax.experimental.pallas import tpu as pltpu

F32 = jnp.float32
BF16 = jnp.bfloat16

HEAD_DIM = 64
CONV_W = 31
FFN_CONV_W = 3
N_MOD = 6
GN_EPS = HEAD_DIM * 1e-5
SUBLANES = 8
LANES = 128
MXU_COLS = 256
LORA_PAD = 128
GROUP = 4
V7X_VMEM_LIMIT = 56 * 1024 * 1024

SLAB_R, SLAB_K, SLAB_V, SLAB_H, SLAB_W, SLAB_A, SLAB_G = range(7)
MU_SLABS = (SLAB_R, SLAB_W, SLAB_K, SLAB_V, SLAB_A, SLAB_G)


class _Tiling(NamedTuple):
    groups: int
    t: int
    gs: int
    tq: int

    @property
    def grid(self):
        return (self.groups // self.gs, self.t // self.tq)

    @property
    def seqs(self):
        return self.gs * GROUP

    def plain(self, cols, col=0):
        return pl.BlockSpec((self.seqs, self.tq, cols), lambda ig, it: (ig, it, col))

    def head_lanes(self, blk=0):
        return pl.BlockSpec((self.gs, HEAD_DIM, self.tq, LANES), lambda ig, it: (ig, blk, it, 0))

    def per_seq(self, rows, cols):
        return pl.BlockSpec((self.seqs, rows, cols), lambda ig, it: (ig, 0, 0))


def _cparams(n_axes):
    return pltpu.CompilerParams(dimension_semantics=("arbitrary",) * n_axes,
                                vmem_limit_bytes=V7X_VMEM_LIMIT)


def _const_spec(shape):
    return pl.BlockSpec(shape, lambda *_: (0,) * len(shape), pipeline_mode=pl.Buffered(1))


def _sigmoid(x):
    return 1.0 / (1.0 + jnp.exp(-x))


def _silu(x):
    return x * _sigmoid(x)


def _softplus(x):
    return jnp.maximum(x, 0.0) + jnp.log(1.0 + jnp.exp(-jnp.abs(x)))


def _swap_lane_groups(m, lane):
    half = lane < 2 * (LANES // GROUP)
    odd = (lane // (LANES // GROUP)) % 2 == 1
    n = [None] * 4
    for s in range(2):
        n[s] = jnp.where(half, m[s], pltpu.roll(m[s + 2], LANES // 2, axis=1))
        n[s + 2] = jnp.where(half, pltpu.roll(m[s], LANES // 2, axis=1), m[s + 2])
    out = [None] * 4
    for p in (0, 2):
        a, b = n[p], n[p + 1]
        out[p] = jnp.where(odd, pltpu.roll(b, LANES // GROUP, axis=1), a)
        out[p + 1] = jnp.where(odd, b, pltpu.roll(a, LANES - LANES // GROUP, axis=1))
    return out


def _store_head_lanes(val, out_ref, gs, tq, q0=0):
    lane = lax.broadcasted_iota(jnp.int32, (tq, LANES), 1)
    for g in range(gs):
        for q in range(val.shape[1] // LANES):
            src = [val[(g * GROUP + s) * tq:(g * GROUP + s + 1) * tq, q * LANES:(q + 1) * LANES]
                   for s in range(GROUP)]
            out = _swap_lane_groups(src, lane)
            for j in range(GROUP):
                out_ref[g, GROUP * (q0 + q) + j] = out[j]


def _rows2d(ref):
    v = ref[...]
    return v.reshape(-1, v.shape[-1])


def _mod_kernel(c_ref, w_ref, b_ref, o_ref):
    s = _silu(c_ref[...]).astype(BF16)
    o_ref[...] = jnp.dot(s, w_ref[...].astype(BF16), preferred_element_type=F32) + b_ref[...]


def _modulation(c, w_ada, b_ada):
    rows, d = c.shape
    n = w_ada.shape[1]
    tn = 1024
    return pl.pallas_call(
        _mod_kernel,
        grid=(n // tn,),
        in_specs=[pl.BlockSpec((rows, d), lambda j: (0, 0)),
                  pl.BlockSpec((d, tn), lambda j: (0, j)),
                  pl.BlockSpec((1, tn), lambda j: (0, j))],
        out_specs=pl.BlockSpec((rows, tn), lambda j: (0, j)),
        out_shape=jax.ShapeDtypeStruct((rows, n), F32),
        compiler_params=_cparams(1),
        name="modulation",
    )(c, w_ada, b_ada.reshape(1, n))


def _prep_kernel(x_ref, m_ref, g_ref, mu_ref, s0_ref, xs_ref, so_ref, hbuf, *, tm):
    x = x_ref[...]
    m = m_ref[...]
    ms = jnp.mean(x * x, axis=-1, keepdims=True)
    h = (x * lax.rsqrt(ms + 1e-6) * g_ref[...]) * (1.0 + m[1:2, :]) + m[0:1, :]
    hbuf[SUBLANES:SUBLANES + tm, :] = h

    @pl.when(pl.program_id(1) == 0)
    def _():
        hbuf[SUBLANES - 1:SUBLANES, :] = s0_ref[...]

    dx = hbuf[SUBLANES - 1:SUBLANES - 1 + tm, :] - h
    for mi, slab in enumerate(MU_SLABS):
        xs_ref[slab] = (h + dx * mu_ref[mi:mi + 1, :]).astype(BF16)
    xs_ref[SLAB_H] = h.astype(BF16)
    last = h[tm - 1:tm, :]
    hbuf[SUBLANES - 1:SUBLANES, :] = last
    so_ref[...] = last


def _prep(x, mod, norm_g, mu, shift0, *, tm):
    b, t, d = x.shape
    return pl.pallas_call(
        functools.partial(_prep_kernel, tm=tm),
        grid=(b, t // tm),
        in_specs=[pl.BlockSpec((None, tm, d), lambda i, j: (i, j, 0)),
                  pl.BlockSpec((None, N_MOD, d), lambda i, j: (i, 0, 0)),
                  _const_spec((1, d)),
                  _const_spec((N_MOD, d)),
                  pl.BlockSpec((None, 1, d), lambda i, j: (i, 0, 0))],
        out_specs=[pl.BlockSpec((7, None, tm, d), lambda i, j: (0, i, j, 0)),
                   pl.BlockSpec((None, 1, d), lambda i, j: (i, 0, 0))],
        out_shape=[jax.ShapeDtypeStruct((7, b, t, d), BF16),
                   jax.ShapeDtypeStruct((b, 1, d), F32)],
        scratch_shapes=[pltpu.VMEM((SUBLANES + tm, d), F32)],
        compiler_params=_cparams(2),
        name="prep",
    )(x, mod, norm_g.reshape(1, d), mu, shift0.reshape(b, 1, d))


def _gate_blocks(gate, rows, tm, rows_per_gate):
    if gate.ndim == 2:
        return gate.reshape(rows // tm, tm, gate.shape[-1]), 1
    assert rows_per_gate % tm == 0, (rows_per_gate, tm)
    return gate, rows_per_gate // tm


def _mm_kernel(*refs, gated, sigmoid):
    if gated:
        a_ref, w_ref, res_ref, gate_ref, o_ref, wb = refs
    else:
        a_ref, w_ref, o_ref, wb = refs

    @pl.when(pl.program_id(1) == 0)
    def _():
        wb[...] = w_ref[...].astype(BF16)

    acc = jnp.dot(a_ref[...], wb[...], preferred_element_type=F32)
    if gated:
        acc = res_ref[...] + gate_ref[...] * acc
    if sigmoid:
        acc = _sigmoid(acc)
    o_ref[...] = acc.astype(o_ref.dtype)


def _matmul(a, w, *, tm, tn, slab=0, col0=0, n=None, res=None, gate=None, rows_per_gate=None,
            sigmoid=False, out_dtype=F32, name):
    _, rows, k = a.shape
    n = w.shape[1] if n is None else n
    gated = res is not None
    in_specs = [pl.BlockSpec((None, tm, k), lambda j, i: (slab, i, 0)),
                pl.BlockSpec((k, tn), lambda j, i: (0, j + col0))]
    args = [a, w]
    if gated:
        gate, tiles_per_gate = _gate_blocks(gate, rows, tm, rows_per_gate)
        gr = gate.shape[1]
        in_specs += [pl.BlockSpec((tm, tn), lambda j, i: (i, j)),
                     pl.BlockSpec((None, gr, tn), lambda j, i: (i // tiles_per_gate, 0, j))]
        args += [res, gate]
    return pl.pallas_call(
        functools.partial(_mm_kernel, gated=gated, sigmoid=sigmoid),
        grid=(pl.cdiv(n, tn), rows // tm),
        in_specs=in_specs,
        out_specs=pl.BlockSpec((tm, tn), lambda j, i: (i, j)),
        out_shape=jax.ShapeDtypeStruct((rows, n), out_dtype),
        scratch_shapes=[pltpu.VMEM((k, tn), BF16)],
        compiler_params=_cparams(2),
        name=name,
    )(*args)


def _proj_hl_kernel(a_ref, w_ref, o_ref, *, gs, tq):
    for c in range(0, w_ref.shape[1], MXU_COLS):
        acc = jnp.dot(_rows2d(a_ref), w_ref[:, c:c + MXU_COLS], preferred_element_type=F32)
        _store_head_lanes(acc, o_ref, gs, tq, q0=c // LANES)


def _proj_head_lanes(xs, w, tl, *, tn, n):
    k = w.shape[0]
    d = n // 3
    slab = lambda j: j * tn // d
    a_spec = pl.BlockSpec((None, tl.seqs, tl.tq, k), lambda j, ig, it: (slab(j), ig, it, 0))
    nq = tn // (LANES // GROUP)
    return pl.pallas_call(
        functools.partial(_proj_hl_kernel, gs=tl.gs, tq=tl.tq),
        grid=(n // tn,) + tl.grid,
        in_specs=[a_spec, pl.BlockSpec((k, tn), lambda j, ig, it: (0, j))],
        out_specs=pl.BlockSpec((tl.gs, nq, tl.tq, LANES), lambda j, ig, it: (ig, j, it, 0)),
        out_shape=jax.ShapeDtypeStruct((tl.groups, n // (LANES // GROUP), tl.t, LANES), F32),
        compiler_params=_cparams(3),
        name="rkv_proj",
    )(xs, w)


def _lora_heads(xw_ref, xa_ref, xg_ref, w1, w2, a1, a2, g1, g2, w0, a0):
    dot = functools.partial(jnp.dot, preferred_element_type=F32)
    hw = jnp.tanh(dot(_rows2d(xw_ref), w1[...])).astype(BF16)
    w_pre = w0[...] + dot(hw, w2[...])
    dec = jnp.exp(-jnp.exp(-_softplus(-w_pre) - 0.5))
    ha = dot(_rows2d(xa_ref), a1[...]).astype(BF16)
    a = _sigmoid(a0[...] + dot(ha, a2[...]))
    hg = _sigmoid(dot(_rows2d(xg_ref), g1[...])).astype(BF16)
    return dec, a, dot(hg, g2[...])


def _lora_kernel(*refs, gs, tq):
    dec_ref, a_ref, g_ref = refs[-3:]
    dec, a, g = _lora_heads(*refs[:-3])
    _store_head_lanes(dec, dec_ref, gs, tq)
    _store_head_lanes(a, a_ref, gs, tq)
    g_ref[...] = g.reshape(g_ref.shape)


def _pad_lora(w_a, w_b):
    r = w_a.shape[1]
    return (jnp.pad(w_a, ((0, 0), (0, LORA_PAD - r))).astype(BF16),
            jnp.pad(w_b, ((0, LORA_PAD - r), (0, 0))).astype(BF16))


def _lora(xs, w0, w1, w2, a0, a1, a2, g1, g2, tl):
    d = w0.shape[0]
    w1p, w2p = _pad_lora(w1, _head_minor(w2))
    a1p, a2p = _pad_lora(a1, _head_minor(a2))
    g1b, g2b = g1.astype(BF16), _head_minor(g2).astype(BF16)
    slab = lambda s: pl.BlockSpec((None, tl.seqs, tl.tq, d), lambda ig, it: (s, ig, it, 0))
    g_shape = (tl.groups * GROUP, tl.t, d)
    hl_shape = jax.ShapeDtypeStruct((tl.groups, HEAD_DIM, tl.t, LANES), F32)
    consts = [w1p, w2p, a1p, a2p, g1b, g2b, _head_minor(w0).reshape(1, d),
              _head_minor(a0).reshape(1, d)]
    return pl.pallas_call(
        functools.partial(_lora_kernel, gs=tl.gs, tq=tl.tq),
        grid=tl.grid,
        in_specs=[slab(SLAB_W), slab(SLAB_A), slab(SLAB_G)] + [_const_spec(c.shape) for c in consts],
        out_specs=[tl.head_lanes(), tl.head_lanes(), tl.plain(d)],
        out_shape=[hl_shape, hl_shape, jax.ShapeDtypeStruct(g_shape, F32)],
        compiler_params=_cparams(2),
        name="lora",
    )(xs, xs, xs, *consts)


def _wkv_step(st, sa, v, w_row, b_row, k2_row, r_row, kk_next_row):
    y = jnp.zeros((HEAD_DIM, LANES), F32)
    sa_next = jnp.zeros((HEAD_DIM, LANES), F32)
    for i in range(HEAD_DIM):
        decayed = st[i] if w_row is None else st[i] * w_row(i)
        si = decayed - sa * b_row(i) + v * k2_row(i)
        st[i] = si
        y = y + si * r_row(i)
        sa_next = sa_next + si * kk_next_row(i)
    return y, sa_next


def _wkv_kernel(r_ref, k_ref, v_ref, a_ref, d_ref, kk_p, ka_p, rk_p, lg_p, lb_p, s0_ref,
                y_ref, so_ref, st, kk_s, b_s, k2_s, r_s, pe_s, bon_s, v_t, y_t, *, nb):
    n = HEAD_DIM
    tt = nb * SUBLANES
    bc = lambda p: p[...][:, None, None, :]

    @pl.when(pl.program_id(1) == 0)
    def _():
        s = s0_ref[...].reshape(LANES, n, n)
        st[...] = jnp.swapaxes(jnp.swapaxes(jnp.swapaxes(s, 0, 1), 1, 2), 0, 1)

    x = d_ref[...].reshape(n * nb, SUBLANES, LANES)
    step = lax.broadcasted_iota(jnp.int32, x.shape, 1)
    for sh in (1, 2, 4):
        x = x * jnp.where(step >= sh, pltpu.roll(x, sh, axis=1), 1.0)
    x = x.reshape(n, nb, SUBLANES, LANES)
    step = step.reshape(x.shape)
    run = jnp.ones((n, 1, LANES), F32)
    incl, excl = [], []
    for tb in range(nb):
        blk = x[:, tb] * run
        incl.append(blk)
        excl.append(jnp.where(step[:, tb] >= 1, pltpu.roll(blk, 1, axis=1), run))
        run = blk[:, SUBLANES - 1:SUBLANES, :]
    p_t = jnp.stack(incl, axis=1)
    inv_p = 1.0 / p_t

    k = k_ref[...]
    a = a_ref[...]
    r = r_ref[...]
    kk = k * bc(kk_p)
    kk = kk * (1.0 / jnp.maximum(jnp.sqrt(jnp.sum(kk * kk, axis=0, keepdims=True)), 1e-12))
    k2 = k * (1.0 + (a - 1.0) * bc(ka_p))
    kk_s[...] = kk * jnp.stack(excl, axis=1)
    b_s[...] = kk * a * inv_p
    k2_s[...] = k2 * inv_p
    r_s[...] = r * p_t
    pe_s[...] = p_t[:, nb - 1, SUBLANES - 1, :]
    bon_s[...] = jnp.sum(r * k2 * bc(rk_p), axis=0)
    v_t[...] = jnp.swapaxes(v_ref[...], 0, 1)

    sa0 = jnp.zeros((n, LANES), F32)
    for i in range(n):
        sa0 = sa0 + st[i] * kk_s[i, 0, 0:1, :]

    def block(tb, sa):
        tb_next = jnp.minimum(tb + 1, nb - 1)
        for j in range(SUBLANES):
            t = tb * SUBLANES + j
            row = lambda ref: (lambda i: ref[i, tb, j:j + 1, :])
            if j + 1 < SUBLANES:
                kk_next = lambda i: kk_s[i, tb, j + 1:j + 2, :]
            else:
                kk_next = lambda i: kk_s[i, tb_next, 0:1, :]
            y_t[t], sa = _wkv_step(st, sa, v_t[t], None, row(b_s), row(k2_s), row(r_s), kk_next)
        return sa

    lax.fori_loop(0, nb, block, sa0)
    for i in range(n):
        st[i] = st[i] * pe_s[i:i + 1, :]

    y = jnp.swapaxes(y_t[...], 0, 1)
    ym = jnp.mean(y, axis=0, keepdims=True)
    yc = y - ym
    yv = jnp.mean(yc * yc, axis=0, keepdims=True)
    bcast = lambda p: p[...][:, None, :]
    yn = yc * lax.rsqrt(yv + GN_EPS) * bcast(lg_p) + bcast(lb_p)
    y_ref[...] = yn + bon_s[...].reshape(1, tt, LANES) * v_ref[...]

    @pl.when(pl.program_id(1) == pl.num_programs(1) - 1)
    def _():
        s = jnp.swapaxes(jnp.swapaxes(jnp.swapaxes(st[...], 0, 1), 1, 2), 0, 1)
        so_ref[...] = s.reshape(so_ref.shape)


def _head_lanes(p):
    hn = p.reshape(-1, HEAD_DIM).T
    return jnp.tile(hn, (1, GROUP))


def _wkv(rkv, a, dec, s0, k_k, k_a, r_k, lnx_g, lnx_b, *, tt):
    g, _, t, lanes = a.shape
    n = HEAD_DIM
    heads = s0.shape[1]
    nb = tt // SUBLANES
    by8 = lambda z: z.reshape(z.shape[0], z.shape[1], t // SUBLANES, SUBLANES, lanes)
    params = [_head_lanes(p) for p in (k_k, k_a, r_k.reshape(-1), lnx_g, lnx_b)]
    rows = lambda blk: pl.BlockSpec((None, n, nb, SUBLANES, lanes), lambda i, j: (i, blk, j, 0, 0))
    tile = lambda blk: pl.BlockSpec((None, n, tt, lanes), lambda i, j: (i, blk, j, 0))
    st = pl.BlockSpec((GROUP, heads, n, n), lambda i, j: (i, 0, 0, 0))
    return pl.pallas_call(
        functools.partial(_wkv_kernel, nb=nb),
        grid=(g, t // tt),
        in_specs=[rows(0), rows(1), tile(2), rows(0), rows(0)] + [_const_spec((n, lanes))] * 5 + [st],
        out_specs=[tile(0), st],
        out_shape=[jax.ShapeDtypeStruct((g, n, t, lanes), F32),
                   jax.ShapeDtypeStruct(s0.shape, F32)],
        scratch_shapes=[pltpu.VMEM((n, n, lanes), F32)]
        + [pltpu.VMEM((n, nb, SUBLANES, lanes), F32)] * 4
        + [pltpu.VMEM((n, lanes), F32), pltpu.VMEM((nb, SUBLANES, lanes), F32)]
        + [pltpu.VMEM((tt, n, lanes), F32)] * 2,
        compiler_params=_cparams(2),
        name="wkv7",
    )(by8(rkv), by8(rkv), rkv, by8(a), by8(dec), *params, s0)


CONV_PAD = 32
CONV_ROWS = 32


def _conv_kernel(u1_ref, u2_ref, bg1, bg2, dwk, dwb, lg, lb, st0_ref, y_ref, sto_ref, zbuf, cbuf,
                 zs, *, tm):
    hist = CONV_W - 1
    lo = CONV_PAD - hist
    d = u1_ref.shape[-1]
    rc = min(CONV_ROWS, tm)
    glu = (u1_ref[...] + bg1[...]) * _sigmoid(u2_ref[...] + bg2[...])
    zbuf[CONV_PAD:CONV_PAD + tm, :] = glu

    @pl.when(pl.program_id(1) == 0)
    def _():
        zbuf[lo:CONV_PAD, :] = st0_ref[...]

    for b in range(SUBLANES):
        span = tm + SUBLANES * ((hist - b) // SUBLANES)
        zs[b, 0:span, :] = zbuf[lo + b:lo + b + span, :]
    for r0 in range(0, tm, rc):
        for c0 in range(0, d, LANES):
            acc = jnp.zeros((rc, LANES), F32)
            for j in range(CONV_W):
                a8 = SUBLANES * (j // SUBLANES)
                acc = acc + (zs[j % SUBLANES, a8 + r0:a8 + r0 + rc, c0:c0 + LANES]
                             * dwk[j:j + 1, c0:c0 + LANES])
            cbuf[r0:r0 + rc, c0:c0 + LANES] = acc
    zc = cbuf[...] + dwb[...]
    m = jnp.mean(zc, axis=-1, keepdims=True)
    ctr = zc - m
    var = jnp.mean(ctr * ctr, axis=-1, keepdims=True)
    ln = ctr * lax.rsqrt(var + 1e-5) * lg[...] + lb[...]
    y_ref[...] = _silu(ln).astype(BF16)
    carry = zbuf[lo + tm:CONV_PAD + tm, :]
    zbuf[lo:CONV_PAD, :] = carry
    sto_ref[...] = carry


def _conv(glu_in, b_glu, dw_k, dw_b, ln_g, ln_b, st0, *, tm):
    b, t, _ = glu_in.shape
    d = dw_k.shape[1]
    hist = CONV_W - 1
    vec = _const_spec((1, d))
    return pl.pallas_call(
        functools.partial(_conv_kernel, tm=tm),
        grid=(b, t // tm),
        in_specs=[pl.BlockSpec((None, tm, d), lambda i, j: (i, j, 0)),
                  pl.BlockSpec((None, tm, d), lambda i, j: (i, j, 1)),
                  pl.BlockSpec((1, d), lambda i, j: (0, 0)),
                  pl.BlockSpec((1, d), lambda i, j: (0, 1)),
                  _const_spec((CONV_W, d)),
                  vec, vec, vec,
                  pl.BlockSpec((None, hist, d), lambda i, j: (i, 0, 0))],
        out_specs=[pl.BlockSpec((None, tm, d), lambda i, j: (i, j, 0)),
                   pl.BlockSpec((None, hist, d), lambda i, j: (i, 0, 0))],
        out_shape=[jax.ShapeDtypeStruct((b, t, d), BF16),
                   jax.ShapeDtypeStruct((b, hist, d), F32)],
        scratch_shapes=[pltpu.VMEM((CONV_PAD + tm, d), F32), pltpu.VMEM((tm, d), F32),
                        pltpu.VMEM((SUBLANES, tm + CONV_PAD - SUBLANES, d), F32)],
        compiler_params=_cparams(2),
        name="conformer_conv",
    )(glu_in, glu_in, b_glu.reshape(1, 2 * d), b_glu.reshape(1, 2 * d), dw_k, dw_b.reshape(1, d),
      ln_g.reshape(1, d), ln_b.reshape(1, d), st0)


def _merge_kernel(yw_ref, g_ref, ga_ref, gb_ref, yb_ref, x_ref, woa_ref, wob_ref, m_ref, n2_ref,
                  x1_ref, h2_ref, ybuf, *, gs, tq):
    seqs = gs * GROUP
    d = ybuf.shape[-1]
    lane = lax.broadcasted_iota(jnp.int32, (tq, LANES), 1)
    for g in range(gs):
        for q in range(d // LANES):
            out = _swap_lane_groups([yw_ref[g, GROUP * q + j] for j in range(GROUP)], lane)
            for s in range(GROUP):
                ybuf[g * GROUP + s, :, q * LANES:(q + 1) * LANES] = out[s]
    seq3 = lambda ref: ref[...].astype(F32).reshape(seqs, tq, d)
    m = m_ref[...]
    dot = lambda z, w_ref: jnp.dot(z.reshape(seqs * tq, d).astype(BF16), w_ref[...],
                                   preferred_element_type=F32)
    branch_a = seq3(ga_ref) * (ybuf[...] * seq3(g_ref))
    branch_b = seq3(gb_ref) * seq3(yb_ref)
    mix = (dot(branch_a, woa_ref) + dot(branch_b, wob_ref)).reshape(seqs, tq, d)
    x1 = seq3(x_ref) + m[:, 2:3, :] * mix
    x1_ref[...] = x1.reshape(x1_ref.shape)
    ms = jnp.mean(x1 * x1, axis=-1, keepdims=True)
    h2 = (x1 * lax.rsqrt(ms + 1e-6) * n2_ref[...]) * (1.0 + m[:, 4:5, :]) + m[:, 3:4, :]
    h2_ref[...] = h2.reshape(seqs * tq, d).astype(BF16).reshape(h2_ref.shape)


def _merge(yw, g, ga, gb, yb, x, w_out, mod, norm_g, tl):
    d = x.shape[-1]
    plain = tl.plain(d)
    w_out_b = w_out.astype(BF16)
    w_out_a = _head_minor(w_out.T).T.astype(BF16)
    return pl.pallas_call(
        functools.partial(_merge_kernel, gs=tl.gs, tq=tl.tq),
        grid=tl.grid,
        in_specs=[tl.head_lanes(), plain, plain, plain, plain, plain, _const_spec((d, d)),
                  _const_spec((d, d)), tl.per_seq(N_MOD, d), _const_spec((1, d))],
        out_specs=[plain, plain],
        out_shape=[jax.ShapeDtypeStruct(x.shape, F32), jax.ShapeDtypeStruct(x.shape, BF16)],
        scratch_shapes=[pltpu.VMEM((tl.seqs, tl.tq, d), F32)],
        compiler_params=_cparams(2),
        name="merge_out_proj",
    )(yw, g, ga, gb, yb, x, w_out_a, w_out_b, mod, norm_g.reshape(1, d))


FFN_TN = 512


def _pad_halves(w, width):
    f = w.shape[-1] // 2
    pad = [(0, 0)] * (w.ndim - 1) + [(0, width - f)]
    return jnp.concatenate([jnp.pad(w[..., :f], pad), jnp.pad(w[..., f:], pad)], axis=-1)


def _up_act_kernel(h_ref, wg_ref, wv_ref, kg_ref, kv_ref, bg_ref, bv_ref, sg_ref, sv_ref,
                   act_ref, stg_ref, stv_ref, carry, *, tq, tiles_per_seq):
    first = pl.program_id(1) % tiles_per_seq == 0
    tn = wg_ref.shape[-1]
    row = lax.broadcasted_iota(jnp.int32, (tq, tn), 0)
    z = []
    halves = ((wg_ref, kg_ref, bg_ref, stg_ref), (wv_ref, kv_ref, bv_ref, stv_ref))

    @pl.when(first)
    def _():
        carry[0] = sg_ref[...]
        carry[1] = sv_ref[...]

    for half, (w_ref, k_ref, b_ref, st_ref) in enumerate(halves):
        u = jnp.dot(h_ref[...], w_ref[...], preferred_element_type=F32)
        prev = carry[half]
        p0, p1 = prev[0:1, :], prev[1:2, :]
        u1 = jnp.where(row == 0, p1, pltpu.roll(u, 1, axis=0))
        u2 = jnp.where(row == 0, p0, jnp.where(row == 1, p1, pltpu.roll(u, 2, axis=0)))
        z.append(b_ref[...] + u2 * k_ref[0:1, :] + u1 * k_ref[1:2, :] + u * k_ref[2:3, :])
        last = u[tq - (FFN_CONV_W - 1):, :]
        carry[half] = last
        st_ref[...] = last
    act_ref[...] = (_silu(z[0]) * z[1]).astype(BF16)


def _up_act(h2, w_up, ffn_dw_k, ffn_dw_b, st0, *, tq, t):
    rows, k = h2.shape
    b = st0.shape[0]
    f = w_up.shape[1] // 2
    nj = f // FFN_TN
    hist = FFN_CONV_W - 1
    tiles_per_seq = t // tq
    col = lambda shape, h: pl.BlockSpec(shape, lambda j, i: (0, j + h * nj))
    state = lambda h: pl.BlockSpec((None, hist, FFN_TN),
                                   lambda j, i: (i // tiles_per_seq, 0, j + h * nj))
    return pl.pallas_call(
        functools.partial(_up_act_kernel, tq=tq, tiles_per_seq=tiles_per_seq),
        grid=(nj, rows // tq),
        in_specs=[pl.BlockSpec((tq, k), lambda j, i: (i, 0)),
                  col((k, FFN_TN), 0), col((k, FFN_TN), 1),
                  col((FFN_CONV_W, FFN_TN), 0), col((FFN_CONV_W, FFN_TN), 1),
                  col((1, FFN_TN), 0), col((1, FFN_TN), 1),
                  state(0), state(1)],
        out_specs=[pl.BlockSpec((tq, FFN_TN), lambda j, i: (i, j)), state(0), state(0)],
        out_shape=[jax.ShapeDtypeStruct((rows, f), BF16),
                   jax.ShapeDtypeStruct((b, hist, f), F32),
                   jax.ShapeDtypeStruct((b, hist, f), F32)],
        scratch_shapes=[pltpu.VMEM((2, hist, FFN_TN), F32)],
        compiler_params=_cparams(2),
        name="up_conv_act",
    )(h2, w_up, w_up, ffn_dw_k, ffn_dw_k, ffn_dw_b, ffn_dw_b, st0, st0)


def _norm_kernel(x_ref, g_ref, o_ref):
    x = x_ref[...]
    ms = jnp.mean(x * x, axis=-1, keepdims=True)
    o_ref[...] = x * lax.rsqrt(ms + 1e-6) * g_ref[...]


def _final_norm(x, g, *, tm):
    rows, d = x.shape
    row = pl.BlockSpec((tm, d), lambda i: (i, 0))
    return pl.pallas_call(
        _norm_kernel,
        grid=(rows // tm,),
        in_specs=[row, _const_spec((1, d))],
        out_specs=row,
        out_shape=jax.ShapeDtypeStruct((rows, d), F32),
        compiler_params=_cparams(1),
        name="final_norm",
    )(x, g.reshape(1, d))


def _prep_tm_kernel(x_ref, m_ref, g_ref, mu_ref, s0_ref, xs_ref, so_ref, hprev):
    @pl.when(pl.program_id(0) == 0)
    def _():
        hprev[...] = s0_ref[...]

    x = x_ref[...]
    ms = jnp.mean(x * x, axis=-1, keepdims=True)
    h = (x * lax.rsqrt(ms + 1e-6) * g_ref[...]) * (1.0 + m_ref[1]) + m_ref[0]
    dx = hprev[...] - h
    for mi, slab in enumerate(MU_SLABS):
        xs_ref[slab] = (h + dx * mu_ref[mi:mi + 1, :]).astype(BF16)
    xs_ref[SLAB_H] = h.astype(BF16)
    hprev[...] = h
    so_ref[...] = h


def _prep_tm(x, mod, norm_g, mu, shift0):
    t, b, d = x.shape
    return pl.pallas_call(
        _prep_tm_kernel,
        grid=(t,),
        in_specs=[pl.BlockSpec((None, b, d), lambda i: (i, 0, 0)), _const_spec((N_MOD, b, d)),
                  _const_spec((1, d)), _const_spec((N_MOD, d)), _const_spec((b, d))],
        out_specs=[pl.BlockSpec((7, b, d), lambda i: (0, i, 0)),
                   pl.BlockSpec((b, d), lambda i: (0, 0))],
        out_shape=[jax.ShapeDtypeStruct((7, t * b, d), BF16), jax.ShapeDtypeStruct((b, d), F32)],
        scratch_shapes=[pltpu.VMEM((b, d), F32)],
        compiler_params=_cparams(1),
        name="prep_tm",
    )(x, mod, norm_g.reshape(1, d), mu, shift0)


def _proj_t_kernel(a_ref, w_ref, o_ref):
    acc = jnp.dot(a_ref[...], w_ref[...].astype(BF16), preferred_element_type=F32)
    o_ref[...] = acc.T


def _proj_transposed(xs, w, *, tn, n):
    _, rows, k = xs.shape
    d = n // 3
    return pl.pallas_call(
        _proj_t_kernel,
        grid=(n // tn,),
        in_specs=[pl.BlockSpec((None, rows, k), lambda j: (j * tn // d, 0, 0)),
                  pl.BlockSpec((k, tn), lambda j: (0, j))],
        out_specs=pl.BlockSpec((tn, rows), lambda j: (j, 0)),
        out_shape=jax.ShapeDtypeStruct((n, rows), F32),
        compiler_params=_cparams(1),
        name="rkv_proj_tm",
    )(xs, w)


def _lora_t_kernel(*refs):
    dec_ref, a_ref, g_ref = refs[-3:]
    dec, a, g = _lora_heads(*refs[:-3])
    dec_ref[...] = dec.T
    a_ref[...] = a.T
    g_ref[...] = g


def _lora_transposed(xs, w0, w1, w2, a0, a1, a2, g1, g2):
    _, rows, d = xs.shape
    w1p, w2p = _pad_lora(w1, w2)
    a1p, a2p = _pad_lora(a1, a2)
    consts = [w1p, w2p, a1p, a2p, g1.astype(BF16), g2.astype(BF16), w0.reshape(1, d),
              a0.reshape(1, d)]
    slab = lambda s: pl.BlockSpec((None, rows, d), lambda i: (s, 0, 0))
    t_shape = jax.ShapeDtypeStruct((d, rows), F32)
    return pl.pallas_call(
        _lora_t_kernel,
        grid=(1,),
        in_specs=[slab(SLAB_W), slab(SLAB_A), slab(SLAB_G)] + [_const_spec(c.shape) for c in consts],
        out_specs=[pl.BlockSpec((d, rows), lambda i: (0, 0)), pl.BlockSpec((d, rows), lambda i: (0, 0)),
                   pl.BlockSpec((rows, d), lambda i: (0, 0))],
        out_shape=[t_shape, t_shape, jax.ShapeDtypeStruct((rows, d), F32)],
        compiler_params=_cparams(1),
        name="lora_tm",
    )(xs, xs, xs, *consts)


def _wkv_bl_kernel(r_ref, k_ref, v_ref, a_ref, d_ref, kk_p, ka_p, rk_p, lg_p, lb_p, s0_ref,
                   y_ref, so_ref, st, kk_s, b_s, k2_s, *, steps):
    n = HEAD_DIM
    tile = lambda ref, t: ref[:, t * LANES:(t + 1) * LANES]
    st[...] = jnp.swapaxes(s0_ref[...], 0, 1)
    bonus = []
    for t in range(steps):
        k = tile(k_ref, t)
        a = tile(a_ref, t)
        kk = k * kk_p[...]
        kk = kk / jnp.maximum(jnp.sqrt(jnp.sum(kk * kk, axis=0, keepdims=True)), 1e-12)
        k2 = k * (1.0 + (a - 1.0) * ka_p[...])
        kk_s[:, t * LANES:(t + 1) * LANES] = kk
        b_s[:, t * LANES:(t + 1) * LANES] = kk * a
        k2_s[:, t * LANES:(t + 1) * LANES] = k2
        bonus.append(jnp.sum(tile(r_ref, t) * k2 * rk_p[...], axis=0, keepdims=True))

    sa = jnp.zeros((n, LANES), F32)
    for i in range(n):
        sa = sa + st[i] * kk_s[i:i + 1, 0:LANES]
    for t in range(steps):
        t_next = min(t + 1, steps - 1)
        row = lambda ref, t=t: (lambda i: ref[i:i + 1, t * LANES:(t + 1) * LANES])
        v = tile(v_ref, t)
        y, sa = _wkv_step(st, sa, v, row(d_ref), row(b_s), row(k2_s), row(r_ref),
                          row(kk_s, t_next))
        ym = jnp.mean(y, axis=0, keepdims=True)
        yc = y - ym
        yv = jnp.mean(yc * yc, axis=0, keepdims=True)
        yn = yc * lax.rsqrt(yv + GN_EPS) * lg_p[...] + lb_p[...]
        y_ref[:, t * LANES:(t + 1) * LANES] = yn + bonus[t] * v
    so_ref[...] = jnp.swapaxes(st[...], 0, 1)


def _wkv_batch_lanes(rkv, a, dec, s0, k_k, k_a, r_k, lnx_g, lnx_b, *, steps):
    d, cols = a.shape
    heads, n = s0.shape[0], HEAD_DIM
    params = [jnp.broadcast_to(p.reshape(heads, n, 1), (heads, n, LANES))
              for p in (k_k, k_a, r_k, lnx_g, lnx_b)]
    chan = lambda c: pl.BlockSpec((n, cols), lambda h: (c * heads + h, 0))
    par = pl.BlockSpec((None, n, LANES), lambda h: (h, 0, 0))
    st = pl.BlockSpec((None, n, n, LANES), lambda h: (h, 0, 0, 0))
    return pl.pallas_call(
        functools.partial(_wkv_bl_kernel, steps=steps),
        grid=(heads,),
        in_specs=[chan(0), chan(1), chan(2), chan(0), chan(0)] + [par] * 5 + [st],
        out_specs=[chan(0), st],
        out_shape=[jax.ShapeDtypeStruct((d, cols), F32), jax.ShapeDtypeStruct(s0.shape, F32)],
        scratch_shapes=[pltpu.VMEM((n, n, LANES), F32)] + [pltpu.VMEM((n, cols), F32)] * 3,
        compiler_params=_cparams(1),
        name="wkv7_tm",
    )(rkv, rkv, rkv, a, dec, *params, s0)


CONV_TM_COLS = 512


def _conv_tm_kernel(u1_ref, u2_ref, bg1, bg2, dwk, dwb, lg, lb, st_ref, y_ref, sto_ref, zc,
                    *, steps, b):
    j = pl.program_id(0)
    hist = CONV_W - 1
    tc = u1_ref.shape[-1]
    for c0 in range(0, tc, LANES):
        cs = slice(c0, c0 + LANES)
        glu = [(u1_ref[t * b:(t + 1) * b, cs] + bg1[:, cs])
               * _sigmoid(u2_ref[t * b:(t + 1) * b, cs] + bg2[:, cs]) for t in range(steps)]
        window = lambda i: st_ref[i, :, cs] if i < hist else glu[i - hist]
        for t in range(steps):
            acc = jnp.zeros((b, LANES), F32)
            for tap in range(CONV_W):
                acc = acc + window(t + tap) * dwk[tap:tap + 1, cs]
            zc[j, t * b:(t + 1) * b, cs] = acc + dwb[:, cs]
        for i in range(hist):
            sto_ref[i, :, cs] = window(i + steps)

    @pl.when(j == pl.num_programs(0) - 1)
    def _():
        z = zc[...]
        m = jnp.mean(jnp.mean(z, axis=-1, keepdims=True), axis=0, keepdims=True)
        ctr = z - m
        var = jnp.mean(jnp.mean(ctr * ctr, axis=-1, keepdims=True), axis=0, keepdims=True)
        ln = ctr * lax.rsqrt(var + 1e-5)
        for jj in range(zc.shape[0]):
            cols = slice(jj * tc, (jj + 1) * tc)
            y_ref[:, cols] = _silu(ln[jj] * lg[:, cols] + lb[:, cols]).astype(BF16)


def _conv_tm(glu_in, b_glu, dw_k, dw_b, ln_g, ln_b, st0, *, steps):
    rows, _ = glu_in.shape
    hist, b, d = st0.shape
    tc = CONV_TM_COLS
    nct = d // tc
    col = lambda shape, h=0: pl.BlockSpec(shape, lambda j: (0, j + h * nct))
    state = pl.BlockSpec((hist, b, tc), lambda j: (0, 0, j))
    return pl.pallas_call(
        functools.partial(_conv_tm_kernel, steps=steps, b=b),
        grid=(nct,),
        in_specs=[col((rows, tc)), col((rows, tc), 1), col((1, tc)), col((1, tc), 1),
                  col((CONV_W, tc)), col((1, tc)), _const_spec((1, d)), _const_spec((1, d)), state],
        out_specs=[pl.BlockSpec((rows, d), lambda j: (0, 0)), state],
        out_shape=[jax.ShapeDtypeStruct((rows, d), BF16), jax.ShapeDtypeStruct(st0.shape, F32)],
        scratch_shapes=[pltpu.VMEM((nct, rows, tc), F32)],
        compiler_params=_cparams(1),
        name="conformer_conv_tm",
    )(glu_in, glu_in, b_glu.reshape(1, 2 * d), b_glu.reshape(1, 2 * d), dw_k, dw_b.reshape(1, d),
      ln_g.reshape(1, d), ln_b.reshape(1, d), st0)


def _merge_tm_kernel(yt_ref, g_ref, ga_ref, gb_ref, yb_ref, x_ref, wo_ref, m_ref, n2_ref,
                     x1_ref, h2_ref):
    f32 = lambda ref: ref[...].astype(F32)
    merged = f32(ga_ref) * (yt_ref[...].T * g_ref[...]) + f32(gb_ref) * f32(yb_ref)
    mix = jnp.dot(merged.astype(BF16), wo_ref[...], preferred_element_type=F32)
    x1 = x_ref[...] + m_ref[2] * mix
    x1_ref[...] = x1
    ms = jnp.mean(x1 * x1, axis=-1, keepdims=True)
    h2 = (x1 * lax.rsqrt(ms + 1e-6) * n2_ref[...]) * (1.0 + m_ref[4]) + m_ref[3]
    h2_ref[...] = h2.astype(BF16)


def _merge_tm(yt, g, ga, gb, yb, x, w_out, mod, norm_g, *, steps):
    rows, d = x.shape
    b = rows // steps
    row = pl.BlockSpec((b, d), lambda i: (i, 0))
    return pl.pallas_call(
        _merge_tm_kernel,
        grid=(steps,),
        in_specs=[pl.BlockSpec((d, b), lambda i: (0, i)), row, row, row, row, row,
                  _const_spec((d, d)), _const_spec((N_MOD, b, d)), _const_spec((1, d))],
        out_specs=[row, row],
        out_shape=[jax.ShapeDtypeStruct((rows, d), F32), jax.ShapeDtypeStruct((rows, d), BF16)],
        compiler_params=_cparams(1),
        name="merge_out_proj_tm",
    )(yt, g, ga, gb, yb, x, w_out, mod, norm_g.reshape(1, d))


def _up_act_tm_kernel(h_ref, wg_ref, wv_ref, kg_ref, kv_ref, bg_ref, bv_ref, sg_ref, sv_ref,
                      act_ref, stg_ref, stv_ref, *, steps, b):
    hist = FFN_CONV_W - 1
    h = h_ref[...]
    tn = wg_ref.shape[-1]
    z = []
    halves = ((wg_ref, kg_ref, bg_ref, sg_ref, stg_ref), (wv_ref, kv_ref, bv_ref, sv_ref, stv_ref))
    for w_ref, k_ref, b_ref, s_ref, st_ref in halves:
        u = jnp.dot(h, w_ref[...], preferred_element_type=F32).reshape(steps, b, tn)
        window = [s_ref[i] for i in range(hist)] + [u[t] for t in range(steps)]
        z.append(jnp.stack([b_ref[...] + sum(window[t + tap] * k_ref[tap:tap + 1, :]
                                             for tap in range(FFN_CONV_W))
                            for t in range(steps)]))
        for i in range(hist):
            st_ref[i] = window[steps + i]
    act_ref[...] = (_silu(z[0]) * z[1]).reshape(steps * b, tn).astype(BF16)


def _up_act_tm(h2, w_up, ffn_dw_k, ffn_dw_b, st0, *, steps):
    rows, k = h2.shape
    hist, b, _ = st0.shape
    f = w_up.shape[1] // 2
    nj = f // FFN_TN
    col = lambda shape, h: pl.BlockSpec(shape, lambda j: (0, j + h * nj))
    state = lambda h: pl.BlockSpec((hist, b, FFN_TN), lambda j: (0, 0, j + h * nj))
    return pl.pallas_call(
        functools.partial(_up_act_tm_kernel, steps=steps, b=b),
        grid=(nj,),
        in_specs=[_const_spec((rows, k)), col((k, FFN_TN), 0), col((k, FFN_TN), 1),
                  col((FFN_CONV_W, FFN_TN), 0), col((FFN_CONV_W, FFN_TN), 1),
                  col((1, FFN_TN), 0), col((1, FFN_TN), 1), state(0), state(1)],
        out_specs=[pl.BlockSpec((rows, FFN_TN), lambda j: (0, j)), state(0), state(0)],
        out_shape=[jax.ShapeDtypeStruct((rows, f), BF16),
                   jax.ShapeDtypeStruct((hist, b, f), F32),
                   jax.ShapeDtypeStruct((hist, b, f), F32)],
        compiler_params=_cparams(1),
        name="up_conv_act_tm",
    )(h2, w_up, w_up, ffn_dw_k, ffn_dw_k, ffn_dw_b, ffn_dw_b, st0, st0)


def _sample_layer(x, mod, shift0, wkv0, conv0, ffn0, p):
    b, t, d = x.shape
    assert b == LANES
    rows = t * b
    tm = lambda z: jnp.swapaxes(z, 0, 1)
    w_in = p["w_in"]
    x_tm = tm(x)
    mod_tm = tm(mod)
    xs, shift1 = _prep_tm(x_tm, mod_tm, p["norm1_g"], p["mu"], shift0)

    tn = 1024
    rkv = _proj_transposed(xs, w_in, tn=tn, n=3 * d)
    proj = functools.partial(_matmul, xs, w_in, tm=rows, tn=tn, slab=SLAB_H)
    glu_in = proj(col0=3 * d // tn, n=2 * d, name="glu_proj")
    ga = proj(col0=5 * d // tn, n=d, sigmoid=True, out_dtype=BF16, name="gate_a_proj")
    gb = proj(col0=6 * d // tn, n=d, sigmoid=True, out_dtype=BF16, name="gate_b_proj")
    dec, a, g = _lora_transposed(xs, p["w0"], p["w1"], p["w2"], p["a0"], p["a1"], p["a2"],
                                 p["g1"], p["g2"])

    yt, s1 = _wkv_batch_lanes(rkv, a, dec, wkv0.transpose(1, 2, 3, 0), p["k_k"], p["k_a"],
                              p["r_k"], p["lnx_g"], p["lnx_b"], steps=t)
    wkv1 = s1.transpose(3, 0, 1, 2)

    yb, conv1 = _conv_tm(glu_in, p["b_glu"], p["dw_k"], p["dw_b"], p["ln_conv_g"],
                         p["ln_conv_b"], tm(conv0), steps=t)
    x1, h2 = _merge_tm(yt, g, ga, gb, yb, x_tm.reshape(rows, d), p["w_out"].astype(BF16), mod_tm,
                       p["norm2_g"], steps=t)

    d_ff = p["w_down"].shape[0]
    f_pad = -(-d_ff // FFN_TN) * FFN_TN
    act, ffn_g, ffn_v = _up_act_tm(
        h2, _pad_halves(p["w_up"], f_pad).astype(BF16), _pad_halves(p["ffn_dw_k"], f_pad),
        _pad_halves(p["ffn_dw_b"], f_pad).reshape(1, 2 * f_pad), _pad_halves(tm(ffn0), f_pad),
        steps=t)
    ffn1 = tm(jnp.concatenate([ffn_g[..., :d_ff], ffn_v[..., :d_ff]], axis=-1))
    w_down = jnp.pad(p["w_down"], ((0, f_pad - d_ff), (0, 0))).astype(BF16)
    x2 = _matmul(act[None], w_down, tm=rows, tn=512, res=x1, gate=jnp.tile(mod_tm[5], (t, 1)),
                 name="down_proj")
    y = _final_norm(x2, p["normf_g"], tm=min(256, rows))
    return tm(y.reshape(t, b, d)), shift1, wkv1, tm(conv1), ffn1


def _head_minor(w):
    lead = w.shape[:-1]
    return w.reshape(*lead, -1, HEAD_DIM).swapaxes(-1, -2).reshape(*lead, w.shape[-1])


def _prompt_layer(x, mod, shift0, wkv0, conv0, ffn0, p, *, tm_seq, tq_proj, tq_tok, tm_mm, tt,
                  tq_ffn):
    b, t, d = x.shape
    rows = b * t
    groups = b // GROUP
    w_in = p["w_in"]

    xs, shift1 = _prep(x, mod, p["norm1_g"], p["mu"], shift0, tm=tm_seq)
    xs2 = xs.reshape(7, rows, d)

    w_hm = jnp.concatenate([_head_minor(w_in[:, i * d:(i + 1) * d]) for i in (0, 1, 2, 5)],
                           axis=1).astype(BF16)
    rkv = _proj_head_lanes(xs, w_hm, _Tiling(groups, t, 1, tq_proj), tn=d, n=3 * d)
    tn = 1024
    ga = _matmul(xs2, w_hm, tm=tm_mm, tn=tn, slab=SLAB_H, col0=3 * d // tn, n=d, sigmoid=True,
                 out_dtype=BF16, name="gate_a_proj")
    glu_in = _matmul(xs2, w_in, tm=tm_mm, tn=tn, slab=SLAB_H, col0=3 * d // tn, n=2 * d,
                     name="glu_proj")
    gb = _matmul(xs2, w_in, tm=tm_mm, tn=tn, slab=SLAB_H, col0=6 * d // tn, n=d, sigmoid=True,
                 out_dtype=BF16, name="gate_b_proj")

    tl = _Tiling(groups, t, 1, tq_tok)
    dec, a, g = _lora(xs, p["w0"], p["w1"], p["w2"], p["a0"], p["a1"], p["a2"], p["g1"], p["g2"], tl)
    yw, wkv1 = _wkv(rkv, a, dec, wkv0, p["k_k"], p["k_a"], p["r_k"], p["lnx_g"], p["lnx_b"], tt=tt)
    yb, conv1 = _conv(glu_in.reshape(b, t, 2 * d), p["b_glu"], p["dw_k"], p["dw_b"],
                      p["ln_conv_g"], p["ln_conv_b"], conv0, tm=tm_seq)
    x1, h2 = _merge(yw, g, ga.reshape(b, t, d), gb.reshape(b, t, d), yb, x, p["w_out"], mod,
                    p["norm2_g"], tl)

    d_ff = p["w_down"].shape[0]
    f_pad = -(-d_ff // FFN_TN) * FFN_TN
    act, ffn_g, ffn_v = _up_act(
        h2.reshape(rows, d), _pad_halves(p["w_up"], f_pad).astype(BF16),
        _pad_halves(p["ffn_dw_k"], f_pad), _pad_halves(p["ffn_dw_b"], f_pad).reshape(1, 2 * f_pad),
        _pad_halves(ffn0, f_pad), tq=tq_ffn, t=t)
    ffn1 = jnp.concatenate([ffn_g[..., :d_ff], ffn_v[..., :d_ff]], axis=-1)
    w_down = jnp.pad(p["w_down"], ((0, f_pad - d_ff), (0, 0))).astype(BF16)
    x2 = _matmul(act[None], w_down, tm=min(tm_mm, 512), tn=512, res=x1.reshape(rows, d),
                 gate=mod[:, 5][:, None, :], rows_per_gate=t, name="down_proj")
    y = _final_norm(x2, p["normf_g"], tm=min(256, rows))
    return y.reshape(b, t, d), shift1.reshape(b, d), wkv1, conv1, ffn1


def kernel(x_prompt, x_sample, state_shift, state_wkv, state_conv, state_ffn, c_prompt, c_sample,
           norm1_g, norm2_g, normf_g, w_ada, b_ada, mu, w_in, b_glu, w0, w1, w2, a0, a1, a2, g1, g2,
           k_k, k_a, r_k, lnx_g, lnx_b, dw_k, dw_b, ln_conv_g, ln_conv_b, w_out, w_up, ffn_dw_k,
           ffn_dw_b, w_down):
    p = dict(norm1_g=norm1_g, norm2_g=norm2_g, normf_g=normf_g, mu=mu, w_in=w_in, b_glu=b_glu,
             w0=w0, w1=w1, w2=w2, a0=a0, a1=a1, a2=a2, g1=g1, g2=g2, k_k=k_k, k_a=k_a, r_k=r_k,
             lnx_g=lnx_g, lnx_b=lnx_b, dw_k=dw_k, dw_b=dw_b, ln_conv_g=ln_conv_g,
             ln_conv_b=ln_conv_b, w_out=w_out, w_up=w_up, ffn_dw_k=ffn_dw_k, ffn_dw_b=ffn_dw_b,
             w_down=w_down)
    bp, _, d = x_prompt.shape
    bs = x_sample.shape[0]
    f2 = w_up.shape[1]

    c_all = jnp.concatenate([c_prompt, c_sample], axis=0)
    c_rows = -(-c_all.shape[0] // SUBLANES) * SUBLANES
    c_all = jnp.pad(c_all, ((0, c_rows - c_all.shape[0]), (0, 0)))
    mod = _modulation(c_all, w_ada, b_ada).reshape(c_rows, N_MOD, d)
    mod_p, mod_s = mod[:bp], mod[bp:bp + bs]

    y_p, shift_p, wkv_p, conv_p, ffn_p = _prompt_layer(
        x_prompt, mod_p, jnp.zeros((bp, d), F32),
        jnp.zeros((bp, d // HEAD_DIM, HEAD_DIM, HEAD_DIM), F32),
        jnp.zeros((bp, CONV_W - 1, d), F32), jnp.zeros((bp, FFN_CONV_W - 1, f2), F32), p,
        tm_seq=256, tq_proj=256, tq_tok=64, tm_mm=1024, tt=32, tq_ffn=1024)

    y_s, shift_s, wkv_s, conv_s, ffn_s = _sample_layer(
        x_sample, mod_s, state_shift, state_wkv, state_conv, state_ffn, p)
    return (y_p, y_s, shift_p, wkv_p, conv_p, ffn_p, shift_s, wkv_s, conv_s, ffn_s)
```

```python
import functools
from typing import NamedTuple

import jax
import jax.numpy as jnp
from jax import lax
from jax.experimental import pallas as pl
from jax.experimental.pallas import tpu as pltpu

F32 = jnp.float32
BF16 = jnp.bfloat16

HEAD_DIM = 64
CONV_W = 31
FFN_CONV_W = 3
N_MOD = 6
GN_EPS = HEAD_DIM * 1e-5
SUBLANES = 8
LANES = 128
MXU_COLS = 256
LORA_PAD = 128
GROUP = 4
V7X_VMEM_LIMIT = 56 * 1024 * 1024

PROJ_TN = 1024
MOD_TN = 2048
DOWN_TM, DOWN_TN = 512, 512
NORM_TM = 256
PROMPT_TILES = dict(tm_seq=256, tq_proj=256, tq_tok=64, tm_mm=1024, tt=32, tq_ffn=1024)

SLAB_R, SLAB_K, SLAB_V, SLAB_H, SLAB_W, SLAB_A, SLAB_G = range(7)
MU_SLABS = (SLAB_R, SLAB_W, SLAB_K, SLAB_V, SLAB_A, SLAB_G)


class _Tiling(NamedTuple):
    groups: int
    t: int
    gs: int
    tq: int

    @property
    def grid(self):
        return (self.groups // self.gs, self.t // self.tq)

    @property
    def seqs(self):
        return self.gs * GROUP

    def plain(self, cols, col=0):
        return pl.BlockSpec((self.seqs, self.tq, cols), lambda ig, it: (ig, it, col))

    def head_lanes(self, blk=0):
        return pl.BlockSpec((self.gs, HEAD_DIM, self.tq, LANES), lambda ig, it: (ig, blk, it, 0))

    def per_seq(self, rows, cols):
        return pl.BlockSpec((self.seqs, rows, cols), lambda ig, it: (ig, 0, 0))


def _cparams(n_axes):
    return pltpu.CompilerParams(dimension_semantics=("arbitrary",) * n_axes,
                                vmem_limit_bytes=V7X_VMEM_LIMIT)


def _const_spec(shape):
    return pl.BlockSpec(shape, lambda *_: (0,) * len(shape), pipeline_mode=pl.Buffered(1))


def _sigmoid(x):
    return 1.0 / (1.0 + jnp.exp(-x))


def _silu(x):
    return x * _sigmoid(x)


def _softplus(x):
    return jnp.maximum(x, 0.0) + jnp.log(1.0 + jnp.exp(-jnp.abs(x)))


def _swap_lane_groups(m, lane):
    half = lane < 2 * (LANES // GROUP)
    odd = (lane // (LANES // GROUP)) % 2 == 1
    n = [None] * 4
    for s in range(2):
        n[s] = jnp.where(half, m[s], pltpu.roll(m[s + 2], LANES // 2, axis=1))
        n[s + 2] = jnp.where(half, pltpu.roll(m[s], LANES // 2, axis=1), m[s + 2])
    out = [None] * 4
    for p in (0, 2):
        a, b = n[p], n[p + 1]
        out[p] = jnp.where(odd, pltpu.roll(b, LANES // GROUP, axis=1), a)
        out[p + 1] = jnp.where(odd, b, pltpu.roll(a, LANES - LANES // GROUP, axis=1))
    return out


def _store_head_lanes(val, out_ref, gs, tq, q0=0):
    lane = lax.broadcasted_iota(jnp.int32, (tq, LANES), 1)
    for g in range(gs):
        for q in range(val.shape[1] // LANES):
            src = [val[(g * GROUP + s) * tq:(g * GROUP + s + 1) * tq, q * LANES:(q + 1) * LANES]
                   for s in range(GROUP)]
            out = _swap_lane_groups(src, lane)
            for j in range(GROUP):
                out_ref[g, GROUP * (q0 + q) + j] = out[j]


def _rows2d(ref):
    v = ref[...]
    return v.reshape(-1, v.shape[-1])


def _mod_kernel(c_ref, w_ref, b_ref, o_ref):
    s = _silu(c_ref[...]).astype(BF16)
    o_ref[...] = jnp.dot(s, w_ref[...].astype(BF16), preferred_element_type=F32) + b_ref[...]


def _modulation(c, w_ada, b_ada):
    rows, d = c.shape
    n = w_ada.shape[1]
    tn = MOD_TN
    return pl.pallas_call(
        _mod_kernel,
        grid=(n // tn,),
        in_specs=[pl.BlockSpec((rows, d), lambda j: (0, 0)),
                  pl.BlockSpec((d, tn), lambda j: (0, j)),
                  pl.BlockSpec((1, tn), lambda j: (0, j))],
        out_specs=pl.BlockSpec((rows, tn), lambda j: (0, j)),
        out_shape=jax.ShapeDtypeStruct((rows, n), F32),
        compiler_params=_cparams(1),
        name="modulation",
    )(c, w_ada, b_ada.reshape(1, n))


def _prep_kernel(x_ref, m_ref, g_ref, mu_ref, s0_ref, xs_ref, so_ref, hbuf, *, tm):
    x = x_ref[...]
    m = m_ref[...]
    ms = jnp.mean(x * x, axis=-1, keepdims=True)
    h = (x * lax.rsqrt(ms + 1e-6) * g_ref[...]) * (1.0 + m[1:2, :]) + m[0:1, :]
    hbuf[SUBLANES:SUBLANES + tm, :] = h

    @pl.when(pl.program_id(1) == 0)
    def _():
        hbuf[SUBLANES - 1:SUBLANES, :] = s0_ref[...]

    dx = hbuf[SUBLANES - 1:SUBLANES - 1 + tm, :] - h
    for mi, slab in enumerate(MU_SLABS):
        xs_ref[slab] = (h + dx * mu_ref[mi:mi + 1, :]).astype(BF16)
    xs_ref[SLAB_H] = h.astype(BF16)
    last = h[tm - 1:tm, :]
    hbuf[SUBLANES - 1:SUBLANES, :] = last
    so_ref[...] = last


def _prep(x, mod, norm_g, mu, shift0, *, tm):
    b, t, d = x.shape
    return pl.pallas_call(
        functools.partial(_prep_kernel, tm=tm),
        grid=(b, t // tm),
        in_specs=[pl.BlockSpec((None, tm, d), lambda i, j: (i, j, 0)),
                  pl.BlockSpec((None, N_MOD, d), lambda i, j: (i, 0, 0)),
                  _const_spec((1, d)),
                  _const_spec((N_MOD, d)),
                  pl.BlockSpec((None, 1, d), lambda i, j: (i, 0, 0))],
        out_specs=[pl.BlockSpec((7, None, tm, d), lambda i, j: (0, i, j, 0)),
                   pl.BlockSpec((None, 1, d), lambda i, j: (i, 0, 0))],
        out_shape=[jax.ShapeDtypeStruct((7, b, t, d), BF16),
                   jax.ShapeDtypeStruct((b, 1, d), F32)],
        scratch_shapes=[pltpu.VMEM((SUBLANES + tm, d), F32)],
        compiler_params=_cparams(2),
        name="prep",
    )(x, mod, norm_g.reshape(1, d), mu, shift0.reshape(b, 1, d))


def _gate_blocks(gate, rows, tm, rows_per_gate):
    if gate.ndim == 2:
        return gate.reshape(rows // tm, tm, gate.shape[-1]), 1
    assert rows_per_gate % tm == 0, (rows_per_gate, tm)
    return gate, rows_per_gate // tm


def _mm_kernel(*refs, gated, sigmoid):
    if gated:
        a_ref, w_ref, res_ref, gate_ref, o_ref, wb = refs
    else:
        a_ref, w_ref, o_ref, wb = refs

    @pl.when(pl.program_id(1) == 0)
    def _():
        wb[...] = w_ref[...].astype(BF16)

    acc = jnp.dot(a_ref[...], wb[...], preferred_element_type=F32)
    if gated:
        acc = res_ref[...] + gate_ref[...] * acc
    if sigmoid:
        acc = _sigmoid(acc)
    o_ref[...] = acc.astype(o_ref.dtype)


def _matmul(a, w, *, tm, tn, slab=0, col0=0, n=None, res=None, gate=None, rows_per_gate=None,
            sigmoid=False, out_dtype=F32, name):
    _, rows, k = a.shape
    n = w.shape[1] if n is None else n
    gated = res is not None
    in_specs = [pl.BlockSpec((None, tm, k), lambda j, i: (slab, i, 0)),
                pl.BlockSpec((k, tn), lambda j, i: (0, j + col0))]
    args = [a, w]
    if gated:
        gate, tiles_per_gate = _gate_blocks(gate, rows, tm, rows_per_gate)
        gr = gate.shape[1]
        in_specs += [pl.BlockSpec((tm, tn), lambda j, i: (i, j)),
                     pl.BlockSpec((None, gr, tn), lambda j, i: (i // tiles_per_gate, 0, j))]
        args += [res, gate]
    return pl.pallas_call(
        functools.partial(_mm_kernel, gated=gated, sigmoid=sigmoid),
        grid=(pl.cdiv(n, tn), rows // tm),
        in_specs=in_specs,
        out_specs=pl.BlockSpec((tm, tn), lambda j, i: (i, j)),
        out_shape=jax.ShapeDtypeStruct((rows, n), out_dtype),
        scratch_shapes=[pltpu.VMEM((k, tn), BF16)],
        compiler_params=_cparams(2),
        name=name,
    )(*args)


def _proj_hl_kernel(a_ref, w_ref, o_ref, *, gs, tq):
    for c in range(0, w_ref.shape[1], MXU_COLS):
        acc = jnp.dot(_rows2d(a_ref), w_ref[:, c:c + MXU_COLS], preferred_element_type=F32)
        _store_head_lanes(acc, o_ref, gs, tq, q0=c // LANES)


def _proj_head_lanes(xs, w, tl, *, tn, n):
    k = w.shape[0]
    d = n // 3
    slab = lambda j: j * tn // d
    a_spec = pl.BlockSpec((None, tl.seqs, tl.tq, k), lambda j, ig, it: (slab(j), ig, it, 0))
    nq = tn // (LANES // GROUP)
    return pl.pallas_call(
        functools.partial(_proj_hl_kernel, gs=tl.gs, tq=tl.tq),
        grid=(n // tn,) + tl.grid,
        in_specs=[a_spec, pl.BlockSpec((k, tn), lambda j, ig, it: (0, j))],
        out_specs=pl.BlockSpec((tl.gs, nq, tl.tq, LANES), lambda j, ig, it: (ig, j, it, 0)),
        out_shape=jax.ShapeDtypeStruct((tl.groups, n // (LANES // GROUP), tl.t, LANES), F32),
        compiler_params=_cparams(3),
        name="rkv_proj",
    )(xs, w)


def _lora_heads(xw_ref, xa_ref, xg_ref, w1, w2, a1, a2, g1, g2, w0, a0):
    dot = functools.partial(jnp.dot, preferred_element_type=F32)
    hw = jnp.tanh(dot(_rows2d(xw_ref), w1[...])).astype(BF16)
    w_pre = w0[...] + dot(hw, w2[...])
    dec = jnp.exp(-jnp.exp(-_softplus(-w_pre) - 0.5))
    ha = dot(_rows2d(xa_ref), a1[...]).astype(BF16)
    a = _sigmoid(a0[...] + dot(ha, a2[...]))
    hg = _sigmoid(dot(_rows2d(xg_ref), g1[...])).astype(BF16)
    return dec, a, dot(hg, g2[...])


def _lora_kernel(*refs, gs, tq):
    dec_ref, a_ref, g_ref = refs[-3:]
    dec, a, g = _lora_heads(*refs[:-3])
    _store_head_lanes(dec, dec_ref, gs, tq)
    _store_head_lanes(a, a_ref, gs, tq)
    g_ref[...] = g.astype(BF16).reshape(g_ref.shape)


def _pad_lora(w_a, w_b):
    r = w_a.shape[1]
    return (jnp.pad(w_a, ((0, 0), (0, LORA_PAD - r))).astype(BF16),
            jnp.pad(w_b, ((0, LORA_PAD - r), (0, 0))).astype(BF16))


def _lora(xs, w0, w1, w2, a0, a1, a2, g1, g2, tl):
    d = w0.shape[0]
    w1p, w2p = _pad_lora(w1, _head_minor(w2))
    a1p, a2p = _pad_lora(a1, _head_minor(a2))
    g1b, g2b = g1.astype(BF16), _head_minor(g2).astype(BF16)
    slab = lambda s: pl.BlockSpec((None, tl.seqs, tl.tq, d), lambda ig, it: (s, ig, it, 0))
    g_shape = (tl.groups * GROUP, tl.t, d)
    hl_shape = jax.ShapeDtypeStruct((tl.groups, HEAD_DIM, tl.t, LANES), F32)
    consts = [w1p, w2p, a1p, a2p, g1b, g2b, _head_minor(w0).reshape(1, d),
              _head_minor(a0).reshape(1, d)]
    return pl.pallas_call(
        functools.partial(_lora_kernel, gs=tl.gs, tq=tl.tq),
        grid=tl.grid,
        in_specs=[slab(SLAB_W), slab(SLAB_A), slab(SLAB_G)] + [_const_spec(c.shape) for c in consts],
        out_specs=[tl.head_lanes(), tl.head_lanes(), tl.plain(d)],
        out_shape=[hl_shape, hl_shape, jax.ShapeDtypeStruct(g_shape, BF16)],
        compiler_params=_cparams(2),
        name="lora",
    )(xs, xs, xs, *consts)


def _wkv_step(st, sa, v, w_row, b_row, k2_row, r_row, kk_next_row):
    y = jnp.zeros((HEAD_DIM, LANES), F32)
    sa_next = jnp.zeros((HEAD_DIM, LANES), F32)
    for i in range(HEAD_DIM):
        decayed = st[i] if w_row is None else st[i] * w_row(i)
        si = decayed - sa * b_row(i) + v * k2_row(i)
        st[i] = si
        y = y + si * r_row(i)
        sa_next = sa_next + si * kk_next_row(i)
    return y, sa_next


def _wkv_kernel(r_ref, k_ref, v_ref, a_ref, d_ref, kk_p, ka_p, rk_p, lg_p, lb_p, s0_ref,
                y_ref, so_ref, st, kk_s, b_s, k2_s, r_s, pe_s, bon_s, v_t, y_t, *, nb):
    n = HEAD_DIM
    tt = nb * SUBLANES
    bc = lambda p: p[...][:, None, None, :]

    @pl.when(pl.program_id(1) == 0)
    def _():
        s = s0_ref[...].reshape(LANES, n, n)
        st[...] = jnp.swapaxes(jnp.swapaxes(jnp.swapaxes(s, 0, 1), 1, 2), 0, 1)

    x = d_ref[...].reshape(n * nb, SUBLANES, LANES)
    step = lax.broadcasted_iota(jnp.int32, x.shape, 1)
    for sh in (1, 2, 4):
        x = x * jnp.where(step >= sh, pltpu.roll(x, sh, axis=1), 1.0)
    x = x.reshape(n, nb, SUBLANES, LANES)
    step = step.reshape(x.shape)
    run = jnp.ones((n, 1, LANES), F32)
    incl, excl = [], []
    for tb in range(nb):
        blk = x[:, tb] * run
        incl.append(blk)
        excl.append(jnp.where(step[:, tb] >= 1, pltpu.roll(blk, 1, axis=1), run))
        run = blk[:, SUBLANES - 1:SUBLANES, :]
    p_t = jnp.stack(incl, axis=1)
    inv_p = 1.0 / p_t

    k = k_ref[...]
    a = a_ref[...]
    r = r_ref[...]
    kk = k * bc(kk_p)
    kk = kk * (1.0 / jnp.maximum(jnp.sqrt(jnp.sum(kk * kk, axis=0, keepdims=True)), 1e-12))
    k2 = k * (1.0 + (a - 1.0) * bc(ka_p))
    kk_s[...] = kk * jnp.stack(excl, axis=1)
    b_s[...] = kk * a * inv_p
    k2_s[...] = k2 * inv_p
    r_s[...] = r * p_t
    pe_s[...] = p_t[:, nb - 1, SUBLANES - 1, :]
    bon_s[...] = jnp.sum(r * k2 * bc(rk_p), axis=0)
    v_t[...] = jnp.swapaxes(v_ref[...], 0, 1)

    sa0 = jnp.zeros((n, LANES), F32)
    for i in range(n):
        sa0 = sa0 + st[i] * kk_s[i, 0, 0:1, :]

    def block(tb, sa):
        tb_next = jnp.minimum(tb + 1, nb - 1)
        for j in range(SUBLANES):
            t = tb * SUBLANES + j
            row = lambda ref: (lambda i: ref[i, tb, j:j + 1, :])
            if j + 1 < SUBLANES:
                kk_next = lambda i: kk_s[i, tb, j + 1:j + 2, :]
            else:
                kk_next = lambda i: kk_s[i, tb_next, 0:1, :]
            y_t[t], sa = _wkv_step(st, sa, v_t[t], None, row(b_s), row(k2_s), row(r_s), kk_next)
        return sa

    lax.fori_loop(0, nb, block, sa0)
    for i in range(n):
        st[i] = st[i] * pe_s[i:i + 1, :]

    y = jnp.swapaxes(y_t[...], 0, 1)
    ym = jnp.mean(y, axis=0, keepdims=True)
    yc = y - ym
    yv = jnp.mean(yc * yc, axis=0, keepdims=True)
    bcast = lambda p: p[...][:, None, :]
    yn = yc * lax.rsqrt(yv + GN_EPS) * bcast(lg_p) + bcast(lb_p)
    y_ref[...] = yn + bon_s[...].reshape(1, tt, LANES) * v_ref[...]

    @pl.when(pl.program_id(1) == pl.num_programs(1) - 1)
    def _():
        s = jnp.swapaxes(jnp.swapaxes(jnp.swapaxes(st[...], 0, 1), 1, 2), 0, 1)
        so_ref[...] = s.reshape(so_ref.shape)


def _head_lanes(p):
    hn = p.reshape(-1, HEAD_DIM).T
    return jnp.tile(hn, (1, GROUP))


def _wkv(rkv, a, dec, s0, k_k, k_a, r_k, lnx_g, lnx_b, *, tt):
    g, _, t, lanes = a.shape
    n = HEAD_DIM
    heads = s0.shape[1]
    nb = tt // SUBLANES
    by8 = lambda z: z.reshape(z.shape[0], z.shape[1], t // SUBLANES, SUBLANES, lanes)
    params = [_head_lanes(p) for p in (k_k, k_a, r_k.reshape(-1), lnx_g, lnx_b)]
    rows = lambda blk: pl.BlockSpec((None, n, nb, SUBLANES, lanes), lambda i, j: (i, blk, j, 0, 0))
    tile = lambda blk: pl.BlockSpec((None, n, tt, lanes), lambda i, j: (i, blk, j, 0))
    st = pl.BlockSpec((GROUP, heads, n, n), lambda i, j: (i, 0, 0, 0))
    return pl.pallas_call(
        functools.partial(_wkv_kernel, nb=nb),
        grid=(g, t // tt),
        in_specs=[rows(0), rows(1), tile(2), rows(0), rows(0)] + [_const_spec((n, lanes))] * 5 + [st],
        out_specs=[tile(0), st],
        out_shape=[jax.ShapeDtypeStruct((g, n, t, lanes), F32),
                   jax.ShapeDtypeStruct(s0.shape, F32)],
        scratch_shapes=[pltpu.VMEM((n, n, lanes), F32)]
        + [pltpu.VMEM((n, nb, SUBLANES, lanes), F32)] * 4
        + [pltpu.VMEM((n, lanes), F32), pltpu.VMEM((nb, SUBLANES, lanes), F32)]
        + [pltpu.VMEM((tt, n, lanes), F32)] * 2,
        compiler_params=_cparams(2),
        name="wkv7",
    )(by8(rkv), by8(rkv), rkv, by8(a), by8(dec), *params, s0)


CONV_PAD = 32
CONV_ROWS = 32


def _conv_kernel(u1_ref, u2_ref, bg1, bg2, dwk, dwb, lg, lb, st0_ref, y_ref, sto_ref, zbuf, cbuf,
                 zs, *, tm):
    hist = CONV_W - 1
    lo = CONV_PAD - hist
    d = u1_ref.shape[-1]
    rc = min(CONV_ROWS, tm)
    glu = (u1_ref[...] + bg1[...]) * _sigmoid(u2_ref[...] + bg2[...])
    zbuf[CONV_PAD:CONV_PAD + tm, :] = glu

    @pl.when(pl.program_id(1) == 0)
    def _():
        zbuf[lo:CONV_PAD, :] = st0_ref[...]

    for b in range(SUBLANES):
        span = tm + SUBLANES * ((hist - b) // SUBLANES)
        zs[b, 0:span, :] = zbuf[lo + b:lo + b + span, :]
    for r0 in range(0, tm, rc):
        for c0 in range(0, d, LANES):
            acc = jnp.zeros((rc, LANES), F32)
            for j in range(CONV_W):
                a8 = SUBLANES * (j // SUBLANES)
                acc = acc + (zs[j % SUBLANES, a8 + r0:a8 + r0 + rc, c0:c0 + LANES]
                             * dwk[j:j + 1, c0:c0 + LANES])
            cbuf[r0:r0 + rc, c0:c0 + LANES] = acc
    zc = cbuf[...] + dwb[...]
    m = jnp.mean(zc, axis=-1, keepdims=True)
    ctr = zc - m
    var = jnp.mean(ctr * ctr, axis=-1, keepdims=True)
    ln = ctr * lax.rsqrt(var + 1e-5) * lg[...] + lb[...]
    y_ref[...] = _silu(ln).astype(BF16)
    carry = zbuf[lo + tm:CONV_PAD + tm, :]
    zbuf[lo:CONV_PAD, :] = carry
    sto_ref[...] = carry


def _conv(glu_in, b_glu, dw_k, dw_b, ln_g, ln_b, st0, *, tm):
    b, t, _ = glu_in.shape
    d = dw_k.shape[1]
    hist = CONV_W - 1
    vec = _const_spec((1, d))
    return pl.pallas_call(
        functools.partial(_conv_kernel, tm=tm),
        grid=(b, t // tm),
        in_specs=[pl.BlockSpec((None, tm, d), lambda i, j: (i, j, 0)),
                  pl.BlockSpec((None, tm, d), lambda i, j: (i, j, 1)),
                  pl.BlockSpec((1, d), lambda i, j: (0, 0)),
                  pl.BlockSpec((1, d), lambda i, j: (0, 1)),
                  _const_spec((CONV_W, d)),
                  vec, vec, vec,
                  pl.BlockSpec((None, hist, d), lambda i, j: (i, 0, 0))],
        out_specs=[pl.BlockSpec((None, tm, d), lambda i, j: (i, j, 0)),
                   pl.BlockSpec((None, hist, d), lambda i, j: (i, 0, 0))],
        out_shape=[jax.ShapeDtypeStruct((b, t, d), BF16),
                   jax.ShapeDtypeStruct((b, hist, d), F32)],
        scratch_shapes=[pltpu.VMEM((CONV_PAD + tm, d), F32), pltpu.VMEM((tm, d), F32),
                        pltpu.VMEM((SUBLANES, tm + CONV_PAD - SUBLANES, d), F32)],
        compiler_params=_cparams(2),
        name="conformer_conv",
    )(glu_in, glu_in, b_glu.reshape(1, 2 * d), b_glu.reshape(1, 2 * d), dw_k, dw_b.reshape(1, d),
      ln_g.reshape(1, d), ln_b.reshape(1, d), st0)


def _merge_kernel(yw_ref, g_ref, ga_ref, gb_ref, yb_ref, x_ref, woa_ref, wob_ref, m_ref, n2_ref,
                  x1_ref, h2_ref, ybuf, *, gs, tq):
    seqs = gs * GROUP
    d = ybuf.shape[-1]
    lane = lax.broadcasted_iota(jnp.int32, (tq, LANES), 1)
    for g in range(gs):
        for q in range(d // LANES):
            out = _swap_lane_groups([yw_ref[g, GROUP * q + j] for j in range(GROUP)], lane)
            for s in range(GROUP):
                ybuf[g * GROUP + s, :, q * LANES:(q + 1) * LANES] = out[s]
    seq3 = lambda ref: ref[...].astype(F32).reshape(seqs, tq, d)
    m = m_ref[...]
    dot = lambda z, w_ref: jnp.dot(z.reshape(seqs * tq, d).astype(BF16), w_ref[...],
                                   preferred_element_type=F32)
    branch_a = seq3(ga_ref) * (ybuf[...] * seq3(g_ref))
    branch_b = seq3(gb_ref) * seq3(yb_ref)
    mix = (dot(branch_a, woa_ref) + dot(branch_b, wob_ref)).reshape(seqs, tq, d)
    x1 = seq3(x_ref) + m[:, 2:3, :] * mix
    x1_ref[...] = x1.reshape(x1_ref.shape)
    ms = jnp.mean(x1 * x1, axis=-1, keepdims=True)
    h2 = (x1 * lax.rsqrt(ms + 1e-6) * n2_ref[...]) * (1.0 + m[:, 4:5, :]) + m[:, 3:4, :]
    h2_ref[...] = h2.reshape(seqs * tq, d).astype(BF16).reshape(h2_ref.shape)


def _merge(yw, g, ga, gb, yb, x, w_out, mod, norm_g, tl):
    d = x.shape[-1]
    plain = tl.plain(d)
    w_out_b = w_out.astype(BF16)
    w_out_a = _head_minor(w_out.T).T.astype(BF16)
    return pl.pallas_call(
        functools.partial(_merge_kernel, gs=tl.gs, tq=tl.tq),
        grid=tl.grid,
        in_specs=[tl.head_lanes(), plain, plain, plain, plain, plain, _const_spec((d, d)),
                  _const_spec((d, d)), tl.per_seq(N_MOD, d), _const_spec((1, d))],
        out_specs=[plain, plain],
        out_shape=[jax.ShapeDtypeStruct(x.shape, F32), jax.ShapeDtypeStruct(x.shape, BF16)],
        scratch_shapes=[pltpu.VMEM((tl.seqs, tl.tq, d), F32)],
        compiler_params=_cparams(2),
        name="merge_out_proj",
    )(yw, g, ga, gb, yb, x, w_out_a, w_out_b, mod, norm_g.reshape(1, d))


FFN_TN = 512


def _pad_halves(w, width):
    f = w.shape[-1] // 2
    pad = [(0, 0)] * (w.ndim - 1) + [(0, width - f)]
    return jnp.concatenate([jnp.pad(w[..., :f], pad), jnp.pad(w[..., f:], pad)], axis=-1)


def _up_act_kernel(h_ref, wg_ref, wv_ref, kg_ref, kv_ref, bg_ref, bv_ref, sg_ref, sv_ref,
                   act_ref, stg_ref, stv_ref, carry, *, tq, tiles_per_seq):
    first = pl.program_id(1) % tiles_per_seq == 0
    tn = wg_ref.shape[-1]
    row = lax.broadcasted_iota(jnp.int32, (tq, tn), 0)
    z = []
    halves = ((wg_ref, kg_ref, bg_ref, stg_ref), (wv_ref, kv_ref, bv_ref, stv_ref))

    @pl.when(first)
    def _():
        carry[0] = sg_ref[...]
        carry[1] = sv_ref[...]

    for half, (w_ref, k_ref, b_ref, st_ref) in enumerate(halves):
        u = jnp.dot(h_ref[...], w_ref[...], preferred_element_type=F32)
        prev = carry[half]
        p0, p1 = prev[0:1, :], prev[1:2, :]
        u1 = jnp.where(row == 0, p1, pltpu.roll(u, 1, axis=0))
        u2 = jnp.where(row == 0, p0, jnp.where(row == 1, p1, pltpu.roll(u, 2, axis=0)))
        z.append(b_ref[...] + u2 * k_ref[0:1, :] + u1 * k_ref[1:2, :] + u * k_ref[2:3, :])
        last = u[tq - (FFN_CONV_W - 1):, :]
        carry[half] = last
        st_ref[...] = last
    act_ref[...] = (_silu(z[0]) * z[1]).astype(BF16)


def _up_act(h2, w_up, ffn_dw_k, ffn_dw_b, st0, *, tq, t):
    rows, k = h2.shape
    b = st0.shape[0]
    f = w_up.shape[1] // 2
    nj = f // FFN_TN
    hist = FFN_CONV_W - 1
    tiles_per_seq = t // tq
    col = lambda shape, h: pl.BlockSpec(shape, lambda j, i: (0, j + h * nj))
    state = lambda h: pl.BlockSpec((None, hist, FFN_TN),
                                   lambda j, i: (i // tiles_per_seq, 0, j + h * nj))
    return pl.pallas_call(
        functools.partial(_up_act_kernel, tq=tq, tiles_per_seq=tiles_per_seq),
        grid=(nj, rows // tq),
        in_specs=[pl.BlockSpec((tq, k), lambda j, i: (i, 0)),
                  col((k, FFN_TN), 0), col((k, FFN_TN), 1),
                  col((FFN_CONV_W, FFN_TN), 0), col((FFN_CONV_W, FFN_TN), 1),
                  col((1, FFN_TN), 0), col((1, FFN_TN), 1),
                  state(0), state(1)],
        out_specs=[pl.BlockSpec((tq, FFN_TN), lambda j, i: (i, j)), state(0), state(0)],
        out_shape=[jax.ShapeDtypeStruct((rows, f), BF16),
                   jax.ShapeDtypeStruct((b, hist, f), F32),
                   jax.ShapeDtypeStruct((b, hist, f), F32)],
        scratch_shapes=[pltpu.VMEM((2, hist, FFN_TN), F32)],
        compiler_params=_cparams(2),
        name="up_conv_act",
    )(h2, w_up, w_up, ffn_dw_k, ffn_dw_k, ffn_dw_b, ffn_dw_b, st0, st0)


def _norm_kernel(x_ref, g_ref, o_ref):
    x = x_ref[...]
    ms = jnp.mean(x * x, axis=-1, keepdims=True)
    o_ref[...] = x * lax.rsqrt(ms + 1e-6) * g_ref[...]


def _final_norm(x, g, *, tm):
    rows, d = x.shape
    row = pl.BlockSpec((tm, d), lambda i: (i, 0))
    return pl.pallas_call(
        _norm_kernel,
        grid=(rows // tm,),
        in_specs=[row, _const_spec((1, d))],
        out_specs=row,
        out_shape=jax.ShapeDtypeStruct((rows, d), F32),
        compiler_params=_cparams(1),
        name="final_norm",
    )(x, g.reshape(1, d))


def _prep_tm_kernel(x_ref, m_ref, g_ref, mu_ref, s0_ref, xs_ref, so_ref, hprev):
    @pl.when(pl.program_id(0) == 0)
    def _():
        hprev[...] = s0_ref[...]

    x = x_ref[...]
    ms = jnp.mean(x * x, axis=-1, keepdims=True)
    h = (x * lax.rsqrt(ms + 1e-6) * g_ref[...]) * (1.0 + m_ref[1]) + m_ref[0]
    dx = hprev[...] - h
    for mi, slab in enumerate(MU_SLABS):
        xs_ref[slab] = (h + dx * mu_ref[mi:mi + 1, :]).astype(BF16)
    xs_ref[SLAB_H] = h.astype(BF16)
    hprev[...] = h
    so_ref[...] = h


def _prep_tm(x, mod, norm_g, mu, shift0):
    t, b, d = x.shape
    return pl.pallas_call(
        _prep_tm_kernel,
        grid=(t,),
        in_specs=[pl.BlockSpec((None, b, d), lambda i: (i, 0, 0)), _const_spec((N_MOD, b, d)),
                  _const_spec((1, d)), _const_spec((N_MOD, d)), _const_spec((b, d))],
        out_specs=[pl.BlockSpec((7, b, d), lambda i: (0, i, 0)),
                   pl.BlockSpec((b, d), lambda i: (0, 0))],
        out_shape=[jax.ShapeDtypeStruct((7, t * b, d), BF16), jax.ShapeDtypeStruct((b, d), F32)],
        scratch_shapes=[pltpu.VMEM((b, d), F32)],
        compiler_params=_cparams(1),
        name="prep_tm",
    )(x, mod, norm_g.reshape(1, d), mu, shift0)


def _proj_t_kernel(a_ref, w_ref, o_ref):
    acc = jnp.dot(a_ref[...], w_ref[...].astype(BF16), preferred_element_type=F32)
    o_ref[...] = acc.T


def _proj_transposed(xs, w, *, tn, n):
    _, rows, k = xs.shape
    d = n // 3
    return pl.pallas_call(
        _proj_t_kernel,
        grid=(n // tn,),
        in_specs=[pl.BlockSpec((None, rows, k), lambda j: (j * tn // d, 0, 0)),
                  pl.BlockSpec((k, tn), lambda j: (0, j))],
        out_specs=pl.BlockSpec((tn, rows), lambda j: (j, 0)),
        out_shape=jax.ShapeDtypeStruct((n, rows), F32),
        compiler_params=_cparams(1),
        name="rkv_proj_tm",
    )(xs, w)


def _lora_t_kernel(*refs):
    dec_ref, a_ref, g_ref = refs[-3:]
    dec, a, g = _lora_heads(*refs[:-3])
    dec_ref[...] = dec.T
    a_ref[...] = a.T
    g_ref[...] = g


def _lora_transposed(xs, w0, w1, w2, a0, a1, a2, g1, g2):
    _, rows, d = xs.shape
    w1p, w2p = _pad_lora(w1, w2)
    a1p, a2p = _pad_lora(a1, a2)
    consts = [w1p, w2p, a1p, a2p, g1.astype(BF16), g2.astype(BF16), w0.reshape(1, d),
              a0.reshape(1, d)]
    slab = lambda s: pl.BlockSpec((None, rows, d), lambda i: (s, 0, 0))
    t_shape = jax.ShapeDtypeStruct((d, rows), F32)
    return pl.pallas_call(
        _lora_t_kernel,
        grid=(1,),
        in_specs=[slab(SLAB_W), slab(SLAB_A), slab(SLAB_G)] + [_const_spec(c.shape) for c in consts],
        out_specs=[pl.BlockSpec((d, rows), lambda i: (0, 0)), pl.BlockSpec((d, rows), lambda i: (0, 0)),
                   pl.BlockSpec((rows, d), lambda i: (0, 0))],
        out_shape=[t_shape, t_shape, jax.ShapeDtypeStruct((rows, d), F32)],
        compiler_params=_cparams(1),
        name="lora_tm",
    )(xs, xs, xs, *consts)


def _wkv_bl_kernel(r_ref, k_ref, v_ref, a_ref, d_ref, kk_p, ka_p, rk_p, lg_p, lb_p, s0_ref,
                   y_ref, so_ref, st, kk_s, b_s, k2_s, *, steps):
    n = HEAD_DIM
    tile = lambda ref, t: ref[:, t * LANES:(t + 1) * LANES]
    st[...] = jnp.swapaxes(s0_ref[...], 0, 1)
    bonus = []
    for t in range(steps):
        k = tile(k_ref, t)
        a = tile(a_ref, t)
        kk = k * kk_p[...]
        kk = kk / jnp.maximum(jnp.sqrt(jnp.sum(kk * kk, axis=0, keepdims=True)), 1e-12)
        k2 = k * (1.0 + (a - 1.0) * ka_p[...])
        kk_s[:, t * LANES:(t + 1) * LANES] = kk
        b_s[:, t * LANES:(t + 1) * LANES] = kk * a
        k2_s[:, t * LANES:(t + 1) * LANES] = k2
        bonus.append(jnp.sum(tile(r_ref, t) * k2 * rk_p[...], axis=0, keepdims=True))

    sa = jnp.zeros((n, LANES), F32)
    for i in range(n):
        sa = sa + st[i] * kk_s[i:i + 1, 0:LANES]
    for t in range(steps):
        t_next = min(t + 1, steps - 1)
        row = lambda ref, t=t: (lambda i: ref[i:i + 1, t * LANES:(t + 1) * LANES])
        v = tile(v_ref, t)
        y, sa = _wkv_step(st, sa, v, row(d_ref), row(b_s), row(k2_s), row(r_ref),
                          row(kk_s, t_next))
        ym = jnp.mean(y, axis=0, keepdims=True)
        yc = y - ym
        yv = jnp.mean(yc * yc, axis=0, keepdims=True)
        yn = yc * lax.rsqrt(yv + GN_EPS) * lg_p[...] + lb_p[...]
        y_ref[:, t * LANES:(t + 1) * LANES] = yn + bonus[t] * v
    so_ref[...] = jnp.swapaxes(st[...], 0, 1)


def _wkv_batch_lanes(rkv, a, dec, s0, k_k, k_a, r_k, lnx_g, lnx_b, *, steps):
    d, cols = a.shape
    heads, n = s0.shape[0], HEAD_DIM
    params = [jnp.broadcast_to(p.reshape(heads, n, 1), (heads, n, LANES))
              for p in (k_k, k_a, r_k, lnx_g, lnx_b)]
    chan = lambda c: pl.BlockSpec((n, cols), lambda h: (c * heads + h, 0))
    par = pl.BlockSpec((None, n, LANES), lambda h: (h, 0, 0))
    st = pl.BlockSpec((None, n, n, LANES), lambda h: (h, 0, 0, 0))
    return pl.pallas_call(
        functools.partial(_wkv_bl_kernel, steps=steps),
        grid=(heads,),
        in_specs=[chan(0), chan(1), chan(2), chan(0), chan(0)] + [par] * 5 + [st],
        out_specs=[chan(0), st],
        out_shape=[jax.ShapeDtypeStruct((d, cols), F32), jax.ShapeDtypeStruct(s0.shape, F32)],
        scratch_shapes=[pltpu.VMEM((n, n, LANES), F32)] + [pltpu.VMEM((n, cols), F32)] * 3,
        compiler_params=_cparams(1),
        name="wkv7_tm",
    )(rkv, rkv, rkv, a, dec, *params, s0)


CONV_TM_COLS = 512


def _conv_tm_kernel(u1_ref, u2_ref, bg1, bg2, dwk, dwb, lg, lb, st_ref, y_ref, sto_ref, zc,
                    *, steps, b):
    j = pl.program_id(0)
    hist = CONV_W - 1
    tc = u1_ref.shape[-1]
    for c0 in range(0, tc, LANES):
        cs = slice(c0, c0 + LANES)
        glu = [(u1_ref[t * b:(t + 1) * b, cs] + bg1[:, cs])
               * _sigmoid(u2_ref[t * b:(t + 1) * b, cs] + bg2[:, cs]) for t in range(steps)]
        window = lambda i: st_ref[i, :, cs] if i < hist else glu[i - hist]
        for t in range(steps):
            acc = jnp.zeros((b, LANES), F32)
            for tap in range(CONV_W):
                acc = acc + window(t + tap) * dwk[tap:tap + 1, cs]
            zc[j, t * b:(t + 1) * b, cs] = acc + dwb[:, cs]
        for i in range(hist):
            sto_ref[i, :, cs] = window(i + steps)

    @pl.when(j == pl.num_programs(0) - 1)
    def _():
        z = zc[...]
        m = jnp.mean(jnp.mean(z, axis=-1, keepdims=True), axis=0, keepdims=True)
        ctr = z - m
        var = jnp.mean(jnp.mean(ctr * ctr, axis=-1, keepdims=True), axis=0, keepdims=True)
        ln = ctr * lax.rsqrt(var + 1e-5)
        for jj in range(zc.shape[0]):
            cols = slice(jj * tc, (jj + 1) * tc)
            y_ref[:, cols] = _silu(ln[jj] * lg[:, cols] + lb[:, cols]).astype(BF16)


def _conv_tm(glu_in, b_glu, dw_k, dw_b, ln_g, ln_b, st0, *, steps):
    rows, _ = glu_in.shape
    hist, b, d = st0.shape
    tc = CONV_TM_COLS
    nct = d // tc
    col = lambda shape, h=0: pl.BlockSpec(shape, lambda j: (0, j + h * nct))
    state = pl.BlockSpec((hist, b, tc), lambda j: (0, 0, j))
    return pl.pallas_call(
        functools.partial(_conv_tm_kernel, steps=steps, b=b),
        grid=(nct,),
        in_specs=[col((rows, tc)), col((rows, tc), 1), col((1, tc)), col((1, tc), 1),
                  col((CONV_W, tc)), col((1, tc)), _const_spec((1, d)), _const_spec((1, d)), state],
        out_specs=[pl.BlockSpec((rows, d), lambda j: (0, 0)), state],
        out_shape=[jax.ShapeDtypeStruct((rows, d), BF16), jax.ShapeDtypeStruct(st0.shape, F32)],
        scratch_shapes=[pltpu.VMEM((nct, rows, tc), F32)],
        compiler_params=_cparams(1),
        name="conformer_conv_tm",
    )(glu_in, glu_in, b_glu.reshape(1, 2 * d), b_glu.reshape(1, 2 * d), dw_k, dw_b.reshape(1, d),
      ln_g.reshape(1, d), ln_b.reshape(1, d), st0)


def _merge_tm_kernel(yt_ref, g_ref, ga_ref, gb_ref, yb_ref, x_ref, wo_ref, m_ref, n2_ref,
                     x1_ref, h2_ref):
    f32 = lambda ref: ref[...].astype(F32)
    merged = f32(ga_ref) * (yt_ref[...].T * g_ref[...]) + f32(gb_ref) * f32(yb_ref)
    mix = jnp.dot(merged.astype(BF16), wo_ref[...], preferred_element_type=F32)
    x1 = x_ref[...] + m_ref[2] * mix
    x1_ref[...] = x1
    ms = jnp.mean(x1 * x1, axis=-1, keepdims=True)
    h2 = (x1 * lax.rsqrt(ms + 1e-6) * n2_ref[...]) * (1.0 + m_ref[4]) + m_ref[3]
    h2_ref[...] = h2.astype(BF16)


def _merge_tm(yt, g, ga, gb, yb, x, w_out, mod, norm_g, *, steps):
    rows, d = x.shape
    b = rows // steps
    row = pl.BlockSpec((b, d), lambda i: (i, 0))
    return pl.pallas_call(
        _merge_tm_kernel,
        grid=(steps,),
        in_specs=[pl.BlockSpec((d, b), lambda i: (0, i)), row, row, row, row, row,
                  _const_spec((d, d)), _const_spec((N_MOD, b, d)), _const_spec((1, d))],
        out_specs=[row, row],
        out_shape=[jax.ShapeDtypeStruct((rows, d), F32), jax.ShapeDtypeStruct((rows, d), BF16)],
        compiler_params=_cparams(1),
        name="merge_out_proj_tm",
    )(yt, g, ga, gb, yb, x, w_out, mod, norm_g.reshape(1, d))


def _up_act_tm_kernel(h_ref, wg_ref, wv_ref, kg_ref, kv_ref, bg_ref, bv_ref, sg_ref, sv_ref,
                      act_ref, stg_ref, stv_ref, *, steps, b):
    hist = FFN_CONV_W - 1
    h = h_ref[...]
    tn = wg_ref.shape[-1]
    z = []
    halves = ((wg_ref, kg_ref, bg_ref, sg_ref, stg_ref), (wv_ref, kv_ref, bv_ref, sv_ref, stv_ref))
    for w_ref, k_ref, b_ref, s_ref, st_ref in halves:
        u = jnp.dot(h, w_ref[...], preferred_element_type=F32).reshape(steps, b, tn)
        window = [s_ref[i] for i in range(hist)] + [u[t] for t in range(steps)]
        z.append(jnp.stack([b_ref[...] + sum(window[t + tap] * k_ref[tap:tap + 1, :]
                                             for tap in range(FFN_CONV_W))
                            for t in range(steps)]))
        for i in range(hist):
            st_ref[i] = window[steps + i]
    act_ref[...] = (_silu(z[0]) * z[1]).reshape(steps * b, tn).astype(BF16)


def _up_act_tm(h2, w_up, ffn_dw_k, ffn_dw_b, st0, *, steps):
    rows, k = h2.shape
    hist, b, _ = st0.shape
    f = w_up.shape[1] // 2
    nj = f // FFN_TN
    col = lambda shape, h: pl.BlockSpec(shape, lambda j: (0, j + h * nj))
    state = lambda h: pl.BlockSpec((hist, b, FFN_TN), lambda j: (0, 0, j + h * nj))
    return pl.pallas_call(
        functools.partial(_up_act_tm_kernel, steps=steps, b=b),
        grid=(nj,),
        in_specs=[_const_spec((rows, k)), col((k, FFN_TN), 0), col((k, FFN_TN), 1),
                  col((FFN_CONV_W, FFN_TN), 0), col((FFN_CONV_W, FFN_TN), 1),
                  col((1, FFN_TN), 0), col((1, FFN_TN), 1), state(0), state(1)],
        out_specs=[pl.BlockSpec((rows, FFN_TN), lambda j: (0, j)), state(0), state(0)],
        out_shape=[jax.ShapeDtypeStruct((rows, f), BF16),
                   jax.ShapeDtypeStruct((hist, b, f), F32),
                   jax.ShapeDtypeStruct((hist, b, f), F32)],
        compiler_params=_cparams(1),
        name="up_conv_act_tm",
    )(h2, w_up, w_up, ffn_dw_k, ffn_dw_k, ffn_dw_b, ffn_dw_b, st0, st0)


def _sample_layer(x, mod, shift0, wkv0, conv0, ffn0, p):
    b, t, d = x.shape
    assert b == LANES
    rows = t * b
    tm = lambda z: jnp.swapaxes(z, 0, 1)
    w_in = p["w_in"]
    x_tm = tm(x)
    mod_tm = tm(mod)
    xs, shift1 = _prep_tm(x_tm, mod_tm, p["norm1_g"], p["mu"], shift0)

    tn = PROJ_TN
    rkv = _proj_transposed(xs, w_in, tn=tn, n=3 * d)
    proj = functools.partial(_matmul, xs, w_in, tm=rows, tn=tn, slab=SLAB_H)
    glu_in = proj(col0=3 * d // tn, n=2 * d, name="glu_proj")
    ga = proj(col0=5 * d // tn, n=d, sigmoid=True, out_dtype=BF16, name="gate_a_proj")
    gb = proj(col0=6 * d // tn, n=d, sigmoid=True, out_dtype=BF16, name="gate_b_proj")
    dec, a, g = _lora_transposed(xs, p["w0"], p["w1"], p["w2"], p["a0"], p["a1"], p["a2"],
                                 p["g1"], p["g2"])

    yt, s1 = _wkv_batch_lanes(rkv, a, dec, wkv0.transpose(1, 2, 3, 0), p["k_k"], p["k_a"],
                              p["r_k"], p["lnx_g"], p["lnx_b"], steps=t)
    wkv1 = s1.transpose(3, 0, 1, 2)

    yb, conv1 = _conv_tm(glu_in, p["b_glu"], p["dw_k"], p["dw_b"], p["ln_conv_g"],
                         p["ln_conv_b"], tm(conv0), steps=t)
    x1, h2 = _merge_tm(yt, g, ga, gb, yb, x_tm.reshape(rows, d), p["w_out"].astype(BF16), mod_tm,
                       p["norm2_g"], steps=t)

    d_ff = p["w_down"].shape[0]
    f_pad = -(-d_ff // FFN_TN) * FFN_TN
    act, ffn_g, ffn_v = _up_act_tm(
        h2, _pad_halves(p["w_up"], f_pad).astype(BF16), _pad_halves(p["ffn_dw_k"], f_pad),
        _pad_halves(p["ffn_dw_b"], f_pad).reshape(1, 2 * f_pad), _pad_halves(tm(ffn0), f_pad),
        steps=t)
    ffn1 = tm(jnp.concatenate([ffn_g[..., :d_ff], ffn_v[..., :d_ff]], axis=-1))
    w_down = jnp.pad(p["w_down"], ((0, f_pad - d_ff), (0, 0))).astype(BF16)
    x2 = _matmul(act[None], w_down, tm=rows, tn=DOWN_TN, res=x1, gate=jnp.tile(mod_tm[5], (t, 1)),
                 name="down_proj")
    y = _final_norm(x2, p["normf_g"], tm=min(NORM_TM, rows))
    return tm(y.reshape(t, b, d)), shift1, wkv1, tm(conv1), ffn1


def _head_minor(w):
    lead = w.shape[:-1]
    return w.reshape(*lead, -1, HEAD_DIM).swapaxes(-1, -2).reshape(*lead, w.shape[-1])


def _prompt_layer(x, mod, shift0, wkv0, conv0, ffn0, p, *, tm_seq, tq_proj, tq_tok, tm_mm, tt,
                  tq_ffn):
    b, t, d = x.shape
    rows = b * t
    groups = b // GROUP
    w_in = p["w_in"]

    xs, shift1 = _prep(x, mod, p["norm1_g"], p["mu"], shift0, tm=tm_seq)
    xs2 = xs.reshape(7, rows, d)

    w_hm = jnp.concatenate([_head_minor(w_in[:, i * d:(i + 1) * d]) for i in (0, 1, 2, 5)],
                           axis=1).astype(BF16)
    rkv = _proj_head_lanes(xs, w_hm, _Tiling(groups, t, 1, tq_proj), tn=d, n=3 * d)
    tn = PROJ_TN
    ga = _matmul(xs2, w_hm, tm=tm_mm, tn=tn, slab=SLAB_H, col0=3 * d // tn, n=d, sigmoid=True,
                 out_dtype=BF16, name="gate_a_proj")
    glu_in = _matmul(xs2, w_in, tm=tm_mm, tn=tn, slab=SLAB_H, col0=3 * d // tn, n=2 * d,
                     name="glu_proj")
    gb = _matmul(xs2, w_in, tm=tm_mm, tn=tn, slab=SLAB_H, col0=6 * d // tn, n=d, sigmoid=True,
                 out_dtype=BF16, name="gate_b_proj")

    tl = _Tiling(groups, t, 1, tq_tok)
    dec, a, g = _lora(xs, p["w0"], p["w1"], p["w2"], p["a0"], p["a1"], p["a2"], p["g1"], p["g2"], tl)
    yw, wkv1 = _wkv(rkv, a, dec, wkv0, p["k_k"], p["k_a"], p["r_k"], p["lnx_g"], p["lnx_b"], tt=tt)
    yb, conv1 = _conv(glu_in.reshape(b, t, 2 * d), p["b_glu"], p["dw_k"], p["dw_b"],
                      p["ln_conv_g"], p["ln_conv_b"], conv0, tm=tm_seq)
    x1, h2 = _merge(yw, g, ga.reshape(b, t, d), gb.reshape(b, t, d), yb, x, p["w_out"], mod,
                    p["norm2_g"], tl)

    d_ff = p["w_down"].shape[0]
    f_pad = -(-d_ff // FFN_TN) * FFN_TN
    act, ffn_g, ffn_v = _up_act(
        h2.reshape(rows, d), _pad_halves(p["w_up"], f_pad).astype(BF16),
        _pad_halves(p["ffn_dw_k"], f_pad), _pad_halves(p["ffn_dw_b"], f_pad).reshape(1, 2 * f_pad),
        _pad_halves(ffn0, f_pad), tq=tq_ffn, t=t)
    ffn1 = jnp.concatenate([ffn_g[..., :d_ff], ffn_v[..., :d_ff]], axis=-1)
    w_down = jnp.pad(p["w_down"], ((0, f_pad - d_ff), (0, 0))).astype(BF16)
    x2 = _matmul(act[None], w_down, tm=min(tm_mm, DOWN_TM), tn=DOWN_TN, res=x1.reshape(rows, d),
                 gate=mod[:, 5][:, None, :], rows_per_gate=t, name="down_proj")
    y = _final_norm(x2, p["normf_g"], tm=min(NORM_TM, rows))
    return y.reshape(b, t, d), shift1.reshape(b, d), wkv1, conv1, ffn1


def kernel(x_prompt, x_sample, state_shift, state_wkv, state_conv, state_ffn, c_prompt, c_sample,
           norm1_g, norm2_g, normf_g, w_ada, b_ada, mu, w_in, b_glu, w0, w1, w2, a0, a1, a2, g1, g2,
           k_k, k_a, r_k, lnx_g, lnx_b, dw_k, dw_b, ln_conv_g, ln_conv_b, w_out, w_up, ffn_dw_k,
           ffn_dw_b, w_down):
    p = dict(norm1_g=norm1_g, norm2_g=norm2_g, normf_g=normf_g, mu=mu, w_in=w_in, b_glu=b_glu,
             w0=w0, w1=w1, w2=w2, a0=a0, a1=a1, a2=a2, g1=g1, g2=g2, k_k=k_k, k_a=k_a, r_k=r_k,
             lnx_g=lnx_g, lnx_b=lnx_b, dw_k=dw_k, dw_b=dw_b, ln_conv_g=ln_conv_g,
             ln_conv_b=ln_conv_b, w_out=w_out, w_up=w_up, ffn_dw_k=ffn_dw_k, ffn_dw_b=ffn_dw_b,
             w_down=w_down)
    bp, _, d = x_prompt.shape
    bs = x_sample.shape[0]
    f2 = w_up.shape[1]

    c_all = jnp.concatenate([c_prompt, c_sample], axis=0)
    c_rows = -(-c_all.shape[0] // SUBLANES) * SUBLANES
    c_all = jnp.pad(c_all, ((0, c_rows - c_all.shape[0]), (0, 0)))
    mod = _modulation(c_all, w_ada, b_ada).reshape(c_rows, N_MOD, d)
    mod_p, mod_s = mod[:bp], mod[bp:bp + bs]

    y_p, shift_p, wkv_p, conv_p, ffn_p = _prompt_layer(
        x_prompt, mod_p, jnp.zeros((bp, d), F32),
        jnp.zeros((bp, d // HEAD_DIM, HEAD_DIM, HEAD_DIM), F32),
        jnp.zeros((bp, CONV_W - 1, d), F32), jnp.zeros((bp, FFN_CONV_W - 1, f2), F32), p,
        **PROMPT_TILES)

    y_s, shift_s, wkv_s, conv_s, ffn_s = _sample_layer(
        x_sample, mod_s, state_shift, state_wkv, state_conv, state_ffn, p)
    return (y_p, y_s, shift_p, wkv_p, conv_p, ffn_p, shift_s, wkv_s, conv_s, ffn_s)
```

```python
import functools
from typing import NamedTuple

import jax
import jax.numpy as jnp
from jax import lax
from jax.experimental import pallas as pl
from jax.experimental.pallas import tpu as pltpu

F32 = jnp.float32
BF16 = jnp.bfloat16

HEAD_DIM = 64
CONV_W = 31
FFN_CONV_W = 3
N_MOD = 6
GN_EPS = HEAD_DIM * 1e-5
SUBLANES = 8
LANES = 128
MXU_COLS = 256
LORA_PAD = 128
GROUP = 4
V7X_VMEM_LIMIT = 56 * 1024 * 1024

PROJ_TN = 1024
MOD_TN = 2048
DOWN_TM, DOWN_TN = 512, 512
NORM_TM = 256
PROMPT_TILES = dict(tm_seq=256, tq_proj=256, tq_tok=64, tm_mm=1024, tt=64, tq_ffn=1024)

SLAB_R, SLAB_K, SLAB_V, SLAB_H, SLAB_W, SLAB_A, SLAB_G = range(7)
MU_SLABS = (SLAB_R, SLAB_W, SLAB_K, SLAB_V, SLAB_A, SLAB_G)


class _Tiling(NamedTuple):
    groups: int
    t: int
    gs: int
    tq: int

    @property
    def grid(self):
        return (self.groups // self.gs, self.t // self.tq)

    @property
    def seqs(self):
        return self.gs * GROUP

    def plain(self, cols, col=0):
        return pl.BlockSpec((self.seqs, self.tq, cols), lambda ig, it: (ig, it, col))

    def head_lanes(self, blk=0):
        return pl.BlockSpec((self.gs, HEAD_DIM, self.tq, LANES), lambda ig, it: (ig, blk, it, 0))

    def per_seq(self, rows, cols):
        return pl.BlockSpec((self.seqs, rows, cols), lambda ig, it: (ig, 0, 0))


def _cparams(n_axes):
    return pltpu.CompilerParams(dimension_semantics=("arbitrary",) * n_axes,
                                vmem_limit_bytes=V7X_VMEM_LIMIT)


def _const_spec(shape):
    return pl.BlockSpec(shape, lambda *_: (0,) * len(shape), pipeline_mode=pl.Buffered(1))


def _sigmoid(x):
    return 1.0 / (1.0 + jnp.exp(-x))


def _silu(x):
    return x * _sigmoid(x)


def _softplus(x):
    return jnp.maximum(x, 0.0) + jnp.log(1.0 + jnp.exp(-jnp.abs(x)))


def _swap_lane_groups(m, lane):
    half = lane < 2 * (LANES // GROUP)
    odd = (lane // (LANES // GROUP)) % 2 == 1
    n = [None] * 4
    for s in range(2):
        n[s] = jnp.where(half, m[s], pltpu.roll(m[s + 2], LANES // 2, axis=1))
        n[s + 2] = jnp.where(half, pltpu.roll(m[s], LANES // 2, axis=1), m[s + 2])
    out = [None] * 4
    for p in (0, 2):
        a, b = n[p], n[p + 1]
        out[p] = jnp.where(odd, pltpu.roll(b, LANES // GROUP, axis=1), a)
        out[p + 1] = jnp.where(odd, b, pltpu.roll(a, LANES - LANES // GROUP, axis=1))
    return out


def _store_head_lanes(val, out_ref, gs, tq, q0=0):
    lane = lax.broadcasted_iota(jnp.int32, (tq, LANES), 1)
    for g in range(gs):
        for q in range(val.shape[1] // LANES):
            src = [val[(g * GROUP + s) * tq:(g * GROUP + s + 1) * tq, q * LANES:(q + 1) * LANES]
                   for s in range(GROUP)]
            out = _swap_lane_groups(src, lane)
            for j in range(GROUP):
                out_ref[g, GROUP * (q0 + q) + j] = out[j]


def _rows2d(ref):
    v = ref[...]
    return v.reshape(-1, v.shape[-1])


def _mod_kernel(c_ref, w_ref, b_ref, o_ref):
    s = _silu(c_ref[...]).astype(BF16)
    o_ref[...] = jnp.dot(s, w_ref[...].astype(BF16), preferred_element_type=F32) + b_ref[...]


def _modulation(c, w_ada, b_ada):
    rows, d = c.shape
    n = w_ada.shape[1]
    tn = MOD_TN
    return pl.pallas_call(
        _mod_kernel,
        grid=(n // tn,),
        in_specs=[pl.BlockSpec((rows, d), lambda j: (0, 0)),
                  pl.BlockSpec((d, tn), lambda j: (0, j)),
                  pl.BlockSpec((1, tn), lambda j: (0, j))],
        out_specs=pl.BlockSpec((rows, tn), lambda j: (0, j)),
        out_shape=jax.ShapeDtypeStruct((rows, n), F32),
        compiler_params=_cparams(1),
        name="modulation",
    )(c, w_ada, b_ada.reshape(1, n))


def _prep_kernel(x_ref, m_ref, g_ref, mu_ref, s0_ref, xs_ref, so_ref, hbuf, *, tm):
    x = x_ref[...]
    m = m_ref[...]
    ms = jnp.mean(x * x, axis=-1, keepdims=True)
    h = (x * lax.rsqrt(ms + 1e-6) * g_ref[...]) * (1.0 + m[1:2, :]) + m[0:1, :]
    hbuf[SUBLANES:SUBLANES + tm, :] = h

    @pl.when(pl.program_id(1) == 0)
    def _():
        hbuf[SUBLANES - 1:SUBLANES, :] = s0_ref[...]

    dx = hbuf[SUBLANES - 1:SUBLANES - 1 + tm, :] - h
    for mi, slab in enumerate(MU_SLABS):
        xs_ref[slab] = (h + dx * mu_ref[mi:mi + 1, :]).astype(BF16)
    xs_ref[SLAB_H] = h.astype(BF16)
    last = h[tm - 1:tm, :]
    hbuf[SUBLANES - 1:SUBLANES, :] = last
    so_ref[...] = last


def _prep(x, mod, norm_g, mu, shift0, *, tm):
    b, t, d = x.shape
    return pl.pallas_call(
        functools.partial(_prep_kernel, tm=tm),
        grid=(b, t // tm),
        in_specs=[pl.BlockSpec((None, tm, d), lambda i, j: (i, j, 0)),
                  pl.BlockSpec((None, N_MOD, d), lambda i, j: (i, 0, 0)),
                  _const_spec((1, d)),
                  _const_spec((N_MOD, d)),
                  pl.BlockSpec((None, 1, d), lambda i, j: (i, 0, 0))],
        out_specs=[pl.BlockSpec((7, None, tm, d), lambda i, j: (0, i, j, 0)),
                   pl.BlockSpec((None, 1, d), lambda i, j: (i, 0, 0))],
        out_shape=[jax.ShapeDtypeStruct((7, b, t, d), BF16),
                   jax.ShapeDtypeStruct((b, 1, d), F32)],
        scratch_shapes=[pltpu.VMEM((SUBLANES + tm, d), F32)],
        compiler_params=_cparams(2),
        name="prep",
    )(x, mod, norm_g.reshape(1, d), mu, shift0.reshape(b, 1, d))


def _gate_blocks(gate, rows, tm, rows_per_gate):
    if gate.ndim == 2:
        return gate.reshape(rows // tm, tm, gate.shape[-1]), 1
    assert rows_per_gate % tm == 0, (rows_per_gate, tm)
    return gate, rows_per_gate // tm


def _mm_kernel(*refs, gated, sigmoid):
    if gated:
        a_ref, w_ref, res_ref, gate_ref, o_ref, wb = refs
    else:
        a_ref, w_ref, o_ref, wb = refs

    @pl.when(pl.program_id(1) == 0)
    def _():
        wb[...] = w_ref[...].astype(BF16)

    acc = jnp.dot(a_ref[...], wb[...], preferred_element_type=F32)
    if gated:
        acc = res_ref[...] + gate_ref[...] * acc
    if sigmoid:
        acc = _sigmoid(acc)
    o_ref[...] = acc.astype(o_ref.dtype)


def _matmul(a, w, *, tm, tn, slab=0, col0=0, n=None, res=None, gate=None, rows_per_gate=None,
            sigmoid=False, out_dtype=F32, name):
    _, rows, k = a.shape
    n = w.shape[1] if n is None else n
    gated = res is not None
    in_specs = [pl.BlockSpec((None, tm, k), lambda j, i: (slab, i, 0)),
                pl.BlockSpec((k, tn), lambda j, i: (0, j + col0))]
    args = [a, w]
    if gated:
        gate, tiles_per_gate = _gate_blocks(gate, rows, tm, rows_per_gate)
        gr = gate.shape[1]
        in_specs += [pl.BlockSpec((tm, tn), lambda j, i: (i, j)),
                     pl.BlockSpec((None, gr, tn), lambda j, i: (i // tiles_per_gate, 0, j))]
        args += [res, gate]
    return pl.pallas_call(
        functools.partial(_mm_kernel, gated=gated, sigmoid=sigmoid),
        grid=(pl.cdiv(n, tn), rows // tm),
        in_specs=in_specs,
        out_specs=pl.BlockSpec((tm, tn), lambda j, i: (i, j)),
        out_shape=jax.ShapeDtypeStruct((rows, n), out_dtype),
        scratch_shapes=[pltpu.VMEM((k, tn), BF16)],
        compiler_params=_cparams(2),
        name=name,
    )(*args)


def _proj_hl_kernel(a_ref, w_ref, o_ref, *, gs, tq):
    for c in range(0, w_ref.shape[1], MXU_COLS):
        acc = jnp.dot(_rows2d(a_ref), w_ref[:, c:c + MXU_COLS], preferred_element_type=F32)
        _store_head_lanes(acc, o_ref, gs, tq, q0=c // LANES)


def _proj_head_lanes(xs, w, tl, *, tn, n):
    k = w.shape[0]
    d = n // 3
    slab = lambda j: j * tn // d
    a_spec = pl.BlockSpec((None, tl.seqs, tl.tq, k), lambda j, ig, it: (slab(j), ig, it, 0))
    nq = tn // (LANES // GROUP)
    return pl.pallas_call(
        functools.partial(_proj_hl_kernel, gs=tl.gs, tq=tl.tq),
        grid=(n // tn,) + tl.grid,
        in_specs=[a_spec, pl.BlockSpec((k, tn), lambda j, ig, it: (0, j))],
        out_specs=pl.BlockSpec((tl.gs, nq, tl.tq, LANES), lambda j, ig, it: (ig, j, it, 0)),
        out_shape=jax.ShapeDtypeStruct((tl.groups, n // (LANES // GROUP), tl.t, LANES), F32),
        compiler_params=_cparams(3),
        name="rkv_proj",
    )(xs, w)


def _lora_heads(xw_ref, xa_ref, xg_ref, w1, w2, a1, a2, g1, g2, w0, a0):
    dot = functools.partial(jnp.dot, preferred_element_type=F32)
    hw = jnp.tanh(dot(_rows2d(xw_ref), w1[...])).astype(BF16)
    w_pre = w0[...] + dot(hw, w2[...])
    dec = jnp.exp(-jnp.exp(-_softplus(-w_pre) - 0.5))
    ha = dot(_rows2d(xa_ref), a1[...]).astype(BF16)
    a = _sigmoid(a0[...] + dot(ha, a2[...]))
    hg = _sigmoid(dot(_rows2d(xg_ref), g1[...])).astype(BF16)
    return dec, a, dot(hg, g2[...])


def _lora_kernel(*refs, gs, tq):
    dec_ref, a_ref, g_ref = refs[-3:]
    dec, a, g = _lora_heads(*refs[:-3])
    _store_head_lanes(dec, dec_ref, gs, tq)
    _store_head_lanes(a, a_ref, gs, tq)
    g_ref[...] = g.astype(BF16).reshape(g_ref.shape)


def _pad_lora(w_a, w_b):
    r = w_a.shape[1]
    return (jnp.pad(w_a, ((0, 0), (0, LORA_PAD - r))).astype(BF16),
            jnp.pad(w_b, ((0, LORA_PAD - r), (0, 0))).astype(BF16))


def _lora(xs, w0, w1, w2, a0, a1, a2, g1, g2, tl):
    d = w0.shape[0]
    w1p, w2p = _pad_lora(w1, _head_minor(w2))
    a1p, a2p = _pad_lora(a1, _head_minor(a2))
    g1b, g2b = g1.astype(BF16), _head_minor(g2).astype(BF16)
    slab = lambda s: pl.BlockSpec((None, tl.seqs, tl.tq, d), lambda ig, it: (s, ig, it, 0))
    g_shape = (tl.groups * GROUP, tl.t, d)
    hl_shape = jax.ShapeDtypeStruct((tl.groups, HEAD_DIM, tl.t, LANES), F32)
    consts = [w1p, w2p, a1p, a2p, g1b, g2b, _head_minor(w0).reshape(1, d),
              _head_minor(a0).reshape(1, d)]
    return pl.pallas_call(
        functools.partial(_lora_kernel, gs=tl.gs, tq=tl.tq),
        grid=tl.grid,
        in_specs=[slab(SLAB_W), slab(SLAB_A), slab(SLAB_G)] + [_const_spec(c.shape) for c in consts],
        out_specs=[tl.head_lanes(), tl.head_lanes(), tl.plain(d)],
        out_shape=[hl_shape, hl_shape, jax.ShapeDtypeStruct(g_shape, BF16)],
        compiler_params=_cparams(2),
        name="lora",
    )(xs, xs, xs, *consts)


def _wkv_step(st, sa, v, w_row, b_row, k2_row, r_row, kk_next_row):
    y = jnp.zeros((HEAD_DIM, LANES), F32)
    sa_next = jnp.zeros((HEAD_DIM, LANES), F32)
    for i in range(HEAD_DIM):
        decayed = st[i] if w_row is None else st[i] * w_row(i)
        si = decayed - sa * b_row(i) + v * k2_row(i)
        st[i] = si
        y = y + si * r_row(i)
        sa_next = sa_next + si * kk_next_row(i)
    return y, sa_next


def _wkv_kernel(r_ref, k_ref, v_ref, a_ref, d_ref, kk_p, ka_p, rk_p, lg_p, lb_p, s0_ref,
                y_ref, so_ref, st, kk_s, b_s, k2_s, r_s, pe_s, bon_s, v_t, y_t, *, nb):
    n = HEAD_DIM
    tt = nb * SUBLANES
    bc = lambda p: p[...][:, None, None, :]

    @pl.when(pl.program_id(1) == 0)
    def _():
        s = s0_ref[...].reshape(LANES, n, n)
        st[...] = jnp.swapaxes(jnp.swapaxes(jnp.swapaxes(s, 0, 1), 1, 2), 0, 1)

    x = d_ref[...].reshape(n * nb, SUBLANES, LANES)
    step = lax.broadcasted_iota(jnp.int32, x.shape, 1)
    for sh in (1, 2, 4):
        x = x * jnp.where(step >= sh, pltpu.roll(x, sh, axis=1), 1.0)
    x = x.reshape(n, nb, SUBLANES, LANES)
    step = step.reshape(x.shape)
    run = jnp.ones((n, 1, LANES), F32)
    incl, excl = [], []
    for tb in range(nb):
        blk = x[:, tb] * run
        incl.append(blk)
        excl.append(jnp.where(step[:, tb] >= 1, pltpu.roll(blk, 1, axis=1), run))
        run = blk[:, SUBLANES - 1:SUBLANES, :]
    p_t = jnp.stack(incl, axis=1)
    inv_p = 1.0 / p_t

    k = k_ref[...]
    a = a_ref[...]
    r = r_ref[...]
    kk = k * bc(kk_p)
    kk = kk * (1.0 / jnp.maximum(jnp.sqrt(jnp.sum(kk * kk, axis=0, keepdims=True)), 1e-12))
    k2 = k * (1.0 + (a - 1.0) * bc(ka_p))
    kk_s[...] = kk * jnp.stack(excl, axis=1)
    b_s[...] = kk * a * inv_p
    k2_s[...] = k2 * inv_p
    r_s[...] = r * p_t
    pe_s[...] = p_t[:, nb - 1, SUBLANES - 1, :]
    bon_s[...] = jnp.sum(r * k2 * bc(rk_p), axis=0)
    v_t[...] = jnp.swapaxes(v_ref[...], 0, 1)

    sa0 = jnp.zeros((n, LANES), F32)
    for i in range(n):
        sa0 = sa0 + st[i] * kk_s[i, 0, 0:1, :]

    def block(tb, sa):
        tb_next = jnp.minimum(tb + 1, nb - 1)
        for j in range(SUBLANES):
            t = tb * SUBLANES + j
            row = lambda ref: (lambda i: ref[i, tb, j:j + 1, :])
            if j + 1 < SUBLANES:
                kk_next = lambda i: kk_s[i, tb, j + 1:j + 2, :]
            else:
                kk_next = lambda i: kk_s[i, tb_next, 0:1, :]
            y_t[t], sa = _wkv_step(st, sa, v_t[t], None, row(b_s), row(k2_s), row(r_s), kk_next)
        return sa

    lax.fori_loop(0, nb, block, sa0)
    for i in range(n):
        st[i] = st[i] * pe_s[i:i + 1, :]

    y = jnp.swapaxes(y_t[...], 0, 1)
    ym = jnp.mean(y, axis=0, keepdims=True)
    yc = y - ym
    yv = jnp.mean(yc * yc, axis=0, keepdims=True)
    bcast = lambda p: p[...][:, None, :]
    yn = yc * lax.rsqrt(yv + GN_EPS) * bcast(lg_p) + bcast(lb_p)
    y_ref[...] = yn + bon_s[...].reshape(1, tt, LANES) * v_ref[...]

    @pl.when(pl.program_id(1) == pl.num_programs(1) - 1)
    def _():
        s = jnp.swapaxes(jnp.swapaxes(jnp.swapaxes(st[...], 0, 1), 1, 2), 0, 1)
        so_ref[...] = s.reshape(so_ref.shape)


def _head_lanes(p):
    hn = p.reshape(-1, HEAD_DIM).T
    return jnp.tile(hn, (1, GROUP))


def _wkv(rkv, a, dec, s0, k_k, k_a, r_k, lnx_g, lnx_b, *, tt):
    g, _, t, lanes = a.shape
    n = HEAD_DIM
    heads = s0.shape[1]
    nb = tt // SUBLANES
    by8 = lambda z: z.reshape(z.shape[0], z.shape[1], t // SUBLANES, SUBLANES, lanes)
    params = [_head_lanes(p) for p in (k_k, k_a, r_k.reshape(-1), lnx_g, lnx_b)]
    rows = lambda blk: pl.BlockSpec((None, n, nb, SUBLANES, lanes), lambda i, j: (i, blk, j, 0, 0))
    tile = lambda blk: pl.BlockSpec((None, n, tt, lanes), lambda i, j: (i, blk, j, 0))
    st = pl.BlockSpec((GROUP, heads, n, n), lambda i, j: (i, 0, 0, 0))
    return pl.pallas_call(
        functools.partial(_wkv_kernel, nb=nb),
        grid=(g, t // tt),
        in_specs=[rows(0), rows(1), tile(2), rows(0), rows(0)] + [_const_spec((n, lanes))] * 5 + [st],
        out_specs=[tile(0), st],
        out_shape=[jax.ShapeDtypeStruct((g, n, t, lanes), F32),
                   jax.ShapeDtypeStruct(s0.shape, F32)],
        scratch_shapes=[pltpu.VMEM((n, n, lanes), F32)]
        + [pltpu.VMEM((n, nb, SUBLANES, lanes), F32)] * 4
        + [pltpu.VMEM((n, lanes), F32), pltpu.VMEM((nb, SUBLANES, lanes), F32)]
        + [pltpu.VMEM((tt, n, lanes), F32)] * 2,
        compiler_params=_cparams(2),
        name="wkv7",
    )(by8(rkv), by8(rkv), rkv, by8(a), by8(dec), *params, s0)


CONV_PAD = 32
CONV_ROWS = 32


def _conv_kernel(u1_ref, u2_ref, bg1, bg2, dwk, dwb, lg, lb, st0_ref, y_ref, sto_ref, zbuf, cbuf,
                 zs, *, tm):
    hist = CONV_W - 1
    lo = CONV_PAD - hist
    d = u1_ref.shape[-1]
    rc = min(CONV_ROWS, tm)
    glu = (u1_ref[...] + bg1[...]) * _sigmoid(u2_ref[...] + bg2[...])
    zbuf[CONV_PAD:CONV_PAD + tm, :] = glu

    @pl.when(pl.program_id(1) == 0)
    def _():
        zbuf[lo:CONV_PAD, :] = st0_ref[...]

    for b in range(SUBLANES):
        span = tm + SUBLANES * ((hist - b) // SUBLANES)
        zs[b, 0:span, :] = zbuf[lo + b:lo + b + span, :]
    for r0 in range(0, tm, rc):
        for c0 in range(0, d, LANES):
            acc = jnp.zeros((rc, LANES), F32)
            for j in range(CONV_W):
                a8 = SUBLANES * (j // SUBLANES)
                acc = acc + (zs[j % SUBLANES, a8 + r0:a8 + r0 + rc, c0:c0 + LANES]
                             * dwk[j:j + 1, c0:c0 + LANES])
            cbuf[r0:r0 + rc, c0:c0 + LANES] = acc
    zc = cbuf[...] + dwb[...]
    m = jnp.mean(zc, axis=-1, keepdims=True)
    ctr = zc - m
    var = jnp.mean(ctr * ctr, axis=-1, keepdims=True)
    ln = ctr * lax.rsqrt(var + 1e-5) * lg[...] + lb[...]
    y_ref[...] = _silu(ln).astype(BF16)
    carry = zbuf[lo + tm:CONV_PAD + tm, :]
    zbuf[lo:CONV_PAD, :] = carry
    sto_ref[...] = carry


def _conv(glu_in, b_glu, dw_k, dw_b, ln_g, ln_b, st0, *, tm):
    b, t, _ = glu_in.shape
    d = dw_k.shape[1]
    hist = CONV_W - 1
    vec = _const_spec((1, d))
    return pl.pallas_call(
        functools.partial(_conv_kernel, tm=tm),
        grid=(b, t // tm),
        in_specs=[pl.BlockSpec((None, tm, d), lambda i, j: (i, j, 0)),
                  pl.BlockSpec((None, tm, d), lambda i, j: (i, j, 1)),
                  pl.BlockSpec((1, d), lambda i, j: (0, 0)),
                  pl.BlockSpec((1, d), lambda i, j: (0, 1)),
                  _const_spec((CONV_W, d)),
                  vec, vec, vec,
                  pl.BlockSpec((None, hist, d), lambda i, j: (i, 0, 0))],
        out_specs=[pl.BlockSpec((None, tm, d), lambda i, j: (i, j, 0)),
                   pl.BlockSpec((None, hist, d), lambda i, j: (i, 0, 0))],
        out_shape=[jax.ShapeDtypeStruct((b, t, d), BF16),
                   jax.ShapeDtypeStruct((b, hist, d), F32)],
        scratch_shapes=[pltpu.VMEM((CONV_PAD + tm, d), F32), pltpu.VMEM((tm, d), F32),
                        pltpu.VMEM((SUBLANES, tm + CONV_PAD - SUBLANES, d), F32)],
        compiler_params=_cparams(2),
        name="conformer_conv",
    )(glu_in, glu_in, b_glu.reshape(1, 2 * d), b_glu.reshape(1, 2 * d), dw_k, dw_b.reshape(1, d),
      ln_g.reshape(1, d), ln_b.reshape(1, d), st0)


def _merge_kernel(yw_ref, g_ref, ga_ref, gb_ref, yb_ref, x_ref, woa_ref, wob_ref, m_ref, n2_ref,
                  x1_ref, h2_ref, ybuf, *, gs, tq):
    seqs = gs * GROUP
    d = ybuf.shape[-1]
    lane = lax.broadcasted_iota(jnp.int32, (tq, LANES), 1)
    for g in range(gs):
        for q in range(d // LANES):
            out = _swap_lane_groups([yw_ref[g, GROUP * q + j] for j in range(GROUP)], lane)
            for s in range(GROUP):
                ybuf[g * GROUP + s, :, q * LANES:(q + 1) * LANES] = out[s]
    seq3 = lambda ref: ref[...].astype(F32).reshape(seqs, tq, d)
    m = m_ref[...]
    dot = lambda z, w_ref: jnp.dot(z.reshape(seqs * tq, d).astype(BF16), w_ref[...],
                                   preferred_element_type=F32)
    branch_a = seq3(ga_ref) * (ybuf[...] * seq3(g_ref))
    branch_b = seq3(gb_ref) * seq3(yb_ref)
    mix = (dot(branch_a, woa_ref) + dot(branch_b, wob_ref)).reshape(seqs, tq, d)
    x1 = seq3(x_ref) + m[:, 2:3, :] * mix
    x1_ref[...] = x1.reshape(x1_ref.shape)
    ms = jnp.mean(x1 * x1, axis=-1, keepdims=True)
    h2 = (x1 * lax.rsqrt(ms + 1e-6) * n2_ref[...]) * (1.0 + m[:, 4:5, :]) + m[:, 3:4, :]
    h2_ref[...] = h2.reshape(seqs * tq, d).astype(BF16).reshape(h2_ref.shape)


def _merge(yw, g, ga, gb, yb, x, w_out, mod, norm_g, tl):
    d = x.shape[-1]
    plain = tl.plain(d)
    w_out_b = w_out.astype(BF16)
    w_out_a = _head_minor(w_out.T).T.astype(BF16)
    return pl.pallas_call(
        functools.partial(_merge_kernel, gs=tl.gs, tq=tl.tq),
        grid=tl.grid,
        in_specs=[tl.head_lanes(), plain, plain, plain, plain, plain, _const_spec((d, d)),
                  _const_spec((d, d)), tl.per_seq(N_MOD, d), _const_spec((1, d))],
        out_specs=[plain, plain],
        out_shape=[jax.ShapeDtypeStruct(x.shape, F32), jax.ShapeDtypeStruct(x.shape, BF16)],
        scratch_shapes=[pltpu.VMEM((tl.seqs, tl.tq, d), F32)],
        compiler_params=_cparams(2),
        name="merge_out_proj",
    )(yw, g, ga, gb, yb, x, w_out_a, w_out_b, mod, norm_g.reshape(1, d))


FFN_TN = 512


def _pad_halves(w, width):
    f = w.shape[-1] // 2
    pad = [(0, 0)] * (w.ndim - 1) + [(0, width - f)]
    return jnp.concatenate([jnp.pad(w[..., :f], pad), jnp.pad(w[..., f:], pad)], axis=-1)


def _up_act_kernel(h_ref, wg_ref, wv_ref, kg_ref, kv_ref, bg_ref, bv_ref, sg_ref, sv_ref,
                   act_ref, stg_ref, stv_ref, carry, *, tq, tiles_per_seq):
    first = pl.program_id(1) % tiles_per_seq == 0
    tn = wg_ref.shape[-1]
    row = lax.broadcasted_iota(jnp.int32, (tq, tn), 0)
    z = []
    halves = ((wg_ref, kg_ref, bg_ref, stg_ref), (wv_ref, kv_ref, bv_ref, stv_ref))

    @pl.when(first)
    def _():
        carry[0] = sg_ref[...]
        carry[1] = sv_ref[...]

    for half, (w_ref, k_ref, b_ref, st_ref) in enumerate(halves):
        u = jnp.dot(h_ref[...], w_ref[...], preferred_element_type=F32)
        prev = carry[half]
        p0, p1 = prev[0:1, :], prev[1:2, :]
        u1 = jnp.where(row == 0, p1, pltpu.roll(u, 1, axis=0))
        u2 = jnp.where(row == 0, p0, jnp.where(row == 1, p1, pltpu.roll(u, 2, axis=0)))
        z.append(b_ref[...] + u2 * k_ref[0:1, :] + u1 * k_ref[1:2, :] + u * k_ref[2:3, :])
        last = u[tq - (FFN_CONV_W - 1):, :]
        carry[half] = last
        st_ref[...] = last
    act_ref[...] = (_silu(z[0]) * z[1]).astype(BF16)


def _up_act(h2, w_up, ffn_dw_k, ffn_dw_b, st0, *, tq, t, d_ff):
    rows, k = h2.shape
    b = st0.shape[0]
    f = w_up.shape[1] // 2
    nj = f // FFN_TN
    hist = FFN_CONV_W - 1
    tiles_per_seq = t // tq
    col = lambda shape, h: pl.BlockSpec(shape, lambda j, i: (0, j + h * nj))
    state = lambda h: pl.BlockSpec((None, hist, FFN_TN),
                                   lambda j, i: (i // tiles_per_seq, 0, j + h * nj))
    return pl.pallas_call(
        functools.partial(_up_act_kernel, tq=tq, tiles_per_seq=tiles_per_seq),
        grid=(nj, rows // tq),
        in_specs=[pl.BlockSpec((tq, k), lambda j, i: (i, 0)),
                  col((k, FFN_TN), 0), col((k, FFN_TN), 1),
                  col((FFN_CONV_W, FFN_TN), 0), col((FFN_CONV_W, FFN_TN), 1),
                  col((1, FFN_TN), 0), col((1, FFN_TN), 1),
                  state(0), state(1)],
        out_specs=[pl.BlockSpec((tq, FFN_TN), lambda j, i: (i, j)), state(0), state(0)],
        out_shape=[jax.ShapeDtypeStruct((rows, d_ff), BF16),
                   jax.ShapeDtypeStruct((b, hist, f), F32),
                   jax.ShapeDtypeStruct((b, hist, f), F32)],
        scratch_shapes=[pltpu.VMEM((2, hist, FFN_TN), F32)],
        compiler_params=_cparams(2),
        name="up_conv_act",
    )(h2, w_up, w_up, ffn_dw_k, ffn_dw_k, ffn_dw_b, ffn_dw_b, st0, st0)


def _norm_kernel(x_ref, g_ref, o_ref):
    x = x_ref[...]
    ms = jnp.mean(x * x, axis=-1, keepdims=True)
    o_ref[...] = x * lax.rsqrt(ms + 1e-6) * g_ref[...]


def _final_norm(x, g, *, tm):
    rows, d = x.shape
    row = pl.BlockSpec((tm, d), lambda i: (i, 0))
    return pl.pallas_call(
        _norm_kernel,
        grid=(rows // tm,),
        in_specs=[row, _const_spec((1, d))],
        out_specs=row,
        out_shape=jax.ShapeDtypeStruct((rows, d), F32),
        compiler_params=_cparams(1),
        name="final_norm",
    )(x, g.reshape(1, d))


def _prep_tm_kernel(x_ref, m_ref, g_ref, mu_ref, s0_ref, xs_ref, so_ref, hprev):
    @pl.when(pl.program_id(0) == 0)
    def _():
        hprev[...] = s0_ref[...]

    x = x_ref[...]
    ms = jnp.mean(x * x, axis=-1, keepdims=True)
    h = (x * lax.rsqrt(ms + 1e-6) * g_ref[...]) * (1.0 + m_ref[1]) + m_ref[0]
    dx = hprev[...] - h
    for mi, slab in enumerate(MU_SLABS):
        xs_ref[slab] = (h + dx * mu_ref[mi:mi + 1, :]).astype(BF16)
    xs_ref[SLAB_H] = h.astype(BF16)
    hprev[...] = h
    so_ref[...] = h


def _prep_tm(x, mod, norm_g, mu, shift0):
    t, b, d = x.shape
    return pl.pallas_call(
        _prep_tm_kernel,
        grid=(t,),
        in_specs=[pl.BlockSpec((None, b, d), lambda i: (i, 0, 0)), _const_spec((N_MOD, b, d)),
                  _const_spec((1, d)), _const_spec((N_MOD, d)), _const_spec((b, d))],
        out_specs=[pl.BlockSpec((7, b, d), lambda i: (0, i, 0)),
                   pl.BlockSpec((b, d), lambda i: (0, 0))],
        out_shape=[jax.ShapeDtypeStruct((7, t * b, d), BF16), jax.ShapeDtypeStruct((b, d), F32)],
        scratch_shapes=[pltpu.VMEM((b, d), F32)],
        compiler_params=_cparams(1),
        name="prep_tm",
    )(x, mod, norm_g.reshape(1, d), mu, shift0)


def _proj_t_kernel(a_ref, w_ref, o_ref):
    acc = jnp.dot(a_ref[...], w_ref[...].astype(BF16), preferred_element_type=F32)
    o_ref[...] = acc.T


def _proj_transposed(xs, w, *, tn, n):
    _, rows, k = xs.shape
    d = n // 3
    return pl.pallas_call(
        _proj_t_kernel,
        grid=(n // tn,),
        in_specs=[pl.BlockSpec((None, rows, k), lambda j: (j * tn // d, 0, 0)),
                  pl.BlockSpec((k, tn), lambda j: (0, j))],
        out_specs=pl.BlockSpec((tn, rows), lambda j: (j, 0)),
        out_shape=jax.ShapeDtypeStruct((n, rows), F32),
        compiler_params=_cparams(1),
        name="rkv_proj_tm",
    )(xs, w)


def _lora_t_kernel(*refs):
    dec_ref, a_ref, g_ref = refs[-3:]
    dec, a, g = _lora_heads(*refs[:-3])
    dec_ref[...] = dec.T
    a_ref[...] = a.T
    g_ref[...] = g


def _lora_transposed(xs, w0, w1, w2, a0, a1, a2, g1, g2):
    _, rows, d = xs.shape
    w1p, w2p = _pad_lora(w1, w2)
    a1p, a2p = _pad_lora(a1, a2)
    consts = [w1p, w2p, a1p, a2p, g1.astype(BF16), g2.astype(BF16), w0.reshape(1, d),
              a0.reshape(1, d)]
    slab = lambda s: pl.BlockSpec((None, rows, d), lambda i: (s, 0, 0))
    t_shape = jax.ShapeDtypeStruct((d, rows), F32)
    return pl.pallas_call(
        _lora_t_kernel,
        grid=(1,),
        in_specs=[slab(SLAB_W), slab(SLAB_A), slab(SLAB_G)] + [_const_spec(c.shape) for c in consts],
        out_specs=[pl.BlockSpec((d, rows), lambda i: (0, 0)), pl.BlockSpec((d, rows), lambda i: (0, 0)),
                   pl.BlockSpec((rows, d), lambda i: (0, 0))],
        out_shape=[t_shape, t_shape, jax.ShapeDtypeStruct((rows, d), F32)],
        compiler_params=_cparams(1),
        name="lora_tm",
    )(xs, xs, xs, *consts)


def _wkv_bl_kernel(r_ref, k_ref, v_ref, a_ref, d_ref, kk_p, ka_p, rk_p, lg_p, lb_p, s0_ref,
                   y_ref, so_ref, st, kk_s, b_s, k2_s, *, steps):
    n = HEAD_DIM
    tile = lambda ref, t: ref[:, t * LANES:(t + 1) * LANES]
    st[...] = jnp.swapaxes(s0_ref[...], 0, 1)
    bonus = []
    for t in range(steps):
        k = tile(k_ref, t)
        a = tile(a_ref, t)
        kk = k * kk_p[...]
        kk = kk / jnp.maximum(jnp.sqrt(jnp.sum(kk * kk, axis=0, keepdims=True)), 1e-12)
        k2 = k * (1.0 + (a - 1.0) * ka_p[...])
        kk_s[:, t * LANES:(t + 1) * LANES] = kk
        b_s[:, t * LANES:(t + 1) * LANES] = kk * a
        k2_s[:, t * LANES:(t + 1) * LANES] = k2
        bonus.append(jnp.sum(tile(r_ref, t) * k2 * rk_p[...], axis=0, keepdims=True))

    sa = jnp.zeros((n, LANES), F32)
    for i in range(n):
        sa = sa + st[i] * kk_s[i:i + 1, 0:LANES]
    for t in range(steps):
        t_next = min(t + 1, steps - 1)
        row = lambda ref, t=t: (lambda i: ref[i:i + 1, t * LANES:(t + 1) * LANES])
        v = tile(v_ref, t)
        y, sa = _wkv_step(st, sa, v, row(d_ref), row(b_s), row(k2_s), row(r_ref),
                          row(kk_s, t_next))
        ym = jnp.mean(y, axis=0, keepdims=True)
        yc = y - ym
        yv = jnp.mean(yc * yc, axis=0, keepdims=True)
        yn = yc * lax.rsqrt(yv + GN_EPS) * lg_p[...] + lb_p[...]
        y_ref[:, t * LANES:(t + 1) * LANES] = yn + bonus[t] * v
    so_ref[...] = jnp.swapaxes(st[...], 0, 1)


def _wkv_batch_lanes(rkv, a, dec, s0, k_k, k_a, r_k, lnx_g, lnx_b, *, steps):
    d, cols = a.shape
    heads, n = s0.shape[0], HEAD_DIM
    params = [jnp.broadcast_to(p.reshape(heads, n, 1), (heads, n, LANES))
              for p in (k_k, k_a, r_k, lnx_g, lnx_b)]
    chan = lambda c: pl.BlockSpec((n, cols), lambda h: (c * heads + h, 0))
    par = pl.BlockSpec((None, n, LANES), lambda h: (h, 0, 0))
    st = pl.BlockSpec((None, n, n, LANES), lambda h: (h, 0, 0, 0))
    return pl.pallas_call(
        functools.partial(_wkv_bl_kernel, steps=steps),
        grid=(heads,),
        in_specs=[chan(0), chan(1), chan(2), chan(0), chan(0)] + [par] * 5 + [st],
        out_specs=[chan(0), st],
        out_shape=[jax.ShapeDtypeStruct((d, cols), F32), jax.ShapeDtypeStruct(s0.shape, F32)],
        scratch_shapes=[pltpu.VMEM((n, n, LANES), F32)] + [pltpu.VMEM((n, cols), F32)] * 3,
        compiler_params=_cparams(1),
        name="wkv7_tm",
    )(rkv, rkv, rkv, a, dec, *params, s0)


CONV_TM_COLS = 512


def _conv_tm_kernel(u1_ref, u2_ref, bg1, bg2, dwk, dwb, lg, lb, st_ref, y_ref, sto_ref, zc,
                    *, steps, b):
    j = pl.program_id(0)
    hist = CONV_W - 1
    tc = u1_ref.shape[-1]
    for c0 in range(0, tc, LANES):
        cs = slice(c0, c0 + LANES)
        glu = [(u1_ref[t * b:(t + 1) * b, cs] + bg1[:, cs])
               * _sigmoid(u2_ref[t * b:(t + 1) * b, cs] + bg2[:, cs]) for t in range(steps)]
        window = lambda i: st_ref[i, :, cs] if i < hist else glu[i - hist]
        for t in range(steps):
            acc = jnp.zeros((b, LANES), F32)
            for tap in range(CONV_W):
                acc = acc + window(t + tap) * dwk[tap:tap + 1, cs]
            zc[j, t * b:(t + 1) * b, cs] = acc + dwb[:, cs]
        for i in range(hist):
            sto_ref[i, :, cs] = window(i + steps)

    @pl.when(j == pl.num_programs(0) - 1)
    def _():
        z = zc[...]
        m = jnp.mean(jnp.mean(z, axis=-1, keepdims=True), axis=0, keepdims=True)
        ctr = z - m
        var = jnp.mean(jnp.mean(ctr * ctr, axis=-1, keepdims=True), axis=0, keepdims=True)
        ln = ctr * lax.rsqrt(var + 1e-5)
        for jj in range(zc.shape[0]):
            cols = slice(jj * tc, (jj + 1) * tc)
            y_ref[:, cols] = _silu(ln[jj] * lg[:, cols] + lb[:, cols]).astype(BF16)


def _conv_tm(glu_in, b_glu, dw_k, dw_b, ln_g, ln_b, st0, *, steps):
    rows, _ = glu_in.shape
    hist, b, d = st0.shape
    tc = CONV_TM_COLS
    nct = d // tc
    col = lambda shape, h=0: pl.BlockSpec(shape, lambda j: (0, j + h * nct))
    state = pl.BlockSpec((hist, b, tc), lambda j: (0, 0, j))
    return pl.pallas_call(
        functools.partial(_conv_tm_kernel, steps=steps, b=b),
        grid=(nct,),
        in_specs=[col((rows, tc)), col((rows, tc), 1), col((1, tc)), col((1, tc), 1),
                  col((CONV_W, tc)), col((1, tc)), _const_spec((1, d)), _const_spec((1, d)), state],
        out_specs=[pl.BlockSpec((rows, d), lambda j: (0, 0)), state],
        out_shape=[jax.ShapeDtypeStruct((rows, d), BF16), jax.ShapeDtypeStruct(st0.shape, F32)],
        scratch_shapes=[pltpu.VMEM((nct, rows, tc), F32)],
        compiler_params=_cparams(1),
        name="conformer_conv_tm",
    )(glu_in, glu_in, b_glu.reshape(1, 2 * d), b_glu.reshape(1, 2 * d), dw_k, dw_b.reshape(1, d),
      ln_g.reshape(1, d), ln_b.reshape(1, d), st0)


def _merge_tm_kernel(yt_ref, g_ref, ga_ref, gb_ref, yb_ref, x_ref, wo_ref, m_ref, n2_ref,
                     x1_ref, h2_ref):
    f32 = lambda ref: ref[...].astype(F32)
    merged = f32(ga_ref) * (yt_ref[...].T * g_ref[...]) + f32(gb_ref) * f32(yb_ref)
    mix = jnp.dot(merged.astype(BF16), wo_ref[...], preferred_element_type=F32)
    x1 = x_ref[...] + m_ref[2] * mix
    x1_ref[...] = x1
    ms = jnp.mean(x1 * x1, axis=-1, keepdims=True)
    h2 = (x1 * lax.rsqrt(ms + 1e-6) * n2_ref[...]) * (1.0 + m_ref[4]) + m_ref[3]
    h2_ref[...] = h2.astype(BF16)


def _merge_tm(yt, g, ga, gb, yb, x, w_out, mod, norm_g, *, steps):
    rows, d = x.shape
    b = rows // steps
    row = pl.BlockSpec((b, d), lambda i: (i, 0))
    return pl.pallas_call(
        _merge_tm_kernel,
        grid=(steps,),
        in_specs=[pl.BlockSpec((d, b), lambda i: (0, i)), row, row, row, row, row,
                  _const_spec((d, d)), _const_spec((N_MOD, b, d)), _const_spec((1, d))],
        out_specs=[row, row],
        out_shape=[jax.ShapeDtypeStruct((rows, d), F32), jax.ShapeDtypeStruct((rows, d), BF16)],
        compiler_params=_cparams(1),
        name="merge_out_proj_tm",
    )(yt, g, ga, gb, yb, x, w_out, mod, norm_g.reshape(1, d))


def _up_act_tm_kernel(h_ref, wg_ref, wv_ref, kg_ref, kv_ref, bg_ref, bv_ref, sg_ref, sv_ref,
                      act_ref, stg_ref, stv_ref, *, steps, b):
    hist = FFN_CONV_W - 1
    h = h_ref[...]
    tn = wg_ref.shape[-1]
    z = []
    halves = ((wg_ref, kg_ref, bg_ref, sg_ref, stg_ref), (wv_ref, kv_ref, bv_ref, sv_ref, stv_ref))
    for w_ref, k_ref, b_ref, s_ref, st_ref in halves:
        u = jnp.dot(h, w_ref[...], preferred_element_type=F32).reshape(steps, b, tn)
        window = [s_ref[i] for i in range(hist)] + [u[t] for t in range(steps)]
        z.append(jnp.stack([b_ref[...] + sum(window[t + tap] * k_ref[tap:tap + 1, :]
                                             for tap in range(FFN_CONV_W))
                            for t in range(steps)]))
        for i in range(hist):
            st_ref[i] = window[steps + i]
    act_ref[...] = (_silu(z[0]) * z[1]).reshape(steps * b, tn).astype(BF16)


def _up_act_tm(h2, w_up, ffn_dw_k, ffn_dw_b, st0, *, steps, d_ff):
    rows, k = h2.shape
    hist, b, _ = st0.shape
    f = w_up.shape[1] // 2
    nj = f // FFN_TN
    col = lambda shape, h: pl.BlockSpec(shape, lambda j: (0, j + h * nj))
    state = lambda h: pl.BlockSpec((hist, b, FFN_TN), lambda j: (0, 0, j + h * nj))
    return pl.pallas_call(
        functools.partial(_up_act_tm_kernel, steps=steps, b=b),
        grid=(nj,),
        in_specs=[_const_spec((rows, k)), col((k, FFN_TN), 0), col((k, FFN_TN), 1),
                  col((FFN_CONV_W, FFN_TN), 0), col((FFN_CONV_W, FFN_TN), 1),
                  col((1, FFN_TN), 0), col((1, FFN_TN), 1), state(0), state(1)],
        out_specs=[pl.BlockSpec((rows, FFN_TN), lambda j: (0, j)), state(0), state(0)],
        out_shape=[jax.ShapeDtypeStruct((rows, d_ff), BF16),
                   jax.ShapeDtypeStruct((hist, b, f), F32),
                   jax.ShapeDtypeStruct((hist, b, f), F32)],
        compiler_params=_cparams(1),
        name="up_conv_act_tm",
    )(h2, w_up, w_up, ffn_dw_k, ffn_dw_k, ffn_dw_b, ffn_dw_b, st0, st0)


def _sample_layer(x, mod, shift0, wkv0, conv0, ffn0, p):
    b, t, d = x.shape
    assert b == LANES
    rows = t * b
    tm = lambda z: jnp.swapaxes(z, 0, 1)
    w_in = p["w_in"]
    x_tm = tm(x)
    mod_tm = tm(mod)
    xs, shift1 = _prep_tm(x_tm, mod_tm, p["norm1_g"], p["mu"], shift0)

    tn = PROJ_TN
    rkv = _proj_transposed(xs, w_in, tn=tn, n=3 * d)
    proj = functools.partial(_matmul, xs, w_in, tm=rows, tn=tn, slab=SLAB_H)
    glu_in = proj(col0=3 * d // tn, n=2 * d, name="glu_proj")
    ga = proj(col0=5 * d // tn, n=d, sigmoid=True, out_dtype=BF16, name="gate_a_proj")
    gb = proj(col0=6 * d // tn, n=d, sigmoid=True, out_dtype=BF16, name="gate_b_proj")
    dec, a, g = _lora_transposed(xs, p["w0"], p["w1"], p["w2"], p["a0"], p["a1"], p["a2"],
                                 p["g1"], p["g2"])

    yt, s1 = _wkv_batch_lanes(rkv, a, dec, wkv0.transpose(1, 2, 3, 0), p["k_k"], p["k_a"],
                              p["r_k"], p["lnx_g"], p["lnx_b"], steps=t)
    wkv1 = s1.transpose(3, 0, 1, 2)

    yb, conv1 = _conv_tm(glu_in, p["b_glu"], p["dw_k"], p["dw_b"], p["ln_conv_g"],
                         p["ln_conv_b"], tm(conv0), steps=t)
    x1, h2 = _merge_tm(yt, g, ga, gb, yb, x_tm.reshape(rows, d), p["w_out"].astype(BF16), mod_tm,
                       p["norm2_g"], steps=t)

    d_ff = p["w_down"].shape[0]
    f_pad = -(-d_ff // FFN_TN) * FFN_TN
    act, ffn_g, ffn_v = _up_act_tm(
        h2, _pad_halves(p["w_up"], f_pad).astype(BF16), _pad_halves(p["ffn_dw_k"], f_pad),
        _pad_halves(p["ffn_dw_b"], f_pad).reshape(1, 2 * f_pad), _pad_halves(tm(ffn0), f_pad),
        steps=t, d_ff=d_ff)
    ffn1 = tm(jnp.concatenate([ffn_g[..., :d_ff], ffn_v[..., :d_ff]], axis=-1))
    x2 = _matmul(act[None], p["w_down"], tm=rows, tn=DOWN_TN, res=x1, gate=jnp.tile(mod_tm[5], (t, 1)),
                 name="down_proj")
    y = _final_norm(x2, p["normf_g"], tm=min(NORM_TM, rows))
    return tm(y.reshape(t, b, d)), shift1, wkv1, tm(conv1), ffn1


def _head_minor(w):
    lead = w.shape[:-1]
    return w.reshape(*lead, -1, HEAD_DIM).swapaxes(-1, -2).reshape(*lead, w.shape[-1])


def _prompt_layer(x, mod, shift0, wkv0, conv0, ffn0, p, *, tm_seq, tq_proj, tq_tok, tm_mm, tt,
                  tq_ffn):
    b, t, d = x.shape
    rows = b * t
    groups = b // GROUP
    w_in = p["w_in"]

    xs, shift1 = _prep(x, mod, p["norm1_g"], p["mu"], shift0, tm=tm_seq)
    xs2 = xs.reshape(7, rows, d)

    w_hm = jnp.concatenate([_head_minor(w_in[:, i * d:(i + 1) * d]) for i in (0, 1, 2, 5)],
                           axis=1).astype(BF16)
    rkv = _proj_head_lanes(xs, w_hm, _Tiling(groups, t, 1, tq_proj), tn=d, n=3 * d)
    tn = PROJ_TN
    ga = _matmul(xs2, w_hm, tm=tm_mm, tn=tn, slab=SLAB_H, col0=3 * d // tn, n=d, sigmoid=True,
                 out_dtype=BF16, name="gate_a_proj")
    glu_in = _matmul(xs2, w_in, tm=tm_mm, tn=tn, slab=SLAB_H, col0=3 * d // tn, n=2 * d,
                     name="glu_proj")
    gb = _matmul(xs2, w_in, tm=tm_mm, tn=tn, slab=SLAB_H, col0=6 * d // tn, n=d, sigmoid=True,
                 out_dtype=BF16, name="gate_b_proj")

    tl = _Tiling(groups, t, 1, tq_tok)
    dec, a, g = _lora(xs, p["w0"], p["w1"], p["w2"], p["a0"], p["a1"], p["a2"], p["g1"], p["g2"], tl)
    yw, wkv1 = _wkv(rkv, a, dec, wkv0, p["k_k"], p["k_a"], p["r_k"], p["lnx_g"], p["lnx_b"], tt=tt)
    yb, conv1 = _conv(glu_in.reshape(b, t, 2 * d), p["b_glu"], p["dw_k"], p["dw_b"],
                      p["ln_conv_g"], p["ln_conv_b"], conv0, tm=tm_seq)
    x1, h2 = _merge(yw, g, ga.reshape(b, t, d), gb.reshape(b, t, d), yb, x, p["w_out"], mod,
                    p["norm2_g"], tl)

    d_ff = p["w_down"].shape[0]
    f_pad = -(-d_ff // FFN_TN) * FFN_TN
    act, ffn_g, ffn_v = _up_act(
        h2.reshape(rows, d), _pad_halves(p["w_up"], f_pad).astype(BF16),
        _pad_halves(p["ffn_dw_k"], f_pad), _pad_halves(p["ffn_dw_b"], f_pad).reshape(1, 2 * f_pad),
        _pad_halves(ffn0, f_pad), tq=tq_ffn, t=t, d_ff=d_ff)
    ffn1 = jnp.concatenate([ffn_g[..., :d_ff], ffn_v[..., :d_ff]], axis=-1)
    x2 = _matmul(act[None], p["w_down"], tm=min(tm_mm, DOWN_TM), tn=DOWN_TN, res=x1.reshape(rows, d),
                 gate=mod[:, 5][:, None, :], rows_per_gate=t, name="down_proj")
    y = _final_norm(x2, p["normf_g"], tm=min(NORM_TM, rows))
    return y.reshape(b, t, d), shift1.reshape(b, d), wkv1, conv1, ffn1


def kernel(x_prompt, x_sample, state_shift, state_wkv, state_conv, state_ffn, c_prompt, c_sample,
           norm1_g, norm2_g, normf_g, w_ada, b_ada, mu, w_in, b_glu, w0, w1, w2, a0, a1, a2, g1, g2,
           k_k, k_a, r_k, lnx_g, lnx_b, dw_k, dw_b, ln_conv_g, ln_conv_b, w_out, w_up, ffn_dw_k,
           ffn_dw_b, w_down):
    p = dict(norm1_g=norm1_g, norm2_g=norm2_g, normf_g=normf_g, mu=mu, w_in=w_in, b_glu=b_glu,
             w0=w0, w1=w1, w2=w2, a0=a0, a1=a1, a2=a2, g1=g1, g2=g2, k_k=k_k, k_a=k_a, r_k=r_k,
             lnx_g=lnx_g, lnx_b=lnx_b, dw_k=dw_k, dw_b=dw_b, ln_conv_g=ln_conv_g,
             ln_conv_b=ln_conv_b, w_out=w_out, w_up=w_up, ffn_dw_k=ffn_dw_k, ffn_dw_b=ffn_dw_b,
             w_down=w_down)
    bp, _, d = x_prompt.shape
    bs = x_sample.shape[0]
    f2 = w_up.shape[1]

    c_all = jnp.concatenate([c_prompt, c_sample], axis=0)
    c_rows = -(-c_all.shape[0] // SUBLANES) * SUBLANES
    c_all = jnp.pad(c_all, ((0, c_rows - c_all.shape[0]), (0, 0)))
    mod = _modulation(c_all, w_ada, b_ada).reshape(c_rows, N_MOD, d)
    mod_p, mod_s = mod[:bp], mod[bp:bp + bs]

    y_p, shift_p, wkv_p, conv_p, ffn_p = _prompt_layer(
        x_prompt, mod_p, jnp.zeros((bp, d), F32),
        jnp.zeros((bp, d // HEAD_DIM, HEAD_DIM, HEAD_DIM), F32),
        jnp.zeros((bp, CONV_W - 1, d), F32), jnp.zeros((bp, FFN_CONV_W - 1, f2), F32), p,
        **PROMPT_TILES)

    y_s, shift_s, wkv_s, conv_s, ffn_s = _sample_layer(
        x_sample, mod_s, state_shift, state_wkv, state_conv, state_ffn, p)
    return (y_p, y_s, shift_p, wkv_p, conv_p, ffn_p, shift_s, wkv_s, conv_s, ffn_s)
```

```python
import functools
import math
from typing import NamedTuple

import jax
import jax.numpy as jnp
from jax import lax
from jax.experimental import pallas as pl
from jax.experimental.pallas import tpu as pltpu

F32 = jnp.float32
BF16 = jnp.bfloat16

HEAD_DIM = 64
CONV_W = 31
FFN_CONV_W = 3
N_MOD = 6
GN_EPS = HEAD_DIM * 1e-5
DECAY_SCALE = math.exp(-0.5)
SUBLANES = 8
LANES = 128
MXU_COLS = 256
LORA_PAD = 128
GROUP = 4
V7X_VMEM_LIMIT = 56 * 1024 * 1024

PROJ_TN = 1024
MOD_TN = 2048
DOWN_TM, DOWN_TN = 512, 512
NORM_TM = 256
PROMPT_TILES = dict(tm_seq=256, tq_proj=256, tq_tok=64, tm_mm=1024, tt=64, tq_ffn=1024)

SLAB_R, SLAB_K, SLAB_V, SLAB_H, SLAB_W, SLAB_A, SLAB_G = range(7)
MU_SLABS = (SLAB_R, SLAB_W, SLAB_K, SLAB_V, SLAB_A, SLAB_G)


class _Tiling(NamedTuple):
    groups: int
    t: int
    gs: int
    tq: int

    @property
    def grid(self):
        return (self.groups // self.gs, self.t // self.tq)

    @property
    def seqs(self):
        return self.gs * GROUP

    def plain(self, cols, col=0):
        return pl.BlockSpec((self.seqs, self.tq, cols), lambda ig, it: (ig, it, col))

    def head_lanes(self, blk=0):
        return pl.BlockSpec((self.gs, HEAD_DIM, self.tq, LANES), lambda ig, it: (ig, blk, it, 0))

    def per_seq(self, rows, cols):
        return pl.BlockSpec((self.seqs, rows, cols), lambda ig, it: (ig, 0, 0))


def _cparams(n_axes):
    return pltpu.CompilerParams(dimension_semantics=("arbitrary",) * n_axes,
                                vmem_limit_bytes=V7X_VMEM_LIMIT)


def _const_spec(shape):
    return pl.BlockSpec(shape, lambda *_: (0,) * len(shape), pipeline_mode=pl.Buffered(1))


def _sigmoid(x):
    return 1.0 / (1.0 + jnp.exp(-x))


def _silu(x):
    return x * _sigmoid(x)


def _swap_lane_groups(m, lane):
    half = lane < 2 * (LANES // GROUP)
    odd = (lane // (LANES // GROUP)) % 2 == 1
    n = [None] * 4
    for s in range(2):
        n[s] = jnp.where(half, m[s], pltpu.roll(m[s + 2], LANES // 2, axis=1))
        n[s + 2] = jnp.where(half, pltpu.roll(m[s], LANES // 2, axis=1), m[s + 2])
    out = [None] * 4
    for p in (0, 2):
        a, b = n[p], n[p + 1]
        out[p] = jnp.where(odd, pltpu.roll(b, LANES // GROUP, axis=1), a)
        out[p + 1] = jnp.where(odd, b, pltpu.roll(a, LANES - LANES // GROUP, axis=1))
    return out


def _store_head_lanes(val, out_ref, gs, tq, q0=0):
    lane = lax.broadcasted_iota(jnp.int32, (tq, LANES), 1)
    for g in range(gs):
        for q in range(val.shape[1] // LANES):
            src = [val[(g * GROUP + s) * tq:(g * GROUP + s + 1) * tq, q * LANES:(q + 1) * LANES]
                   for s in range(GROUP)]
            out = _swap_lane_groups(src, lane)
            for j in range(GROUP):
                out_ref[g, GROUP * (q0 + q) + j] = out[j]


def _rows2d(ref):
    v = ref[...]
    return v.reshape(-1, v.shape[-1])


def _mod_kernel(c_ref, w_ref, b_ref, o_ref):
    s = _silu(c_ref[...]).astype(BF16)
    o_ref[...] = jnp.dot(s, w_ref[...].astype(BF16), preferred_element_type=F32) + b_ref[...]


def _modulation(c, w_ada, b_ada):
    rows, d = c.shape
    n = w_ada.shape[1]
    tn = MOD_TN
    return pl.pallas_call(
        _mod_kernel,
        grid=(n // tn,),
        in_specs=[pl.BlockSpec((rows, d), lambda j: (0, 0)),
                  pl.BlockSpec((d, tn), lambda j: (0, j)),
                  pl.BlockSpec((1, tn), lambda j: (0, j))],
        out_specs=pl.BlockSpec((rows, tn), lambda j: (0, j)),
        out_shape=jax.ShapeDtypeStruct((rows, n), F32),
        compiler_params=_cparams(1),
        name="modulation",
    )(c, w_ada, b_ada.reshape(1, n))


def _prep_kernel(x_ref, m_ref, g_ref, mu_ref, s0_ref, xs_ref, so_ref, hbuf, *, tm):
    x = x_ref[...]
    m = m_ref[...]
    ms = jnp.mean(x * x, axis=-1, keepdims=True)
    h = (x * lax.rsqrt(ms + 1e-6) * g_ref[...]) * (1.0 + m[1:2, :]) + m[0:1, :]
    hbuf[SUBLANES:SUBLANES + tm, :] = h

    @pl.when(pl.program_id(1) == 0)
    def _():
        hbuf[SUBLANES - 1:SUBLANES, :] = s0_ref[...]

    dx = hbuf[SUBLANES - 1:SUBLANES - 1 + tm, :] - h
    for mi, slab in enumerate(MU_SLABS):
        xs_ref[slab] = (h + dx * mu_ref[mi:mi + 1, :]).astype(BF16)
    xs_ref[SLAB_H] = h.astype(BF16)
    last = h[tm - 1:tm, :]
    hbuf[SUBLANES - 1:SUBLANES, :] = last
    so_ref[...] = last


def _prep(x, mod, norm_g, mu, shift0, *, tm):
    b, t, d = x.shape
    return pl.pallas_call(
        functools.partial(_prep_kernel, tm=tm),
        grid=(b, t // tm),
        in_specs=[pl.BlockSpec((None, tm, d), lambda i, j: (i, j, 0)),
                  pl.BlockSpec((None, N_MOD, d), lambda i, j: (i, 0, 0)),
                  _const_spec((1, d)),
                  _const_spec((N_MOD, d)),
                  pl.BlockSpec((None, 1, d), lambda i, j: (i, 0, 0))],
        out_specs=[pl.BlockSpec((7, None, tm, d), lambda i, j: (0, i, j, 0)),
                   pl.BlockSpec((None, 1, d), lambda i, j: (i, 0, 0))],
        out_shape=[jax.ShapeDtypeStruct((7, b, t, d), BF16),
                   jax.ShapeDtypeStruct((b, 1, d), F32)],
        scratch_shapes=[pltpu.VMEM((SUBLANES + tm, d), F32)],
        compiler_params=_cparams(2),
        name="prep",
    )(x, mod, norm_g.reshape(1, d), mu, shift0.reshape(b, 1, d))


def _gate_blocks(gate, rows, tm, rows_per_gate):
    if gate.ndim == 2:
        return gate.reshape(rows // tm, tm, gate.shape[-1]), 1
    assert rows_per_gate % tm == 0, (rows_per_gate, tm)
    return gate, rows_per_gate // tm


def _mm_kernel(*refs, gated, sigmoid):
    if gated:
        a_ref, w_ref, res_ref, gate_ref, o_ref, wb = refs
    else:
        a_ref, w_ref, o_ref, wb = refs

    @pl.when(pl.program_id(1) == 0)
    def _():
        wb[...] = w_ref[...].astype(BF16)

    acc = jnp.dot(a_ref[...], wb[...], preferred_element_type=F32)
    if gated:
        acc = res_ref[...] + gate_ref[...] * acc
    if sigmoid:
        acc = _sigmoid(acc)
    o_ref[...] = acc.astype(o_ref.dtype)


def _matmul(a, w, *, tm, tn, slab=0, col0=0, n=None, res=None, gate=None, rows_per_gate=None,
            sigmoid=False, out_dtype=F32, name):
    _, rows, k = a.shape
    n = w.shape[1] if n is None else n
    gated = res is not None
    in_specs = [pl.BlockSpec((None, tm, k), lambda j, i: (slab, i, 0)),
                pl.BlockSpec((k, tn), lambda j, i: (0, j + col0))]
    args = [a, w]
    if gated:
        gate, tiles_per_gate = _gate_blocks(gate, rows, tm, rows_per_gate)
        gr = gate.shape[1]
        in_specs += [pl.BlockSpec((tm, tn), lambda j, i: (i, j)),
                     pl.BlockSpec((None, gr, tn), lambda j, i: (i // tiles_per_gate, 0, j))]
        args += [res, gate]
    return pl.pallas_call(
        functools.partial(_mm_kernel, gated=gated, sigmoid=sigmoid),
        grid=(pl.cdiv(n, tn), rows // tm),
        in_specs=in_specs,
        out_specs=pl.BlockSpec((tm, tn), lambda j, i: (i, j)),
        out_shape=jax.ShapeDtypeStruct((rows, n), out_dtype),
        scratch_shapes=[pltpu.VMEM((k, tn), BF16)],
        compiler_params=_cparams(2),
        name=name,
    )(*args)


def _proj_hl_kernel(a_ref, w_ref, o_ref, *, gs, tq):
    for c in range(0, w_ref.shape[1], MXU_COLS):
        acc = jnp.dot(_rows2d(a_ref), w_ref[:, c:c + MXU_COLS], preferred_element_type=F32)
        _store_head_lanes(acc, o_ref, gs, tq, q0=c // LANES)


def _proj_head_lanes(xs, w, tl, *, tn, n):
    k = w.shape[0]
    d = n // 3
    slab = lambda j: j * tn // d
    a_spec = pl.BlockSpec((None, tl.seqs, tl.tq, k), lambda j, ig, it: (slab(j), ig, it, 0))
    nq = tn // (LANES // GROUP)
    return pl.pallas_call(
        functools.partial(_proj_hl_kernel, gs=tl.gs, tq=tl.tq),
        grid=(n // tn,) + tl.grid,
        in_specs=[a_spec, pl.BlockSpec((k, tn), lambda j, ig, it: (0, j))],
        out_specs=pl.BlockSpec((tl.gs, nq, tl.tq, LANES), lambda j, ig, it: (ig, j, it, 0)),
        out_shape=jax.ShapeDtypeStruct((tl.groups, n // (LANES // GROUP), tl.t, LANES), F32),
        compiler_params=_cparams(3),
        name="rkv_proj",
    )(xs, w)


def _lora_heads(xw_ref, xa_ref, xg_ref, w1, w2, a1, a2, g1, g2, w0, a0):
    dot = functools.partial(jnp.dot, preferred_element_type=F32)
    hw = jnp.tanh(dot(_rows2d(xw_ref), w1[...])).astype(BF16)
    w_pre = w0[...] + dot(hw, w2[...])
    dec = jnp.exp(-DECAY_SCALE * _sigmoid(w_pre))
    ha = dot(_rows2d(xa_ref), a1[...]).astype(BF16)
    a = _sigmoid(a0[...] + dot(ha, a2[...]))
    hg = _sigmoid(dot(_rows2d(xg_ref), g1[...])).astype(BF16)
    return dec, a, dot(hg, g2[...])


def _lora_kernel(*refs, gs, tq):
    dec_ref, a_ref, g_ref = refs[-3:]
    dec, a, g = _lora_heads(*refs[:-3])
    _store_head_lanes(dec, dec_ref, gs, tq)
    _store_head_lanes(a, a_ref, gs, tq)
    g_ref[...] = g.astype(BF16).reshape(g_ref.shape)


def _pad_lora(w_a, w_b):
    r = w_a.shape[1]
    return (jnp.pad(w_a, ((0, 0), (0, LORA_PAD - r))).astype(BF16),
            jnp.pad(w_b, ((0, LORA_PAD - r), (0, 0))).astype(BF16))


def _lora(xs, w0, w1, w2, a0, a1, a2, g1, g2, tl):
    d = w0.shape[0]
    w1p, w2p = _pad_lora(w1, _head_minor(w2))
    a1p, a2p = _pad_lora(a1, _head_minor(a2))
    g1b, g2b = g1.astype(BF16), _head_minor(g2).astype(BF16)
    slab = lambda s: pl.BlockSpec((None, tl.seqs, tl.tq, d), lambda ig, it: (s, ig, it, 0))
    g_shape = (tl.groups * GROUP, tl.t, d)
    hl_shape = jax.ShapeDtypeStruct((tl.groups, HEAD_DIM, tl.t, LANES), F32)
    consts = [w1p, w2p, a1p, a2p, g1b, g2b, _head_minor(w0).reshape(1, d),
              _head_minor(a0).reshape(1, d)]
    return pl.pallas_call(
        functools.partial(_lora_kernel, gs=tl.gs, tq=tl.tq),
        grid=tl.grid,
        in_specs=[slab(SLAB_W), slab(SLAB_A), slab(SLAB_G)] + [_const_spec(c.shape) for c in consts],
        out_specs=[tl.head_lanes(), tl.head_lanes(), tl.plain(d)],
        out_shape=[hl_shape, hl_shape, jax.ShapeDtypeStruct(g_shape, BF16)],
        compiler_params=_cparams(2),
        name="lora",
    )(xs, xs, xs, *consts)


def _wkv_step(st, sa, v, w_row, b_row, k2_row, r_row, kk_next_row):
    y = jnp.zeros((HEAD_DIM, LANES), F32)
    sa_next = jnp.zeros((HEAD_DIM, LANES), F32)
    for i in range(HEAD_DIM):
        decayed = st[i] if w_row is None else st[i] * w_row(i)
        si = decayed - sa * b_row(i) + v * k2_row(i)
        st[i] = si
        y = y + si * r_row(i)
        sa_next = sa_next + si * kk_next_row(i)
    return y, sa_next


def _wkv_kernel(r_ref, k_ref, v_ref, a_ref, d_ref, kk_p, ka_p, rk_p, lg_p, lb_p, s0_ref,
                y_ref, so_ref, st, kk_s, b_s, k2_s, r_s, pe_s, bon_s, v_t, y_t, *, nb):
    n = HEAD_DIM
    tt = nb * SUBLANES
    bc = lambda p: p[...][:, None, None, :]

    @pl.when(pl.program_id(1) == 0)
    def _():
        s = s0_ref[...].reshape(LANES, n, n)
        st[...] = jnp.swapaxes(jnp.swapaxes(jnp.swapaxes(s, 0, 1), 1, 2), 0, 1)

    x = d_ref[...].reshape(n * nb, SUBLANES, LANES)
    step = lax.broadcasted_iota(jnp.int32, x.shape, 1)
    for sh in (1, 2, 4):
        x = x * jnp.where(step >= sh, pltpu.roll(x, sh, axis=1), 1.0)
    x = x.reshape(n, nb, SUBLANES, LANES)
    step = step.reshape(x.shape)
    run = jnp.ones((n, 1, LANES), F32)
    incl, excl = [], []
    for tb in range(nb):
        blk = x[:, tb] * run
        incl.append(blk)
        excl.append(jnp.where(step[:, tb] >= 1, pltpu.roll(blk, 1, axis=1), run))
        run = blk[:, SUBLANES - 1:SUBLANES, :]
    p_t = jnp.stack(incl, axis=1)
    inv_p = 1.0 / p_t

    k = k_ref[...]
    a = a_ref[...]
    r = r_ref[...]
    kk = k * bc(kk_p)
    kk = kk * (1.0 / jnp.maximum(jnp.sqrt(jnp.sum(kk * kk, axis=0, keepdims=True)), 1e-12))
    k2 = k * (1.0 + (a - 1.0) * bc(ka_p))
    kk_s[...] = kk * jnp.stack(excl, axis=1)
    b_s[...] = kk * a * inv_p
    k2_s[...] = k2 * inv_p
    r_s[...] = r * p_t
    pe_s[...] = p_t[:, nb - 1, SUBLANES - 1, :]
    bon_s[...] = jnp.sum(r * k2 * bc(rk_p), axis=0)
    v_t[...] = jnp.swapaxes(v_ref[...], 0, 1)

    sa0 = jnp.zeros((n, LANES), F32)
    for i in range(n):
        sa0 = sa0 + st[i] * kk_s[i, 0, 0:1, :]

    def block(tb, sa):
        tb_next = jnp.minimum(tb + 1, nb - 1)
        for j in range(SUBLANES):
            t = tb * SUBLANES + j
            row = lambda ref: (lambda i: ref[i, tb, j:j + 1, :])
            if j + 1 < SUBLANES:
                kk_next = lambda i: kk_s[i, tb, j + 1:j + 2, :]
            else:
                kk_next = lambda i: kk_s[i, tb_next, 0:1, :]
            y_t[t], sa = _wkv_step(st, sa, v_t[t], None, row(b_s), row(k2_s), row(r_s), kk_next)
        return sa

    lax.fori_loop(0, nb, block, sa0)
    for i in range(n):
        st[i] = st[i] * pe_s[i:i + 1, :]

    y = jnp.swapaxes(y_t[...], 0, 1)
    ym = jnp.mean(y, axis=0, keepdims=True)
    yc = y - ym
    yv = jnp.mean(yc * yc, axis=0, keepdims=True)
    bcast = lambda p: p[...][:, None, :]
    yn = yc * lax.rsqrt(yv + GN_EPS) * bcast(lg_p) + bcast(lb_p)
    y_ref[...] = yn + bon_s[...].reshape(1, tt, LANES) * v_ref[...]

    @pl.when(pl.program_id(1) == pl.num_programs(1) - 1)
    def _():
        s = jnp.swapaxes(jnp.swapaxes(jnp.swapaxes(st[...], 0, 1), 1, 2), 0, 1)
        so_ref[...] = s.reshape(so_ref.shape)


def _head_lanes(p):
    hn = p.reshape(-1, HEAD_DIM).T
    return jnp.tile(hn, (1, GROUP))


def _wkv(rkv, a, dec, s0, k_k, k_a, r_k, lnx_g, lnx_b, *, tt):
    g, _, t, lanes = a.shape
    n = HEAD_DIM
    heads = s0.shape[1]
    nb = tt // SUBLANES
    by8 = lambda z: z.reshape(z.shape[0], z.shape[1], t // SUBLANES, SUBLANES, lanes)
    params = [_head_lanes(p) for p in (k_k, k_a, r_k.reshape(-1), lnx_g, lnx_b)]
    rows = lambda blk: pl.BlockSpec((None, n, nb, SUBLANES, lanes), lambda i, j: (i, blk, j, 0, 0))
    tile = lambda blk: pl.BlockSpec((None, n, tt, lanes), lambda i, j: (i, blk, j, 0))
    st = pl.BlockSpec((GROUP, heads, n, n), lambda i, j: (i, 0, 0, 0))
    return pl.pallas_call(
        functools.partial(_wkv_kernel, nb=nb),
        grid=(g, t // tt),
        in_specs=[rows(0), rows(1), tile(2), rows(0), rows(0)] + [_const_spec((n, lanes))] * 5 + [st],
        out_specs=[tile(0), st],
        out_shape=[jax.ShapeDtypeStruct((g, n, t, lanes), F32),
                   jax.ShapeDtypeStruct(s0.shape, F32)],
        scratch_shapes=[pltpu.VMEM((n, n, lanes), F32)]
        + [pltpu.VMEM((n, nb, SUBLANES, lanes), F32)] * 4
        + [pltpu.VMEM((n, lanes), F32), pltpu.VMEM((nb, SUBLANES, lanes), F32)]
        + [pltpu.VMEM((tt, n, lanes), F32)] * 2,
        compiler_params=_cparams(2),
        name="wkv7",
    )(by8(rkv), by8(rkv), rkv, by8(a), by8(dec), *params, s0)


CONV_PAD = 32
CONV_ROWS = 32


def _conv_kernel(u1_ref, u2_ref, bg1, bg2, dwk, dwb, lg, lb, st0_ref, y_ref, sto_ref, zbuf, cbuf,
                 zs, *, tm):
    hist = CONV_W - 1
    lo = CONV_PAD - hist
    d = u1_ref.shape[-1]
    rc = min(CONV_ROWS, tm)
    glu = (u1_ref[...] + bg1[...]) * _sigmoid(u2_ref[...] + bg2[...])
    zbuf[CONV_PAD:CONV_PAD + tm, :] = glu

    @pl.when(pl.program_id(1) == 0)
    def _():
        zbuf[lo:CONV_PAD, :] = st0_ref[...]

    for b in range(SUBLANES):
        span = tm + SUBLANES * ((hist - b) // SUBLANES)
        zs[b, 0:span, :] = zbuf[lo + b:lo + b + span, :]
    for r0 in range(0, tm, rc):
        for c0 in range(0, d, LANES):
            acc = jnp.zeros((rc, LANES), F32)
            for j in range(CONV_W):
                a8 = SUBLANES * (j // SUBLANES)
                acc = acc + (zs[j % SUBLANES, a8 + r0:a8 + r0 + rc, c0:c0 + LANES]
                             * dwk[j:j + 1, c0:c0 + LANES])
            cbuf[r0:r0 + rc, c0:c0 + LANES] = acc
    zc = cbuf[...] + dwb[...]
    m = jnp.mean(zc, axis=-1, keepdims=True)
    ctr = zc - m
    var = jnp.mean(ctr * ctr, axis=-1, keepdims=True)
    ln = ctr * lax.rsqrt(var + 1e-5) * lg[...] + lb[...]
    y_ref[...] = _silu(ln).astype(BF16)
    carry = zbuf[lo + tm:CONV_PAD + tm, :]
    zbuf[lo:CONV_PAD, :] = carry
    sto_ref[...] = carry


def _conv(glu_in, b_glu, dw_k, dw_b, ln_g, ln_b, st0, *, tm):
    b, t, _ = glu_in.shape
    d = dw_k.shape[1]
    hist = CONV_W - 1
    vec = _const_spec((1, d))
    return pl.pallas_call(
        functools.partial(_conv_kernel, tm=tm),
        grid=(b, t // tm),
        in_specs=[pl.BlockSpec((None, tm, d), lambda i, j: (i, j, 0)),
                  pl.BlockSpec((None, tm, d), lambda i, j: (i, j, 1)),
                  pl.BlockSpec((1, d), lambda i, j: (0, 0)),
                  pl.BlockSpec((1, d), lambda i, j: (0, 1)),
                  _const_spec((CONV_W, d)),
                  vec, vec, vec,
                  pl.BlockSpec((None, hist, d), lambda i, j: (i, 0, 0))],
        out_specs=[pl.BlockSpec((None, tm, d), lambda i, j: (i, j, 0)),
                   pl.BlockSpec((None, hist, d), lambda i, j: (i, 0, 0))],
        out_shape=[jax.ShapeDtypeStruct((b, t, d), BF16),
                   jax.ShapeDtypeStruct((b, hist, d), F32)],
        scratch_shapes=[pltpu.VMEM((CONV_PAD + tm, d), F32), pltpu.VMEM((tm, d), F32),
                        pltpu.VMEM((SUBLANES, tm + CONV_PAD - SUBLANES, d), F32)],
        compiler_params=_cparams(2),
        name="conformer_conv",
    )(glu_in, glu_in, b_glu.reshape(1, 2 * d), b_glu.reshape(1, 2 * d), dw_k, dw_b.reshape(1, d),
      ln_g.reshape(1, d), ln_b.reshape(1, d), st0)


def _merge_kernel(yw_ref, g_ref, ga_ref, gb_ref, yb_ref, x_ref, woa_ref, wob_ref, m_ref, n2_ref,
                  x1_ref, h2_ref, ybuf, *, gs, tq):
    seqs = gs * GROUP
    d = ybuf.shape[-1]
    lane = lax.broadcasted_iota(jnp.int32, (tq, LANES), 1)
    for g in range(gs):
        for q in range(d // LANES):
            out = _swap_lane_groups([yw_ref[g, GROUP * q + j] for j in range(GROUP)], lane)
            for s in range(GROUP):
                ybuf[g * GROUP + s, :, q * LANES:(q + 1) * LANES] = out[s]
    seq3 = lambda ref: ref[...].astype(F32).reshape(seqs, tq, d)
    m = m_ref[...]
    dot = lambda z, w_ref: jnp.dot(z.reshape(seqs * tq, d).astype(BF16), w_ref[...],
                                   preferred_element_type=F32)
    branch_a = seq3(ga_ref) * (ybuf[...] * seq3(g_ref))
    branch_b = seq3(gb_ref) * seq3(yb_ref)
    mix = (dot(branch_a, woa_ref) + dot(branch_b, wob_ref)).reshape(seqs, tq, d)
    x1 = seq3(x_ref) + m[:, 2:3, :] * mix
    x1_ref[...] = x1.reshape(x1_ref.shape)
    ms = jnp.mean(x1 * x1, axis=-1, keepdims=True)
    h2 = (x1 * lax.rsqrt(ms + 1e-6) * n2_ref[...]) * (1.0 + m[:, 4:5, :]) + m[:, 3:4, :]
    h2_ref[...] = h2.reshape(seqs * tq, d).astype(BF16).reshape(h2_ref.shape)


def _merge(yw, g, ga, gb, yb, x, w_out, mod, norm_g, tl):
    d = x.shape[-1]
    plain = tl.plain(d)
    w_out_b = w_out.astype(BF16)
    w_out_a = _head_minor(w_out.T).T.astype(BF16)
    return pl.pallas_call(
        functools.partial(_merge_kernel, gs=tl.gs, tq=tl.tq),
        grid=tl.grid,
        in_specs=[tl.head_lanes(), plain, plain, plain, plain, plain, _const_spec((d, d)),
                  _const_spec((d, d)), tl.per_seq(N_MOD, d), _const_spec((1, d))],
        out_specs=[plain, plain],
        out_shape=[jax.ShapeDtypeStruct(x.shape, F32), jax.ShapeDtypeStruct(x.shape, BF16)],
        scratch_shapes=[pltpu.VMEM((tl.seqs, tl.tq, d), F32)],
        compiler_params=_cparams(2),
        name="merge_out_proj",
    )(yw, g, ga, gb, yb, x, w_out_a, w_out_b, mod, norm_g.reshape(1, d))


FFN_TN = 512


def _pad_halves(w, width):
    f = w.shape[-1] // 2
    pad = [(0, 0)] * (w.ndim - 1) + [(0, width - f)]
    return jnp.concatenate([jnp.pad(w[..., :f], pad), jnp.pad(w[..., f:], pad)], axis=-1)


def _up_act_kernel(h_ref, wg_ref, wv_ref, kg_ref, kv_ref, bg_ref, bv_ref, sg_ref, sv_ref,
                   act_ref, stg_ref, stv_ref, carry, *, tq, tiles_per_seq):
    first = pl.program_id(1) % tiles_per_seq == 0
    tn = wg_ref.shape[-1]
    row = lax.broadcasted_iota(jnp.int32, (tq, tn), 0)
    z = []
    halves = ((wg_ref, kg_ref, bg_ref, stg_ref), (wv_ref, kv_ref, bv_ref, stv_ref))

    @pl.when(first)
    def _():
        carry[0] = sg_ref[...]
        carry[1] = sv_ref[...]

    for half, (w_ref, k_ref, b_ref, st_ref) in enumerate(halves):
        u = jnp.dot(h_ref[...], w_ref[...], preferred_element_type=F32)
        prev = carry[half]
        p0, p1 = prev[0:1, :], prev[1:2, :]
        u1 = jnp.where(row == 0, p1, pltpu.roll(u, 1, axis=0))
        u2 = jnp.where(row == 0, p0, jnp.where(row == 1, p1, pltpu.roll(u, 2, axis=0)))
        z.append(b_ref[...] + u2 * k_ref[0:1, :] + u1 * k_ref[1:2, :] + u * k_ref[2:3, :])
        last = u[tq - (FFN_CONV_W - 1):, :]
        carry[half] = last
        st_ref[...] = last
    act_ref[...] = (_silu(z[0]) * z[1]).astype(BF16)


def _up_act(h2, w_up, ffn_dw_k, ffn_dw_b, st0, *, tq, t, d_ff):
    rows, k = h2.shape
    b = st0.shape[0]
    f = w_up.shape[1] // 2
    nj = f // FFN_TN
    hist = FFN_CONV_W - 1
    tiles_per_seq = t // tq
    col = lambda shape, h: pl.BlockSpec(shape, lambda j, i: (0, j + h * nj))
    state = lambda h: pl.BlockSpec((None, hist, FFN_TN),
                                   lambda j, i: (i // tiles_per_seq, 0, j + h * nj))
    return pl.pallas_call(
        functools.partial(_up_act_kernel, tq=tq, tiles_per_seq=tiles_per_seq),
        grid=(nj, rows // tq),
        in_specs=[pl.BlockSpec((tq, k), lambda j, i: (i, 0)),
                  col((k, FFN_TN), 0), col((k, FFN_TN), 1),
                  col((FFN_CONV_W, FFN_TN), 0), col((FFN_CONV_W, FFN_TN), 1),
                  col((1, FFN_TN), 0), col((1, FFN_TN), 1),
                  state(0), state(1)],
        out_specs=[pl.BlockSpec((tq, FFN_TN), lambda j, i: (i, j)), state(0), state(0)],
        out_shape=[jax.ShapeDtypeStruct((rows, d_ff), BF16),
                   jax.ShapeDtypeStruct((b, hist, f), F32),
                   jax.ShapeDtypeStruct((b, hist, f), F32)],
        scratch_shapes=[pltpu.VMEM((2, hist, FFN_TN), F32)],
        compiler_params=_cparams(2),
        name="up_conv_act",
    )(h2, w_up, w_up, ffn_dw_k, ffn_dw_k, ffn_dw_b, ffn_dw_b, st0, st0)


def _norm_kernel(x_ref, g_ref, o_ref):
    x = x_ref[...]
    ms = jnp.mean(x * x, axis=-1, keepdims=True)
    o_ref[...] = x * lax.rsqrt(ms + 1e-6) * g_ref[...]


def _final_norm(x, g, *, tm):
    rows, d = x.shape
    row = pl.BlockSpec((tm, d), lambda i: (i, 0))
    return pl.pallas_call(
        _norm_kernel,
        grid=(rows // tm,),
        in_specs=[row, _const_spec((1, d))],
        out_specs=row,
        out_shape=jax.ShapeDtypeStruct((rows, d), F32),
        compiler_params=_cparams(1),
        name="final_norm",
    )(x, g.reshape(1, d))


def _prep_tm_kernel(x_ref, m_ref, g_ref, mu_ref, s0_ref, xs_ref, so_ref, hprev):
    @pl.when(pl.program_id(0) == 0)
    def _():
        hprev[...] = s0_ref[...]

    x = x_ref[...]
    ms = jnp.mean(x * x, axis=-1, keepdims=True)
    h = (x * lax.rsqrt(ms + 1e-6) * g_ref[...]) * (1.0 + m_ref[1]) + m_ref[0]
    dx = hprev[...] - h
    for mi, slab in enumerate(MU_SLABS):
        xs_ref[slab] = (h + dx * mu_ref[mi:mi + 1, :]).astype(BF16)
    xs_ref[SLAB_H] = h.astype(BF16)
    hprev[...] = h
    so_ref[...] = h


def _prep_tm(x, mod, norm_g, mu, shift0):
    t, b, d = x.shape
    return pl.pallas_call(
        _prep_tm_kernel,
        grid=(t,),
        in_specs=[pl.BlockSpec((None, b, d), lambda i: (i, 0, 0)), _const_spec((N_MOD, b, d)),
                  _const_spec((1, d)), _const_spec((N_MOD, d)), _const_spec((b, d))],
        out_specs=[pl.BlockSpec((7, b, d), lambda i: (0, i, 0)),
                   pl.BlockSpec((b, d), lambda i: (0, 0))],
        out_shape=[jax.ShapeDtypeStruct((7, t * b, d), BF16), jax.ShapeDtypeStruct((b, d), F32)],
        scratch_shapes=[pltpu.VMEM((b, d), F32)],
        compiler_params=_cparams(1),
        name="prep_tm",
    )(x, mod, norm_g.reshape(1, d), mu, shift0)


def _proj_t_kernel(a_ref, w_ref, o_ref):
    acc = jnp.dot(a_ref[...], w_ref[...].astype(BF16), preferred_element_type=F32)
    o_ref[...] = acc.T


def _proj_transposed(xs, w, *, tn, n):
    _, rows, k = xs.shape
    d = n // 3
    return pl.pallas_call(
        _proj_t_kernel,
        grid=(n // tn,),
        in_specs=[pl.BlockSpec((None, rows, k), lambda j: (j * tn // d, 0, 0)),
                  pl.BlockSpec((k, tn), lambda j: (0, j))],
        out_specs=pl.BlockSpec((tn, rows), lambda j: (j, 0)),
        out_shape=jax.ShapeDtypeStruct((n, rows), F32),
        compiler_params=_cparams(1),
        name="rkv_proj_tm",
    )(xs, w)


def _lora_t_kernel(*refs):
    dec_ref, a_ref, g_ref = refs[-3:]
    dec, a, g = _lora_heads(*refs[:-3])
    dec_ref[...] = dec.T
    a_ref[...] = a.T
    g_ref[...] = g


def _lora_transposed(xs, w0, w1, w2, a0, a1, a2, g1, g2):
    _, rows, d = xs.shape
    w1p, w2p = _pad_lora(w1, w2)
    a1p, a2p = _pad_lora(a1, a2)
    consts = [w1p, w2p, a1p, a2p, g1.astype(BF16), g2.astype(BF16), w0.reshape(1, d),
              a0.reshape(1, d)]
    slab = lambda s: pl.BlockSpec((None, rows, d), lambda i: (s, 0, 0))
    t_shape = jax.ShapeDtypeStruct((d, rows), F32)
    return pl.pallas_call(
        _lora_t_kernel,
        grid=(1,),
        in_specs=[slab(SLAB_W), slab(SLAB_A), slab(SLAB_G)] + [_const_spec(c.shape) for c in consts],
        out_specs=[pl.BlockSpec((d, rows), lambda i: (0, 0)), pl.BlockSpec((d, rows), lambda i: (0, 0)),
                   pl.BlockSpec((rows, d), lambda i: (0, 0))],
        out_shape=[t_shape, t_shape, jax.ShapeDtypeStruct((rows, d), F32)],
        compiler_params=_cparams(1),
        name="lora_tm",
    )(xs, xs, xs, *consts)


def _wkv_bl_kernel(r_ref, k_ref, v_ref, a_ref, d_ref, kk_p, ka_p, rk_p, lg_p, lb_p, s0_ref,
                   y_ref, so_ref, st, kk_s, b_s, k2_s, *, steps):
    n = HEAD_DIM
    tile = lambda ref, t: ref[:, t * LANES:(t + 1) * LANES]
    st[...] = jnp.swapaxes(s0_ref[...], 0, 1)
    bonus = []
    for t in range(steps):
        k = tile(k_ref, t)
        a = tile(a_ref, t)
        kk = k * kk_p[...]
        kk = kk / jnp.maximum(jnp.sqrt(jnp.sum(kk * kk, axis=0, keepdims=True)), 1e-12)
        k2 = k * (1.0 + (a - 1.0) * ka_p[...])
        kk_s[:, t * LANES:(t + 1) * LANES] = kk
        b_s[:, t * LANES:(t + 1) * LANES] = kk * a
        k2_s[:, t * LANES:(t + 1) * LANES] = k2
        bonus.append(jnp.sum(tile(r_ref, t) * k2 * rk_p[...], axis=0, keepdims=True))

    sa = jnp.zeros((n, LANES), F32)
    for i in range(n):
        sa = sa + st[i] * kk_s[i:i + 1, 0:LANES]
    for t in range(steps):
        t_next = min(t + 1, steps - 1)
        row = lambda ref, t=t: (lambda i: ref[i:i + 1, t * LANES:(t + 1) * LANES])
        v = tile(v_ref, t)
        y, sa = _wkv_step(st, sa, v, row(d_ref), row(b_s), row(k2_s), row(r_ref),
                          row(kk_s, t_next))
        ym = jnp.mean(y, axis=0, keepdims=True)
        yc = y - ym
        yv = jnp.mean(yc * yc, axis=0, keepdims=True)
        yn = yc * lax.rsqrt(yv + GN_EPS) * lg_p[...] + lb_p[...]
        y_ref[:, t * LANES:(t + 1) * LANES] = yn + bonus[t] * v
    so_ref[...] = jnp.swapaxes(st[...], 0, 1)


def _wkv_batch_lanes(rkv, a, dec, s0, k_k, k_a, r_k, lnx_g, lnx_b, *, steps):
    d, cols = a.shape
    heads, n = s0.shape[0], HEAD_DIM
    params = [jnp.broadcast_to(p.reshape(heads, n, 1), (heads, n, LANES))
              for p in (k_k, k_a, r_k, lnx_g, lnx_b)]
    chan = lambda c: pl.BlockSpec((n, cols), lambda h: (c * heads + h, 0))
    par = pl.BlockSpec((None, n, LANES), lambda h: (h, 0, 0))
    st = pl.BlockSpec((None, n, n, LANES), lambda h: (h, 0, 0, 0))
    return pl.pallas_call(
        functools.partial(_wkv_bl_kernel, steps=steps),
        grid=(heads,),
        in_specs=[chan(0), chan(1), chan(2), chan(0), chan(0)] + [par] * 5 + [st],
        out_specs=[chan(0), st],
        out_shape=[jax.ShapeDtypeStruct((d, cols), F32), jax.ShapeDtypeStruct(s0.shape, F32)],
        scratch_shapes=[pltpu.VMEM((n, n, LANES), F32)] + [pltpu.VMEM((n, cols), F32)] * 3,
        compiler_params=_cparams(1),
        name="wkv7_tm",
    )(rkv, rkv, rkv, a, dec, *params, s0)


CONV_TM_COLS = 512


def _conv_tm_kernel(u1_ref, u2_ref, bg1, bg2, dwk, dwb, lg, lb, st_ref, y_ref, sto_ref, zc,
                    *, steps, b):
    j = pl.program_id(0)
    hist = CONV_W - 1
    tc = u1_ref.shape[-1]
    for c0 in range(0, tc, LANES):
        cs = slice(c0, c0 + LANES)
        glu = [(u1_ref[t * b:(t + 1) * b, cs] + bg1[:, cs])
               * _sigmoid(u2_ref[t * b:(t + 1) * b, cs] + bg2[:, cs]) for t in range(steps)]
        window = lambda i: st_ref[i, :, cs] if i < hist else glu[i - hist]
        for t in range(steps):
            acc = jnp.zeros((b, LANES), F32)
            for tap in range(CONV_W):
                acc = acc + window(t + tap) * dwk[tap:tap + 1, cs]
            zc[j, t * b:(t + 1) * b, cs] = acc + dwb[:, cs]
        for i in range(hist):
            sto_ref[i, :, cs] = window(i + steps)

    @pl.when(j == pl.num_programs(0) - 1)
    def _():
        z = zc[...]
        m = jnp.mean(jnp.mean(z, axis=-1, keepdims=True), axis=0, keepdims=True)
        ctr = z - m
        var = jnp.mean(jnp.mean(ctr * ctr, axis=-1, keepdims=True), axis=0, keepdims=True)
        ln = ctr * lax.rsqrt(var + 1e-5)
        for jj in range(zc.shape[0]):
            cols = slice(jj * tc, (jj + 1) * tc)
            y_ref[:, cols] = _silu(ln[jj] * lg[:, cols] + lb[:, cols]).astype(BF16)


def _conv_tm(glu_in, b_glu, dw_k, dw_b, ln_g, ln_b, st0, *, steps):
    rows, _ = glu_in.shape
    hist, b, d = st0.shape
    tc = CONV_TM_COLS
    nct = d // tc
    col = lambda shape, h=0: pl.BlockSpec(shape, lambda j: (0, j + h * nct))
    state = pl.BlockSpec((hist, b, tc), lambda j: (0, 0, j))
    return pl.pallas_call(
        functools.partial(_conv_tm_kernel, steps=steps, b=b),
        grid=(nct,),
        in_specs=[col((rows, tc)), col((rows, tc), 1), col((1, tc)), col((1, tc), 1),
                  col((CONV_W, tc)), col((1, tc)), _const_spec((1, d)), _const_spec((1, d)), state],
        out_specs=[pl.BlockSpec((rows, d), lambda j: (0, 0)), state],
        out_shape=[jax.ShapeDtypeStruct((rows, d), BF16), jax.ShapeDtypeStruct(st0.shape, F32)],
        scratch_shapes=[pltpu.VMEM((nct, rows, tc), F32)],
        compiler_params=_cparams(1),
        name="conformer_conv_tm",
    )(glu_in, glu_in, b_glu.reshape(1, 2 * d), b_glu.reshape(1, 2 * d), dw_k, dw_b.reshape(1, d),
      ln_g.reshape(1, d), ln_b.reshape(1, d), st0)


def _merge_tm_kernel(yt_ref, g_ref, ga_ref, gb_ref, yb_ref, x_ref, wo_ref, m_ref, n2_ref,
                     x1_ref, h2_ref):
    f32 = lambda ref: ref[...].astype(F32)
    merged = f32(ga_ref) * (yt_ref[...].T * g_ref[...]) + f32(gb_ref) * f32(yb_ref)
    mix = jnp.dot(merged.astype(BF16), wo_ref[...], preferred_element_type=F32)
    x1 = x_ref[...] + m_ref[2] * mix
    x1_ref[...] = x1
    ms = jnp.mean(x1 * x1, axis=-1, keepdims=True)
    h2 = (x1 * lax.rsqrt(ms + 1e-6) * n2_ref[...]) * (1.0 + m_ref[4]) + m_ref[3]
    h2_ref[...] = h2.astype(BF16)


def _merge_tm(yt, g, ga, gb, yb, x, w_out, mod, norm_g, *, steps):
    rows, d = x.shape
    b = rows // steps
    row = pl.BlockSpec((b, d), lambda i: (i, 0))
    return pl.pallas_call(
        _merge_tm_kernel,
        grid=(steps,),
        in_specs=[pl.BlockSpec((d, b), lambda i: (0, i)), row, row, row, row, row,
                  _const_spec((d, d)), _const_spec((N_MOD, b, d)), _const_spec((1, d))],
        out_specs=[row, row],
        out_shape=[jax.ShapeDtypeStruct((rows, d), F32), jax.ShapeDtypeStruct((rows, d), BF16)],
        compiler_params=_cparams(1),
        name="merge_out_proj_tm",
    )(yt, g, ga, gb, yb, x, w_out, mod, norm_g.reshape(1, d))


def _up_act_tm_kernel(h_ref, wg_ref, wv_ref, kg_ref, kv_ref, bg_ref, bv_ref, sg_ref, sv_ref,
                      act_ref, stg_ref, stv_ref, *, steps, b):
    hist = FFN_CONV_W - 1
    h = h_ref[...]
    tn = wg_ref.shape[-1]
    z = []
    halves = ((wg_ref, kg_ref, bg_ref, sg_ref, stg_ref), (wv_ref, kv_ref, bv_ref, sv_ref, stv_ref))
    for w_ref, k_ref, b_ref, s_ref, st_ref in halves:
        u = jnp.dot(h, w_ref[...], preferred_element_type=F32).reshape(steps, b, tn)
        window = [s_ref[i] for i in range(hist)] + [u[t] for t in range(steps)]
        z.append(jnp.stack([b_ref[...] + sum(window[t + tap] * k_ref[tap:tap + 1, :]
                                             for tap in range(FFN_CONV_W))
                            for t in range(steps)]))
        for i in range(hist):
            st_ref[i] = window[steps + i]
    act_ref[...] = (_silu(z[0]) * z[1]).reshape(steps * b, tn).astype(BF16)


def _up_act_tm(h2, w_up, ffn_dw_k, ffn_dw_b, st0, *, steps, d_ff):
    rows, k = h2.shape
    hist, b, _ = st0.shape
    f = w_up.shape[1] // 2
    nj = f // FFN_TN
    col = lambda shape, h: pl.BlockSpec(shape, lambda j: (0, j + h * nj))
    state = lambda h: pl.BlockSpec((hist, b, FFN_TN), lambda j: (0, 0, j + h * nj))
    return pl.pallas_call(
        functools.partial(_up_act_tm_kernel, steps=steps, b=b),
        grid=(nj,),
        in_specs=[_const_spec((rows, k)), col((k, FFN_TN), 0), col((k, FFN_TN), 1),
                  col((FFN_CONV_W, FFN_TN), 0), col((FFN_CONV_W, FFN_TN), 1),
                  col((1, FFN_TN), 0), col((1, FFN_TN), 1), state(0), state(1)],
        out_specs=[pl.BlockSpec((rows, FFN_TN), lambda j: (0, j)), state(0), state(0)],
        out_shape=[jax.ShapeDtypeStruct((rows, d_ff), BF16),
                   jax.ShapeDtypeStruct((hist, b, f), F32),
                   jax.ShapeDtypeStruct((hist, b, f), F32)],
        compiler_params=_cparams(1),
        name="up_conv_act_tm",
    )(h2, w_up, w_up, ffn_dw_k, ffn_dw_k, ffn_dw_b, ffn_dw_b, st0, st0)


def _sample_layer(x, mod, shift0, wkv0, conv0, ffn0, p):
    b, t, d = x.shape
    assert b == LANES
    rows = t * b
    tm = lambda z: jnp.swapaxes(z, 0, 1)
    w_in = p["w_in"]
    x_tm = tm(x)
    mod_tm = tm(mod)
    xs, shift1 = _prep_tm(x_tm, mod_tm, p["norm1_g"], p["mu"], shift0)

    tn = PROJ_TN
    rkv = _proj_transposed(xs, w_in, tn=tn, n=3 * d)
    proj = functools.partial(_matmul, xs, w_in, tm=rows, tn=tn, slab=SLAB_H)
    glu_in = proj(col0=3 * d // tn, n=2 * d, name="glu_proj")
    ga = proj(col0=5 * d // tn, n=d, sigmoid=True, out_dtype=BF16, name="gate_a_proj")
    gb = proj(col0=6 * d // tn, n=d, sigmoid=True, out_dtype=BF16, name="gate_b_proj")
    dec, a, g = _lora_transposed(xs, p["w0"], p["w1"], p["w2"], p["a0"], p["a1"], p["a2"],
                                 p["g1"], p["g2"])

    yt, s1 = _wkv_batch_lanes(rkv, a, dec, wkv0.transpose(1, 2, 3, 0), p["k_k"], p["k_a"],
                              p["r_k"], p["lnx_g"], p["lnx_b"], steps=t)
    wkv1 = s1.transpose(3, 0, 1, 2)

    yb, conv1 = _conv_tm(glu_in, p["b_glu"], p["dw_k"], p["dw_b"], p["ln_conv_g"],
                         p["ln_conv_b"], tm(conv0), steps=t)
    x1, h2 = _merge_tm(yt, g, ga, gb, yb, x_tm.reshape(rows, d), p["w_out"].astype(BF16), mod_tm,
                       p["norm2_g"], steps=t)

    d_ff = p["w_down"].shape[0]
    f_pad = -(-d_ff // FFN_TN) * FFN_TN
    act, ffn_g, ffn_v = _up_act_tm(
        h2, _pad_halves(p["w_up"], f_pad).astype(BF16), _pad_halves(p["ffn_dw_k"], f_pad),
        _pad_halves(p["ffn_dw_b"], f_pad).reshape(1, 2 * f_pad), _pad_halves(tm(ffn0), f_pad),
        steps=t, d_ff=d_ff)
    ffn1 = tm(jnp.concatenate([ffn_g[..., :d_ff], ffn_v[..., :d_ff]], axis=-1))
    x2 = _matmul(act[None], p["w_down"], tm=rows, tn=DOWN_TN, res=x1, gate=jnp.tile(mod_tm[5], (t, 1)),
                 name="down_proj")
    y = _final_norm(x2, p["normf_g"], tm=min(NORM_TM, rows))
    return tm(y.reshape(t, b, d)), shift1, wkv1, tm(conv1), ffn1


def _head_minor(w):
    lead = w.shape[:-1]
    return w.reshape(*lead, -1, HEAD_DIM).swapaxes(-1, -2).reshape(*lead, w.shape[-1])


def _prompt_layer(x, mod, shift0, wkv0, conv0, ffn0, p, *, tm_seq, tq_proj, tq_tok, tm_mm, tt,
                  tq_ffn):
    b, t, d = x.shape
    rows = b * t
    groups = b // GROUP
    w_in = p["w_in"]

    xs, shift1 = _prep(x, mod, p["norm1_g"], p["mu"], shift0, tm=tm_seq)
    xs2 = xs.reshape(7, rows, d)

    w_hm = jnp.concatenate([_head_minor(w_in[:, i * d:(i + 1) * d]) for i in (0, 1, 2, 5)],
                           axis=1).astype(BF16)
    rkv = _proj_head_lanes(xs, w_hm, _Tiling(groups, t, 1, tq_proj), tn=d, n=3 * d)
    tn = PROJ_TN
    ga = _matmul(xs2, w_hm, tm=tm_mm, tn=tn, slab=SLAB_H, col0=3 * d // tn, n=d, sigmoid=True,
                 out_dtype=BF16, name="gate_a_proj")
    glu_in = _matmul(xs2, w_in, tm=tm_mm, tn=tn, slab=SLAB_H, col0=3 * d // tn, n=2 * d,
                     name="glu_proj")
    gb = _matmul(xs2, w_in, tm=tm_mm, tn=tn, slab=SLAB_H, col0=6 * d // tn, n=d, sigmoid=True,
                 out_dtype=BF16, name="gate_b_proj")

    tl = _Tiling(groups, t, 1, tq_tok)
    dec, a, g = _lora(xs, p["w0"], p["w1"], p["w2"], p["a0"], p["a1"], p["a2"], p["g1"], p["g2"], tl)
    yw, wkv1 = _wkv(rkv, a, dec, wkv0, p["k_k"], p["k_a"], p["r_k"], p["lnx_g"], p["lnx_b"], tt=tt)
    yb, conv1 = _conv(glu_in.reshape(b, t, 2 * d), p["b_glu"], p["dw_k"], p["dw_b"],
                      p["ln_conv_g"], p["ln_conv_b"], conv0, tm=tm_seq)
    x1, h2 = _merge(yw, g, ga.reshape(b, t, d), gb.reshape(b, t, d), yb, x, p["w_out"], mod,
                    p["norm2_g"], tl)

    d_ff = p["w_down"].shape[0]
    f_pad = -(-d_ff // FFN_TN) * FFN_TN
    act, ffn_g, ffn_v = _up_act(
        h2.reshape(rows, d), _pad_halves(p["w_up"], f_pad).astype(BF16),
        _pad_halves(p["ffn_dw_k"], f_pad), _pad_halves(p["ffn_dw_b"], f_pad).reshape(1, 2 * f_pad),
        _pad_halves(ffn0, f_pad), tq=tq_ffn, t=t, d_ff=d_ff)
    ffn1 = jnp.concatenate([ffn_g[..., :d_ff], ffn_v[..., :d_ff]], axis=-1)
    x2 = _matmul(act[None], p["w_down"], tm=min(tm_mm, DOWN_TM), tn=DOWN_TN, res=x1.reshape(rows, d),
                 gate=mod[:, 5][:, None, :], rows_per_gate=t, name="down_proj")
    y = _final_norm(x2, p["normf_g"], tm=min(NORM_TM, rows))
    return y.reshape(b, t, d), shift1.reshape(b, d), wkv1, conv1, ffn1


def kernel(x_prompt, x_sample, state_shift, state_wkv, state_conv, state_ffn, c_prompt, c_sample,
           norm1_g, norm2_g, normf_g, w_ada, b_ada, mu, w_in, b_glu, w0, w1, w2, a0, a1, a2, g1, g2,
           k_k, k_a, r_k, lnx_g, lnx_b, dw_k, dw_b, ln_conv_g, ln_conv_b, w_out, w_up, ffn_dw_k,
           ffn_dw_b, w_down):
    p = dict(norm1_g=norm1_g, norm2_g=norm2_g, normf_g=normf_g, mu=mu, w_in=w_in, b_glu=b_glu,
             w0=w0, w1=w1, w2=w2, a0=a0, a1=a1, a2=a2, g1=g1, g2=g2, k_k=k_k, k_a=k_a, r_k=r_k,
             lnx_g=lnx_g, lnx_b=lnx_b, dw_k=dw_k, dw_b=dw_b, ln_conv_g=ln_conv_g,
             ln_conv_b=ln_conv_b, w_out=w_out, w_up=w_up, ffn_dw_k=ffn_dw_k, ffn_dw_b=ffn_dw_b,
             w_down=w_down)
    bp, _, d = x_prompt.shape
    bs = x_sample.shape[0]
    f2 = w_up.shape[1]

    c_all = jnp.concatenate([c_prompt, c_sample], axis=0)
    c_rows = -(-c_all.shape[0] // SUBLANES) * SUBLANES
    c_all = jnp.pad(c_all, ((0, c_rows - c_all.shape[0]), (0, 0)))
    mod = _modulation(c_all, w_ada, b_ada).reshape(c_rows, N_MOD, d)
    mod_p, mod_s = mod[:bp], mod[bp:bp + bs]

    y_p, shift_p, wkv_p, conv_p, ffn_p = _prompt_layer(
        x_prompt, mod_p, jnp.zeros((bp, d), F32),
        jnp.zeros((bp, d // HEAD_DIM, HEAD_DIM, HEAD_DIM), F32),
        jnp.zeros((bp, CONV_W - 1, d), F32), jnp.zeros((bp, FFN_CONV_W - 1, f2), F32), p,
        **PROMPT_TILES)

    y_s, shift_s, wkv_s, conv_s, ffn_s = _sample_layer(
        x_sample, mod_s, state_shift, state_wkv, state_conv, state_ffn, p)
    return (y_p, y_s, shift_p, wkv_p, conv_p, ffn_p, shift_s, wkv_s, conv_s, ffn_s)
```

```python
import functools
import math
from typing import NamedTuple

import jax
import jax.numpy as jnp
from jax import lax
from jax.experimental import pallas as pl
from jax.experimental.pallas import tpu as pltpu

F32 = jnp.float32
BF16 = jnp.bfloat16

HEAD_DIM = 64
CONV_W = 31
FFN_CONV_W = 3
N_MOD = 6
GN_EPS = HEAD_DIM * 1e-5
DECAY_SCALE = math.exp(-0.5)
SUBLANES = 8
LANES = 128
MXU_COLS = 256
LORA_PAD = 128
GROUP = 4
V7X_VMEM_LIMIT = 56 * 1024 * 1024

PROJ_TN = 1024
MOD_TN = 2048
DOWN_TM, DOWN_TN = 512, 512
NORM_TM = 256
PROMPT_TILES = dict(tm_seq=256, tq_proj=256, tq_tok=64, tm_mm=1024, tt=64, tq_ffn=1024)

SLAB_R, SLAB_K, SLAB_V, SLAB_H, SLAB_W, SLAB_A, SLAB_G = range(7)
MU_SLABS = (SLAB_R, SLAB_W, SLAB_K, SLAB_V, SLAB_A, SLAB_G)


class _Tiling(NamedTuple):
    groups: int
    t: int
    gs: int
    tq: int

    @property
    def grid(self):
        return (self.groups // self.gs, self.t // self.tq)

    @property
    def seqs(self):
        return self.gs * GROUP

    def plain(self, cols, col=0):
        return pl.BlockSpec((self.seqs, self.tq, cols), lambda ig, it: (ig, it, col))

    def head_lanes(self, blk=0):
        return pl.BlockSpec((self.gs, HEAD_DIM, self.tq, LANES), lambda ig, it: (ig, blk, it, 0))

    def per_seq(self, rows, cols):
        return pl.BlockSpec((self.seqs, rows, cols), lambda ig, it: (ig, 0, 0))


def _cparams(n_axes):
    return pltpu.CompilerParams(dimension_semantics=("arbitrary",) * n_axes,
                                vmem_limit_bytes=V7X_VMEM_LIMIT)


def _const_spec(shape):
    return pl.BlockSpec(shape, lambda *_: (0,) * len(shape), pipeline_mode=pl.Buffered(1))


def _sigmoid(x):
    return 1.0 / (1.0 + jnp.exp(-x))


def _silu(x):
    return x * _sigmoid(x)


def _swap_lane_groups(m, lane):
    half = lane < 2 * (LANES // GROUP)
    odd = (lane // (LANES // GROUP)) % 2 == 1
    n = [None] * 4
    for s in range(2):
        n[s] = jnp.where(half, m[s], pltpu.roll(m[s + 2], LANES // 2, axis=1))
        n[s + 2] = jnp.where(half, pltpu.roll(m[s], LANES // 2, axis=1), m[s + 2])
    out = [None] * 4
    for p in (0, 2):
        a, b = n[p], n[p + 1]
        out[p] = jnp.where(odd, pltpu.roll(b, LANES // GROUP, axis=1), a)
        out[p + 1] = jnp.where(odd, b, pltpu.roll(a, LANES - LANES // GROUP, axis=1))
    return out


def _store_head_lanes(val, out_ref, gs, tq, q0=0):
    lane = lax.broadcasted_iota(jnp.int32, (tq, LANES), 1)
    for g in range(gs):
        for q in range(val.shape[1] // LANES):
            src = [val[(g * GROUP + s) * tq:(g * GROUP + s + 1) * tq, q * LANES:(q + 1) * LANES]
                   for s in range(GROUP)]
            out = _swap_lane_groups(src, lane)
            for j in range(GROUP):
                out_ref[g, GROUP * (q0 + q) + j] = out[j]


def _rows2d(ref):
    v = ref[...]
    return v.reshape(-1, v.shape[-1])


def _mod_kernel(c_ref, w_ref, b_ref, o_ref):
    s = _silu(c_ref[...]).astype(BF16)
    o_ref[...] = jnp.dot(s, w_ref[...].astype(BF16), preferred_element_type=F32) + b_ref[...]


def _modulation(c, w_ada, b_ada):
    rows, d = c.shape
    n = w_ada.shape[1]
    tn = MOD_TN
    return pl.pallas_call(
        _mod_kernel,
        grid=(n // tn,),
        in_specs=[pl.BlockSpec((rows, d), lambda j: (0, 0)),
                  pl.BlockSpec((d, tn), lambda j: (0, j)),
                  pl.BlockSpec((1, tn), lambda j: (0, j))],
        out_specs=pl.BlockSpec((rows, tn), lambda j: (0, j)),
        out_shape=jax.ShapeDtypeStruct((rows, n), F32),
        compiler_params=_cparams(1),
        name="modulation",
    )(c, w_ada, b_ada.reshape(1, n))


def _prep_kernel(x_ref, m_ref, g_ref, mu_ref, s0_ref, xs_ref, so_ref, hbuf, *, tm):
    x = x_ref[...]
    m = m_ref[...]
    ms = jnp.mean(x * x, axis=-1, keepdims=True)
    h = (x * lax.rsqrt(ms + 1e-6) * g_ref[...]) * (1.0 + m[1:2, :]) + m[0:1, :]
    hbuf[SUBLANES:SUBLANES + tm, :] = h

    @pl.when(pl.program_id(1) == 0)
    def _():
        hbuf[SUBLANES - 1:SUBLANES, :] = s0_ref[...]

    dx = hbuf[SUBLANES - 1:SUBLANES - 1 + tm, :] - h
    for mi, slab in enumerate(MU_SLABS):
        xs_ref[slab] = (h + dx * mu_ref[mi:mi + 1, :]).astype(BF16)
    xs_ref[SLAB_H] = h.astype(BF16)
    last = h[tm - 1:tm, :]
    hbuf[SUBLANES - 1:SUBLANES, :] = last
    so_ref[...] = last


def _prep(x, mod, norm_g, mu, shift0, *, tm):
    b, t, d = x.shape
    return pl.pallas_call(
        functools.partial(_prep_kernel, tm=tm),
        grid=(b, t // tm),
        in_specs=[pl.BlockSpec((None, tm, d), lambda i, j: (i, j, 0)),
                  pl.BlockSpec((None, N_MOD, d), lambda i, j: (i, 0, 0)),
                  _const_spec((1, d)),
                  _const_spec((N_MOD, d)),
                  pl.BlockSpec((None, 1, d), lambda i, j: (i, 0, 0))],
        out_specs=[pl.BlockSpec((7, None, tm, d), lambda i, j: (0, i, j, 0)),
                   pl.BlockSpec((None, 1, d), lambda i, j: (i, 0, 0))],
        out_shape=[jax.ShapeDtypeStruct((7, b, t, d), BF16),
                   jax.ShapeDtypeStruct((b, 1, d), F32)],
        scratch_shapes=[pltpu.VMEM((SUBLANES + tm, d), F32)],
        compiler_params=_cparams(2),
        name="prep",
    )(x, mod, norm_g.reshape(1, d), mu, shift0.reshape(b, 1, d))


def _gate_blocks(gate, rows, tm, rows_per_gate):
    if gate.ndim == 2:
        return gate.reshape(rows // tm, tm, gate.shape[-1]), 1
    assert rows_per_gate % tm == 0, (rows_per_gate, tm)
    return gate, rows_per_gate // tm


def _mm_kernel(*refs, gated, sigmoid):
    if gated:
        a_ref, w_ref, res_ref, gate_ref, o_ref, wb = refs
    else:
        a_ref, w_ref, o_ref, wb = refs

    @pl.when(pl.program_id(1) == 0)
    def _():
        wb[...] = w_ref[...].astype(BF16)

    acc = jnp.dot(a_ref[...], wb[...], preferred_element_type=F32)
    if gated:
        acc = res_ref[...] + gate_ref[...] * acc
    if sigmoid:
        acc = _sigmoid(acc)
    o_ref[...] = acc.astype(o_ref.dtype)


def _matmul(a, w, *, tm, tn, slab=0, col0=0, n=None, res=None, gate=None, rows_per_gate=None,
            sigmoid=False, out_dtype=F32, name):
    _, rows, k = a.shape
    n = w.shape[1] if n is None else n
    gated = res is not None
    in_specs = [pl.BlockSpec((None, tm, k), lambda j, i: (slab, i, 0)),
                pl.BlockSpec((k, tn), lambda j, i: (0, j + col0))]
    args = [a, w]
    if gated:
        gate, tiles_per_gate = _gate_blocks(gate, rows, tm, rows_per_gate)
        gr = gate.shape[1]
        in_specs += [pl.BlockSpec((tm, tn), lambda j, i: (i, j)),
                     pl.BlockSpec((None, gr, tn), lambda j, i: (i // tiles_per_gate, 0, j))]
        args += [res, gate]
    return pl.pallas_call(
        functools.partial(_mm_kernel, gated=gated, sigmoid=sigmoid),
        grid=(pl.cdiv(n, tn), rows // tm),
        in_specs=in_specs,
        out_specs=pl.BlockSpec((tm, tn), lambda j, i: (i, j)),
        out_shape=jax.ShapeDtypeStruct((rows, n), out_dtype),
        scratch_shapes=[pltpu.VMEM((k, tn), BF16)],
        compiler_params=_cparams(2),
        name=name,
    )(*args)


def _proj_hl_kernel(a_ref, w_ref, o_ref, *, gs, tq):
    for c in range(0, w_ref.shape[1], MXU_COLS):
        acc = jnp.dot(_rows2d(a_ref), w_ref[:, c:c + MXU_COLS], preferred_element_type=F32)
        _store_head_lanes(acc, o_ref, gs, tq, q0=c // LANES)


def _proj_head_lanes(xs, w, tl, *, tn, n):
    k = w.shape[0]
    d = n // 3
    slab = lambda j: j * tn // d
    a_spec = pl.BlockSpec((None, tl.seqs, tl.tq, k), lambda j, ig, it: (slab(j), ig, it, 0))
    nq = tn // (LANES // GROUP)
    return pl.pallas_call(
        functools.partial(_proj_hl_kernel, gs=tl.gs, tq=tl.tq),
        grid=(n // tn,) + tl.grid,
        in_specs=[a_spec, pl.BlockSpec((k, tn), lambda j, ig, it: (0, j))],
        out_specs=pl.BlockSpec((tl.gs, nq, tl.tq, LANES), lambda j, ig, it: (ig, j, it, 0)),
        out_shape=jax.ShapeDtypeStruct((tl.groups, n // (LANES // GROUP), tl.t, LANES), F32),
        compiler_params=_cparams(3),
        name="rkv_proj",
    )(xs, w)


def _lora_heads(xw_ref, xa_ref, xg_ref, w1, w2, a1, a2, g1, g2, w0, a0):
    dot = functools.partial(jnp.dot, preferred_element_type=F32)
    hw = jnp.tanh(dot(_rows2d(xw_ref), w1[...])).astype(BF16)
    w_pre = w0[...] + dot(hw, w2[...])
    dec = jnp.exp(-DECAY_SCALE * _sigmoid(w_pre))
    ha = dot(_rows2d(xa_ref), a1[...]).astype(BF16)
    a = _sigmoid(a0[...] + dot(ha, a2[...]))
    hg = _sigmoid(dot(_rows2d(xg_ref), g1[...])).astype(BF16)
    return dec, a, dot(hg, g2[...])


def _lora_kernel(*refs, gs, tq):
    dec_ref, a_ref, g_ref = refs[-3:]
    dec, a, g = _lora_heads(*refs[:-3])
    _store_head_lanes(dec, dec_ref, gs, tq)
    _store_head_lanes(a, a_ref, gs, tq)
    g_ref[...] = g.astype(BF16).reshape(g_ref.shape)


def _pad_lora(w_a, w_b):
    r = w_a.shape[1]
    return (jnp.pad(w_a, ((0, 0), (0, LORA_PAD - r))).astype(BF16),
            jnp.pad(w_b, ((0, LORA_PAD - r), (0, 0))).astype(BF16))


def _lora(xs, w0, w1, w2, a0, a1, a2, g1, g2, tl):
    d = w0.shape[0]
    w1p, w2p = _pad_lora(w1, _head_minor(w2))
    a1p, a2p = _pad_lora(a1, _head_minor(a2))
    g1b, g2b = g1.astype(BF16), _head_minor(g2).astype(BF16)
    slab = lambda s: pl.BlockSpec((None, tl.seqs, tl.tq, d), lambda ig, it: (s, ig, it, 0))
    g_shape = (tl.groups * GROUP, tl.t, d)
    hl_shape = jax.ShapeDtypeStruct((tl.groups, HEAD_DIM, tl.t, LANES), F32)
    consts = [w1p, w2p, a1p, a2p, g1b, g2b, _head_minor(w0).reshape(1, d),
              _head_minor(a0).reshape(1, d)]
    return pl.pallas_call(
        functools.partial(_lora_kernel, gs=tl.gs, tq=tl.tq),
        grid=tl.grid,
        in_specs=[slab(SLAB_W), slab(SLAB_A), slab(SLAB_G)] + [_const_spec(c.shape) for c in consts],
        out_specs=[tl.head_lanes(), tl.head_lanes(), tl.plain(d)],
        out_shape=[hl_shape, hl_shape, jax.ShapeDtypeStruct(g_shape, BF16)],
        compiler_params=_cparams(2),
        name="lora",
    )(xs, xs, xs, *consts)


def _wkv_step(st, sa, v, b_row, k2_row, r_row, kk_next_row):
    y = jnp.zeros((HEAD_DIM, LANES), F32)
    sa_next = jnp.zeros((HEAD_DIM, LANES), F32)
    for i in range(HEAD_DIM):
        si = st[i] - sa * b_row(i) + v * k2_row(i)
        st[i] = si
        y = y + si * r_row(i)
        sa_next = sa_next + si * kk_next_row(i)
    return y, sa_next


def _wkv_kernel(r_ref, k_ref, v_ref, a_ref, d_ref, kk_p, ka_p, rk_p, lg_p, lb_p, s0_ref,
                y_ref, so_ref, st, kk_s, b_s, k2_s, r_s, pe_s, bon_s, v_t, y_t, *, nb):
    n = HEAD_DIM
    tt = nb * SUBLANES
    bc = lambda p: p[...][:, None, None, :]

    @pl.when(pl.program_id(1) == 0)
    def _():
        s = s0_ref[...].reshape(LANES, n, n)
        st[...] = jnp.swapaxes(jnp.swapaxes(jnp.swapaxes(s, 0, 1), 1, 2), 0, 1)

    x = d_ref[...].reshape(n * nb, SUBLANES, LANES)
    step = lax.broadcasted_iota(jnp.int32, x.shape, 1)
    for sh in (1, 2, 4):
        x = x * jnp.where(step >= sh, pltpu.roll(x, sh, axis=1), 1.0)
    x = x.reshape(n, nb, SUBLANES, LANES)
    step = step.reshape(x.shape)
    run = jnp.ones((n, 1, LANES), F32)
    incl, excl = [], []
    for tb in range(nb):
        blk = x[:, tb] * run
        incl.append(blk)
        excl.append(jnp.where(step[:, tb] >= 1, pltpu.roll(blk, 1, axis=1), run))
        run = blk[:, SUBLANES - 1:SUBLANES, :]
    p_t = jnp.stack(incl, axis=1)
    inv_p = 1.0 / p_t

    k = k_ref[...]
    a = a_ref[...]
    r = r_ref[...]
    kk = k * bc(kk_p)
    kk = kk * (1.0 / jnp.maximum(jnp.sqrt(jnp.sum(kk * kk, axis=0, keepdims=True)), 1e-12))
    k2 = k * (1.0 + (a - 1.0) * bc(ka_p))
    kk_s[...] = kk * jnp.stack(excl, axis=1)
    b_s[...] = kk * a * inv_p
    k2_s[...] = k2 * inv_p
    r_s[...] = r * p_t
    pe_s[...] = p_t[:, nb - 1, SUBLANES - 1, :]
    bon_s[...] = jnp.sum(r * k2 * bc(rk_p), axis=0)
    v_t[...] = jnp.swapaxes(v_ref[...], 0, 1)

    sa0 = jnp.zeros((n, LANES), F32)
    for i in range(n):
        sa0 = sa0 + st[i] * kk_s[i, 0, 0:1, :]

    def block(tb, sa):
        tb_next = jnp.minimum(tb + 1, nb - 1)
        for j in range(SUBLANES):
            t = tb * SUBLANES + j
            row = lambda ref: (lambda i: ref[i, tb, j:j + 1, :])
            if j + 1 < SUBLANES:
                kk_next = lambda i: kk_s[i, tb, j + 1:j + 2, :]
            else:
                kk_next = lambda i: kk_s[i, tb_next, 0:1, :]
            y_t[t], sa = _wkv_step(st, sa, v_t[t], row(b_s), row(k2_s), row(r_s), kk_next)
        return sa

    lax.fori_loop(0, nb, block, sa0)
    for i in range(n):
        st[i] = st[i] * pe_s[i:i + 1, :]

    y = jnp.swapaxes(y_t[...], 0, 1)
    ym = jnp.mean(y, axis=0, keepdims=True)
    yc = y - ym
    yv = jnp.mean(yc * yc, axis=0, keepdims=True)
    bcast = lambda p: p[...][:, None, :]
    yn = yc * lax.rsqrt(yv + GN_EPS) * bcast(lg_p) + bcast(lb_p)
    y_ref[...] = yn + bon_s[...].reshape(1, tt, LANES) * v_ref[...]

    @pl.when(pl.program_id(1) == pl.num_programs(1) - 1)
    def _():
        s = jnp.swapaxes(jnp.swapaxes(jnp.swapaxes(st[...], 0, 1), 1, 2), 0, 1)
        so_ref[...] = s.reshape(so_ref.shape)


def _head_lanes(p):
    hn = p.reshape(-1, HEAD_DIM).T
    return jnp.tile(hn, (1, GROUP))


def _wkv(rkv, a, dec, s0, k_k, k_a, r_k, lnx_g, lnx_b, *, tt):
    g, _, t, lanes = a.shape
    n = HEAD_DIM
    heads = s0.shape[1]
    nb = tt // SUBLANES
    by8 = lambda z: z.reshape(z.shape[0], z.shape[1], t // SUBLANES, SUBLANES, lanes)
    params = [_head_lanes(p) for p in (k_k, k_a, r_k.reshape(-1), lnx_g, lnx_b)]
    rows = lambda blk: pl.BlockSpec((None, n, nb, SUBLANES, lanes), lambda i, j: (i, blk, j, 0, 0))
    tile = lambda blk: pl.BlockSpec((None, n, tt, lanes), lambda i, j: (i, blk, j, 0))
    st = pl.BlockSpec((GROUP, heads, n, n), lambda i, j: (i, 0, 0, 0))
    return pl.pallas_call(
        functools.partial(_wkv_kernel, nb=nb),
        grid=(g, t // tt),
        in_specs=[rows(0), rows(1), tile(2), rows(0), rows(0)] + [_const_spec((n, lanes))] * 5 + [st],
        out_specs=[tile(0), st],
        out_shape=[jax.ShapeDtypeStruct((g, n, t, lanes), F32),
                   jax.ShapeDtypeStruct(s0.shape, F32)],
        scratch_shapes=[pltpu.VMEM((n, n, lanes), F32)]
        + [pltpu.VMEM((n, nb, SUBLANES, lanes), F32)] * 4
        + [pltpu.VMEM((n, lanes), F32), pltpu.VMEM((nb, SUBLANES, lanes), F32)]
        + [pltpu.VMEM((tt, n, lanes), F32)] * 2,
        compiler_params=_cparams(2),
        name="wkv7",
    )(by8(rkv), by8(rkv), rkv, by8(a), by8(dec), *params, s0)


CONV_PAD = 32
CONV_ROWS = 32


def _conv_kernel(u1_ref, u2_ref, bg1, bg2, dwk, dwb, lg, lb, st0_ref, y_ref, sto_ref, zbuf, cbuf,
                 zs, *, tm):
    hist = CONV_W - 1
    lo = CONV_PAD - hist
    d = u1_ref.shape[-1]
    rc = min(CONV_ROWS, tm)
    glu = (u1_ref[...] + bg1[...]) * _sigmoid(u2_ref[...] + bg2[...])
    zbuf[CONV_PAD:CONV_PAD + tm, :] = glu

    @pl.when(pl.program_id(1) == 0)
    def _():
        zbuf[lo:CONV_PAD, :] = st0_ref[...]

    for b in range(SUBLANES):
        span = tm + SUBLANES * ((hist - b) // SUBLANES)
        zs[b, 0:span, :] = zbuf[lo + b:lo + b + span, :]
    for r0 in range(0, tm, rc):
        for c0 in range(0, d, LANES):
            acc = jnp.zeros((rc, LANES), F32)
            for j in range(CONV_W):
                a8 = SUBLANES * (j // SUBLANES)
                acc = acc + (zs[j % SUBLANES, a8 + r0:a8 + r0 + rc, c0:c0 + LANES]
                             * dwk[j:j + 1, c0:c0 + LANES])
            cbuf[r0:r0 + rc, c0:c0 + LANES] = acc
    zc = cbuf[...] + dwb[...]
    m = jnp.mean(zc, axis=-1, keepdims=True)
    ctr = zc - m
    var = jnp.mean(ctr * ctr, axis=-1, keepdims=True)
    ln = ctr * lax.rsqrt(var + 1e-5) * lg[...] + lb[...]
    y_ref[...] = _silu(ln).astype(BF16)
    carry = zbuf[lo + tm:CONV_PAD + tm, :]
    zbuf[lo:CONV_PAD, :] = carry
    sto_ref[...] = carry


def _conv(glu_in, b_glu, dw_k, dw_b, ln_g, ln_b, st0, *, tm):
    b, t, _ = glu_in.shape
    d = dw_k.shape[1]
    hist = CONV_W - 1
    vec = _const_spec((1, d))
    return pl.pallas_call(
        functools.partial(_conv_kernel, tm=tm),
        grid=(b, t // tm),
        in_specs=[pl.BlockSpec((None, tm, d), lambda i, j: (i, j, 0)),
                  pl.BlockSpec((None, tm, d), lambda i, j: (i, j, 1)),
                  pl.BlockSpec((1, d), lambda i, j: (0, 0)),
                  pl.BlockSpec((1, d), lambda i, j: (0, 1)),
                  _const_spec((CONV_W, d)),
                  vec, vec, vec,
                  pl.BlockSpec((None, hist, d), lambda i, j: (i, 0, 0))],
        out_specs=[pl.BlockSpec((None, tm, d), lambda i, j: (i, j, 0)),
                   pl.BlockSpec((None, hist, d), lambda i, j: (i, 0, 0))],
        out_shape=[jax.ShapeDtypeStruct((b, t, d), BF16),
                   jax.ShapeDtypeStruct((b, hist, d), F32)],
        scratch_shapes=[pltpu.VMEM((CONV_PAD + tm, d), F32), pltpu.VMEM((tm, d), F32),
                        pltpu.VMEM((SUBLANES, tm + CONV_PAD - SUBLANES, d), F32)],
        compiler_params=_cparams(2),
        name="conformer_conv",
    )(glu_in, glu_in, b_glu.reshape(1, 2 * d), b_glu.reshape(1, 2 * d), dw_k, dw_b.reshape(1, d),
      ln_g.reshape(1, d), ln_b.reshape(1, d), st0)


def _merge_kernel(yw_ref, g_ref, ga_ref, gb_ref, yb_ref, x_ref, woa_ref, wob_ref, m_ref, n2_ref,
                  x1_ref, h2_ref, ybuf, *, gs, tq):
    seqs = gs * GROUP
    d = ybuf.shape[-1]
    lane = lax.broadcasted_iota(jnp.int32, (tq, LANES), 1)
    for g in range(gs):
        for q in range(d // LANES):
            out = _swap_lane_groups([yw_ref[g, GROUP * q + j] for j in range(GROUP)], lane)
            for s in range(GROUP):
                ybuf[g * GROUP + s, :, q * LANES:(q + 1) * LANES] = out[s]
    seq3 = lambda ref: ref[...].astype(F32).reshape(seqs, tq, d)
    m = m_ref[...]
    dot = lambda z, w_ref: jnp.dot(z.reshape(seqs * tq, d).astype(BF16), w_ref[...],
                                   preferred_element_type=F32)
    branch_a = seq3(ga_ref) * (ybuf[...] * seq3(g_ref))
    branch_b = seq3(gb_ref) * seq3(yb_ref)
    mix = (dot(branch_a, woa_ref) + dot(branch_b, wob_ref)).reshape(seqs, tq, d)
    x1 = seq3(x_ref) + m[:, 2:3, :] * mix
    x1_ref[...] = x1.reshape(x1_ref.shape)
    ms = jnp.mean(x1 * x1, axis=-1, keepdims=True)
    h2 = (x1 * lax.rsqrt(ms + 1e-6) * n2_ref[...]) * (1.0 + m[:, 4:5, :]) + m[:, 3:4, :]
    h2_ref[...] = h2.reshape(seqs * tq, d).astype(BF16).reshape(h2_ref.shape)


def _merge(yw, g, ga, gb, yb, x, w_out, mod, norm_g, tl):
    d = x.shape[-1]
    plain = tl.plain(d)
    w_out_b = w_out.astype(BF16)
    w_out_a = _head_minor(w_out.T).T.astype(BF16)
    return pl.pallas_call(
        functools.partial(_merge_kernel, gs=tl.gs, tq=tl.tq),
        grid=tl.grid,
        in_specs=[tl.head_lanes(), plain, plain, plain, plain, plain, _const_spec((d, d)),
                  _const_spec((d, d)), tl.per_seq(N_MOD, d), _const_spec((1, d))],
        out_specs=[plain, plain],
        out_shape=[jax.ShapeDtypeStruct(x.shape, F32), jax.ShapeDtypeStruct(x.shape, BF16)],
        scratch_shapes=[pltpu.VMEM((tl.seqs, tl.tq, d), F32)],
        compiler_params=_cparams(2),
        name="merge_out_proj",
    )(yw, g, ga, gb, yb, x, w_out_a, w_out_b, mod, norm_g.reshape(1, d))


FFN_TN = 512


def _pad_halves(w, width):
    f = w.shape[-1] // 2
    pad = [(0, 0)] * (w.ndim - 1) + [(0, width - f)]
    return jnp.concatenate([jnp.pad(w[..., :f], pad), jnp.pad(w[..., f:], pad)], axis=-1)


def _up_act_kernel(h_ref, wg_ref, wv_ref, kg_ref, kv_ref, bg_ref, bv_ref, sg_ref, sv_ref,
                   act_ref, stg_ref, stv_ref, carry, *, tq, tiles_per_seq):
    first = pl.program_id(1) % tiles_per_seq == 0
    tn = wg_ref.shape[-1]
    row = lax.broadcasted_iota(jnp.int32, (tq, tn), 0)
    z = []
    halves = ((wg_ref, kg_ref, bg_ref, stg_ref), (wv_ref, kv_ref, bv_ref, stv_ref))

    @pl.when(first)
    def _():
        carry[0] = sg_ref[...]
        carry[1] = sv_ref[...]

    for half, (w_ref, k_ref, b_ref, st_ref) in enumerate(halves):
        u = jnp.dot(h_ref[...], w_ref[...], preferred_element_type=F32)
        prev = carry[half]
        p0, p1 = prev[0:1, :], prev[1:2, :]
        u1 = jnp.where(row == 0, p1, pltpu.roll(u, 1, axis=0))
        u2 = jnp.where(row == 0, p0, jnp.where(row == 1, p1, pltpu.roll(u, 2, axis=0)))
        z.append(b_ref[...] + u2 * k_ref[0:1, :] + u1 * k_ref[1:2, :] + u * k_ref[2:3, :])
        last = u[tq - (FFN_CONV_W - 1):, :]
        carry[half] = last
        st_ref[...] = last
    act_ref[...] = (_silu(z[0]) * z[1]).astype(BF16)


def _up_act(h2, w_up, ffn_dw_k, ffn_dw_b, st0, *, tq, t, d_ff):
    rows, k = h2.shape
    b = st0.shape[0]
    f = w_up.shape[1] // 2
    nj = f // FFN_TN
    hist = FFN_CONV_W - 1
    tiles_per_seq = t // tq
    col = lambda shape, h: pl.BlockSpec(shape, lambda j, i: (0, j + h * nj))
    state = lambda h: pl.BlockSpec((None, hist, FFN_TN),
                                   lambda j, i: (i // tiles_per_seq, 0, j + h * nj))
    return pl.pallas_call(
        functools.partial(_up_act_kernel, tq=tq, tiles_per_seq=tiles_per_seq),
        grid=(nj, rows // tq),
        in_specs=[pl.BlockSpec((tq, k), lambda j, i: (i, 0)),
                  col((k, FFN_TN), 0), col((k, FFN_TN), 1),
                  col((FFN_CONV_W, FFN_TN), 0), col((FFN_CONV_W, FFN_TN), 1),
                  col((1, FFN_TN), 0), col((1, FFN_TN), 1),
                  state(0), state(1)],
        out_specs=[pl.BlockSpec((tq, FFN_TN), lambda j, i: (i, j)), state(0), state(0)],
        out_shape=[jax.ShapeDtypeStruct((rows, d_ff), BF16),
                   jax.ShapeDtypeStruct((b, hist, f), F32),
                   jax.ShapeDtypeStruct((b, hist, f), F32)],
        scratch_shapes=[pltpu.VMEM((2, hist, FFN_TN), F32)],
        compiler_params=_cparams(2),
        name="up_conv_act",
    )(h2, w_up, w_up, ffn_dw_k, ffn_dw_k, ffn_dw_b, ffn_dw_b, st0, st0)


def _norm_kernel(x_ref, g_ref, o_ref):
    x = x_ref[...]
    ms = jnp.mean(x * x, axis=-1, keepdims=True)
    o_ref[...] = x * lax.rsqrt(ms + 1e-6) * g_ref[...]


def _final_norm(x, g, *, tm):
    rows, d = x.shape
    row = pl.BlockSpec((tm, d), lambda i: (i, 0))
    return pl.pallas_call(
        _norm_kernel,
        grid=(rows // tm,),
        in_specs=[row, _const_spec((1, d))],
        out_specs=row,
        out_shape=jax.ShapeDtypeStruct((rows, d), F32),
        compiler_params=_cparams(1),
        name="final_norm",
    )(x, g.reshape(1, d))


def _prep_tm_kernel(x_ref, m_ref, g_ref, mu_ref, s0_ref, xs_ref, so_ref, hprev):
    @pl.when(pl.program_id(0) == 0)
    def _():
        hprev[...] = s0_ref[...]

    x = x_ref[...]
    ms = jnp.mean(x * x, axis=-1, keepdims=True)
    h = (x * lax.rsqrt(ms + 1e-6) * g_ref[...]) * (1.0 + m_ref[1]) + m_ref[0]
    dx = hprev[...] - h
    for mi, slab in enumerate(MU_SLABS):
        xs_ref[slab] = (h + dx * mu_ref[mi:mi + 1, :]).astype(BF16)
    xs_ref[SLAB_H] = h.astype(BF16)
    hprev[...] = h
    so_ref[...] = h


def _prep_tm(x, mod, norm_g, mu, shift0):
    t, b, d = x.shape
    return pl.pallas_call(
        _prep_tm_kernel,
        grid=(t,),
        in_specs=[pl.BlockSpec((None, b, d), lambda i: (i, 0, 0)), _const_spec((N_MOD, b, d)),
                  _const_spec((1, d)), _const_spec((N_MOD, d)), _const_spec((b, d))],
        out_specs=[pl.BlockSpec((7, b, d), lambda i: (0, i, 0)),
                   pl.BlockSpec((b, d), lambda i: (0, 0))],
        out_shape=[jax.ShapeDtypeStruct((7, t * b, d), BF16), jax.ShapeDtypeStruct((b, d), F32)],
        scratch_shapes=[pltpu.VMEM((b, d), F32)],
        compiler_params=_cparams(1),
        name="prep_tm",
    )(x, mod, norm_g.reshape(1, d), mu, shift0)


def _proj_t_kernel(a_ref, w_ref, o_ref):
    acc = jnp.dot(a_ref[...], w_ref[...].astype(BF16), preferred_element_type=F32)
    o_ref[...] = acc.T


def _proj_transposed(xs, w, *, tn, n):
    _, rows, k = xs.shape
    d = n // 3
    return pl.pallas_call(
        _proj_t_kernel,
        grid=(n // tn,),
        in_specs=[pl.BlockSpec((None, rows, k), lambda j: (j * tn // d, 0, 0)),
                  pl.BlockSpec((k, tn), lambda j: (0, j))],
        out_specs=pl.BlockSpec((tn, rows), lambda j: (j, 0)),
        out_shape=jax.ShapeDtypeStruct((n, rows), F32),
        compiler_params=_cparams(1),
        name="rkv_proj_tm",
    )(xs, w)


def _lora_t_kernel(*refs):
    dec_ref, a_ref, g_ref = refs[-3:]
    dec, a, g = _lora_heads(*refs[:-3])
    dec_ref[...] = dec.T
    a_ref[...] = a.T
    g_ref[...] = g


def _lora_transposed(xs, w0, w1, w2, a0, a1, a2, g1, g2):
    _, rows, d = xs.shape
    w1p, w2p = _pad_lora(w1, w2)
    a1p, a2p = _pad_lora(a1, a2)
    consts = [w1p, w2p, a1p, a2p, g1.astype(BF16), g2.astype(BF16), w0.reshape(1, d),
              a0.reshape(1, d)]
    slab = lambda s: pl.BlockSpec((None, rows, d), lambda i: (s, 0, 0))
    t_shape = jax.ShapeDtypeStruct((d, rows), F32)
    return pl.pallas_call(
        _lora_t_kernel,
        grid=(1,),
        in_specs=[slab(SLAB_W), slab(SLAB_A), slab(SLAB_G)] + [_const_spec(c.shape) for c in consts],
        out_specs=[pl.BlockSpec((d, rows), lambda i: (0, 0)), pl.BlockSpec((d, rows), lambda i: (0, 0)),
                   pl.BlockSpec((rows, d), lambda i: (0, 0))],
        out_shape=[t_shape, t_shape, jax.ShapeDtypeStruct((rows, d), F32)],
        compiler_params=_cparams(1),
        name="lora_tm",
    )(xs, xs, xs, *consts)


def _wkv_bl_kernel(r_ref, k_ref, v_ref, a_ref, d_ref, kk_p, ka_p, rk_p, lg_p, lb_p, s0_ref,
                   y_ref, so_ref, st, kk_s, b_s, k2_s, r_s, pe_s, *, steps):
    n = HEAD_DIM
    tile = lambda ref, t: ref[:, t * LANES:(t + 1) * LANES]
    st[...] = jnp.swapaxes(s0_ref[...], 0, 1)
    bonus = []
    p_t = jnp.ones((n, LANES), F32)
    for t in range(steps):
        cols = slice(t * LANES, (t + 1) * LANES)
        k = tile(k_ref, t)
        a = tile(a_ref, t)
        r = tile(r_ref, t)
        kk = k * kk_p[...]
        kk = kk * (1.0 / jnp.maximum(jnp.sqrt(jnp.sum(kk * kk, axis=0, keepdims=True)), 1e-12))
        k2 = k * (1.0 + (a - 1.0) * ka_p[...])
        bonus.append(jnp.sum(r * k2 * rk_p[...], axis=0, keepdims=True))
        kk_s[:, cols] = kk * p_t
        p_t = p_t * tile(d_ref, t)
        inv_p = 1.0 / p_t
        b_s[:, cols] = kk * a * inv_p
        k2_s[:, cols] = k2 * inv_p
        r_s[:, cols] = r * p_t
    pe_s[...] = p_t

    sa = jnp.zeros((n, LANES), F32)
    for i in range(n):
        sa = sa + st[i] * kk_s[i:i + 1, 0:LANES]
    for t in range(steps):
        t_next = min(t + 1, steps - 1)
        row = lambda ref, t=t: (lambda i: ref[i:i + 1, t * LANES:(t + 1) * LANES])
        v = tile(v_ref, t)
        y, sa = _wkv_step(st, sa, v, row(b_s), row(k2_s), row(r_s), row(kk_s, t_next))
        ym = jnp.mean(y, axis=0, keepdims=True)
        yc = y - ym
        yv = jnp.mean(yc * yc, axis=0, keepdims=True)
        yn = yc * lax.rsqrt(yv + GN_EPS) * lg_p[...] + lb_p[...]
        y_ref[:, t * LANES:(t + 1) * LANES] = yn + bonus[t] * v
    for i in range(n):
        st[i] = st[i] * pe_s[i:i + 1, :]
    so_ref[...] = jnp.swapaxes(st[...], 0, 1)


def _wkv_batch_lanes(rkv, a, dec, s0, k_k, k_a, r_k, lnx_g, lnx_b, *, steps):
    d, cols = a.shape
    heads, n = s0.shape[0], HEAD_DIM
    params = [jnp.broadcast_to(p.reshape(heads, n, 1), (heads, n, LANES))
              for p in (k_k, k_a, r_k, lnx_g, lnx_b)]
    chan = lambda c: pl.BlockSpec((n, cols), lambda h: (c * heads + h, 0))
    par = pl.BlockSpec((None, n, LANES), lambda h: (h, 0, 0))
    st = pl.BlockSpec((None, n, n, LANES), lambda h: (h, 0, 0, 0))
    return pl.pallas_call(
        functools.partial(_wkv_bl_kernel, steps=steps),
        grid=(heads,),
        in_specs=[chan(0), chan(1), chan(2), chan(0), chan(0)] + [par] * 5 + [st],
        out_specs=[chan(0), st],
        out_shape=[jax.ShapeDtypeStruct((d, cols), F32), jax.ShapeDtypeStruct(s0.shape, F32)],
        scratch_shapes=[pltpu.VMEM((n, n, LANES), F32)] + [pltpu.VMEM((n, cols), F32)] * 4
        + [pltpu.VMEM((n, LANES), F32)],
        compiler_params=_cparams(1),
        name="wkv7_tm",
    )(rkv, rkv, rkv, a, dec, *params, s0)


CONV_TM_COLS = 512


def _conv_tm_kernel(u1_ref, u2_ref, bg1, bg2, dwk, dwb, lg, lb, st_ref, y_ref, sto_ref, zc,
                    *, steps, b):
    j = pl.program_id(0)
    hist = CONV_W - 1
    tc = u1_ref.shape[-1]
    for c0 in range(0, tc, LANES):
        cs = slice(c0, c0 + LANES)
        glu = [(u1_ref[t * b:(t + 1) * b, cs] + bg1[:, cs])
               * _sigmoid(u2_ref[t * b:(t + 1) * b, cs] + bg2[:, cs]) for t in range(steps)]
        window = lambda i: st_ref[i, :, cs] if i < hist else glu[i - hist]
        for t in range(steps):
            acc = jnp.zeros((b, LANES), F32)
            for tap in range(CONV_W):
                acc = acc + window(t + tap) * dwk[tap:tap + 1, cs]
            zc[j, t * b:(t + 1) * b, cs] = acc + dwb[:, cs]
        for i in range(hist):
            sto_ref[i, :, cs] = window(i + steps)

    @pl.when(j == pl.num_programs(0) - 1)
    def _():
        z = zc[...]
        m = jnp.mean(jnp.mean(z, axis=-1, keepdims=True), axis=0, keepdims=True)
        ctr = z - m
        var = jnp.mean(jnp.mean(ctr * ctr, axis=-1, keepdims=True), axis=0, keepdims=True)
        ln = ctr * lax.rsqrt(var + 1e-5)
        for jj in range(zc.shape[0]):
            cols = slice(jj * tc, (jj + 1) * tc)
            y_ref[:, cols] = _silu(ln[jj] * lg[:, cols] + lb[:, cols]).astype(BF16)


def _conv_tm(glu_in, b_glu, dw_k, dw_b, ln_g, ln_b, st0, *, steps):
    rows, _ = glu_in.shape
    hist, b, d = st0.shape
    tc = CONV_TM_COLS
    nct = d // tc
    col = lambda shape, h=0: pl.BlockSpec(shape, lambda j: (0, j + h * nct))
    state = pl.BlockSpec((hist, b, tc), lambda j: (0, 0, j))
    return pl.pallas_call(
        functools.partial(_conv_tm_kernel, steps=steps, b=b),
        grid=(nct,),
        in_specs=[col((rows, tc)), col((rows, tc), 1), col((1, tc)), col((1, tc), 1),
                  col((CONV_W, tc)), col((1, tc)), _const_spec((1, d)), _const_spec((1, d)), state],
        out_specs=[pl.BlockSpec((rows, d), lambda j: (0, 0)), state],
        out_shape=[jax.ShapeDtypeStruct((rows, d), BF16), jax.ShapeDtypeStruct(st0.shape, F32)],
        scratch_shapes=[pltpu.VMEM((nct, rows, tc), F32)],
        compiler_params=_cparams(1),
        name="conformer_conv_tm",
    )(glu_in, glu_in, b_glu.reshape(1, 2 * d), b_glu.reshape(1, 2 * d), dw_k, dw_b.reshape(1, d),
      ln_g.reshape(1, d), ln_b.reshape(1, d), st0)


def _merge_tm_kernel(yt_ref, g_ref, ga_ref, gb_ref, yb_ref, x_ref, wo_ref, m_ref, n2_ref,
                     x1_ref, h2_ref):
    f32 = lambda ref: ref[...].astype(F32)
    merged = f32(ga_ref) * (yt_ref[...].T * g_ref[...]) + f32(gb_ref) * f32(yb_ref)
    mix = jnp.dot(merged.astype(BF16), wo_ref[...], preferred_element_type=F32)
    x1 = x_ref[...] + m_ref[2] * mix
    x1_ref[...] = x1
    ms = jnp.mean(x1 * x1, axis=-1, keepdims=True)
    h2 = (x1 * lax.rsqrt(ms + 1e-6) * n2_ref[...]) * (1.0 + m_ref[4]) + m_ref[3]
    h2_ref[...] = h2.astype(BF16)


def _merge_tm(yt, g, ga, gb, yb, x, w_out, mod, norm_g, *, steps):
    rows, d = x.shape
    b = rows // steps
    row = pl.BlockSpec((b, d), lambda i: (i, 0))
    return pl.pallas_call(
        _merge_tm_kernel,
        grid=(steps,),
        in_specs=[pl.BlockSpec((d, b), lambda i: (0, i)), row, row, row, row, row,
                  _const_spec((d, d)), _const_spec((N_MOD, b, d)), _const_spec((1, d))],
        out_specs=[row, row],
        out_shape=[jax.ShapeDtypeStruct((rows, d), F32), jax.ShapeDtypeStruct((rows, d), BF16)],
        compiler_params=_cparams(1),
        name="merge_out_proj_tm",
    )(yt, g, ga, gb, yb, x, w_out, mod, norm_g.reshape(1, d))


def _up_act_tm_kernel(h_ref, wg_ref, wv_ref, kg_ref, kv_ref, bg_ref, bv_ref, sg_ref, sv_ref,
                      act_ref, stg_ref, stv_ref, *, steps, b):
    hist = FFN_CONV_W - 1
    h = h_ref[...]
    tn = wg_ref.shape[-1]
    z = []
    halves = ((wg_ref, kg_ref, bg_ref, sg_ref, stg_ref), (wv_ref, kv_ref, bv_ref, sv_ref, stv_ref))
    for w_ref, k_ref, b_ref, s_ref, st_ref in halves:
        u = jnp.dot(h, w_ref[...], preferred_element_type=F32).reshape(steps, b, tn)
        window = [s_ref[i] for i in range(hist)] + [u[t] for t in range(steps)]
        z.append(jnp.stack([b_ref[...] + sum(window[t + tap] * k_ref[tap:tap + 1, :]
                                             for tap in range(FFN_CONV_W))
                            for t in range(steps)]))
        for i in range(hist):
            st_ref[i] = window[steps + i]
    act_ref[...] = (_silu(z[0]) * z[1]).reshape(steps * b, tn).astype(BF16)


def _up_act_tm(h2, w_up, ffn_dw_k, ffn_dw_b, st0, *, steps, d_ff):
    rows, k = h2.shape
    hist, b, _ = st0.shape
    f = w_up.shape[1] // 2
    nj = f // FFN_TN
    col = lambda shape, h: pl.BlockSpec(shape, lambda j: (0, j + h * nj))
    state = lambda h: pl.BlockSpec((hist, b, FFN_TN), lambda j: (0, 0, j + h * nj))
    return pl.pallas_call(
        functools.partial(_up_act_tm_kernel, steps=steps, b=b),
        grid=(nj,),
        in_specs=[_const_spec((rows, k)), col((k, FFN_TN), 0), col((k, FFN_TN), 1),
                  col((FFN_CONV_W, FFN_TN), 0), col((FFN_CONV_W, FFN_TN), 1),
                  col((1, FFN_TN), 0), col((1, FFN_TN), 1), state(0), state(1)],
        out_specs=[pl.BlockSpec((rows, FFN_TN), lambda j: (0, j)), state(0), state(0)],
        out_shape=[jax.ShapeDtypeStruct((rows, d_ff), BF16),
                   jax.ShapeDtypeStruct((hist, b, f), F32),
                   jax.ShapeDtypeStruct((hist, b, f), F32)],
        compiler_params=_cparams(1),
        name="up_conv_act_tm",
    )(h2, w_up, w_up, ffn_dw_k, ffn_dw_k, ffn_dw_b, ffn_dw_b, st0, st0)


def _sample_layer(x, mod, shift0, wkv0, conv0, ffn0, p):
    b, t, d = x.shape
    assert b == LANES
    rows = t * b
    tm = lambda z: jnp.swapaxes(z, 0, 1)
    w_in = p["w_in"]
    x_tm = tm(x)
    mod_tm = tm(mod)
    xs, shift1 = _prep_tm(x_tm, mod_tm, p["norm1_g"], p["mu"], shift0)

    tn = PROJ_TN
    rkv = _proj_transposed(xs, w_in, tn=tn, n=3 * d)
    proj = functools.partial(_matmul, xs, w_in, tm=rows, tn=tn, slab=SLAB_H)
    glu_in = proj(col0=3 * d // tn, n=2 * d, name="glu_proj")
    ga = proj(col0=5 * d // tn, n=d, sigmoid=True, out_dtype=BF16, name="gate_a_proj")
    gb = proj(col0=6 * d // tn, n=d, sigmoid=True, out_dtype=BF16, name="gate_b_proj")
    dec, a, g = _lora_transposed(xs, p["w0"], p["w1"], p["w2"], p["a0"], p["a1"], p["a2"],
                                 p["g1"], p["g2"])

    yt, s1 = _wkv_batch_lanes(rkv, a, dec, wkv0.transpose(1, 2, 3, 0), p["k_k"], p["k_a"],
                              p["r_k"], p["lnx_g"], p["lnx_b"], steps=t)
    wkv1 = s1.transpose(3, 0, 1, 2)

    yb, conv1 = _conv_tm(glu_in, p["b_glu"], p["dw_k"], p["dw_b"], p["ln_conv_g"],
                         p["ln_conv_b"], tm(conv0), steps=t)
    x1, h2 = _merge_tm(yt, g, ga, gb, yb, x_tm.reshape(rows, d), p["w_out"].astype(BF16), mod_tm,
                       p["norm2_g"], steps=t)

    d_ff = p["w_down"].shape[0]
    f_pad = -(-d_ff // FFN_TN) * FFN_TN
    act, ffn_g, ffn_v = _up_act_tm(
        h2, _pad_halves(p["w_up"], f_pad).astype(BF16), _pad_halves(p["ffn_dw_k"], f_pad),
        _pad_halves(p["ffn_dw_b"], f_pad).reshape(1, 2 * f_pad), _pad_halves(tm(ffn0), f_pad),
        steps=t, d_ff=d_ff)
    ffn1 = tm(jnp.concatenate([ffn_g[..., :d_ff], ffn_v[..., :d_ff]], axis=-1))
    x2 = _matmul(act[None], p["w_down"], tm=rows, tn=DOWN_TN, res=x1, gate=jnp.tile(mod_tm[5], (t, 1)),
                 name="down_proj")
    y = _final_norm(x2, p["normf_g"], tm=min(NORM_TM, rows))
    return tm(y.reshape(t, b, d)), shift1, wkv1, tm(conv1), ffn1


def _head_minor(w):
    lead = w.shape[:-1]
    return w.reshape(*lead, -1, HEAD_DIM).swapaxes(-1, -2).reshape(*lead, w.shape[-1])


def _prompt_layer(x, mod, shift0, wkv0, conv0, ffn0, p, *, tm_seq, tq_proj, tq_tok, tm_mm, tt,
                  tq_ffn):
    b, t, d = x.shape
    rows = b * t
    groups = b // GROUP
    w_in = p["w_in"]

    xs, shift1 = _prep(x, mod, p["norm1_g"], p["mu"], shift0, tm=tm_seq)
    xs2 = xs.reshape(7, rows, d)

    w_hm = jnp.concatenate([_head_minor(w_in[:, i * d:(i + 1) * d]) for i in (0, 1, 2, 5)],
                           axis=1).astype(BF16)
    rkv = _proj_head_lanes(xs, w_hm, _Tiling(groups, t, 1, tq_proj), tn=d, n=3 * d)
    tn = PROJ_TN
    ga = _matmul(xs2, w_hm, tm=tm_mm, tn=tn, slab=SLAB_H, col0=3 * d // tn, n=d, sigmoid=True,
                 out_dtype=BF16, name="gate_a_proj")
    glu_in = _matmul(xs2, w_in, tm=tm_mm, tn=tn, slab=SLAB_H, col0=3 * d // tn, n=2 * d,
                     name="glu_proj")
    gb = _matmul(xs2, w_in, tm=tm_mm, tn=tn, slab=SLAB_H, col0=6 * d // tn, n=d, sigmoid=True,
                 out_dtype=BF16, name="gate_b_proj")

    tl = _Tiling(groups, t, 1, tq_tok)
    dec, a, g = _lora(xs, p["w0"], p["w1"], p["w2"], p["a0"], p["a1"], p["a2"], p["g1"], p["g2"], tl)
    yw, wkv1 = _wkv(rkv, a, dec, wkv0, p["k_k"], p["k_a"], p["r_k"], p["lnx_g"], p["lnx_b"], tt=tt)
    yb, conv1 = _conv(glu_in.reshape(b, t, 2 * d), p["b_glu"], p["dw_k"], p["dw_b"],
                      p["ln_conv_g"], p["ln_conv_b"], conv0, tm=tm_seq)
    x1, h2 = _merge(yw, g, ga.reshape(b, t, d), gb.reshape(b, t, d), yb, x, p["w_out"], mod,
                    p["norm2_g"], tl)

    d_ff = p["w_down"].shape[0]
    f_pad = -(-d_ff // FFN_TN) * FFN_TN
    act, ffn_g, ffn_v = _up_act(
        h2.reshape(rows, d), _pad_halves(p["w_up"], f_pad).astype(BF16),
        _pad_halves(p["ffn_dw_k"], f_pad), _pad_halves(p["ffn_dw_b"], f_pad).reshape(1, 2 * f_pad),
        _pad_halves(ffn0, f_pad), tq=tq_ffn, t=t, d_ff=d_ff)
    ffn1 = jnp.concatenate([ffn_g[..., :d_ff], ffn_v[..., :d_ff]], axis=-1)
    x2 = _matmul(act[None], p["w_down"], tm=min(tm_mm, DOWN_TM), tn=DOWN_TN, res=x1.reshape(rows, d),
                 gate=mod[:, 5][:, None, :], rows_per_gate=t, name="down_proj")
    y = _final_norm(x2, p["normf_g"], tm=min(NORM_TM, rows))
    return y.reshape(b, t, d), shift1.reshape(b, d), wkv1, conv1, ffn1


def kernel(x_prompt, x_sample, state_shift, state_wkv, state_conv, state_ffn, c_prompt, c_sample,
           norm1_g, norm2_g, normf_g, w_ada, b_ada, mu, w_in, b_glu, w0, w1, w2, a0, a1, a2, g1, g2,
           k_k, k_a, r_k, lnx_g, lnx_b, dw_k, dw_b, ln_conv_g, ln_conv_b, w_out, w_up, ffn_dw_k,
           ffn_dw_b, w_down):
    p = dict(norm1_g=norm1_g, norm2_g=norm2_g, normf_g=normf_g, mu=mu, w_in=w_in, b_glu=b_glu,
             w0=w0, w1=w1, w2=w2, a0=a0, a1=a1, a2=a2, g1=g1, g2=g2, k_k=k_k, k_a=k_a, r_k=r_k,
             lnx_g=lnx_g, lnx_b=lnx_b, dw_k=dw_k, dw_b=dw_b, ln_conv_g=ln_conv_g,
             ln_conv_b=ln_conv_b, w_out=w_out, w_up=w_up, ffn_dw_k=ffn_dw_k, ffn_dw_b=ffn_dw_b,
             w_down=w_down)
    bp, _, d = x_prompt.shape
    bs = x_sample.shape[0]
    f2 = w_up.shape[1]

    c_all = jnp.concatenate([c_prompt, c_sample], axis=0)
    c_rows = -(-c_all.shape[0] // SUBLANES) * SUBLANES
    c_all = jnp.pad(c_all, ((0, c_rows - c_all.shape[0]), (0, 0)))
    mod = _modulation(c_all, w_ada, b_ada).reshape(c_rows, N_MOD, d)
    mod_p, mod_s = mod[:bp], mod[bp:bp + bs]

    y_p, shift_p, wkv_p, conv_p, ffn_p = _prompt_layer(
        x_prompt, mod_p, jnp.zeros((bp, d), F32),
        jnp.zeros((bp, d // HEAD_DIM, HEAD_DIM, HEAD_DIM), F32),
        jnp.zeros((bp, CONV_W - 1, d), F32), jnp.zeros((bp, FFN_CONV_W - 1, f2), F32), p,
        **PROMPT_TILES)

    y_s, shift_s, wkv_s, conv_s, ffn_s = _sample_layer(
        x_sample, mod_s, state_shift, state_wkv, state_conv, state_ffn, p)
    return (y_p, y_s, shift_p, wkv_p, conv_p, ffn_p, shift_s, wkv_s, conv_s, ffn_s)
```

```python
import functools
import math
from typing import NamedTuple

import jax
import jax.numpy as jnp
from jax import lax
from jax.experimental import pallas as pl
from jax.experimental.pallas import tpu as pltpu

F32 = jnp.float32
BF16 = jnp.bfloat16

HEAD_DIM = 64
CONV_W = 31
FFN_CONV_W = 3
N_MOD = 6
GN_EPS = HEAD_DIM * 1e-5
DECAY_SCALE = math.exp(-0.5)
SUBLANES = 8
LANES = 128
MXU_COLS = 256
LORA_PAD = 128
GROUP = 4
V7X_VMEM_LIMIT = 56 * 1024 * 1024

PROJ_TN = 1024
MOD_TN = 2048
DOWN_TM, DOWN_TN = 512, 512
NORM_TM = 256
PROMPT_TILES = dict(tm_seq=256, tq_proj=256, tq_tok=64, tm_mm=1024, tt=64, tq_ffn=1024)

SLAB_R, SLAB_K, SLAB_V, SLAB_H, SLAB_W, SLAB_A, SLAB_G = range(7)
MU_SLABS = (SLAB_R, SLAB_W, SLAB_K, SLAB_V, SLAB_A, SLAB_G)


class _Tiling(NamedTuple):
    groups: int
    t: int
    gs: int
    tq: int

    @property
    def grid(self):
        return (self.groups // self.gs, self.t // self.tq)

    @property
    def seqs(self):
        return self.gs * GROUP

    def plain(self, cols, col=0):
        return pl.BlockSpec((self.seqs, self.tq, cols), lambda ig, it: (ig, it, col))

    def head_lanes(self, blk=0):
        return pl.BlockSpec((self.gs, HEAD_DIM, self.tq, LANES), lambda ig, it: (ig, blk, it, 0))

    def per_seq(self, rows, cols):
        return pl.BlockSpec((self.seqs, rows, cols), lambda ig, it: (ig, 0, 0))


def _cparams(n_axes):
    return pltpu.CompilerParams(dimension_semantics=("arbitrary",) * n_axes,
                                vmem_limit_bytes=V7X_VMEM_LIMIT)


def _const_spec(shape):
    return pl.BlockSpec(shape, lambda *_: (0,) * len(shape), pipeline_mode=pl.Buffered(1))


def _sigmoid(x):
    return 1.0 / (1.0 + jnp.exp(-x))


def _silu(x):
    return x * _sigmoid(x)


def _swap_lane_groups(m, lane):
    half = lane < 2 * (LANES // GROUP)
    odd = (lane // (LANES // GROUP)) % 2 == 1
    n = [None] * 4
    for s in range(2):
        n[s] = jnp.where(half, m[s], pltpu.roll(m[s + 2], LANES // 2, axis=1))
        n[s + 2] = jnp.where(half, pltpu.roll(m[s], LANES // 2, axis=1), m[s + 2])
    out = [None] * 4
    for p in (0, 2):
        a, b = n[p], n[p + 1]
        out[p] = jnp.where(odd, pltpu.roll(b, LANES // GROUP, axis=1), a)
        out[p + 1] = jnp.where(odd, b, pltpu.roll(a, LANES - LANES // GROUP, axis=1))
    return out


def _store_head_lanes(val, out_ref, gs, tq, q0=0):
    lane = lax.broadcasted_iota(jnp.int32, (tq, LANES), 1)
    for g in range(gs):
        for q in range(val.shape[1] // LANES):
            src = [val[(g * GROUP + s) * tq:(g * GROUP + s + 1) * tq, q * LANES:(q + 1) * LANES]
                   for s in range(GROUP)]
            out = _swap_lane_groups(src, lane)
            for j in range(GROUP):
                out_ref[g, GROUP * (q0 + q) + j] = out[j]


def _rows2d(ref):
    v = ref[...]
    return v.reshape(-1, v.shape[-1])


def _mod_kernel(c_ref, w_ref, b_ref, o_ref):
    s = _silu(c_ref[...]).astype(BF16)
    o_ref[...] = jnp.dot(s, w_ref[...].astype(BF16), preferred_element_type=F32) + b_ref[...]


def _modulation(c, w_ada, b_ada):
    rows, d = c.shape
    n = w_ada.shape[1]
    tn = MOD_TN
    return pl.pallas_call(
        _mod_kernel,
        grid=(n // tn,),
        in_specs=[pl.BlockSpec((rows, d), lambda j: (0, 0)),
                  pl.BlockSpec((d, tn), lambda j: (0, j)),
                  pl.BlockSpec((1, tn), lambda j: (0, j))],
        out_specs=pl.BlockSpec((rows, tn), lambda j: (0, j)),
        out_shape=jax.ShapeDtypeStruct((rows, n), F32),
        compiler_params=_cparams(1),
        name="modulation",
    )(c, w_ada, b_ada.reshape(1, n))


def _prep_kernel(x_ref, m_ref, g_ref, mu_ref, s0_ref, xs_ref, so_ref, hbuf, *, tm):
    x = x_ref[...]
    m = m_ref[...]
    ms = jnp.mean(x * x, axis=-1, keepdims=True)
    h = (x * lax.rsqrt(ms + 1e-6) * g_ref[...]) * (1.0 + m[1:2, :]) + m[0:1, :]
    hbuf[SUBLANES:SUBLANES + tm, :] = h

    @pl.when(pl.program_id(1) == 0)
    def _():
        hbuf[SUBLANES - 1:SUBLANES, :] = s0_ref[...]

    dx = hbuf[SUBLANES - 1:SUBLANES - 1 + tm, :] - h
    for mi, slab in enumerate(MU_SLABS):
        xs_ref[slab] = (h + dx * mu_ref[mi:mi + 1, :]).astype(BF16)
    xs_ref[SLAB_H] = h.astype(BF16)
    last = h[tm - 1:tm, :]
    hbuf[SUBLANES - 1:SUBLANES, :] = last
    so_ref[...] = last


def _prep(x, mod, norm_g, mu, shift0, *, tm):
    b, t, d = x.shape
    return pl.pallas_call(
        functools.partial(_prep_kernel, tm=tm),
        grid=(b, t // tm),
        in_specs=[pl.BlockSpec((None, tm, d), lambda i, j: (i, j, 0)),
                  pl.BlockSpec((None, N_MOD, d), lambda i, j: (i, 0, 0)),
                  _const_spec((1, d)),
                  _const_spec((N_MOD, d)),
                  pl.BlockSpec((None, 1, d), lambda i, j: (i, 0, 0))],
        out_specs=[pl.BlockSpec((7, None, tm, d), lambda i, j: (0, i, j, 0)),
                   pl.BlockSpec((None, 1, d), lambda i, j: (i, 0, 0))],
        out_shape=[jax.ShapeDtypeStruct((7, b, t, d), BF16),
                   jax.ShapeDtypeStruct((b, 1, d), F32)],
        scratch_shapes=[pltpu.VMEM((SUBLANES + tm, d), F32)],
        compiler_params=_cparams(2),
        name="prep",
    )(x, mod, norm_g.reshape(1, d), mu, shift0.reshape(b, 1, d))


def _gate_blocks(gate, rows, tm, rows_per_gate):
    if gate.ndim == 2:
        return gate.reshape(rows // tm, tm, gate.shape[-1]), 1
    assert rows_per_gate % tm == 0, (rows_per_gate, tm)
    return gate, rows_per_gate // tm


def _mm_kernel(*refs, gated, sigmoid):
    if gated:
        a_ref, w_ref, res_ref, gate_ref, o_ref, wb = refs
    else:
        a_ref, w_ref, o_ref, wb = refs

    @pl.when(pl.program_id(1) == 0)
    def _():
        wb[...] = w_ref[...].astype(BF16)

    acc = jnp.dot(a_ref[...], wb[...], preferred_element_type=F32)
    if gated:
        acc = res_ref[...] + gate_ref[...] * acc
    if sigmoid:
        acc = _sigmoid(acc)
    o_ref[...] = acc.astype(o_ref.dtype)


def _matmul(a, w, *, tm, tn, slab=0, col0=0, n=None, res=None, gate=None, rows_per_gate=None,
            sigmoid=False, out_dtype=F32, name):
    _, rows, k = a.shape
    n = w.shape[1] if n is None else n
    gated = res is not None
    in_specs = [pl.BlockSpec((None, tm, k), lambda j, i: (slab, i, 0)),
                pl.BlockSpec((k, tn), lambda j, i: (0, j + col0))]
    args = [a, w]
    if gated:
        gate, tiles_per_gate = _gate_blocks(gate, rows, tm, rows_per_gate)
        gr = gate.shape[1]
        in_specs += [pl.BlockSpec((tm, tn), lambda j, i: (i, j)),
                     pl.BlockSpec((None, gr, tn), lambda j, i: (i // tiles_per_gate, 0, j))]
        args += [res, gate]
    return pl.pallas_call(
        functools.partial(_mm_kernel, gated=gated, sigmoid=sigmoid),
        grid=(pl.cdiv(n, tn), rows // tm),
        in_specs=in_specs,
        out_specs=pl.BlockSpec((tm, tn), lambda j, i: (i, j)),
        out_shape=jax.ShapeDtypeStruct((rows, n), out_dtype),
        scratch_shapes=[pltpu.VMEM((k, tn), BF16)],
        compiler_params=_cparams(2),
        name=name,
    )(*args)


def _proj_hl_kernel(a_ref, w_ref, o_ref, *, gs, tq):
    for c in range(0, w_ref.shape[1], MXU_COLS):
        acc = jnp.dot(_rows2d(a_ref), w_ref[:, c:c + MXU_COLS], preferred_element_type=F32)
        _store_head_lanes(acc, o_ref, gs, tq, q0=c // LANES)


def _proj_head_lanes(xs, w, tl, *, tn, n):
    k = w.shape[0]
    d = n // 3
    slab = lambda j: j * tn // d
    a_spec = pl.BlockSpec((None, tl.seqs, tl.tq, k), lambda j, ig, it: (slab(j), ig, it, 0))
    nq = tn // (LANES // GROUP)
    return pl.pallas_call(
        functools.partial(_proj_hl_kernel, gs=tl.gs, tq=tl.tq),
        grid=(n // tn,) + tl.grid,
        in_specs=[a_spec, pl.BlockSpec((k, tn), lambda j, ig, it: (0, j))],
        out_specs=pl.BlockSpec((tl.gs, nq, tl.tq, LANES), lambda j, ig, it: (ig, j, it, 0)),
        out_shape=jax.ShapeDtypeStruct((tl.groups, n // (LANES // GROUP), tl.t, LANES), F32),
        compiler_params=_cparams(3),
        name="rkv_proj",
    )(xs, w)


def _lora_heads(xw_ref, xa_ref, xg_ref, w1, w2, a1, a2, g1, g2, w0, a0):
    dot = functools.partial(jnp.dot, preferred_element_type=F32)
    hw = jnp.tanh(dot(_rows2d(xw_ref), w1[...])).astype(BF16)
    w_pre = w0[...] + dot(hw, w2[...])
    dec = jnp.exp(-DECAY_SCALE * _sigmoid(w_pre))
    ha = dot(_rows2d(xa_ref), a1[...]).astype(BF16)
    a = _sigmoid(a0[...] + dot(ha, a2[...]))
    hg = _sigmoid(dot(_rows2d(xg_ref), g1[...])).astype(BF16)
    return dec, a, dot(hg, g2[...])


def _lora_kernel(*refs, gs, tq):
    dec_ref, a_ref, g_ref = refs[-3:]
    dec, a, g = _lora_heads(*refs[:-3])
    _store_head_lanes(dec, dec_ref, gs, tq)
    _store_head_lanes(a, a_ref, gs, tq)
    g_ref[...] = g.astype(BF16).reshape(g_ref.shape)


def _pad_lora(w_a, w_b):
    r = w_a.shape[1]
    return (jnp.pad(w_a, ((0, 0), (0, LORA_PAD - r))).astype(BF16),
            jnp.pad(w_b, ((0, LORA_PAD - r), (0, 0))).astype(BF16))


def _lora(xs, w0, w1, w2, a0, a1, a2, g1, g2, tl):
    d = w0.shape[0]
    w1p, w2p = _pad_lora(w1, _head_minor(w2))
    a1p, a2p = _pad_lora(a1, _head_minor(a2))
    g1b, g2b = g1.astype(BF16), _head_minor(g2).astype(BF16)
    slab = lambda s: pl.BlockSpec((None, tl.seqs, tl.tq, d), lambda ig, it: (s, ig, it, 0))
    g_shape = (tl.groups * GROUP, tl.t, d)
    hl_shape = jax.ShapeDtypeStruct((tl.groups, HEAD_DIM, tl.t, LANES), F32)
    consts = [w1p, w2p, a1p, a2p, g1b, g2b, _head_minor(w0).reshape(1, d),
              _head_minor(a0).reshape(1, d)]
    return pl.pallas_call(
        functools.partial(_lora_kernel, gs=tl.gs, tq=tl.tq),
        grid=tl.grid,
        in_specs=[slab(SLAB_W), slab(SLAB_A), slab(SLAB_G)] + [_const_spec(c.shape) for c in consts],
        out_specs=[tl.head_lanes(), tl.head_lanes(), tl.plain(d)],
        out_shape=[hl_shape, hl_shape, jax.ShapeDtypeStruct(g_shape, BF16)],
        compiler_params=_cparams(2),
        name="lora",
    )(xs, xs, xs, *consts)


def _wkv_step(st, sa, v, b_row, k2_row, r_row, kk_next_row):
    y = jnp.zeros((HEAD_DIM, LANES), F32)
    sa_next = jnp.zeros((HEAD_DIM, LANES), F32)
    for i in range(HEAD_DIM):
        si = st[i] - sa * b_row(i) + v * k2_row(i)
        st[i] = si
        y = y + si * r_row(i)
        sa_next = sa_next + si * kk_next_row(i)
    return y, sa_next


def _wkv_kernel(r_ref, k_ref, v_ref, a_ref, d_ref, kk_p, ka_p, rk_p, lg_p, lb_p, s0_ref,
                y_ref, so_ref, st, kk_s, b_s, k2_s, r_s, pe_s, bon_s, v_t, y_t, *, nb):
    n = HEAD_DIM
    tt = nb * SUBLANES
    bc = lambda p: p[...][:, None, None, :]

    @pl.when(pl.program_id(1) == 0)
    def _():
        s = s0_ref[...].reshape(LANES, n, n)
        st[...] = jnp.swapaxes(jnp.swapaxes(jnp.swapaxes(s, 0, 1), 1, 2), 0, 1)

    x = d_ref[...].reshape(n * nb, SUBLANES, LANES)
    step = lax.broadcasted_iota(jnp.int32, x.shape, 1)
    for sh in (1, 2, 4):
        x = x * jnp.where(step >= sh, pltpu.roll(x, sh, axis=1), 1.0)
    x = x.reshape(n, nb, SUBLANES, LANES)
    step = step.reshape(x.shape)
    run = jnp.ones((n, 1, LANES), F32)
    incl, excl = [], []
    for tb in range(nb):
        blk = x[:, tb] * run
        incl.append(blk)
        excl.append(jnp.where(step[:, tb] >= 1, pltpu.roll(blk, 1, axis=1), run))
        run = blk[:, SUBLANES - 1:SUBLANES, :]
    p_t = jnp.stack(incl, axis=1)
    inv_p = 1.0 / p_t

    k = k_ref[...]
    a = a_ref[...]
    r = r_ref[...]
    kk = k * bc(kk_p)
    kk = kk * (1.0 / jnp.maximum(jnp.sqrt(jnp.sum(kk * kk, axis=0, keepdims=True)), 1e-12))
    k2 = k * (1.0 + (a - 1.0) * bc(ka_p))
    kk_s[...] = kk * jnp.stack(excl, axis=1)
    b_s[...] = kk * a * inv_p
    k2_s[...] = k2 * inv_p
    r_s[...] = r * p_t
    pe_s[...] = p_t[:, nb - 1, SUBLANES - 1, :]
    bon_s[...] = jnp.sum(r * k2 * bc(rk_p), axis=0)
    v_t[...] = jnp.swapaxes(v_ref[...], 0, 1)

    sa0 = jnp.zeros((n, LANES), F32)
    for i in range(n):
        sa0 = sa0 + st[i] * kk_s[i, 0, 0:1, :]

    def block(tb, sa):
        tb_next = jnp.minimum(tb + 1, nb - 1)
        for j in range(SUBLANES):
            t = tb * SUBLANES + j
            row = lambda ref: (lambda i: ref[i, tb, j:j + 1, :])
            if j + 1 < SUBLANES:
                kk_next = lambda i: kk_s[i, tb, j + 1:j + 2, :]
            else:
                kk_next = lambda i: kk_s[i, tb_next, 0:1, :]
            y_t[t], sa = _wkv_step(st, sa, v_t[t], row(b_s), row(k2_s), row(r_s), kk_next)
        return sa

    lax.fori_loop(0, nb, block, sa0)
    for i in range(n):
        st[i] = st[i] * pe_s[i:i + 1, :]

    y = jnp.swapaxes(y_t[...], 0, 1)
    ym = jnp.mean(y, axis=0, keepdims=True)
    yc = y - ym
    yv = jnp.mean(yc * yc, axis=0, keepdims=True)
    bcast = lambda p: p[...][:, None, :]
    yn = yc * lax.rsqrt(yv + GN_EPS) * bcast(lg_p) + bcast(lb_p)
    y_ref[...] = yn + bon_s[...].reshape(1, tt, LANES) * v_ref[...]

    @pl.when(pl.program_id(1) == pl.num_programs(1) - 1)
    def _():
        s = jnp.swapaxes(jnp.swapaxes(jnp.swapaxes(st[...], 0, 1), 1, 2), 0, 1)
        so_ref[...] = s.reshape(so_ref.shape)


def _head_lanes(p):
    hn = p.reshape(-1, HEAD_DIM).T
    return jnp.tile(hn, (1, GROUP))


def _wkv(rkv, a, dec, s0, k_k, k_a, r_k, lnx_g, lnx_b, *, tt):
    g, _, t, lanes = a.shape
    n = HEAD_DIM
    heads = s0.shape[1]
    nb = tt // SUBLANES
    by8 = lambda z: z.reshape(z.shape[0], z.shape[1], t // SUBLANES, SUBLANES, lanes)
    params = [_head_lanes(p) for p in (k_k, k_a, r_k.reshape(-1), lnx_g, lnx_b)]
    rows = lambda blk: pl.BlockSpec((None, n, nb, SUBLANES, lanes), lambda i, j: (i, blk, j, 0, 0))
    tile = lambda blk: pl.BlockSpec((None, n, tt, lanes), lambda i, j: (i, blk, j, 0))
    st = pl.BlockSpec((GROUP, heads, n, n), lambda i, j: (i, 0, 0, 0))
    return pl.pallas_call(
        functools.partial(_wkv_kernel, nb=nb),
        grid=(g, t // tt),
        in_specs=[rows(0), rows(1), tile(2), rows(0), rows(0)] + [_const_spec((n, lanes))] * 5 + [st],
        out_specs=[tile(0), st],
        out_shape=[jax.ShapeDtypeStruct((g, n, t, lanes), F32),
                   jax.ShapeDtypeStruct(s0.shape, F32)],
        scratch_shapes=[pltpu.VMEM((n, n, lanes), F32)]
        + [pltpu.VMEM((n, nb, SUBLANES, lanes), F32)] * 4
        + [pltpu.VMEM((n, lanes), F32), pltpu.VMEM((nb, SUBLANES, lanes), F32)]
        + [pltpu.VMEM((tt, n, lanes), F32)] * 2,
        compiler_params=_cparams(2),
        name="wkv7",
    )(by8(rkv), by8(rkv), rkv, by8(a), by8(dec), *params, s0)


CONV_PAD = 32
CONV_ROWS = 32


def _conv_kernel(u1_ref, u2_ref, bg1, bg2, dwk, dwb, lg, lb, st0_ref, y_ref, sto_ref, zbuf, cbuf,
                 zs, *, tm):
    hist = CONV_W - 1
    lo = CONV_PAD - hist
    d = u1_ref.shape[-1]
    rc = min(CONV_ROWS, tm)
    glu = (u1_ref[...] + bg1[...]) * _sigmoid(u2_ref[...] + bg2[...])
    zbuf[CONV_PAD:CONV_PAD + tm, :] = glu

    @pl.when(pl.program_id(1) == 0)
    def _():
        zbuf[lo:CONV_PAD, :] = st0_ref[...]

    for b in range(SUBLANES):
        span = tm + SUBLANES * ((hist - b) // SUBLANES)
        zs[b, 0:span, :] = zbuf[lo + b:lo + b + span, :]
    for r0 in range(0, tm, rc):
        for c0 in range(0, d, LANES):
            acc = jnp.zeros((rc, LANES), F32)
            for j in range(CONV_W):
                a8 = SUBLANES * (j // SUBLANES)
                acc = acc + (zs[j % SUBLANES, a8 + r0:a8 + r0 + rc, c0:c0 + LANES]
                             * dwk[j:j + 1, c0:c0 + LANES])
            cbuf[r0:r0 + rc, c0:c0 + LANES] = acc
    zc = cbuf[...] + dwb[...]
    m = jnp.mean(zc, axis=-1, keepdims=True)
    ctr = zc - m
    var = jnp.mean(ctr * ctr, axis=-1, keepdims=True)
    ln = ctr * lax.rsqrt(var + 1e-5) * lg[...] + lb[...]
    y_ref[...] = _silu(ln).astype(BF16)
    carry = zbuf[lo + tm:CONV_PAD + tm, :]
    zbuf[lo:CONV_PAD, :] = carry
    sto_ref[...] = carry


def _conv(glu_in, b_glu, dw_k, dw_b, ln_g, ln_b, st0, *, tm):
    b, t, _ = glu_in.shape
    d = dw_k.shape[1]
    hist = CONV_W - 1
    vec = _const_spec((1, d))
    return pl.pallas_call(
        functools.partial(_conv_kernel, tm=tm),
        grid=(b, t // tm),
        in_specs=[pl.BlockSpec((None, tm, d), lambda i, j: (i, j, 0)),
                  pl.BlockSpec((None, tm, d), lambda i, j: (i, j, 1)),
                  pl.BlockSpec((1, d), lambda i, j: (0, 0)),
                  pl.BlockSpec((1, d), lambda i, j: (0, 1)),
                  _const_spec((CONV_W, d)),
                  vec, vec, vec,
                  pl.BlockSpec((None, hist, d), lambda i, j: (i, 0, 0))],
        out_specs=[pl.BlockSpec((None, tm, d), lambda i, j: (i, j, 0)),
                   pl.BlockSpec((None, hist, d), lambda i, j: (i, 0, 0))],
        out_shape=[jax.ShapeDtypeStruct((b, t, d), BF16),
                   jax.ShapeDtypeStruct((b, hist, d), F32)],
        scratch_shapes=[pltpu.VMEM((CONV_PAD + tm, d), F32), pltpu.VMEM((tm, d), F32),
                        pltpu.VMEM((SUBLANES, tm + CONV_PAD - SUBLANES, d), F32)],
        compiler_params=_cparams(2),
        name="conformer_conv",
    )(glu_in, glu_in, b_glu.reshape(1, 2 * d), b_glu.reshape(1, 2 * d), dw_k, dw_b.reshape(1, d),
      ln_g.reshape(1, d), ln_b.reshape(1, d), st0)


def _merge_kernel(yw_ref, g_ref, ga_ref, gb_ref, yb_ref, x_ref, woa_ref, wob_ref, m_ref, n2_ref,
                  x1_ref, h2_ref, ybuf, *, gs, tq):
    seqs = gs * GROUP
    d = ybuf.shape[-1]
    lane = lax.broadcasted_iota(jnp.int32, (tq, LANES), 1)
    for g in range(gs):
        for q in range(d // LANES):
            out = _swap_lane_groups([yw_ref[g, GROUP * q + j] for j in range(GROUP)], lane)
            for s in range(GROUP):
                ybuf[g * GROUP + s, :, q * LANES:(q + 1) * LANES] = out[s]
    seq3 = lambda ref: ref[...].astype(F32).reshape(seqs, tq, d)
    m = m_ref[...]
    dot = lambda z, w_ref: jnp.dot(z.reshape(seqs * tq, d).astype(BF16), w_ref[...],
                                   preferred_element_type=F32)
    branch_a = seq3(ga_ref) * (ybuf[...] * seq3(g_ref))
    branch_b = seq3(gb_ref) * seq3(yb_ref)
    mix = (dot(branch_a, woa_ref) + dot(branch_b, wob_ref)).reshape(seqs, tq, d)
    x1 = seq3(x_ref) + m[:, 2:3, :] * mix
    x1_ref[...] = x1.reshape(x1_ref.shape)
    ms = jnp.mean(x1 * x1, axis=-1, keepdims=True)
    h2 = (x1 * lax.rsqrt(ms + 1e-6) * n2_ref[...]) * (1.0 + m[:, 4:5, :]) + m[:, 3:4, :]
    h2_ref[...] = h2.reshape(seqs * tq, d).astype(BF16).reshape(h2_ref.shape)


def _merge(yw, g, ga, gb, yb, x, w_out, mod, norm_g, tl):
    d = x.shape[-1]
    plain = tl.plain(d)
    w_out_b = w_out.astype(BF16)
    w_out_a = _head_minor(w_out.T).T.astype(BF16)
    return pl.pallas_call(
        functools.partial(_merge_kernel, gs=tl.gs, tq=tl.tq),
        grid=tl.grid,
        in_specs=[tl.head_lanes(), plain, plain, plain, plain, plain, _const_spec((d, d)),
                  _const_spec((d, d)), tl.per_seq(N_MOD, d), _const_spec((1, d))],
        out_specs=[plain, plain],
        out_shape=[jax.ShapeDtypeStruct(x.shape, F32), jax.ShapeDtypeStruct(x.shape, BF16)],
        scratch_shapes=[pltpu.VMEM((tl.seqs, tl.tq, d), F32)],
        compiler_params=_cparams(2),
        name="merge_out_proj",
    )(yw, g, ga, gb, yb, x, w_out_a, w_out_b, mod, norm_g.reshape(1, d))


FFN_TN = 512


def _pad_halves(w, width):
    f = w.shape[-1] // 2
    pad = [(0, 0)] * (w.ndim - 1) + [(0, width - f)]
    return jnp.concatenate([jnp.pad(w[..., :f], pad), jnp.pad(w[..., f:], pad)], axis=-1)


def _up_act_kernel(h_ref, wg_ref, wv_ref, kg_ref, kv_ref, bg_ref, bv_ref, sg_ref, sv_ref,
                   act_ref, stg_ref, stv_ref, carry, *, tq, tiles_per_seq):
    first = pl.program_id(1) % tiles_per_seq == 0
    tn = wg_ref.shape[-1]
    row = lax.broadcasted_iota(jnp.int32, (tq, tn), 0)
    z = []
    halves = ((wg_ref, kg_ref, bg_ref, stg_ref), (wv_ref, kv_ref, bv_ref, stv_ref))

    @pl.when(first)
    def _():
        carry[0] = sg_ref[...]
        carry[1] = sv_ref[...]

    for half, (w_ref, k_ref, b_ref, st_ref) in enumerate(halves):
        u = jnp.dot(h_ref[...], w_ref[...], preferred_element_type=F32)
        prev = carry[half]
        p0, p1 = prev[0:1, :], prev[1:2, :]
        u1 = jnp.where(row == 0, p1, pltpu.roll(u, 1, axis=0))
        u2 = jnp.where(row == 0, p0, jnp.where(row == 1, p1, pltpu.roll(u, 2, axis=0)))
        z.append(b_ref[...] + u2 * k_ref[0:1, :] + u1 * k_ref[1:2, :] + u * k_ref[2:3, :])
        last = u[tq - (FFN_CONV_W - 1):, :]
        carry[half] = last
        st_ref[...] = last
    act_ref[...] = (_silu(z[0]) * z[1]).astype(BF16)


def _up_act(h2, w_up, ffn_dw_k, ffn_dw_b, st0, *, tq, t, d_ff):
    rows, k = h2.shape
    b = st0.shape[0]
    f = w_up.shape[1] // 2
    nj = f // FFN_TN
    hist = FFN_CONV_W - 1
    tiles_per_seq = t // tq
    col = lambda shape, h: pl.BlockSpec(shape, lambda j, i: (0, j + h * nj))
    state = lambda h: pl.BlockSpec((None, hist, FFN_TN),
                                   lambda j, i: (i // tiles_per_seq, 0, j + h * nj))
    return pl.pallas_call(
        functools.partial(_up_act_kernel, tq=tq, tiles_per_seq=tiles_per_seq),
        grid=(nj, rows // tq),
        in_specs=[pl.BlockSpec((tq, k), lambda j, i: (i, 0)),
                  col((k, FFN_TN), 0), col((k, FFN_TN), 1),
                  col((FFN_CONV_W, FFN_TN), 0), col((FFN_CONV_W, FFN_TN), 1),
                  col((1, FFN_TN), 0), col((1, FFN_TN), 1),
                  state(0), state(1)],
        out_specs=[pl.BlockSpec((tq, FFN_TN), lambda j, i: (i, j)), state(0), state(0)],
        out_shape=[jax.ShapeDtypeStruct((rows, d_ff), BF16),
                   jax.ShapeDtypeStruct((b, hist, f), F32),
                   jax.ShapeDtypeStruct((b, hist, f), F32)],
        scratch_shapes=[pltpu.VMEM((2, hist, FFN_TN), F32)],
        compiler_params=_cparams(2),
        name="up_conv_act",
    )(h2, w_up, w_up, ffn_dw_k, ffn_dw_k, ffn_dw_b, ffn_dw_b, st0, st0)


def _norm_kernel(x_ref, g_ref, o_ref):
    x = x_ref[...]
    ms = jnp.mean(x * x, axis=-1, keepdims=True)
    o_ref[...] = x * lax.rsqrt(ms + 1e-6) * g_ref[...]


def _final_norm(x, g, *, tm):
    rows, d = x.shape
    row = pl.BlockSpec((tm, d), lambda i: (i, 0))
    return pl.pallas_call(
        _norm_kernel,
        grid=(rows // tm,),
        in_specs=[row, _const_spec((1, d))],
        out_specs=row,
        out_shape=jax.ShapeDtypeStruct((rows, d), F32),
        compiler_params=_cparams(1),
        name="final_norm",
    )(x, g.reshape(1, d))


def _prep_tm_kernel(x_ref, m_ref, g_ref, mu_ref, s0_ref, xs_ref, so_ref, hprev):
    @pl.when(pl.program_id(0) == 0)
    def _():
        hprev[...] = s0_ref[...]

    x = x_ref[...]
    ms = jnp.mean(x * x, axis=-1, keepdims=True)
    h = (x * lax.rsqrt(ms + 1e-6) * g_ref[...]) * (1.0 + m_ref[1]) + m_ref[0]
    dx = hprev[...] - h
    for mi, slab in enumerate(MU_SLABS):
        xs_ref[slab] = (h + dx * mu_ref[mi:mi + 1, :]).astype(BF16)
    xs_ref[SLAB_H] = h.astype(BF16)
    hprev[...] = h
    so_ref[...] = h


def _prep_tm(x, mod, norm_g, mu, shift0):
    t, b, d = x.shape
    return pl.pallas_call(
        _prep_tm_kernel,
        grid=(t,),
        in_specs=[pl.BlockSpec((None, b, d), lambda i: (i, 0, 0)), _const_spec((N_MOD, b, d)),
                  _const_spec((1, d)), _const_spec((N_MOD, d)), _const_spec((b, d))],
        out_specs=[pl.BlockSpec((7, b, d), lambda i: (0, i, 0)),
                   pl.BlockSpec((b, d), lambda i: (0, 0))],
        out_shape=[jax.ShapeDtypeStruct((7, t * b, d), BF16), jax.ShapeDtypeStruct((b, d), F32)],
        scratch_shapes=[pltpu.VMEM((b, d), F32)],
        compiler_params=_cparams(1),
        name="prep_tm",
    )(x, mod, norm_g.reshape(1, d), mu, shift0)


def _proj_t_kernel(a_ref, w_ref, o_ref):
    acc = jnp.dot(a_ref[...], w_ref[...].astype(BF16), preferred_element_type=F32)
    o_ref[...] = acc.T


def _proj_transposed(xs, w, *, tn, n):
    _, rows, k = xs.shape
    d = n // 3
    return pl.pallas_call(
        _proj_t_kernel,
        grid=(n // tn,),
        in_specs=[pl.BlockSpec((None, rows, k), lambda j: (j * tn // d, 0, 0)),
                  pl.BlockSpec((k, tn), lambda j: (0, j))],
        out_specs=pl.BlockSpec((tn, rows), lambda j: (j, 0)),
        out_shape=jax.ShapeDtypeStruct((n, rows), F32),
        compiler_params=_cparams(1),
        name="rkv_proj_tm",
    )(xs, w)


def _lora_t_kernel(*refs):
    dec_ref, a_ref, g_ref = refs[-3:]
    dec, a, g = _lora_heads(*refs[:-3])
    dec_ref[...] = dec.T
    a_ref[...] = a.T
    g_ref[...] = g


def _lora_transposed(xs, w0, w1, w2, a0, a1, a2, g1, g2):
    _, rows, d = xs.shape
    w1p, w2p = _pad_lora(w1, w2)
    a1p, a2p = _pad_lora(a1, a2)
    consts = [w1p, w2p, a1p, a2p, g1.astype(BF16), g2.astype(BF16), w0.reshape(1, d),
              a0.reshape(1, d)]
    slab = lambda s: pl.BlockSpec((None, rows, d), lambda i: (s, 0, 0))
    t_shape = jax.ShapeDtypeStruct((d, rows), F32)
    return pl.pallas_call(
        _lora_t_kernel,
        grid=(1,),
        in_specs=[slab(SLAB_W), slab(SLAB_A), slab(SLAB_G)] + [_const_spec(c.shape) for c in consts],
        out_specs=[pl.BlockSpec((d, rows), lambda i: (0, 0)), pl.BlockSpec((d, rows), lambda i: (0, 0)),
                   pl.BlockSpec((rows, d), lambda i: (0, 0))],
        out_shape=[t_shape, t_shape, jax.ShapeDtypeStruct((rows, d), F32)],
        compiler_params=_cparams(1),
        name="lora_tm",
    )(xs, xs, xs, *consts)


def _wkv_bl_kernel(r_ref, k_ref, v_ref, a_ref, d_ref, kk_p, ka_p, rk_p, lg_p, lb_p, s0_ref,
                   y_ref, so_ref, st, kk_s, b_s, k2_s, r_s, pe_s, *, steps):
    n = HEAD_DIM
    tile = lambda ref, t: ref[:, t * LANES:(t + 1) * LANES]
    st[...] = jnp.swapaxes(s0_ref[...], 0, 1)
    bonus = []
    p_t = jnp.ones((n, LANES), F32)
    for t in range(steps):
        cols = slice(t * LANES, (t + 1) * LANES)
        k = tile(k_ref, t)
        a = tile(a_ref, t)
        r = tile(r_ref, t)
        kk = k * kk_p[...]
        kk = kk * (1.0 / jnp.maximum(jnp.sqrt(jnp.sum(kk * kk, axis=0, keepdims=True)), 1e-12))
        k2 = k * (1.0 + (a - 1.0) * ka_p[...])
        bonus.append(jnp.sum(r * k2 * rk_p[...], axis=0, keepdims=True))
        kk_s[:, cols] = kk * p_t
        p_t = p_t * tile(d_ref, t)
        inv_p = 1.0 / p_t
        b_s[:, cols] = kk * a * inv_p
        k2_s[:, cols] = k2 * inv_p
        r_s[:, cols] = r * p_t
    pe_s[...] = p_t

    sa = jnp.zeros((n, LANES), F32)
    for i in range(n):
        sa = sa + st[i] * kk_s[i:i + 1, 0:LANES]
    for t in range(steps):
        t_next = min(t + 1, steps - 1)
        row = lambda ref, t=t: (lambda i: ref[i:i + 1, t * LANES:(t + 1) * LANES])
        v = tile(v_ref, t)
        y, sa = _wkv_step(st, sa, v, row(b_s), row(k2_s), row(r_s), row(kk_s, t_next))
        ym = jnp.mean(y, axis=0, keepdims=True)
        yc = y - ym
        yv = jnp.mean(yc * yc, axis=0, keepdims=True)
        yn = yc * lax.rsqrt(yv + GN_EPS) * lg_p[...] + lb_p[...]
        y_ref[:, t * LANES:(t + 1) * LANES] = yn + bonus[t] * v
    for i in range(n):
        st[i] = st[i] * pe_s[i:i + 1, :]
    so_ref[...] = jnp.swapaxes(st[...], 0, 1)


def _wkv_batch_lanes(rkv, a, dec, s0, k_k, k_a, r_k, lnx_g, lnx_b, *, steps):
    d, cols = a.shape
    heads, n = s0.shape[0], HEAD_DIM
    params = [jnp.broadcast_to(p.reshape(heads, n, 1), (heads, n, LANES))
              for p in (k_k, k_a, r_k, lnx_g, lnx_b)]
    chan = lambda c: pl.BlockSpec((n, cols), lambda h: (c * heads + h, 0))
    par = pl.BlockSpec((None, n, LANES), lambda h: (h, 0, 0))
    st = pl.BlockSpec((None, n, n, LANES), lambda h: (h, 0, 0, 0))
    return pl.pallas_call(
        functools.partial(_wkv_bl_kernel, steps=steps),
        grid=(heads,),
        in_specs=[chan(0), chan(1), chan(2), chan(0), chan(0)] + [par] * 5 + [st],
        out_specs=[chan(0), st],
        out_shape=[jax.ShapeDtypeStruct((d, cols), F32), jax.ShapeDtypeStruct(s0.shape, F32)],
        scratch_shapes=[pltpu.VMEM((n, n, LANES), F32)] + [pltpu.VMEM((n, cols), F32)] * 4
        + [pltpu.VMEM((n, LANES), F32)],
        compiler_params=_cparams(1),
        name="wkv7_tm",
    )(rkv, rkv, rkv, a, dec, *params, s0)


CONV_TM_COLS = 512


def _conv_tm_kernel(u1_ref, u2_ref, bg1, bg2, dwk, dwb, lg, lb, st_ref, y_ref, sto_ref, zc,
                    *, steps, b):
    j = pl.program_id(0)
    hist = CONV_W - 1
    tc = u1_ref.shape[-1]
    for c0 in range(0, tc, LANES):
        cs = slice(c0, c0 + LANES)
        glu = [(u1_ref[t * b:(t + 1) * b, cs] + bg1[:, cs])
               * _sigmoid(u2_ref[t * b:(t + 1) * b, cs] + bg2[:, cs]) for t in range(steps)]
        window = lambda i: st_ref[i, :, cs] if i < hist else glu[i - hist]
        for t in range(steps):
            acc = jnp.zeros((b, LANES), F32)
            for tap in range(CONV_W):
                acc = acc + window(t + tap) * dwk[tap:tap + 1, cs]
            zc[j, t * b:(t + 1) * b, cs] = acc + dwb[:, cs]
        for i in range(hist):
            sto_ref[i, :, cs] = window(i + steps)

    @pl.when(j == pl.num_programs(0) - 1)
    def _():
        z = zc[...]
        m = jnp.mean(jnp.mean(z, axis=-1, keepdims=True), axis=0, keepdims=True)
        ctr = z - m
        var = jnp.mean(jnp.mean(ctr * ctr, axis=-1, keepdims=True), axis=0, keepdims=True)
        ln = ctr * lax.rsqrt(var + 1e-5)
        for jj in range(zc.shape[0]):
            cols = slice(jj * tc, (jj + 1) * tc)
            y_ref[:, cols] = _silu(ln[jj] * lg[:, cols] + lb[:, cols]).astype(BF16)


def _conv_tm(glu_in, b_glu, dw_k, dw_b, ln_g, ln_b, st0, *, steps):
    rows, _ = glu_in.shape
    hist, b, d = st0.shape
    tc = CONV_TM_COLS
    nct = d // tc
    col = lambda shape, h=0: pl.BlockSpec(shape, lambda j: (0, j + h * nct))
    state = pl.BlockSpec((hist, b, tc), lambda j: (0, 0, j))
    return pl.pallas_call(
        functools.partial(_conv_tm_kernel, steps=steps, b=b),
        grid=(nct,),
        in_specs=[col((rows, tc)), col((rows, tc), 1), col((1, tc)), col((1, tc), 1),
                  col((CONV_W, tc)), col((1, tc)), _const_spec((1, d)), _const_spec((1, d)), state],
        out_specs=[pl.BlockSpec((rows, d), lambda j: (0, 0)), state],
        out_shape=[jax.ShapeDtypeStruct((rows, d), BF16), jax.ShapeDtypeStruct(st0.shape, F32)],
        scratch_shapes=[pltpu.VMEM((nct, rows, tc), F32)],
        compiler_params=_cparams(1),
        name="conformer_conv_tm",
    )(glu_in, glu_in, b_glu.reshape(1, 2 * d), b_glu.reshape(1, 2 * d), dw_k, dw_b.reshape(1, d),
      ln_g.reshape(1, d), ln_b.reshape(1, d), st0)


def _merge_tm_kernel(yt_ref, g_ref, ga_ref, gb_ref, yb_ref, x_ref, wo_ref, m_ref, n2_ref,
                     x1_ref, h2_ref):
    f32 = lambda ref: ref[...].astype(F32)
    merged = (_sigmoid(ga_ref[...]) * (yt_ref[...].T * g_ref[...])
              + _sigmoid(gb_ref[...]) * f32(yb_ref))
    mix = jnp.dot(merged.astype(BF16), wo_ref[...], preferred_element_type=F32)
    x1 = x_ref[...] + m_ref[2] * mix
    x1_ref[...] = x1
    ms = jnp.mean(x1 * x1, axis=-1, keepdims=True)
    h2 = (x1 * lax.rsqrt(ms + 1e-6) * n2_ref[...]) * (1.0 + m_ref[4]) + m_ref[3]
    h2_ref[...] = h2.astype(BF16)


def _merge_tm(yt, g, ga, gb, yb, x, w_out, mod, norm_g, *, steps):
    rows, d = x.shape
    b = rows // steps
    row = pl.BlockSpec((b, d), lambda i: (i, 0))
    return pl.pallas_call(
        _merge_tm_kernel,
        grid=(steps,),
        in_specs=[pl.BlockSpec((d, b), lambda i: (0, i)), row, row, row, row, row,
                  _const_spec((d, d)), _const_spec((N_MOD, b, d)), _const_spec((1, d))],
        out_specs=[row, row],
        out_shape=[jax.ShapeDtypeStruct((rows, d), F32), jax.ShapeDtypeStruct((rows, d), BF16)],
        compiler_params=_cparams(1),
        name="merge_out_proj_tm",
    )(yt, g, ga, gb, yb, x, w_out, mod, norm_g.reshape(1, d))


def _up_act_tm_kernel(h_ref, wg_ref, wv_ref, kg_ref, kv_ref, bg_ref, bv_ref, sg_ref, sv_ref,
                      act_ref, stg_ref, stv_ref, *, steps, b):
    hist = FFN_CONV_W - 1
    h = h_ref[...]
    tn = wg_ref.shape[-1]
    z = []
    halves = ((wg_ref, kg_ref, bg_ref, sg_ref, stg_ref), (wv_ref, kv_ref, bv_ref, sv_ref, stv_ref))
    for w_ref, k_ref, b_ref, s_ref, st_ref in halves:
        u = jnp.dot(h, w_ref[...], preferred_element_type=F32).reshape(steps, b, tn)
        window = [s_ref[i] for i in range(hist)] + [u[t] for t in range(steps)]
        z.append(jnp.stack([b_ref[...] + sum(window[t + tap] * k_ref[tap:tap + 1, :]
                                             for tap in range(FFN_CONV_W))
                            for t in range(steps)]))
        for i in range(hist):
            st_ref[i] = window[steps + i]
    act_ref[...] = (_silu(z[0]) * z[1]).reshape(steps * b, tn).astype(BF16)


def _up_act_tm(h2, w_up, ffn_dw_k, ffn_dw_b, st0, *, steps, d_ff):
    rows, k = h2.shape
    hist, b, _ = st0.shape
    f = w_up.shape[1] // 2
    nj = f // FFN_TN
    col = lambda shape, h: pl.BlockSpec(shape, lambda j: (0, j + h * nj))
    state = lambda h: pl.BlockSpec((hist, b, FFN_TN), lambda j: (0, 0, j + h * nj))
    return pl.pallas_call(
        functools.partial(_up_act_tm_kernel, steps=steps, b=b),
        grid=(nj,),
        in_specs=[_const_spec((rows, k)), col((k, FFN_TN), 0), col((k, FFN_TN), 1),
                  col((FFN_CONV_W, FFN_TN), 0), col((FFN_CONV_W, FFN_TN), 1),
                  col((1, FFN_TN), 0), col((1, FFN_TN), 1), state(0), state(1)],
        out_specs=[pl.BlockSpec((rows, FFN_TN), lambda j: (0, j)), state(0), state(0)],
        out_shape=[jax.ShapeDtypeStruct((rows, d_ff), BF16),
                   jax.ShapeDtypeStruct((hist, b, f), F32),
                   jax.ShapeDtypeStruct((hist, b, f), F32)],
        compiler_params=_cparams(1),
        name="up_conv_act_tm",
    )(h2, w_up, w_up, ffn_dw_k, ffn_dw_k, ffn_dw_b, ffn_dw_b, st0, st0)


def _sample_layer(x, mod, shift0, wkv0, conv0, ffn0, p):
    b, t, d = x.shape
    assert b == LANES
    rows = t * b
    tm = lambda z: jnp.swapaxes(z, 0, 1)
    w_in = p["w_in"]
    x_tm = tm(x)
    mod_tm = tm(mod)
    xs, shift1 = _prep_tm(x_tm, mod_tm, p["norm1_g"], p["mu"], shift0)

    tn = PROJ_TN
    rkv = _proj_transposed(xs, w_in, tn=tn, n=3 * d)
    glu_in = _matmul(xs, w_in, tm=rows, tn=tn, slab=SLAB_H, col0=3 * d // tn, n=4 * d,
                     name="glu_gate_proj")
    ga, gb = glu_in[:, 2 * d:3 * d], glu_in[:, 3 * d:]
    dec, a, g = _lora_transposed(xs, p["w0"], p["w1"], p["w2"], p["a0"], p["a1"], p["a2"],
                                 p["g1"], p["g2"])

    yt, s1 = _wkv_batch_lanes(rkv, a, dec, wkv0.transpose(1, 2, 3, 0), p["k_k"], p["k_a"],
                              p["r_k"], p["lnx_g"], p["lnx_b"], steps=t)
    wkv1 = s1.transpose(3, 0, 1, 2)

    yb, conv1 = _conv_tm(glu_in, p["b_glu"], p["dw_k"], p["dw_b"], p["ln_conv_g"],
                         p["ln_conv_b"], tm(conv0), steps=t)
    x1, h2 = _merge_tm(yt, g, ga, gb, yb, x_tm.reshape(rows, d), p["w_out"].astype(BF16), mod_tm,
                       p["norm2_g"], steps=t)

    d_ff = p["w_down"].shape[0]
    f_pad = -(-d_ff // FFN_TN) * FFN_TN
    act, ffn_g, ffn_v = _up_act_tm(
        h2, _pad_halves(p["w_up"], f_pad).astype(BF16), _pad_halves(p["ffn_dw_k"], f_pad),
        _pad_halves(p["ffn_dw_b"], f_pad).reshape(1, 2 * f_pad), _pad_halves(tm(ffn0), f_pad),
        steps=t, d_ff=d_ff)
    ffn1 = tm(jnp.concatenate([ffn_g[..., :d_ff], ffn_v[..., :d_ff]], axis=-1))
    x2 = _matmul(act[None], p["w_down"], tm=rows, tn=DOWN_TN, res=x1, gate=jnp.tile(mod_tm[5], (t, 1)),
                 name="down_proj")
    y = _final_norm(x2, p["normf_g"], tm=min(NORM_TM, rows))
    return tm(y.reshape(t, b, d)), shift1, wkv1, tm(conv1), ffn1


def _head_minor(w):
    lead = w.shape[:-1]
    return w.reshape(*lead, -1, HEAD_DIM).swapaxes(-1, -2).reshape(*lead, w.shape[-1])


def _prompt_layer(x, mod, shift0, wkv0, conv0, ffn0, p, *, tm_seq, tq_proj, tq_tok, tm_mm, tt,
                  tq_ffn):
    b, t, d = x.shape
    rows = b * t
    groups = b // GROUP
    w_in = p["w_in"]

    xs, shift1 = _prep(x, mod, p["norm1_g"], p["mu"], shift0, tm=tm_seq)
    xs2 = xs.reshape(7, rows, d)

    w_hm = jnp.concatenate([_head_minor(w_in[:, i * d:(i + 1) * d]) for i in (0, 1, 2, 5)],
                           axis=1).astype(BF16)
    rkv = _proj_head_lanes(xs, w_hm, _Tiling(groups, t, 1, tq_proj), tn=d, n=3 * d)
    tn = PROJ_TN
    ga = _matmul(xs2, w_hm, tm=tm_mm, tn=tn, slab=SLAB_H, col0=3 * d // tn, n=d, sigmoid=True,
                 out_dtype=BF16, name="gate_a_proj")
    glu_in = _matmul(xs2, w_in, tm=tm_mm, tn=tn, slab=SLAB_H, col0=3 * d // tn, n=2 * d,
                     name="glu_proj")
    gb = _matmul(xs2, w_in, tm=tm_mm, tn=tn, slab=SLAB_H, col0=6 * d // tn, n=d, sigmoid=True,
                 out_dtype=BF16, name="gate_b_proj")

    tl = _Tiling(groups, t, 1, tq_tok)
    dec, a, g = _lora(xs, p["w0"], p["w1"], p["w2"], p["a0"], p["a1"], p["a2"], p["g1"], p["g2"], tl)
    yw, wkv1 = _wkv(rkv, a, dec, wkv0, p["k_k"], p["k_a"], p["r_k"], p["lnx_g"], p["lnx_b"], tt=tt)
    yb, conv1 = _conv(glu_in.reshape(b, t, 2 * d), p["b_glu"], p["dw_k"], p["dw_b"],
                      p["ln_conv_g"], p["ln_conv_b"], conv0, tm=tm_seq)
    x1, h2 = _merge(yw, g, ga.reshape(b, t, d), gb.reshape(b, t, d), yb, x, p["w_out"], mod,
                    p["norm2_g"], tl)

    d_ff = p["w_down"].shape[0]
    f_pad = -(-d_ff // FFN_TN) * FFN_TN
    act, ffn_g, ffn_v = _up_act(
        h2.reshape(rows, d), _pad_halves(p["w_up"], f_pad).astype(BF16),
        _pad_halves(p["ffn_dw_k"], f_pad), _pad_halves(p["ffn_dw_b"], f_pad).reshape(1, 2 * f_pad),
        _pad_halves(ffn0, f_pad), tq=tq_ffn, t=t, d_ff=d_ff)
    ffn1 = jnp.concatenate([ffn_g[..., :d_ff], ffn_v[..., :d_ff]], axis=-1)
    x2 = _matmul(act[None], p["w_down"], tm=min(tm_mm, DOWN_TM), tn=DOWN_TN, res=x1.reshape(rows, d),
                 gate=mod[:, 5][:, None, :], rows_per_gate=t, name="down_proj")
    y = _final_norm(x2, p["normf_g"], tm=min(NORM_TM, rows))
    return y.reshape(b, t, d), shift1.reshape(b, d), wkv1, conv1, ffn1


def kernel(x_prompt, x_sample, state_shift, state_wkv, state_conv, state_ffn, c_prompt, c_sample,
           norm1_g, norm2_g, normf_g, w_ada, b_ada, mu, w_in, b_glu, w0, w1, w2, a0, a1, a2, g1, g2,
           k_k, k_a, r_k, lnx_g, lnx_b, dw_k, dw_b, ln_conv_g, ln_conv_b, w_out, w_up, ffn_dw_k,
           ffn_dw_b, w_down):
    p = dict(norm1_g=norm1_g, norm2_g=norm2_g, normf_g=normf_g, mu=mu, w_in=w_in, b_glu=b_glu,
             w0=w0, w1=w1, w2=w2, a0=a0, a1=a1, a2=a2, g1=g1, g2=g2, k_k=k_k, k_a=k_a, r_k=r_k,
             lnx_g=lnx_g, lnx_b=lnx_b, dw_k=dw_k, dw_b=dw_b, ln_conv_g=ln_conv_g,
             ln_conv_b=ln_conv_b, w_out=w_out, w_up=w_up, ffn_dw_k=ffn_dw_k, ffn_dw_b=ffn_dw_b,
             w_down=w_down)
    bp, _, d = x_prompt.shape
    bs = x_sample.shape[0]
    f2 = w_up.shape[1]

    c_all = jnp.concatenate([c_prompt, c_sample], axis=0)
    c_rows = -(-c_all.shape[0] // SUBLANES) * SUBLANES
    c_all = jnp.pad(c_all, ((0, c_rows - c_all.shape[0]), (0, 0)))
    mod = _modulation(c_all, w_ada, b_ada).reshape(c_rows, N_MOD, d)
    mod_p, mod_s = mod[:bp], mod[bp:bp + bs]

    y_p, shift_p, wkv_p, conv_p, ffn_p = _prompt_layer(
        x_prompt, mod_p, jnp.zeros((bp, d), F32),
        jnp.zeros((bp, d // HEAD_DIM, HEAD_DIM, HEAD_DIM), F32),
        jnp.zeros((bp, CONV_W - 1, d), F32), jnp.zeros((bp, FFN_CONV_W - 1, f2), F32), p,
        **PROMPT_TILES)

    y_s, shift_s, wkv_s, conv_s, ffn_s = _sample_layer(
        x_sample, mod_s, state_shift, state_wkv, state_conv, state_ffn, p)
    return (y_p, y_s, shift_p, wkv_p, conv_p, ffn_p, shift_s, wkv_s, conv_s, ffn_s)
```
